```python
import jax, jax.numpy as jnp
from jax import lax
import numpy as np

D_MODEL = 1024
BATCH = 2
SEQ = 8192
DEPTH = 4
DEC_BATCH = 32
DEC_SEQ = 1
PAST_LEN = 8192
PAGE_SIZE = 128

HEAD_DIM = 64
SCALE = HEAD_DIM ** -0.5
N_MIXERS = 2
N_A_LAYERS = (DEPTH + 1) // N_MIXERS
N_B_LAYERS = DEPTH // N_MIXERS
RMS_EPS = 1e-6
N_MEM = 256
X_HEADS = 4
X_WIDTH = X_HEADS * HEAD_DIM
A_PATTERNS = ((128, 1), (512, 4), (2048, 16))
A_GROUPS = len(A_PATTERNS)
A_HEADS = 4
A_WIDTH = A_HEADS * HEAD_DIM
A_QKV = 3 * A_GROUPS * A_WIDTH
W_IN_A = A_QKV + A_WIDTH + 2 * X_WIDTH
B_HEADS = 12
B_KV = 2
B_GROUP = B_HEADS // B_KV
B_WIDTH = B_HEADS * HEAD_DIM
B_KV_WIDTH = 6 * B_KV * HEAD_DIM
W_IN_B = B_WIDTH + B_KV_WIDTH + 3 * B_HEADS + B_WIDTH + 2 * X_WIDTH
CMP_LEN = 32
CMP_STRIDE = 16
CMP_HIDDEN = 128
SEL_BLOCK = 64
SEL_TOPK = 16
WIN_B = 512
Q_BLOCK = 128
FORCE_SCORE = 1e4

kernel_name = "hybrid_dilated_nsa_memory_decoder_step"


def _rms(x, g):
    xf = x.astype(jnp.float32)
    y = xf * lax.rsqrt(jnp.mean(xf * xf, axis=-1, keepdims=True) + RMS_EPS)
    return (y * g.astype(jnp.float32)).astype(x.dtype)


def _alibi_slopes(n):
    return 2.0 ** (-8.0 * jnp.arange(1, n + 1, dtype=jnp.float32) / n)


def _masked_softmax(s, mask):
    s = jnp.where(mask, s, -jnp.inf)
    m = jnp.max(s, axis=-1, keepdims=True)
    m = jnp.where(jnp.isfinite(m), m, 0.0)
    e = jnp.where(mask, jnp.exp(s - m), 0.0)
    den = jnp.maximum(jnp.sum(e, axis=-1, keepdims=True), 1e-30)
    return e / den, (m + jnp.log(den))[..., 0]


def _cross(qx, mkv):
    s = jnp.einsum('nthe,nmhe->nhtm', qx, mkv[:, :, 0], preferred_element_type=jnp.float32) * SCALE
    p = jax.nn.softmax(s, axis=-1)
    return jnp.einsum('nhtm,nmhe->nthe', p.astype(mkv.dtype), mkv[:, :, 1])


def _finish(x, mix, gate_m, qx, gate_x, mkv, w_out, g_post):
    n, t = x.shape[:2]
    cx = _cross(qx, mkv).reshape(n, t, X_WIDTH)
    z = jnp.concatenate([mix.astype(x.dtype) * jax.nn.silu(gate_m), cx * jax.nn.silu(gate_x)], axis=-1)
    return x + _rms(z @ w_out, g_post)


def _split_a(proj):
    n, t, _ = proj.shape
    qkv = proj[..., :A_QKV].reshape(n, t, 3, A_GROUPS, A_HEADS, HEAD_DIM)
    gate_m = proj[..., A_QKV:A_QKV + A_WIDTH]
    qx = proj[..., A_QKV + A_WIDTH:A_QKV + A_WIDTH + X_WIDTH].reshape(n, t, X_HEADS, HEAD_DIM)
    gate_x = proj[..., A_QKV + A_WIDTH + X_WIDTH:]
    return qkv, gate_m, qx, gate_x


def _dilated_group_prompt(q, k, v, dil, n_back, slopes):
    n, s_len, h, e = q.shape
    c = n_back
    span = dil * c
    s_pad = -(-s_len // span) * span
    u = s_pad // dil
    nb = u // c

    def to_blocks(t):
        t = jnp.pad(t, ((0, 0), (0, s_pad - s_len), (0, 0), (0, 0)))
        t = t.reshape(n, u, dil, h, e).transpose(0, 2, 1, 3, 4)
        return t.reshape(n, dil, nb, c, h, e)

    def with_prev(t):
        prev = jnp.pad(t, ((0, 0), (0, 0), (1, 0), (0, 0), (0, 0), (0, 0)))[:, :, :-1]
        return jnp.concatenate([prev, t], axis=3)

    qb = to_blocks(q)
    kk = with_prev(to_blocks(k))
    vv = with_prev(to_blocks(v))
    i = jnp.arange(c)[:, None]
    j = jnp.arange(2 * c)[None, :]
    back = c + i - j
    blk = jnp.arange(nb)[:, None, None]
    valid = (back >= 0) & (back <= n_back) & ((blk > 0) | (j >= c))
    s = jnp.einsum('nrbihe,nrbjhe->nrbhij', qb, kk, preferred_element_type=jnp.float32) * SCALE
    s = s - slopes[:, None, None] * (back * dil).astype(jnp.float32)
    p, lse = _masked_softmax(s, valid[:, None])
    o = jnp.einsum('nrbhij,nrbjhe->nrbihe', p.astype(v.dtype), vv)
    o = o.reshape(n, dil, u, h, e).transpose(0, 2, 1, 3, 4).reshape(n, s_pad, h, e)[:, :s_len]
    lse = lse.transpose(0, 1, 2, 4, 3).reshape(n, dil, u, h).transpose(0, 2, 1, 3).reshape(n, s_pad, h)[:, :s_len]
    return o, lse


def _dilated_group_step(q, kk, vv, lb, dil, n_back, slopes):
    t = q.shape[1]
    jj = jnp.arange(n_back + 1)
    idx = lb + jnp.arange(t)[:, None] - jj[None, :] * dil
    valid = idx >= 0
    idx = jnp.maximum(idx, 0)
    kg = kk[:, idx]
    vg = vv[:, idx]
    s = jnp.einsum('nthe,ntjhe->nhtj', q, kg, preferred_element_type=jnp.float32) * SCALE
    s = s - slopes[:, None, None] * (jj * dil).astype(jnp.float32)
    p, lse = _masked_softmax(s, valid)
    o = jnp.einsum('nhtj,ntjhe->nthe', p.astype(vv.dtype), vg)
    return o, lse.transpose(0, 2, 1)


def _dilated_combine(outs, lses):
    w = jax.nn.softmax(jnp.stack(lses, axis=0), axis=0)
    o = jnp.einsum('gnth,gnthe->nthe', w, jnp.stack(outs, axis=0).astype(jnp.float32))
    n, t = o.shape[:2]
    return o.reshape(n, t, A_WIDTH)


def _mixer_a_prompt(qkv, slopes):
    s_len = qkv.shape[1]
    outs, lses, bufs = [], [], []
    for g, (win, dil) in enumerate(A_PATTERNS):
        o, lse = _dilated_group_prompt(qkv[:, :, 0, g], qkv[:, :, 1, g], qkv[:, :, 2, g], dil, win // dil, slopes[g])
        outs.append(o)
        lses.append(lse)
        bufs.append(qkv[:, s_len - min(win, s_len):, 1:, g])
    return _dilated_combine(outs, lses), bufs


def _mixer_a_step(qkv, bufs, slopes):
    outs, lses, rows_new = [], [], []
    for g, (win, dil) in enumerate(A_PATTERNS):
        new = qkv[:, :, 1:, g]
        rows = jnp.concatenate([bufs[g], new], axis=1)
        o, lse = _dilated_group_step(qkv[:, :, 0, g], rows[:, :, 0], rows[:, :, 1], bufs[g].shape[1], dil, win // dil, slopes[g])
        outs.append(o)
        lses.append(lse)
        rows_new.append(new)
    return _dilated_combine(outs, lses), rows_new


def _split_b(proj):
    n, t, _ = proj.shape
    o1 = B_WIDTH
    o2 = o1 + B_KV_WIDTH
    o3 = o2 + 3 * B_HEADS
    o4 = o3 + B_WIDTH
    o5 = o4 + X_WIDTH
    q = proj[..., :o1].reshape(n, t, B_KV, B_GROUP, HEAD_DIM)
    kv6 = proj[..., o1:o2].reshape(n, t, 6, B_KV, HEAD_DIM)
    gates = jax.nn.sigmoid(proj[..., o2:o3].astype(jnp.float32)).reshape(n, t, B_KV, B_GROUP, 3)
    return q, kv6, gates, proj[..., o3:o4], proj[..., o4:o5].reshape(n, t, X_HEADS, HEAD_DIM), proj[..., o5:]


def _compress(k, pos_emb, w1, w2):
    n, l, kv, e = k.shape
    c = (l - CMP_LEN) // CMP_STRIDE + 1
    idx = jnp.arange(c)[:, None] * CMP_STRIDE + jnp.arange(CMP_LEN)[None, :]
    blocks = k[:, idx] + pos_emb[:, None, :]
    flat = blocks.transpose(0, 1, 3, 2, 4).reshape(n, c, kv, CMP_LEN * e)
    return jax.nn.silu(flat @ w1) @ w2


def _to_sel_blocks(k, n_sel):
    n, l, kv, e = k.shape
    k = jnp.pad(k, ((0, 0), (0, n_sel * SEL_BLOCK - l), (0, 0), (0, 0)))
    return k.reshape(n, n_sel, SEL_BLOCK, kv, e).transpose(0, 3, 1, 2, 4)


def _block_importance(p, n_sel):
    c = p.shape[-1]
    per = SEL_BLOCK // CMP_STRIDE
    ov = CMP_LEN // CMP_STRIDE
    length = n_sel * per
    pp = jnp.pad(p, [(0, 0)] * (p.ndim - 1) + [(ov - 1, length - c)])
    r = sum(pp[..., ov - 1 - m:ov - 1 - m + length] for m in range(ov))
    return r.reshape(p.shape[:-1] + (n_sel, per)).sum(-1)


def _nsa_core(q, q_pos, gates, kcmp, vcmp, c_end, ksb, vsb, kw, vw, kw_pos, slopes):
    n, t, kv, g, e = q.shape
    tq = q_pos[:, None]
    sl = slopes[:, :, None, None]
    s = jnp.einsum('ntkge,ncke->nkgtc', q, kcmp, preferred_element_type=jnp.float32) * SCALE
    s = s - sl * (tq - c_end[None, :]).astype(jnp.float32)
    p_c, _ = _masked_softmax(s, c_end[None, :] <= tq)
    o_c = jnp.einsum('nkgtc,ncke->ntkge', p_c.astype(vcmp.dtype), vcmp)
    n_sel = ksb.shape[2]
    imp = _block_importance(p_c.sum(2), n_sel)
    blk = jnp.arange(n_sel)[None, :]
    cur = tq // SEL_BLOCK
    forced = (blk == 0) | (blk == cur) | (blk == cur - 1)
    imp = jnp.where(blk * SEL_BLOCK > tq, -jnp.inf, jnp.where(forced, FORCE_SCORE, imp))
    _, sel = lax.top_k(imp, min(SEL_TOPK, n_sel))
    gather = jax.vmap(jax.vmap(lambda blocks, ix: blocks[ix]))
    kg = gather(ksb, sel)
    vg = gather(vsb, sel)
    kpos = sel[..., None] * SEL_BLOCK + jnp.arange(SEL_BLOCK)
    dist = q_pos[:, None, None] - kpos
    s = jnp.einsum('ntkge,nktjse->nkgtjs', q, kg, preferred_element_type=jnp.float32) * SCALE
    s = s - slopes[None, :, :, None, None, None] * dist[:, :, None].astype(jnp.float32)
    kk = sel.shape[-1] * SEL_BLOCK
    p_s, _ = _masked_softmax(s.reshape(n, kv, g, t, kk), (dist >= 0)[:, :, None].reshape(n, kv, 1, t, kk))
    o_s = jnp.einsum('nkgtx,nktxe->ntkge', p_s.astype(vg.dtype), vg.reshape(n, kv, t, kk, e))
    dw = tq - kw_pos[None, :]
    s = jnp.einsum('ntkge,nlke->nkgtl', q, kw, preferred_element_type=jnp.float32) * SCALE
    s = s - sl * dw.astype(jnp.float32)
    p_w, _ = _masked_softmax(s, (dw >= 0) & (dw <= WIN_B) & (kw_pos[None, :] >= 0))
    o_w = jnp.einsum('nkgtl,nlke->ntkge', p_w.astype(vw.dtype), vw)
    return gates[..., 0:1] * o_c + gates[..., 1:2] * o_s + gates[..., 2:3] * o_w


def _cmp_pair(kc, vc, cmp_pos, cmp_w1, cmp_w2):
    kcmp = _compress(kc, cmp_pos[0], cmp_w1[0], cmp_w2[0])
    vcmp = _compress(vc, cmp_pos[1], cmp_w1[1], cmp_w2[1])
    c_end = jnp.arange(kcmp.shape[1]) * CMP_STRIDE + CMP_LEN - 1
    return kcmp, vcmp, c_end


def _mixer_b_prompt(q, kv6, gates, slopes, cmp_pos, cmp_w1, cmp_w2):
    n, s_len = q.shape[:2]
    kcmp, vcmp, c_end = _cmp_pair(kv6[:, :, 0], kv6[:, :, 1], cmp_pos, cmp_w1, cmp_w2)
    n_sel = -(-s_len // SEL_BLOCK)
    ksb = _to_sel_blocks(kv6[:, :, 2], n_sel)
    vsb = _to_sel_blocks(kv6[:, :, 3], n_sel)
    pad = ((0, 0), (WIN_B, 0), (0, 0), (0, 0))
    kw_pad = jnp.pad(kv6[:, :, 4], pad)
    vw_pad = jnp.pad(kv6[:, :, 5], pad)

    def one(b):
        start = b * Q_BLOCK
        qb = lax.dynamic_slice_in_dim(q, start, Q_BLOCK, axis=1)
        gb = lax.dynamic_slice_in_dim(gates, start, Q_BLOCK, axis=1)
        kwb = lax.dynamic_slice_in_dim(kw_pad, start, Q_BLOCK + WIN_B, axis=1)
        vwb = lax.dynamic_slice_in_dim(vw_pad, start, Q_BLOCK + WIN_B, axis=1)
        q_pos = start + jnp.arange(Q_BLOCK)
        kw_pos = start - WIN_B + jnp.arange(Q_BLOCK + WIN_B)
        return _nsa_core(qb, q_pos, gb, kcmp, vcmp, c_end, ksb, vsb, kwb, vwb, kw_pos, slopes)

    o = lax.map(one, jnp.arange(s_len // Q_BLOCK))
    o = o.transpose(1, 0, 2, 3, 4, 5).reshape(n, s_len, B_WIDTH)
    return o, kv6[:, :, :4], kv6[:, s_len - min(WIN_B, s_len):, 4:]


def _mixer_b_step(q, kv6, gates, past_rows, win_buf, past_len, slopes, cmp_pos, cmp_w1, cmp_w2):
    n, t = q.shape[:2]
    rows = jnp.concatenate([past_rows, kv6[:, :, :4]], axis=1)
    kcmp, vcmp, c_end = _cmp_pair(rows[:, :, 0], rows[:, :, 1], cmp_pos, cmp_w1, cmp_w2)
    n_sel = -(-rows.shape[1] // SEL_BLOCK)
    ksb = _to_sel_blocks(rows[:, :, 2], n_sel)
    vsb = _to_sel_blocks(rows[:, :, 3], n_sel)
    lb = win_buf.shape[1]
    win = jnp.concatenate([win_buf, kv6[:, :, 4:]], axis=1)
    q_pos = past_len + jnp.arange(t)
    kw_pos = past_len - lb + jnp.arange(lb + t)
    o = _nsa_core(q, q_pos, gates, kcmp, vcmp, c_end, ksb, vsb, win[:, :, 0], win[:, :, 1], kw_pos, slopes)
    return o.reshape(n, t, B_WIDTH), kv6[:, :, :4], kv6[:, :, 4:]


def setup_inputs(seed: int = 0) -> dict:
    key = jax.random.key(seed)
    ks = jax.random.split(key, 24)
    nrm = jax.random.normal
    n_pages = PAST_LEN // PAGE_SIZE
    n_used = DEC_BATCH * n_pages
    n_pool = n_used + -(-n_used // 4)
    return {
        "x_prompt": nrm(ks[0], (BATCH, SEQ, D_MODEL), jnp.float32),
        "x_sample": nrm(ks[1], (DEC_BATCH, DEC_SEQ, D_MODEL), jnp.float32),
        "cache_mem_kv": nrm(ks[2], (DEPTH, DEC_BATCH, N_MEM, 2, X_HEADS, HEAD_DIM), jnp.float32),
        "cache_a_w128_kv": nrm(ks[3], (N_A_LAYERS, DEC_BATCH, min(A_PATTERNS[0][0], PAST_LEN), 2, A_HEADS, HEAD_DIM), jnp.float32),
        "cache_a_w512_kv": nrm(ks[4], (N_A_LAYERS, DEC_BATCH, min(A_PATTERNS[1][0], PAST_LEN), 2, A_HEADS, HEAD_DIM), jnp.float32),
        "cache_a_w2048_kv": nrm(ks[5], (N_A_LAYERS, DEC_BATCH, min(A_PATTERNS[2][0], PAST_LEN), 2, A_HEADS, HEAD_DIM), jnp.float32),
        "cache_b_pages": nrm(ks[6], (n_pool, PAGE_SIZE, N_B_LAYERS, 4, B_KV, HEAD_DIM), jnp.float32),
        "cache_b_win_kv": nrm(ks[7], (N_B_LAYERS, DEC_BATCH, min(WIN_B, PAST_LEN), 2, B_KV, HEAD_DIM), jnp.float32),
        "page_table": jax.random.permutation(ks[8], n_pool)[:n_used].reshape(DEC_BATCH, n_pages).astype(jnp.int32),
        "mem_prompt": nrm(ks[9], (BATCH, N_MEM, D_MODEL), jnp.float32),
        "norm_pre": 1.0 + 0.05 * nrm(ks[10], (DEPTH, D_MODEL), jnp.float32),
        "norm_post": 1.0 + 0.05 * nrm(ks[11], (DEPTH, D_MODEL), jnp.float32),
        "norm_mem": 1.0 + 0.05 * nrm(ks[12], (DEPTH, D_MODEL), jnp.float32),
        "w_mem_kv": nrm(ks[13], (DEPTH, D_MODEL, 2 * X_WIDTH), jnp.float32) * D_MODEL ** -0.5,
        "w_in_a": nrm(ks[14], (N_A_LAYERS, D_MODEL, W_IN_A), jnp.float32) * D_MODEL ** -0.5,
        "w_out_a": nrm(ks[15], (N_A_LAYERS, A_WIDTH + X_WIDTH, D_MODEL), jnp.float32) * (A_WIDTH + X_WIDTH) ** -0.5,
        "w_in_b": nrm(ks[16], (N_B_LAYERS, D_MODEL, W_IN_B), jnp.float32) * D_MODEL ** -0.5,
        "w_out_b": nrm(ks[17], (N_B_LAYERS, B_WIDTH + X_WIDTH, D_MODEL), jnp.float32) * (B_WIDTH + X_WIDTH) ** -0.5,
        "cmp_pos": 0.02 * nrm(ks[18], (N_B_LAYERS, 2, CMP_LEN, HEAD_DIM), jnp.float32),
        "cmp_w1": nrm(ks[19], (N_B_LAYERS, 2, CMP_LEN * HEAD_DIM, CMP_HIDDEN), jnp.float32) * (CMP_LEN * HEAD_DIM) ** -0.5,
        "cmp_w2": nrm(ks[20], (N_B_LAYERS, 2, CMP_HIDDEN, HEAD_DIM), jnp.float32) * CMP_HIDDEN ** -0.5,
    }


def reference(x_prompt, x_sample, cache_mem_kv, cache_a_w128_kv, cache_a_w512_kv, cache_a_w2048_kv,
              cache_b_pages, cache_b_win_kv, page_table, mem_prompt, norm_pre, norm_post, norm_mem,
              w_mem_kv, w_in_a, w_out_a, w_in_b, w_out_b, cmp_pos, cmp_w1, cmp_w2):
    cache_a = (cache_a_w128_kv, cache_a_w512_kv, cache_a_w2048_kv)
    n_pages = page_table.shape[1]
    past_len = n_pages * cache_b_pages.shape[1]
    slopes_a = _alibi_slopes(A_GROUPS * A_HEADS).reshape(A_GROUPS, A_HEADS)
    slopes_b = _alibi_slopes(B_HEADS).reshape(B_KV, B_GROUP)
    xp, xs = x_prompt, x_sample
    n_p, n_s = xp.shape[0], xs.shape[0]
    mem_new = []
    a_p = [[] for _ in A_PATTERNS]
    a_s = [[] for _ in A_PATTERNS]
    b_p, b_s, bw_p, bw_s = [], [], [], []
    for i in range(DEPTH):
        li = i // N_MIXERS
        hp = _rms(xp, norm_pre[i])
        hs = _rms(xs, norm_pre[i])
        mkv_p = (_rms(mem_prompt, norm_mem[i]) @ w_mem_kv[i]).reshape(n_p, N_MEM, 2, X_HEADS, HEAD_DIM)
        mem_new.append(mkv_p)
        mkv_s = cache_mem_kv[i]
        if i % N_MIXERS == 0:
            qkv_p, gm_p, qx_p, gx_p = _split_a(hp @ w_in_a[li])
            qkv_s, gm_s, qx_s, gx_s = _split_a(hs @ w_in_a[li])
            mix_p, bufs_p = _mixer_a_prompt(qkv_p, slopes_a)
            mix_s, rows_s = _mixer_a_step(qkv_s, [c[li] for c in cache_a], slopes_a)
            for g in range(A_GROUPS):
                a_p[g].append(bufs_p[g])
                a_s[g].append(rows_s[g])
            w_out = w_out_a[li]
        else:
            q_p, kv6_p, gt_p, gm_p, qx_p, gx_p = _split_b(hp @ w_in_b[li])
            q_s, kv6_s, gt_s, gm_s, qx_s, gx_s = _split_b(hs @ w_in_b[li])
            mix_p, rows_p, win_p = _mixer_b_prompt(q_p, kv6_p, gt_p, slopes_b, cmp_pos[li], cmp_w1[li], cmp_w2[li])
            past = cache_b_pages[page_table, :, li].reshape(n_s, past_len, 4, B_KV, HEAD_DIM)
            mix_s, rows_s, win_s = _mixer_b_step(q_s, kv6_s, gt_s, past, cache_b_win_kv[li], past_len, slopes_b,
                                                 cmp_pos[li], cmp_w1[li], cmp_w2[li])
            b_p.append(rows_p)
            b_s.append(rows_s)
            bw_p.append(win_p)
            bw_s.append(win_s)
            w_out = w_out_b[li]
        xp = _finish(xp, mix_p, gm_p, qx_p, gx_p, mkv_p, w_out, norm_post[i])
        xs = _finish(xs, mix_s, gm_s, qx_s, gx_s, mkv_s, w_out, norm_post[i])
    p_mem_kv = jnp.stack(mem_new, axis=0)
    p_a_w128_kv = jnp.stack(a_p[0], axis=0)
    p_a_w512_kv = jnp.stack(a_p[1], axis=0)
    p_a_w2048_kv = jnp.stack(a_p[2], axis=0)
    p_b_rows = jnp.stack(b_p, axis=2)
    p_b_win_kv = jnp.stack(bw_p, axis=0)
    s_a_w128_kv = jnp.stack(a_s[0], axis=0)
    s_a_w512_kv = jnp.stack(a_s[1], axis=0)
    s_a_w2048_kv = jnp.stack(a_s[2], axis=0)
    s_b_rows = jnp.stack(b_s, axis=2)
    s_b_win_kv = jnp.stack(bw_s, axis=0)
    return (xp, xs, p_mem_kv, p_a_w128_kv, p_a_w512_kv, p_a_w2048_kv, p_b_rows, p_b_win_kv,
            s_a_w128_kv, s_a_w512_kv, s_a_w2048_kv, s_b_rows, s_b_win_kv)
```

```python
import functools

import numpy as np
import jax
import jax.numpy as jnp
from jax import lax
from jax.experimental import pallas as pl
from jax.experimental.pallas import tpu as pltpu

F32 = jnp.float32
BF16 = jnp.bfloat16

D_MODEL = 1024
HEAD_DIM = 64
SCALE = HEAD_DIM ** -0.5
RMS_EPS = 1e-6
N_MEM = 256
X_WIDTH = 256
A_PATTERNS = ((128, 1), (512, 4), (2048, 16))
A_WIDTH = 256
W_IN_A = 3072
B_HEADS = 12
B_KV = 2
B_GROUP = 6
B_WIDTH = 768
W_IN_B = 2852
W_IN_B_PAD = 2944
CMP_LEN = 32
CMP_STRIDE = 16
CMP_HIDDEN = 128
SEL_BLOCK = 64
SEL_TOPK = 16
WIN_B = 512
Q_BLOCK = 128
FORCE_SCORE = 1e4
PAGE_SIZE = 128
NEG = -1e30
VMEM_LIMIT = 56 * 1024 * 1024

NT = (((1,), (1,)), ((), ()))


def _alibi(n):
    k = np.arange(1, n + 1, dtype=np.float32)
    return [float(v) for v in np.float32(2.0) ** (np.float32(-8.0) * k / np.float32(n))]


SLOPES_A = _alibi(12)
SLOPES_B = _alibi(12)


def _params(n_axes):
    return pltpu.CompilerParams(dimension_semantics=("arbitrary",) * n_axes, vmem_limit_bytes=VMEM_LIMIT)


def _sigmoid(x):
    return 1.0 / (1.0 + jnp.exp(-x))


def _silu(x):
    return x * _sigmoid(x)


def _iota(shape, dim):
    return lax.broadcasted_iota(jnp.int32, shape, dim)


def _split3(x):
    hi = x.astype(BF16)
    r1 = x - hi.astype(F32)
    mid = r1.astype(BF16)
    lo = (r1 - mid.astype(F32)).astype(BF16)
    return hi, mid, lo


def _rms_proj_body(x_ref, g_ref, w_ref, o_ref):
    x = x_ref[...]
    y = x * lax.rsqrt(jnp.mean(x * x, axis=-1, keepdims=True) + RMS_EPS)
    y = (y * g_ref[...]).astype(BF16)
    o_ref[...] = jnp.dot(y, w_ref[...], preferred_element_type=F32)


def _rms_proj(x, g, w, tm):
    m, d = x.shape
    n = w.shape[1]
    return pl.pallas_call(
        _rms_proj_body,
        grid=(m // tm,),
        in_specs=[pl.BlockSpec((tm, d), lambda i: (i, 0)),
                  pl.BlockSpec((1, d), lambda i: (0, 0)),
                  pl.BlockSpec((d, n), lambda i: (0, 0))],
        out_specs=pl.BlockSpec((tm, n), lambda i: (i, 0)),
        out_shape=jax.ShapeDtypeStruct((m, n), F32),
        compiler_params=_params(1),
        name="rms_proj",
    )(x, g.reshape(1, d), w)


def _a_prompt_body(q_ref, kc_ref, kp_ref, vc_ref, vp_ref, o_ref, l_ref, *, dil, slopes):
    u = pl.program_id(2)
    q = q_ref[0]
    k = jnp.concatenate([kp_ref[0], kc_ref[0]], axis=0).astype(BF16)
    v = jnp.concatenate([vp_ref[0], vc_ref[0]], axis=0).astype(BF16)
    i = _iota((128, 256), 0)
    j = _iota((128, 256), 1)
    back = 128 + i - j
    jmin = jnp.where(u > 0, 0, 128)
    valid = (back >= 0) & (back <= 128) & (j >= jmin)
    dist = (back * dil).astype(F32)
    lane = _iota((128, 128), 1)
    for pair in range(2):
        sl = slice(pair * 128, (pair + 1) * 128)
        qp, kp, vp = q[:, sl], k[:, sl], v[:, sl]
        o_pair = None
        l_pair = None
        for hh in range(2):
            hm = (lane < 64) if hh == 0 else (lane >= 64)
            qm = jnp.where(hm, qp, 0.0).astype(BF16)
            s = lax.dot_general(qm, kp, NT, preferred_element_type=F32) * SCALE
            s = s - slopes[pair * 2 + hh] * dist
            s = jnp.where(valid, s, NEG)
            m = jnp.max(s, axis=-1, keepdims=True)
            e = jnp.where(valid, jnp.exp(s - m), 0.0)
            den = jnp.maximum(jnp.sum(e, axis=-1, keepdims=True), 1e-30)
            p = (e / den).astype(BF16)
            oh = jnp.dot(p, vp, preferred_element_type=F32)
            lh = jnp.broadcast_to(m + jnp.log(den), (128, 128))
            o_pair = oh if hh == 0 else jnp.where(lane < 64, o_pair, oh)
            l_pair = lh if hh == 0 else jnp.where(lane < 64, l_pair, lh)
        o_ref[0, :, sl] = o_pair
        l_ref[0, :, sl] = l_pair


def _a_prompt_group(proj3, g, dil, n, s_len):
    u_len = s_len // dil
    nblk = W_IN_A // 256
    body = functools.partial(_a_prompt_body, dil=dil, slopes=tuple(SLOPES_A[g * 4:(g + 1) * 4]))
    cur = lambda off: pl.BlockSpec((1, 128, 256), lambda b, r, u: (b, u, r * nblk + off))
    prev = lambda off: pl.BlockSpec((1, 128, 256), lambda b, r, u: (b, jnp.maximum(u - 1, 0), r * nblk + off))
    out_spec = pl.BlockSpec((1, 128, 256), lambda b, r, u: (b, u, r))
    o, l = pl.pallas_call(
        body,
        grid=(n, dil, u_len // 128),
        in_specs=[cur(g), cur(3 + g), prev(3 + g), cur(6 + g), prev(6 + g)],
        out_specs=[out_spec, out_spec],
        out_shape=[jax.ShapeDtypeStruct((n, u_len, dil * 256), F32)] * 2,
        compiler_params=_params(3),
        name=f"a_prompt_g{g}",
    )(proj3, proj3, proj3, proj3, proj3)
    return o.reshape(n, s_len, 256), l.reshape(n, s_len, 256)


def _cross_rows(qx, kx, vx):
    tm = qx.shape[0]
    lane = _iota((tm, 128), 1)
    outs = []
    for pair in range(2):
        sl = slice(pair * 128, (pair + 1) * 128)
        qp, kp, vp = qx[:, sl], kx[:, sl], vx[:, sl]
        o_pair = None
        for hh in range(2):
            hm = (lane < 64) if hh == 0 else (lane >= 64)
            qm = jnp.where(hm, qp, 0.0).astype(BF16)
            s = lax.dot_general(qm, kp, NT, preferred_element_type=F32) * SCALE
            m = jnp.max(s, axis=-1, keepdims=True)
            e = jnp.exp(s - m)
            p = (e / jnp.sum(e, axis=-1, keepdims=True)).astype(BF16)
            oh = jnp.dot(p, vp, preferred_element_type=F32)
            o_pair = oh if hh == 0 else jnp.where(lane < 64, o_pair, oh)
        outs.append(o_pair)
    return jnp.concatenate(outs, axis=1)


def _out_norm_residual(x, z, w, g):
    y = jnp.dot(z.astype(BF16), w, preferred_element_type=F32)
    y = y * lax.rsqrt(jnp.mean(y * y, axis=-1, keepdims=True) + RMS_EPS)
    return x + y * g


def _finish_a_body(x_ref, o0, l0, o1, l1, o2, l2, gm_ref, qx_ref, gx_ref, mkv_ref, w_ref, g_ref, out_ref):
    la, lb, lc = l0[0], l1[0], l2[0]
    m = jnp.maximum(jnp.maximum(la, lb), lc)
    ea, eb, ec = jnp.exp(la - m), jnp.exp(lb - m), jnp.exp(lc - m)
    mix = (ea * o0[0] + eb * o1[0] + ec * o2[0]) / (ea + eb + ec)
    mkv = mkv_ref[0]
    cx = _cross_rows(qx_ref[0], mkv[:, :X_WIDTH].astype(BF16), mkv[:, X_WIDTH:].astype(BF16))
    z = jnp.concatenate([mix * _silu(gm_ref[0]), cx * _silu(gx_ref[0])], axis=1)
    out_ref[0] = _out_norm_residual(x_ref[0], z, w_ref[...], g_ref[...])


def _finish_b_body(x_ref, mix_ref, gm_ref, qx_ref, gx_ref, mkv_ref, w_ref, g_ref, out_ref):
    mkv = mkv_ref[0]
    cx = _cross_rows(qx_ref[0], mkv[:, :X_WIDTH].astype(BF16), mkv[:, X_WIDTH:].astype(BF16))
    z = jnp.concatenate([mix_ref[0] * _silu(gm_ref[0]), cx * _silu(gx_ref[0])], axis=1)
    out_ref[0] = _out_norm_residual(x_ref[0], z, w_ref[...], g_ref[...])


def _finish_a(x, ols, proj, mkv, w_out, g_post, tm):
    n, s_len, d = x.shape
    row = lambda w, c: pl.BlockSpec((1, tm, w), lambda b, t: (b, t, c))
    in_specs = ([row(d, 0)] + [row(256, 0)] * 6 + [row(256, 9), row(256, 10), row(256, 11)]
                + [pl.BlockSpec((1, N_MEM, 2 * X_WIDTH), lambda b, t: (b, 0, 0)),
                   pl.BlockSpec(w_out.shape, lambda b, t: (0, 0)),
                   pl.BlockSpec((1, d), lambda b, t: (0, 0))])
    return pl.pallas_call(
        _finish_a_body, grid=(n, s_len // tm), in_specs=in_specs, out_specs=row(d, 0),
        out_shape=jax.ShapeDtypeStruct((n, s_len, d), F32), compiler_params=_params(2), name="finish_a",
    )(x, *ols, proj, proj, proj, mkv, w_out, g_post.reshape(1, d))


def _finish_b(x, mix, proj, mkv, w_out, g_post, tm):
    n, s_len, d = x.shape
    row = lambda w, c: pl.BlockSpec((1, tm, w), lambda b, t: (b, t, c))
    in_specs = [row(d, 0), row(B_WIDTH, 0), row(B_WIDTH, 2), row(256, 9), row(256, 10),
                pl.BlockSpec((1, N_MEM, 2 * X_WIDTH), lambda b, t: (b, 0, 0)),
                pl.BlockSpec(w_out.shape, lambda b, t: (0, 0)),
                pl.BlockSpec((1, d), lambda b, t: (0, 0))]
    return pl.pallas_call(
        _finish_b_body, grid=(n, s_len // tm), in_specs=in_specs, out_specs=row(d, 0),
        out_shape=jax.ShapeDtypeStruct((n, s_len, d), F32), compiler_params=_params(2), name="finish_b",
    )(x, mix, proj, proj, proj, mkv, w_out, g_post.reshape(1, d))


def _tail_body(x_ref, z_ref, w_ref, g_ref, out_ref):
    out_ref[...] = _out_norm_residual(x_ref[...], z_ref[...], w_ref[...], g_ref[...])


def _tail(x, z, w_out, g_post):
    m, d = x.shape
    full = lambda a: pl.BlockSpec(a.shape, lambda i: (0,) * a.ndim)
    g2 = g_post.reshape(1, d)
    return pl.pallas_call(
        _tail_body, grid=(1,), in_specs=[full(x), full(z), full(w_out), full(g2)], out_specs=full(x),
        out_shape=jax.ShapeDtypeStruct((m, d), F32), compiler_params=_params(1), name="sample_tail",
    )(x, z, w_out, g2)


def _compress_rows(load_rows, pos_ref, w1_ref, w2_ref, n_cmp):
    outs = []
    for t in range(2):
        a = jnp.zeros((n_cmp, 2 * CMP_HIDDEN), F32)
        b = jnp.zeros((n_cmp, 2 * CMP_HIDDEN), F32)
        for l in range(CMP_STRIDE):
            y = load_rows(t, l)
            ya = (y + pos_ref[t, l:l + 1, :]).astype(BF16)
            yb = (y + pos_ref[t, l + CMP_STRIDE:l + CMP_STRIDE + 1, :]).astype(BF16)
            a = a + jnp.dot(ya, w1_ref[t, l], preferred_element_type=F32)
            b = b + jnp.dot(yb, w1_ref[t, l + CMP_STRIDE], preferred_element_type=F32)
        h = a + pltpu.roll(b, n_cmp - 1, axis=0)
        outs.append(jnp.dot(_silu(h).astype(BF16), w2_ref[t], preferred_element_type=F32))
    return outs


def _compress_body(k_ref, v_ref, pos_ref, w1_ref, w2_ref, o_ref, *, n_cmp):
    refs = (k_ref, v_ref)
    load = lambda t, l: refs[t][0, pl.ds(l, n_cmp, stride=CMP_STRIDE), :]
    ck, cv = _compress_rows(load, pos_ref, w1_ref, w2_ref, n_cmp)
    o_ref[0, :, 0:128] = ck.astype(BF16)
    o_ref[0, :, 128:256] = cv.astype(BF16)


def _compress_prompt(proj, posw, w1bd, w2bd):
    n, s_len, _ = proj.shape
    n_cmp = s_len // CMP_STRIDE
    full = lambda a: pl.BlockSpec(a.shape, lambda b: (0,) * a.ndim)
    return pl.pallas_call(
        functools.partial(_compress_body, n_cmp=n_cmp), grid=(n,),
        in_specs=[pl.BlockSpec((1, s_len, 128), lambda b: (b, 0, 6)), pl.BlockSpec((1, s_len, 128), lambda b: (b, 0, 7)),
                  full(posw), full(w1bd), full(w2bd)],
        out_specs=pl.BlockSpec((1, n_cmp, 256), lambda b: (b, 0, 0)),
        out_shape=jax.ShapeDtypeStruct((n, n_cmp, 256), BF16), compiler_params=_params(1), name="compress_prompt",
    )(proj, proj, posw, w1bd, w2bd)


def _place_heads(tiles, lane):
    chunks = []
    for c in range(B_HEADS // 2):
        t0, t1 = tiles[2 * c], tiles[2 * c + 1]
        if (2 * c) // B_GROUP == 1:
            t0 = pltpu.roll(t0, 64, axis=1)
        if (2 * c + 1) // B_GROUP == 0:
            t1 = pltpu.roll(t1, 64, axis=1)
        chunks.append(jnp.where(lane < 64, t0, t1))
    return jnp.concatenate(chunks, axis=1)


def _masked_softmax_rows(s, ok):
    s = jnp.where(ok, s, NEG)
    m = jnp.max(s, axis=-1, keepdims=True)
    e = jnp.where(ok, jnp.exp(s - m), 0.0)
    den = jnp.maximum(jnp.sum(e, axis=-1, keepdims=True), 1e-30)
    return e / den


def _nsa_prompt_body(q_ref, gt_ref, cmp_ref, kvb_ref, e_ref, mt_ref, eg_ref, out_ref,
                     m_sc, l_sc, acc_sc, *, s_len, kc):
    qb = pl.program_id(1)
    qstart = qb * Q_BLOCK
    n_cmp = s_len // CMP_STRIDE
    q = q_ref[0] * SCALE
    lane = _iota((Q_BLOCK, 128), 1)
    tq = qstart + _iota((Q_BLOCK, 1), 0)
    oc_t, os_t, ow_t = [None] * B_HEADS, [None] * B_HEADS, [None] * B_HEADS

    for kv in range(B_KV):
        lm = (lane < 64) if kv == 0 else (lane >= 64)
        parts = []
        for g in range(B_GROUP):
            h = kv * B_GROUP + g
            ch = q[:, (h // 2) * 128:(h // 2 + 1) * 128]
            if h % 2 != kv:
                ch = pltpu.roll(ch, 64, axis=1)
            parts.append(jnp.where(lm, ch, 0.0).astype(BF16))
        q6 = jnp.concatenate(parts, axis=0)

        kcm = cmp_ref[0, :, 0:128]
        vcm = cmp_ref[0, :, 128:256]
        s_all = lax.dot_general(q6, kcm, NT, preferred_element_type=F32)
        cend = CMP_STRIDE * _iota((1, n_cmp), 1) + (CMP_LEN - 1)
        cdist = (tq - cend).astype(F32)
        cok = cend <= tq
        psum = jnp.zeros((Q_BLOCK, n_cmp), F32)
        for g in range(B_GROUP):
            h = kv * B_GROUP + g
            p = _masked_softmax_rows(s_all[g * 128:(g + 1) * 128] - SLOPES_B[h] * cdist, cok)
            psum = psum + p
            oc_t[h] = jnp.dot(p.astype(BF16), vcm, preferred_element_type=F32)

        mt = mt_ref[...]
        imp = sum(lax.dot_general(mt, t, NT, preferred_element_type=F32) for t in _split3(psum))
        blk = _iota((128, Q_BLOCK), 0)
        ql = _iota((128, Q_BLOCK), 1)
        cur = jnp.where(ql >= SEL_BLOCK, qb * 2 + 1, qb * 2)
        forced = (blk == 0) | (blk == cur) | (blk == cur - 1)
        imp = jnp.where(blk > cur, -jnp.inf, jnp.where(forced, FORCE_SCORE, imp))
        blkf = blk.astype(F32)
        sel = jnp.zeros((128, Q_BLOCK), F32)
        for _ in range(SEL_TOPK):
            mx = jnp.max(imp, axis=0, keepdims=True)
            idx = jnp.min(jnp.where(imp == mx, blkf, 1e9), axis=0, keepdims=True)
            hit = blkf == idx
            sel = jnp.where(hit, 1.0, sel)
            imp = jnp.where(hit, -jnp.inf, imp)
        selq = sel.T.astype(BF16)

        m_sc[...] = jnp.full(m_sc.shape, NEG, F32)
        l_sc[...] = jnp.zeros(l_sc.shape, F32)
        acc_sc[...] = jnp.zeros(acc_sc.shape, F32)

        def chunk(c, carry):
            k0 = pl.multiple_of(c * kc, kc)
            kch = kvb_ref[0, pl.ds(k0, kc), 0:128]
            vch = kvb_ref[0, pl.ds(k0, kc), 128:256]
            s_c = lax.dot_general(q6, kch, NT, preferred_element_type=F32)
            selexp = jnp.dot(selq, e_ref[c], preferred_element_type=F32)
            kpos = k0 + _iota((1, kc), 1)
            d_i = tq - kpos
            ok = (selexp > 0.5) & (d_i >= 0)
            d_f = d_i.astype(F32)
            for g in range(B_GROUP):
                rows = slice(g * 128, (g + 1) * 128)
                s = jnp.where(ok, s_c[rows] - SLOPES_B[kv * B_GROUP + g] * d_f, NEG)
                m_old = m_sc[rows]
                m_new = jnp.maximum(m_old, jnp.max(s, axis=-1, keepdims=True))
                alpha = jnp.exp(m_old - m_new)
                p = jnp.exp(s - m_new)
                l_sc[rows] = alpha * l_sc[rows] + jnp.sum(p, axis=-1, keepdims=True)
                acc_sc[rows] = alpha * acc_sc[rows] + jnp.dot(p.astype(BF16), vch, preferred_element_type=F32)
                m_sc[rows] = m_new
            return carry

        lax.fori_loop(0, (qstart + Q_BLOCK + kc - 1) // kc, chunk, 0)
        for g in range(B_GROUP):
            rows = slice(g * 128, (g + 1) * 128)
            os_t[kv * B_GROUP + g] = acc_sc[rows] / l_sc[rows]

        kparts, vparts = [], []
        n_wb = WIN_B // Q_BLOCK + 1
        for wb in range(n_wb):
            r0 = pl.multiple_of(jnp.maximum(qb - (n_wb - 1) + wb, 0) * Q_BLOCK, Q_BLOCK)
            kparts.append(kvb_ref[0, pl.ds(r0, Q_BLOCK), 256:384])
            vparts.append(kvb_ref[0, pl.ds(r0, Q_BLOCK), 384:512])
        kw = jnp.concatenate(kparts, axis=0)
        vw = jnp.concatenate(vparts, axis=0)
        s_w = lax.dot_general(q6, kw, NT, preferred_element_type=F32)
        kwpos = qstart - WIN_B + _iota((1, n_wb * Q_BLOCK), 1)
        dw = tq - kwpos
        wok = (dw >= 0) & (dw <= WIN_B) & (kwpos >= 0)
        dwf = dw.astype(F32)
        for g in range(B_GROUP):
            h = kv * B_GROUP + g
            p = _masked_softmax_rows(s_w[g * 128:(g + 1) * 128] - SLOPES_B[h] * dwf, wok)
            ow_t[h] = jnp.dot(p.astype(BF16), vw, preferred_element_type=F32)

    sg = _sigmoid(gt_ref[0])
    eg = eg_ref[...]
    gexp = sum(jnp.dot(t, eg, preferred_element_type=F32) for t in _split3(sg))
    out_ref[0] = (gexp[:, 0:B_WIDTH] * _place_heads(oc_t, lane)
                  + gexp[:, B_WIDTH:2 * B_WIDTH] * _place_heads(os_t, lane)
                  + gexp[:, 2 * B_WIDTH:] * _place_heads(ow_t, lane))


def _nsa_prompt(proj, cmpd, kvb, e3, mt, eg, kc):
    n, s_len, _ = proj.shape
    n_cmp = s_len // CMP_STRIDE
    full = lambda a: pl.BlockSpec(a.shape, lambda b, t: (0,) * a.ndim)
    return pl.pallas_call(
        functools.partial(_nsa_prompt_body, s_len=s_len, kc=kc),
        grid=(n, s_len // Q_BLOCK),
        in_specs=[pl.BlockSpec((1, Q_BLOCK, B_WIDTH), lambda b, t: (b, t, 0)),
                  pl.BlockSpec((1, Q_BLOCK, 128), lambda b, t: (b, t, 22)),
                  pl.BlockSpec((1, n_cmp, 256), lambda b, t: (b, 0, 0)),
                  pl.BlockSpec((1, s_len, 512), lambda b, t: (b, 0, 0)),
                  full(e3), full(mt), full(eg)],
        out_specs=pl.BlockSpec((1, Q_BLOCK, B_WIDTH), lambda b, t: (b, t, 0)),
        out_shape=jax.ShapeDtypeStruct((n, s_len, B_WIDTH), F32),
        scratch_shapes=[pltpu.VMEM((B_GROUP * Q_BLOCK, 1), F32), pltpu.VMEM((B_GROUP * Q_BLOCK, 1), F32),
                        pltpu.VMEM((B_GROUP * Q_BLOCK, 128), F32)],
        compiler_params=_params(2), name="nsa_prompt",
    )(proj, proj, cmpd, kvb, e3, mt, eg)


def _heads_rows(vec, n_rows, width):
    r = _iota((n_rows, width), 0)
    l = _iota((n_rows, width), 1)
    hm = (l >= r * HEAD_DIM) & (l < r * HEAD_DIM + HEAD_DIM)
    return jnp.where(hm, jnp.broadcast_to(vec, (n_rows, width)), 0.0), hm


def _bf(x):
    return x.astype(BF16).astype(F32)


def _row_consts(n_rows, vals):
    r = _iota((n_rows, 1), 0)
    out = jnp.zeros((n_rows, 1), F32)
    for i, v in enumerate(vals):
        out = jnp.where(r == i, v, out)
    return out


def _sample_cross(qx_row, mkv):
    q8, hm = _heads_rows(qx_row, 8, X_WIDTH)
    s = lax.dot_general(q8.astype(BF16), mkv[:, :X_WIDTH].astype(BF16), NT, preferred_element_type=F32) * SCALE
    m = jnp.max(s, axis=-1, keepdims=True)
    e = jnp.exp(s - m)
    p = (e / jnp.sum(e, axis=-1, keepdims=True)).astype(BF16)
    o8 = jnp.dot(p, mkv[:, X_WIDTH:].astype(BF16), preferred_element_type=F32)
    return jnp.sum(jnp.where(hm, o8, 0.0), axis=0, keepdims=True)


def _sample_a_body(row_ref, c0_ref, c1_ref, c2_ref, mkv_ref, z_ref):
    row = row_ref[0]
    outs, lses = [], []
    hm = None
    for g, (win, dil) in enumerate(A_PATTERNS):
        cache = (c0_ref, c1_ref, c2_ref)[g][0]
        q8, hm = _heads_rows(row[:, g * 256:(g + 1) * 256], 8, A_WIDTH)
        knew = row[:, 768 + g * 256:768 + (g + 1) * 256]
        vnew = row[:, 1536 + g * 256:1536 + (g + 1) * 256]
        q8b = q8.astype(BF16)
        slope = _row_consts(8, SLOPES_A[g * 4:(g + 1) * 4])
        s = lax.dot_general(q8b, cache[:, :A_WIDTH].astype(BF16), NT, preferred_element_type=F32) * SCALE
        n_back = cache.shape[0]
        dist = ((n_back - _iota((8, n_back), 1)) * dil).astype(F32)
        s = s - slope * dist
        s_new = jnp.sum(q8b.astype(F32) * _bf(knew), axis=-1, keepdims=True) * SCALE
        m = jnp.maximum(jnp.max(s, axis=-1, keepdims=True), s_new)
        e = jnp.exp(s - m)
        e_new = jnp.exp(s_new - m)
        den = jnp.sum(e, axis=-1, keepdims=True) + e_new
        o8 = (jnp.dot((e / den).astype(BF16), cache[:, A_WIDTH:].astype(BF16), preferred_element_type=F32)
              + _bf(e_new / den) * _bf(vnew))
        outs.append(o8)
        lses.append(m + jnp.log(den))
    mx = jnp.maximum(jnp.maximum(lses[0], lses[1]), lses[2])
    ws = [jnp.exp(l - mx) for l in lses]
    mix8 = (ws[0] * outs[0] + ws[1] * outs[1] + ws[2] * outs[2]) / (ws[0] + ws[1] + ws[2])
    mix = jnp.sum(jnp.where(hm, mix8, 0.0), axis=0, keepdims=True)
    cx = _sample_cross(row[:, 2560:2816], mkv_ref[0])
    z_ref[0] = jnp.concatenate([mix * _silu(row[:, 2304:2560]), cx * _silu(row[:, 2816:3072])], axis=1)


def _sample_a(proj_s, caches, mkv):
    ns = proj_s.shape[0]
    row3 = proj_s.reshape(ns, 1, W_IN_A)
    views = [c.reshape(ns, c.shape[1] // dil, dil * 512) for c, (_, dil) in zip(caches, A_PATTERNS)]
    cspec = lambda v: pl.BlockSpec((1, v.shape[1], 512), lambda b: (b, 0, 0))
    return pl.pallas_call(
        _sample_a_body, grid=(ns,),
        in_specs=[pl.BlockSpec((1, 1, W_IN_A), lambda b: (b, 0, 0))] + [cspec(v) for v in views]
                 + [pl.BlockSpec((1, N_MEM, 512), lambda b: (b, 0, 0))],
        out_specs=pl.BlockSpec((1, 1, A_WIDTH + X_WIDTH), lambda b: (b, 0, 0)),
        out_shape=jax.ShapeDtypeStruct((ns, 1, A_WIDTH + X_WIDTH), F32), compiler_params=_params(1), name="sample_a",
    )(row3, *views, mkv).reshape(ns, A_WIDTH + X_WIDTH)


def _q16(row):
    r = _iota((16, 128), 0)
    l = _iota((16, 128), 1)
    acc = jnp.zeros((16, 128), F32)
    for c in range(B_HEADS // 2):
        ch = jnp.broadcast_to(row[:, c * 128:(c + 1) * 128], (16, 128))
        rolled = pltpu.roll(ch, 64, axis=1)
        for hh in range(2):
            h = 2 * c + hh
            kv = h // B_GROUP
            lm = (l < 64) if kv == 0 else (l >= 64)
            acc = jnp.where((r == h) & lm, ch if hh == kv else rolled, acc)
    return acc * SCALE


def _sample_b1_body(pt_ref, row_ref, pos_ref, w1_ref, w2_ref, mm_ref, pages_ref, oc_ref, sel_ref,
                    buf, sem, imp_sc, *, li, n_pages, ns, n_layers_b):
    n = pl.program_id(0)
    past = n_pages * PAGE_SIZE
    n_cmp = past // CMP_STRIDE
    col0 = li * 512

    def page_copy(page, p, slot, t):
        return pltpu.make_async_copy(pages_ref.at[page, :, pl.ds(col0 + t * 128, 128)],
                                     buf.at[slot, t, pl.ds(p * PAGE_SIZE, PAGE_SIZE), :], sem.at[slot])

    def fetch(nn, slot):
        def body(p, c):
            page = pt_ref[nn * n_pages + p]
            page_copy(page, p, slot, 0).start()
            page_copy(page, p, slot, 1).start()
            return c
        lax.fori_loop(0, n_pages, body, 0)

    @pl.when(n == 0)
    def _():
        fetch(0, 0)

    @pl.when(n + 1 < ns)
    def _():
        fetch(n + 1, (n + 1) % 2)

    slot = n % 2

    def wbody(p, c):
        page_copy(0, p, slot, 0).wait()
        page_copy(0, p, slot, 1).wait()
        return c
    lax.fori_loop(0, n_pages, wbody, 0)

    load = lambda t, l: buf[slot, t, pl.ds(l, n_cmp, stride=CMP_STRIDE), :]
    ck, cv = _compress_rows(load, pos_ref, w1_ref, w2_ref, n_cmp)

    q16 = _q16(row_ref[0]).astype(BF16)
    slope = _row_consts(16, SLOPES_B)
    s = lax.dot_general(q16, ck.astype(BF16), NT, preferred_element_type=F32)
    cend = CMP_STRIDE * _iota((1, n_cmp), 1) + (CMP_LEN - 1)
    p = _masked_softmax_rows(s - slope * (past - cend).astype(F32), cend <= past)
    oc_ref[0] = jnp.dot(p.astype(BF16), cv.astype(BF16), preferred_element_type=F32)

    r16 = _iota((16, n_cmp), 0)
    ps0 = jnp.sum(jnp.where(r16 < B_GROUP, p, 0.0), axis=0, keepdims=True)
    ps1 = jnp.sum(jnp.where((r16 >= B_GROUP) & (r16 < B_HEADS), p, 0.0), axis=0, keepdims=True)
    psum = jnp.concatenate([ps0, ps1, jnp.zeros((6, n_cmp), F32)], axis=0)
    mm = mm_ref[...]
    imp = sum(jnp.dot(t, mm, preferred_element_type=F32) for t in _split3(psum))
    blk = _iota((8, 256), 1)
    cur = past // SEL_BLOCK
    forced = (blk == 0) | (blk == cur) | (blk == cur - 1)
    imp_sc[n] = jnp.where(blk > cur, -jnp.inf, jnp.where(forced, FORCE_SCORE, imp))

    @pl.when(n == ns - 1)
    def _():
        impa = imp_sc[...]
        blkf = _iota(impa.shape, 2).astype(F32)
        lane = _iota((ns, 8, 128), 2)
        out = jnp.zeros((ns, 8, 128), F32)
        for r in range(SEL_TOPK):
            mx = jnp.max(impa, axis=-1, keepdims=True)
            idx = jnp.min(jnp.where(impa == mx, blkf, 1e9), axis=-1, keepdims=True)
            impa = jnp.where(blkf == idx, -jnp.inf, impa)
            out = jnp.where(lane == r, idx, out)
        sel_ref[...] = out.astype(jnp.int32)


def _sample_b1(page_table, proj_s, posw, w1bd, w2bd, mm, pages3, li):
    ns, n_pages = page_table.shape
    past = n_pages * PAGE_SIZE
    n_cmp = past // CMP_STRIDE
    row3 = proj_s.reshape(ns, 1, W_IN_B_PAD)
    full = lambda a: pl.BlockSpec(a.shape, lambda b, pt: (0,) * a.ndim)
    grid_spec = pltpu.PrefetchScalarGridSpec(
        num_scalar_prefetch=1, grid=(ns,),
        in_specs=[pl.BlockSpec((1, 1, W_IN_B_PAD), lambda b, pt: (b, 0, 0)), full(posw), full(w1bd), full(w2bd),
                  full(mm), pl.BlockSpec(memory_space=pl.ANY)],
        out_specs=[pl.BlockSpec((1, 16, 128), lambda b, pt: (b, 0, 0)),
                   pl.BlockSpec((ns, 8, 128), lambda b, pt: (0, 0, 0))],
        scratch_shapes=[pltpu.VMEM((2, 2, past, 128), F32), pltpu.SemaphoreType.DMA((2,)),
                        pltpu.VMEM((ns, 8, 256), F32)])
    return pl.pallas_call(
        functools.partial(_sample_b1_body, li=li, n_pages=n_pages, ns=ns, n_layers_b=pages3.shape[2] // 512),
        grid_spec=grid_spec,
        out_shape=[jax.ShapeDtypeStruct((ns, 16, 128), F32), jax.ShapeDtypeStruct((ns, 8, 128), jnp.int32)],
        compiler_params=_params(1), name="sample_b1",
    )(page_table.reshape(-1), row3, posw, w1bd, w2bd, mm, pages3)


def _sample_b2_body(pt_ref, sf_ref, row_ref, oc_ref, sel_ref, win_ref, mkv_ref, e16_ref, pages_ref, z_ref,
                    buf, sem, *, li, n_pages, ns):
    n = pl.program_id(0)
    past = n_pages * PAGE_SIZE
    n_blk = past // SEL_BLOCK
    per_page = PAGE_SIZE // SEL_BLOCK
    n_sel = B_KV * SEL_TOPK
    col0 = li * 512 + 256

    def blk_copy(page, r0, i, slot):
        return pltpu.make_async_copy(pages_ref.at[page, pl.ds(r0, SEL_BLOCK), pl.ds(col0, 256)],
                                     buf.at[slot, i], sem.at[slot])

    def fetch(nn, slot):
        def body(i, c):
            j = jnp.minimum(sf_ref[nn * n_sel + i], n_blk - 1)
            page = pt_ref[nn * n_pages + j // per_page]
            r0 = pl.multiple_of((j % per_page) * SEL_BLOCK, SEL_BLOCK)
            blk_copy(page, r0, i, slot).start()
            return c
        lax.fori_loop(0, n_sel, body, 0)

    @pl.when(n == 0)
    def _():
        fetch(0, 0)

    @pl.when(n + 1 < ns)
    def _():
        fetch(n + 1, (n + 1) % 2)

    slot = n % 2

    def wbody(i, c):
        blk_copy(0, 0, i, slot).wait()
        return c
    lax.fori_loop(0, n_sel, wbody, 0)

    row = row_ref[0]
    q16f = _q16(row)
    q16 = q16f.astype(BF16)
    q16r = q16.astype(F32)
    slope = _row_consts(16, SLOPES_B)
    r16 = _iota((16, 128), 0)

    def new_key(col):
        kn = _bf(row[:, col:col + 128])
        return jnp.sum(q16r * kn, axis=-1, keepdims=True)

    def attend(s, ok, s_new, vals, v_new):
        s = jnp.where(ok, s, NEG)
        m = jnp.maximum(jnp.max(s, axis=-1, keepdims=True), s_new)
        e = jnp.where(ok, jnp.exp(s - m), 0.0)
        e_new = jnp.exp(s_new - m)
        den = jnp.sum(e, axis=-1, keepdims=True) + e_new
        return (jnp.dot((e / den).astype(BF16), vals, preferred_element_type=F32)
                + _bf(e_new / den) * _bf(v_new))

    n_keys = SEL_TOPK * SEL_BLOCK
    selexp = jnp.dot(sel_ref[0].astype(F32).astype(BF16), e16_ref[...], preferred_element_type=F32)
    kpos = selexp * SEL_BLOCK + jnp.bitwise_and(_iota((8, n_keys), 1), SEL_BLOCK - 1).astype(F32)
    s_new = new_key(1024)
    v_new = row[:, 1152:1280]
    os16 = jnp.zeros((16, 128), F32)
    for kv in range(B_KV):
        kg = buf[slot, kv * SEL_TOPK:(kv + 1) * SEL_TOPK].reshape(n_keys, 256)
        dist = past - kpos[kv:kv + 1]
        s = lax.dot_general(q16, kg[:, 0:128].astype(BF16), NT, preferred_element_type=F32) - slope * dist
        o = attend(s, dist >= 1.0, s_new, kg[:, 128:256].astype(BF16), v_new)
        rows_kv = (r16 < B_GROUP) if kv == 0 else (r16 >= B_GROUP)
        os16 = jnp.where(rows_kv, o, os16)

    win = win_ref[0]
    lb = win.shape[0]
    dw = (lb - _iota((1, lb), 1)).astype(F32)
    s = lax.dot_general(q16, win[:, 0:128].astype(BF16), NT, preferred_element_type=F32) - slope * dw
    ow16 = attend(s, dw <= float(WIN_B), new_key(1280), win[:, 128:256].astype(BF16), row[:, 1408:1536])

    sg = jnp.broadcast_to(_sigmoid(row[:, 2816:2944]), (16, 128))
    l16 = _iota((16, 128), 1)
    gate = lambda b: jnp.sum(jnp.where(l16 == r16 * 3 + b, sg, 0.0), axis=-1, keepdims=True)
    out16 = gate(0) * oc_ref[0] + gate(1) * os16 + gate(2) * ow16
    lane1 = _iota((1, 128), 1)
    mix = _place_heads([out16[h:h + 1, :] for h in range(B_HEADS)], lane1)

    cx = _sample_cross(row[:, 2304:2560], mkv_ref[0])
    z_ref[0] = jnp.concatenate([mix * _silu(row[:, 1536:2304]), cx * _silu(row[:, 2560:2816])], axis=1)


def _sample_b2(page_table, sel, proj_s, oc, win, mkv, e16, pages3, li):
    ns, n_pages = page_table.shape
    row3 = proj_s.reshape(ns, 1, W_IN_B_PAD)
    full = lambda a: pl.BlockSpec(a.shape, lambda b, pt, sf: (0,) * a.ndim)
    per = lambda a: pl.BlockSpec((1,) + a.shape[1:], lambda b, pt, sf: (b,) + (0,) * (a.ndim - 1))
    grid_spec = pltpu.PrefetchScalarGridSpec(
        num_scalar_prefetch=2, grid=(ns,),
        in_specs=[per(row3), per(oc), per(sel), per(win), per(mkv), full(e16), pl.BlockSpec(memory_space=pl.ANY)],
        out_specs=pl.BlockSpec((1, 1, B_WIDTH + X_WIDTH), lambda b, pt, sf: (b, 0, 0)),
        scratch_shapes=[pltpu.VMEM((2, B_KV * SEL_TOPK, SEL_BLOCK, 256), F32), pltpu.SemaphoreType.DMA((2,))])
    return pl.pallas_call(
        functools.partial(_sample_b2_body, li=li, n_pages=n_pages, ns=ns),
        grid_spec=grid_spec,
        out_shape=jax.ShapeDtypeStruct((ns, 1, B_WIDTH + X_WIDTH), F32),
        compiler_params=_params(1), name="sample_b2",
    )(page_table.reshape(-1), sel[:, :B_KV, :SEL_TOPK].reshape(-1), row3, oc, sel, win, mkv, e16, pages3
      ).reshape(ns, B_WIDTH + X_WIDTH)


def _importance_matrix(n_cmp, n_cols):
    c = np.arange(n_cmp)[:, None]
    j = np.arange(n_cols)[None, :]
    per = SEL_BLOCK // CMP_STRIDE
    m = ((c >= per * j) & (c <= per * j + per - 1)).astype(np.float32)
    m = m + ((c + 1 >= per * j) & (c + 1 <= per * j + per - 1)).astype(np.float32)
    m[n_cmp - 1, :] = 0.0
    return m


def _gate_expand():
    eg = np.zeros((128, 3 * B_WIDTH), np.float32)
    for h in range(B_HEADS):
        for b in range(3):
            eg[h * 3 + b, b * B_WIDTH + h * HEAD_DIM:b * B_WIDTH + (h + 1) * HEAD_DIM] = 1.0
    return eg


def _block_expand(n_rows, n_keys):
    return (np.arange(n_keys)[None, :] // SEL_BLOCK == np.arange(n_rows)[:, None]).astype(np.float32)


def _compress_weights(cmp_pos, cmp_w1, cmp_w2):
    eye = jnp.eye(B_KV, dtype=F32)
    posw = jnp.concatenate([cmp_pos, cmp_pos], axis=-1)
    w1 = cmp_w1.reshape(2, CMP_LEN, HEAD_DIM, CMP_HIDDEN)
    w1bd = jnp.einsum('tlek,jm->tljemk', w1, eye).reshape(2, CMP_LEN, 2 * HEAD_DIM, 2 * CMP_HIDDEN)
    w2bd = jnp.einsum('tke,jm->tjkme', cmp_w2, eye).reshape(2, 2 * CMP_HIDDEN, 2 * HEAD_DIM)
    return posw, w1bd.astype(BF16), w2bd.astype(BF16)


def _permute_w_in_b(w):
    d = w.shape[0]
    return jnp.concatenate([w[:, :1536], w[:, 1572:W_IN_B], w[:, 1536:1572],
                            jnp.zeros((d, W_IN_B_PAD - W_IN_B), w.dtype)], axis=1)


def kernel(x_prompt, x_sample, cache_mem_kv, cache_a_w128_kv, cache_a_w512_kv, cache_a_w2048_kv, cache_b_pages,
           cache_b_win_kv, page_table, mem_prompt, norm_pre, norm_post, norm_mem, w_mem_kv, w_in_a, w_out_a,
           w_in_b, w_out_b, cmp_pos, cmp_w1, cmp_w2):
    n, s_len, d = x_prompt.shape
    ns = x_sample.shape[0]
    depth = norm_pre.shape[0]
    n_pool, page_size, n_lb = cache_b_pages.shape[:3]
    n_pages = page_table.shape[1]
    past = n_pages * page_size
    assert d == D_MODEL and x_sample.shape[1] == 1 and page_size == PAGE_SIZE
    assert s_len % 2048 == 0 and past % 2048 == 0 and s_len >= 2048 and past >= 2048
    caches_a = (cache_a_w128_kv, cache_a_w512_kv, cache_a_w2048_kv)
    for c, (win, _) in zip(caches_a, A_PATTERNS):
        assert c.shape[2] == win
    assert cache_b_win_kv.shape[2] == WIN_B

    tm = 512
    kc = 512
    n_cmp_p = s_len // CMP_STRIDE
    n_cmp_s = past // CMP_STRIDE
    e3 = jnp.asarray(_block_expand(128, s_len).reshape(128, s_len // kc, kc).transpose(1, 0, 2), BF16)
    mt = jnp.asarray(_importance_matrix(n_cmp_p, 128).T, BF16)
    mm = jnp.asarray(_importance_matrix(n_cmp_s, 256), BF16)
    eg = jnp.asarray(_gate_expand(), BF16)
    e16 = jnp.asarray(_block_expand(128, SEL_TOPK * SEL_BLOCK), BF16)
    pages3 = cache_b_pages.reshape(n_pool, page_size, n_lb * 512)

    xp = x_prompt
    xs = x_sample.reshape(ns, d)
    mem2 = mem_prompt.reshape(n * N_MEM, d)
    mem_new = []
    a_p = [[] for _ in A_PATTERNS]
    a_s = [[] for _ in A_PATTERNS]
    b_p, b_s, bw_p, bw_s = [], [], [], []
    for i in range(depth):
        li = i // 2
        mkv_p = _rms_proj(mem2, norm_mem[i], w_mem_kv[i].astype(BF16), tm=N_MEM).reshape(n, N_MEM, 2 * X_WIDTH)
        mem_new.append(mkv_p.reshape(n, N_MEM, 2, 4, HEAD_DIM))
        mkv_s = cache_mem_kv[i].reshape(ns, N_MEM, 2 * X_WIDTH)
        if i % 2 == 0:
            w_in = w_in_a[li].astype(BF16)
            w_out = w_out_a[li].astype(BF16)
            proj_p = _rms_proj(xp.reshape(n * s_len, d), norm_pre[i], w_in, tm=tm).reshape(n, s_len, W_IN_A)
            proj_s = _rms_proj(xs, norm_pre[i], w_in, tm=ns)
            ols = []
            for g, (win, dil) in enumerate(A_PATTERNS):
                o, l = _a_prompt_group(proj_p.reshape(n, s_len // dil, dil * W_IN_A), g, dil, n, s_len)
                ols += [o, l]
                kv_p = proj_p[:, s_len - win:, 768:2304].reshape(n, win, 2, 3, 4, HEAD_DIM)[:, :, :, g]
                a_p[g].append(kv_p)
                a_s[g].append(proj_s[:, 768:2304].reshape(ns, 1, 2, 3, 4, HEAD_DIM)[:, :, :, g])
            xp = _finish_a(xp, ols, proj_p, mkv_p, w_out, norm_post[i], tm=256)
            z = _sample_a(proj_s, [c[li] for c in caches_a], mkv_s)
            xs = _tail(xs, z, w_out, norm_post[i])
        else:
            w_in = _permute_w_in_b(w_in_b[li]).astype(BF16)
            w_out = w_out_b[li].astype(BF16)
            posw, w1bd, w2bd = _compress_weights(cmp_pos[li], cmp_w1[li], cmp_w2[li])
            proj_p = _rms_proj(xp.reshape(n * s_len, d), norm_pre[i], w_in, tm=tm).reshape(n, s_len, W_IN_B_PAD)
            proj_s = _rms_proj(xs, norm_pre[i], w_in, tm=ns)
            cmpd = _compress_prompt(proj_p, posw, w1bd, w2bd)
            kvb = proj_p[:, :, 1024:1536].astype(BF16)
            mix = _nsa_prompt(proj_p, cmpd, kvb, e3, mt, eg, kc)
            xp = _finish_b(xp, mix, proj_p, mkv_p, w_out, norm_post[i], tm=256)
            oc, sel = _sample_b1(page_table, proj_s, posw, w1bd, w2bd, mm, pages3, li)
            win = cache_b_win_kv[li].reshape(ns, WIN_B, 256)
            z = _sample_b2(page_table, sel, proj_s, oc, win, mkv_s, e16, pages3, li)
            xs = _tail(xs, z, w_out, norm_post[i])
            b_p.append(proj_p[:, :, 768:1280].reshape(n, s_len, 4, B_KV, HEAD_DIM))
            bw_p.append(proj_p[:, s_len - WIN_B:, 1280:1536].reshape(n, WIN_B, 2, B_KV, HEAD_DIM))
            b_s.append(proj_s[:, 768:1280].reshape(ns, 1, 4, B_KV, HEAD_DIM))
            bw_s.append(proj_s[:, 1280:1536].reshape(ns, 1, 2, B_KV, HEAD_DIM))
    return (xp, xs.reshape(ns, 1, d), jnp.stack(mem_new, axis=0),
            jnp.stack(a_p[0], axis=0), jnp.stack(a_p[1], axis=0), jnp.stack(a_p[2], axis=0),
            jnp.stack(b_p, axis=2), jnp.stack(bw_p, axis=0),
            jnp.stack(a_s[0], axis=0), jnp.stack(a_s[1], axis=0), jnp.stack(a_s[2], axis=0),
            jnp.stack(b_s, axis=2), jnp.stack(bw_s, axis=0))
```

```python
import functools

import numpy as np
import jax
import jax.numpy as jnp
from jax import lax
from jax.experimental import pallas as pl
from jax.experimental.pallas import tpu as pltpu

F32 = jnp.float32
BF16 = jnp.bfloat16

D_MODEL = 1024
HEAD_DIM = 64
SCALE = HEAD_DIM ** -0.5
RMS_EPS = 1e-6
N_MEM = 256
X_WIDTH = 256
A_PATTERNS = ((128, 1), (512, 4), (2048, 16))
A_WIDTH = 256
W_IN_A = 3072
B_HEADS = 12
B_KV = 2
B_GROUP = 6
B_WIDTH = 768
W_IN_B = 2852
W_IN_B_PAD = 2944
CMP_LEN = 32
CMP_STRIDE = 16
CMP_HIDDEN = 128
SEL_BLOCK = 64
SEL_TOPK = 16
WIN_B = 512
Q_BLOCK = 128
FORCE_SCORE = 1e4
PAGE_SIZE = 128
NEG = -1e30
GROUPS_PER_CHUNK = 4
VMEM_LIMIT = 56 * 1024 * 1024

NT = (((1,), (1,)), ((), ()))


def _alibi(n):
    k = np.arange(1, n + 1, dtype=np.float32)
    return [float(v) for v in np.float32(2.0) ** (np.float32(-8.0) * k / np.float32(n))]


SLOPES_A = _alibi(12)
SLOPES_B = _alibi(12)


def _params(n_axes):
    return pltpu.CompilerParams(dimension_semantics=("arbitrary",) * n_axes, vmem_limit_bytes=VMEM_LIMIT)


def _sigmoid(x):
    return 1.0 / (1.0 + jnp.exp(-x))


def _silu(x):
    return x * _sigmoid(x)


def _iota(shape, dim):
    return lax.broadcasted_iota(jnp.int32, shape, dim)


def _split3(x):
    hi = x.astype(BF16)
    r1 = x - hi.astype(F32)
    mid = r1.astype(BF16)
    lo = (r1 - mid.astype(F32)).astype(BF16)
    return hi, mid, lo


def _rms_proj_body(x_ref, g_ref, w_ref, o_ref):
    x = x_ref[...]
    y = x * lax.rsqrt(jnp.mean(x * x, axis=-1, keepdims=True) + RMS_EPS)
    y = (y * g_ref[...]).astype(BF16)
    o_ref[...] = jnp.dot(y, w_ref[...], preferred_element_type=F32)


def _rms_proj(x, g, w, tm):
    m, d = x.shape
    n = w.shape[1]
    return pl.pallas_call(
        _rms_proj_body,
        grid=(m // tm,),
        in_specs=[pl.BlockSpec((tm, d), lambda i: (i, 0)),
                  pl.BlockSpec((1, d), lambda i: (0, 0)),
                  pl.BlockSpec((d, n), lambda i: (0, 0))],
        out_specs=pl.BlockSpec((tm, n), lambda i: (i, 0)),
        out_shape=jax.ShapeDtypeStruct((m, n), F32),
        compiler_params=_params(1),
        name="rms_proj",
    )(x, g.reshape(1, d), w)


def _a_prompt_body(q_ref, kc_ref, kp_ref, vc_ref, vp_ref, o_ref, l_ref, *, dil, slopes):
    u = pl.program_id(2)
    q = q_ref[0]
    k = jnp.concatenate([kp_ref[0], kc_ref[0]], axis=0).astype(BF16)
    v = jnp.concatenate([vp_ref[0], vc_ref[0]], axis=0).astype(BF16)
    i = _iota((128, 256), 0)
    j = _iota((128, 256), 1)
    back = 128 + i - j
    jmin = jnp.where(u > 0, 0, 128)
    valid = (back >= 0) & (back <= 128) & (j >= jmin)
    dist = (back * dil).astype(F32)
    lane = _iota((128, 128), 1)
    for pair in range(2):
        sl = slice(pair * 128, (pair + 1) * 128)
        qp, kp, vp = q[:, sl], k[:, sl], v[:, sl]
        o_pair = None
        l_pair = None
        for hh in range(2):
            hm = (lane < 64) if hh == 0 else (lane >= 64)
            qm = jnp.where(hm, qp, 0.0).astype(BF16)
            s = lax.dot_general(qm, kp, NT, preferred_element_type=F32) * SCALE
            s = s - slopes[pair * 2 + hh] * dist
            s = jnp.where(valid, s, NEG)
            m = jnp.max(s, axis=-1, keepdims=True)
            e = jnp.where(valid, jnp.exp(s - m), 0.0)
            den = jnp.maximum(jnp.sum(e, axis=-1, keepdims=True), 1e-30)
            p = (e / den).astype(BF16)
            oh = jnp.dot(p, vp, preferred_element_type=F32)
            lh = jnp.broadcast_to(m + jnp.log(den), (128, 128))
            o_pair = oh if hh == 0 else jnp.where(lane < 64, o_pair, oh)
            l_pair = lh if hh == 0 else jnp.where(lane < 64, l_pair, lh)
        o_ref[0, :, sl] = o_pair
        l_ref[0, :, sl] = l_pair


def _a_prompt_group(proj3, g, dil, n, s_len):
    u_len = s_len // dil
    nblk = W_IN_A // 256
    body = functools.partial(_a_prompt_body, dil=dil, slopes=tuple(SLOPES_A[g * 4:(g + 1) * 4]))
    cur = lambda off: pl.BlockSpec((1, 128, 256), lambda b, r, u: (b, u, r * nblk + off))
    prev = lambda off: pl.BlockSpec((1, 128, 256), lambda b, r, u: (b, jnp.maximum(u - 1, 0), r * nblk + off))
    out_spec = pl.BlockSpec((1, 128, 256), lambda b, r, u: (b, u, r))
    o, l = pl.pallas_call(
        body,
        grid=(n, dil, u_len // 128),
        in_specs=[cur(g), cur(3 + g), prev(3 + g), cur(6 + g), prev(6 + g)],
        out_specs=[out_spec, out_spec],
        out_shape=[jax.ShapeDtypeStruct((n, u_len, dil * 256), F32)] * 2,
        compiler_params=_params(3),
        name=f"a_prompt_g{g}",
    )(proj3, proj3, proj3, proj3, proj3)
    return o.reshape(n, s_len, 256), l.reshape(n, s_len, 256)


def _cross_rows(qx, kx, vx):
    tm = qx.shape[0]
    lane = _iota((tm, 128), 1)
    outs = []
    for pair in range(2):
        sl = slice(pair * 128, (pair + 1) * 128)
        qp, kp, vp = qx[:, sl], kx[:, sl], vx[:, sl]
        o_pair = None
        for hh in range(2):
            hm = (lane < 64) if hh == 0 else (lane >= 64)
            qm = jnp.where(hm, qp, 0.0).astype(BF16)
            s = lax.dot_general(qm, kp, NT, preferred_element_type=F32) * SCALE
            m = jnp.max(s, axis=-1, keepdims=True)
            e = jnp.exp(s - m)
            p = (e / jnp.sum(e, axis=-1, keepdims=True)).astype(BF16)
            oh = jnp.dot(p, vp, preferred_element_type=F32)
            o_pair = oh if hh == 0 else jnp.where(lane < 64, o_pair, oh)
        outs.append(o_pair)
    return jnp.concatenate(outs, axis=1)


def _out_norm_residual(x, z, w, g):
    y = jnp.dot(z.astype(BF16), w, preferred_element_type=F32)
    y = y * lax.rsqrt(jnp.mean(y * y, axis=-1, keepdims=True) + RMS_EPS)
    return x + y * g


def _finish_a_body(x_ref, o0, l0, o1, l1, o2, l2, gm_ref, qx_ref, gx_ref, mkv_ref, w_ref, g_ref, out_ref):
    la, lb, lc = l0[0], l1[0], l2[0]
    m = jnp.maximum(jnp.maximum(la, lb), lc)
    ea, eb, ec = jnp.exp(la - m), jnp.exp(lb - m), jnp.exp(lc - m)
    mix = (ea * o0[0] + eb * o1[0] + ec * o2[0]) / (ea + eb + ec)
    mkv = mkv_ref[0]
    cx = _cross_rows(qx_ref[0], mkv[:, :X_WIDTH].astype(BF16), mkv[:, X_WIDTH:].astype(BF16))
    z = jnp.concatenate([mix * _silu(gm_ref[0]), cx * _silu(gx_ref[0])], axis=1)
    out_ref[0] = _out_norm_residual(x_ref[0], z, w_ref[...], g_ref[...])


def _finish_b_body(x_ref, mix_ref, gm_ref, qx_ref, gx_ref, mkv_ref, w_ref, g_ref, out_ref):
    mkv = mkv_ref[0]
    cx = _cross_rows(qx_ref[0], mkv[:, :X_WIDTH].astype(BF16), mkv[:, X_WIDTH:].astype(BF16))
    z = jnp.concatenate([mix_ref[0] * _silu(gm_ref[0]), cx * _silu(gx_ref[0])], axis=1)
    out_ref[0] = _out_norm_residual(x_ref[0], z, w_ref[...], g_ref[...])


def _finish_a(x, ols, proj, mkv, w_out, g_post, tm):
    n, s_len, d = x.shape
    row = lambda w, c: pl.BlockSpec((1, tm, w), lambda b, t: (b, t, c))
    in_specs = ([row(d, 0)] + [row(256, 0)] * 6 + [row(256, 9), row(256, 10), row(256, 11)]
                + [pl.BlockSpec((1, N_MEM, 2 * X_WIDTH), lambda b, t: (b, 0, 0)),
                   pl.BlockSpec(w_out.shape, lambda b, t: (0, 0)),
                   pl.BlockSpec((1, d), lambda b, t: (0, 0))])
    return pl.pallas_call(
        _finish_a_body, grid=(n, s_len // tm), in_specs=in_specs, out_specs=row(d, 0),
        out_shape=jax.ShapeDtypeStruct((n, s_len, d), F32), compiler_params=_params(2), name="finish_a",
    )(x, *ols, proj, proj, proj, mkv, w_out, g_post.reshape(1, d))


def _finish_b(x, mix, proj, mkv, w_out, g_post, tm):
    n, s_len, d = x.shape
    row = lambda w, c: pl.BlockSpec((1, tm, w), lambda b, t: (b, t, c))
    in_specs = [row(d, 0), row(B_WIDTH, 0), row(B_WIDTH, 2), row(256, 9), row(256, 10),
                pl.BlockSpec((1, N_MEM, 2 * X_WIDTH), lambda b, t: (b, 0, 0)),
                pl.BlockSpec(w_out.shape, lambda b, t: (0, 0)),
                pl.BlockSpec((1, d), lambda b, t: (0, 0))]
    return pl.pallas_call(
        _finish_b_body, grid=(n, s_len // tm), in_specs=in_specs, out_specs=row(d, 0),
        out_shape=jax.ShapeDtypeStruct((n, s_len, d), F32), compiler_params=_params(2), name="finish_b",
    )(x, mix, proj, proj, proj, mkv, w_out, g_post.reshape(1, d))


def _tail_body(x_ref, z_ref, w_ref, g_ref, out_ref):
    out_ref[...] = _out_norm_residual(x_ref[...], z_ref[...], w_ref[...], g_ref[...])


def _tail(x, z, w_out, g_post):
    m, d = x.shape
    full = lambda a: pl.BlockSpec(a.shape, lambda i: (0,) * a.ndim)
    g2 = g_post.reshape(1, d)
    return pl.pallas_call(
        _tail_body, grid=(1,), in_specs=[full(x), full(z), full(w_out), full(g2)], out_specs=full(x),
        out_shape=jax.ShapeDtypeStruct((m, d), F32), compiler_params=_params(1), name="sample_tail",
    )(x, z, w_out, g2)


def _compress_rows(load_rows, pos_ref, w1_ref, w2_ref, n_cmp):
    outs = []
    for t in range(2):
        a = jnp.zeros((n_cmp, 2 * CMP_HIDDEN), F32)
        b = jnp.zeros((n_cmp, 2 * CMP_HIDDEN), F32)
        for l in range(CMP_STRIDE):
            y = load_rows(t, l)
            ya = (y + pos_ref[t, l:l + 1, :]).astype(BF16)
            yb = (y + pos_ref[t, l + CMP_STRIDE:l + CMP_STRIDE + 1, :]).astype(BF16)
            a = a + jnp.dot(ya, w1_ref[t, l], preferred_element_type=F32)
            b = b + jnp.dot(yb, w1_ref[t, l + CMP_STRIDE], preferred_element_type=F32)
        h = a + pltpu.roll(b, n_cmp - 1, axis=0)
        outs.append(jnp.dot(_silu(h).astype(BF16), w2_ref[t], preferred_element_type=F32))
    return outs


def _compress_body(k_ref, v_ref, pos_ref, w1_ref, w2_ref, o_ref, *, n_cmp):
    refs = (k_ref, v_ref)
    load = lambda t, l: refs[t][0, pl.ds(l, n_cmp, stride=CMP_STRIDE), :]
    ck, cv = _compress_rows(load, pos_ref, w1_ref, w2_ref, n_cmp)
    o_ref[0, :, 0:128] = ck.astype(BF16)
    o_ref[0, :, 128:256] = cv.astype(BF16)


def _compress_prompt(proj, posw, w1bd, w2bd):
    n, s_len, _ = proj.shape
    n_cmp = s_len // CMP_STRIDE
    full = lambda a: pl.BlockSpec(a.shape, lambda b: (0,) * a.ndim)
    return pl.pallas_call(
        functools.partial(_compress_body, n_cmp=n_cmp), grid=(n,),
        in_specs=[pl.BlockSpec((1, s_len, 128), lambda b: (b, 0, 6)), pl.BlockSpec((1, s_len, 128), lambda b: (b, 0, 7)),
                  full(posw), full(w1bd), full(w2bd)],
        out_specs=pl.BlockSpec((1, n_cmp, 256), lambda b: (b, 0, 0)),
        out_shape=jax.ShapeDtypeStruct((n, n_cmp, 256), BF16), compiler_params=_params(1), name="compress_prompt",
    )(proj, proj, posw, w1bd, w2bd)


def _place_heads(tiles, lane):
    chunks = []
    for c in range(B_HEADS // 2):
        t0, t1 = tiles[2 * c], tiles[2 * c + 1]
        if (2 * c) // B_GROUP == 1:
            t0 = pltpu.roll(t0, 64, axis=1)
        if (2 * c + 1) // B_GROUP == 0:
            t1 = pltpu.roll(t1, 64, axis=1)
        chunks.append(jnp.where(lane < 64, t0, t1))
    return jnp.concatenate(chunks, axis=1)


def _masked_softmax_rows(s, ok):
    s = jnp.where(ok, s, NEG)
    m = jnp.max(s, axis=-1, keepdims=True)
    e = jnp.where(ok, jnp.exp(s - m), 0.0)
    den = jnp.maximum(jnp.sum(e, axis=-1, keepdims=True), 1e-30)
    return e * (1.0 / den)


def _ratio_of_halves(a):
    return a * (1.0 / pltpu.roll(a, 64, axis=1))


def _nsa_prompt_body(q_ref, gt_ref, kc_ref, vct_ref, kw_ref, vwt_ref, ks_ref, vst_ref, et_ref, mt_ref, eg_ref,
                     sl_ref, out_ref, q6_sc, m_sc, acc_sc, sel_sc, words_sm, idx_sm, *, s_len):
    qb = pl.program_id(1)
    qstart = qb * Q_BLOCK
    n_cmp = s_len // CMP_STRIDE
    q = q_ref[0] * SCALE
    lane = _iota((Q_BLOCK, 128), 1)
    tq_row = qstart + _iota((1, Q_BLOCK), 1)
    oc_t, os_t, ow_t = [None] * B_HEADS, [None] * B_HEADS, [None] * B_HEADS

    for kv in range(B_KV):
        row_kv = (_iota((128, 128), 0) < 64) if kv == 0 else (_iota((128, 128), 0) >= 64)
        for g in range(B_GROUP):
            h = kv * B_GROUP + g
            ch = q[:, (h // 2) * 128:(h // 2 + 1) * 128]
            if h % 2 == 1:
                ch = pltpu.roll(ch, 64, axis=1)
            q6_sc[g * 128:(g + 1) * 128, :] = jnp.where(lane < 64, ch, sl_ref[h:h + 1, :]).astype(BF16)
        q6 = q6_sc[...]

        s_t = lax.dot_general(kc_ref[0, kv], q6, NT, preferred_element_type=F32)
        cok = (CMP_STRIDE * _iota((n_cmp, Q_BLOCK), 0) + (CMP_LEN - 1)) <= tq_row
        q_ok = tq_row >= (CMP_LEN - 1)
        psum = jnp.zeros((n_cmp, Q_BLOCK), F32)
        ps = []
        for g in range(B_GROUP):
            s = jnp.where(cok, s_t[:, g * 128:(g + 1) * 128], NEG)
            e = jnp.exp(s - jnp.max(s, axis=0, keepdims=True))
            p = e * jnp.where(q_ok, 1.0 / jnp.sum(e, axis=0, keepdims=True), 0.0)
            psum = psum + p
            ps.append(p.astype(BF16))
        oc = jnp.dot(vct_ref[0], jnp.concatenate(ps, axis=1), preferred_element_type=F32)
        for g in range(B_GROUP):
            oc_t[kv * B_GROUP + g] = oc[:, g * 128:(g + 1) * 128].T

        kparts, vparts, pparts = [], [], []
        n_wb = WIN_B // Q_BLOCK + 1
        for wb in range(n_wb):
            b_raw = qb - (n_wb - 1) + wb
            b = jnp.maximum(b_raw, 0)
            r0 = pl.multiple_of(b * Q_BLOCK, Q_BLOCK)
            kparts.append(kw_ref[0, kv, pl.ds(r0, Q_BLOCK), :])
            vparts.append(jnp.where(row_kv, vwt_ref[0, b], 1.0).astype(BF16))
            pparts.append(jnp.where(b_raw >= 0, r0, s_len) + _iota((128, Q_BLOCK), 0))
        s_t = lax.dot_general(jnp.concatenate(kparts, axis=0), q6, NT, preferred_element_type=F32)
        dw = tq_row - jnp.concatenate(pparts, axis=0)
        wok = (dw >= 0) & (dw <= WIN_B)
        ps = []
        for g in range(B_GROUP):
            s = jnp.where(wok, s_t[:, g * 128:(g + 1) * 128], NEG)
            ps.append(jnp.exp(s - jnp.max(s, axis=0, keepdims=True)).astype(BF16))
        ow = jnp.dot(jnp.concatenate(vparts, axis=1), jnp.concatenate(ps, axis=1), preferred_element_type=F32)
        for g in range(B_GROUP):
            ow_t[kv * B_GROUP + g] = _ratio_of_halves(ow[:, g * 128:(g + 1) * 128].T)

        mt = mt_ref[...]
        imp = sum(jnp.dot(mt, t, preferred_element_type=F32) for t in _split3(psum))
        blk = _iota((128, Q_BLOCK), 0)
        ql = _iota((128, Q_BLOCK), 1)
        cur = jnp.where(ql >= SEL_BLOCK, qb * 2 + 1, qb * 2)
        forced = (blk == 0) | (blk == cur) | (blk == cur - 1)
        imp = jnp.where(blk > cur, -jnp.inf, jnp.where(forced, FORCE_SCORE, imp))
        blkf = blk.astype(F32)
        sel = jnp.zeros((128, Q_BLOCK), F32)
        for _ in range(SEL_TOPK):
            mx = jnp.max(imp, axis=0, keepdims=True)
            idx = jnp.min(jnp.where(imp == mx, blkf, 1e9), axis=0, keepdims=True)
            hit = blkf == idx
            sel = jnp.where(hit, 1.0, sel)
            imp = jnp.where(hit, -jnp.inf, imp)
        sel_sc[...] = sel.astype(BF16)

        blk_col = _iota((128, 1), 0)
        weight = lax.shift_left(jnp.ones((128, 1), jnp.int32), blk_col & 15).astype(F32)
        contrib = jnp.max(sel, axis=1, keepdims=True) * weight
        for w in range(8):
            words_sm[w] = jnp.sum(contrib[16 * w:16 * (w + 1), :]).astype(jnp.int32)

        m_sc[...] = jnp.full(m_sc.shape, NEG, F32)
        acc_sc[...] = jnp.zeros(acc_sc.shape, F32)

        def scan(gi, cnt):
            bits = (words_sm[gi >> 3] >> ((gi & 7) * 2)) & 3
            idx_sm[cnt] = gi
            return cnt + jnp.where(bits != 0, 1, 0)

        cnt = lax.fori_loop(0, qb + 1, scan, 0)
        for j in range(GROUPS_PER_CHUNK - 1):
            idx_sm[cnt + j] = -1

        def chunk(c, carry):
            kts, vts, hits, kposs = [], [], [], []
            for j in range(GROUPS_PER_CHUNK):
                gi_raw = idx_sm[c * GROUPS_PER_CHUNK + j]
                gi = jnp.maximum(gi_raw, 0)
                k0 = pl.multiple_of(gi * Q_BLOCK, Q_BLOCK)
                kts.append(ks_ref[0, kv, pl.ds(k0, Q_BLOCK), :])
                vts.append(jnp.where(row_kv, vst_ref[0, gi], 1.0).astype(BF16))
                hits.append(jnp.dot(et_ref[gi], sel_sc[...], preferred_element_type=F32))
                kposs.append(jnp.where(gi_raw >= 0, k0, s_len) + _iota((128, 128), 0))
            s_t = lax.dot_general(jnp.concatenate(kts, axis=0), q6_sc[...], NT,
                                  preferred_element_type=F32)
            ok = ((jnp.concatenate(hits, axis=0) > 0.5)
                  & (jnp.concatenate(kposs, axis=0) <= qstart + _iota((GROUPS_PER_CHUNK * 128, 128), 1)))
            m_old = m_sc[...]
            m_new, ps = [], []
            for g in range(B_GROUP):
                cols = slice(g * 128, (g + 1) * 128)
                s = jnp.where(ok, s_t[:, cols], NEG)
                mg = jnp.maximum(m_old[:, cols], jnp.max(s, axis=0, keepdims=True))
                ps.append(jnp.exp(s - mg).astype(BF16))
                m_new.append(mg)
            m_new = jnp.concatenate(m_new, axis=1)
            pv = jnp.dot(jnp.concatenate(vts, axis=1), jnp.concatenate(ps, axis=1), preferred_element_type=F32)
            acc_sc[...] = jnp.exp(m_old - m_new) * acc_sc[...] + pv
            m_sc[...] = m_new
            return carry

        lax.fori_loop(0, (cnt + GROUPS_PER_CHUNK - 1) // GROUPS_PER_CHUNK, chunk, 0)
        for g in range(B_GROUP):
            os_t[kv * B_GROUP + g] = _ratio_of_halves(acc_sc[:, g * 128:(g + 1) * 128].T)

    sg = _sigmoid(gt_ref[0])
    eg = eg_ref[...]
    gexp = sum(jnp.dot(t, eg, preferred_element_type=F32) for t in _split3(sg))
    out_ref[0] = (gexp[:, 0:B_WIDTH] * _place_heads(oc_t, lane)
                  + gexp[:, B_WIDTH:2 * B_WIDTH] * _place_heads(os_t, lane)
                  + gexp[:, 2 * B_WIDTH:] * _place_heads(ow_t, lane))


def _nsa_prompt(proj, kc, vc_t, kw, vw_t, ks, vs_t, et3, mt, eg, slope_lanes):
    n, s_len, _ = proj.shape
    n_cmp = s_len // CMP_STRIDE
    n_grp = s_len // Q_BLOCK
    full = lambda a: pl.BlockSpec(a.shape, lambda b, t: (0,) * a.ndim)
    per_n = lambda a: pl.BlockSpec((1,) + a.shape[1:], lambda b, t: (b,) + (0,) * (a.ndim - 1))
    return pl.pallas_call(
        functools.partial(_nsa_prompt_body, s_len=s_len),
        grid=(n, n_grp),
        in_specs=[pl.BlockSpec((1, Q_BLOCK, B_WIDTH), lambda b, t: (b, t, 0)),
                  pl.BlockSpec((1, Q_BLOCK, 128), lambda b, t: (b, t, 22)),
                  per_n(kc), per_n(vc_t), per_n(kw), per_n(vw_t), per_n(ks), per_n(vs_t),
                  full(et3), full(mt), full(eg), full(slope_lanes)],
        out_specs=pl.BlockSpec((1, Q_BLOCK, B_WIDTH), lambda b, t: (b, t, 0)),
        out_shape=jax.ShapeDtypeStruct((n, s_len, B_WIDTH), F32),
        scratch_shapes=[pltpu.VMEM((B_GROUP * Q_BLOCK, 128), BF16), pltpu.VMEM((1, B_GROUP * Q_BLOCK), F32),
                        pltpu.VMEM((128, B_GROUP * Q_BLOCK), F32), pltpu.VMEM((128, Q_BLOCK), BF16),
                        pltpu.SMEM((8,), jnp.int32), pltpu.SMEM((n_grp + GROUPS_PER_CHUNK,), jnp.int32)],
        compiler_params=_params(2), name="nsa_prompt",
    )(proj, proj, kc, vc_t, kw, vw_t, ks, vs_t, et3, mt, eg, slope_lanes)


POS_LANE = 64


def _slope_lanes():
    out = np.zeros((B_HEADS, 128), np.float32)
    for h, s in enumerate(SLOPES_B):
        hi = np.float32(np.asarray(s, np.float32).astype(jnp.bfloat16))
        mid = np.float32(np.asarray(np.float32(s) - hi, np.float32).astype(jnp.bfloat16))
        lo = np.float32(np.asarray(np.float32(s) - hi - mid, np.float32).astype(jnp.bfloat16))
        out[h, POS_LANE:POS_LANE + 6] = [hi, mid, lo, hi, mid, lo]
    return out


def _keys_with_pos(k2, pos):
    n, n_keys, _ = k2.shape
    lo = (pos % Q_BLOCK).astype(BF16)[None, :, None]
    hi = (pos - pos % Q_BLOCK).astype(BF16)[None, :, None]
    tail = jnp.concatenate([jnp.broadcast_to(lo, (n, n_keys, 3)), jnp.broadcast_to(hi, (n, n_keys, 3)),
                            jnp.zeros((n, n_keys, 128 - POS_LANE - 6), BF16)], axis=-1)
    return jnp.stack([jnp.concatenate([k2[..., kv * 64:(kv + 1) * 64], tail], axis=-1) for kv in range(B_KV)], axis=1)


def _values_by_group(v2):
    n, s_len, _ = v2.shape
    return v2.reshape(n, s_len // Q_BLOCK, Q_BLOCK, 128).transpose(0, 1, 3, 2)


def _heads_rows(vec, n_rows, width):
    r = _iota((n_rows, width), 0)
    l = _iota((n_rows, width), 1)
    hm = (l >= r * HEAD_DIM) & (l < r * HEAD_DIM + HEAD_DIM)
    return jnp.where(hm, jnp.broadcast_to(vec, (n_rows, width)), 0.0), hm


def _bf(x):
    return x.astype(BF16).astype(F32)


def _row_consts(n_rows, vals):
    r = _iota((n_rows, 1), 0)
    out = jnp.zeros((n_rows, 1), F32)
    for i, v in enumerate(vals):
        out = jnp.where(r == i, v, out)
    return out


def _sample_cross(qx_row, mkv):
    q8, hm = _heads_rows(qx_row, 8, X_WIDTH)
    s = lax.dot_general(q8.astype(BF16), mkv[:, :X_WIDTH].astype(BF16), NT, preferred_element_type=F32) * SCALE
    m = jnp.max(s, axis=-1, keepdims=True)
    e = jnp.exp(s - m)
    p = (e / jnp.sum(e, axis=-1, keepdims=True)).astype(BF16)
    o8 = jnp.dot(p, mkv[:, X_WIDTH:].astype(BF16), preferred_element_type=F32)
    return jnp.sum(jnp.where(hm, o8, 0.0), axis=0, keepdims=True)


def _sample_a_body(row_ref, c0_ref, c1_ref, c2_ref, mkv_ref, z_ref):
    row = row_ref[0]
    outs, lses = [], []
    hm = None
    for g, (win, dil) in enumerate(A_PATTERNS):
        cache = (c0_ref, c1_ref, c2_ref)[g][0]
        q8, hm = _heads_rows(row[:, g * 256:(g + 1) * 256], 8, A_WIDTH)
        knew = row[:, 768 + g * 256:768 + (g + 1) * 256]
        vnew = row[:, 1536 + g * 256:1536 + (g + 1) * 256]
        q8b = q8.astype(BF16)
        slope = _row_consts(8, SLOPES_A[g * 4:(g + 1) * 4])
        s = lax.dot_general(q8b, cache[:, :A_WIDTH].astype(BF16), NT, preferred_element_type=F32) * SCALE
        n_back = cache.shape[0]
        dist = ((n_back - _iota((8, n_back), 1)) * dil).astype(F32)
        s = s - slope * dist
        s_new = jnp.sum(q8b.astype(F32) * _bf(knew), axis=-1, keepdims=True) * SCALE
        m = jnp.maximum(jnp.max(s, axis=-1, keepdims=True), s_new)
        e = jnp.exp(s - m)
        e_new = jnp.exp(s_new - m)
        den = jnp.sum(e, axis=-1, keepdims=True) + e_new
        o8 = (jnp.dot((e / den).astype(BF16), cache[:, A_WIDTH:].astype(BF16), preferred_element_type=F32)
              + _bf(e_new / den) * _bf(vnew))
        outs.append(o8)
        lses.append(m + jnp.log(den))
    mx = jnp.maximum(jnp.maximum(lses[0], lses[1]), lses[2])
    ws = [jnp.exp(l - mx) for l in lses]
    mix8 = (ws[0] * outs[0] + ws[1] * outs[1] + ws[2] * outs[2]) / (ws[0] + ws[1] + ws[2])
    mix = jnp.sum(jnp.where(hm, mix8, 0.0), axis=0, keepdims=True)
    cx = _sample_cross(row[:, 2560:2816], mkv_ref[0])
    z_ref[0] = jnp.concatenate([mix * _silu(row[:, 2304:2560]), cx * _silu(row[:, 2816:3072])], axis=1)


def _sample_a(proj_s, caches, mkv):
    ns = proj_s.shape[0]
    row3 = proj_s.reshape(ns, 1, W_IN_A)
    views = [c.reshape(ns, c.shape[1] // dil, dil * 512) for c, (_, dil) in zip(caches, A_PATTERNS)]
    cspec = lambda v: pl.BlockSpec((1, v.shape[1], 512), lambda b: (b, 0, 0))
    return pl.pallas_call(
        _sample_a_body, grid=(ns,),
        in_specs=[pl.BlockSpec((1, 1, W_IN_A), lambda b: (b, 0, 0))] + [cspec(v) for v in views]
                 + [pl.BlockSpec((1, N_MEM, 512), lambda b: (b, 0, 0))],
        out_specs=pl.BlockSpec((1, 1, A_WIDTH + X_WIDTH), lambda b: (b, 0, 0)),
        out_shape=jax.ShapeDtypeStruct((ns, 1, A_WIDTH + X_WIDTH), F32), compiler_params=_params(1), name="sample_a",
    )(row3, *views, mkv).reshape(ns, A_WIDTH + X_WIDTH)


def _q16(row):
    r = _iota((16, 128), 0)
    l = _iota((16, 128), 1)
    acc = jnp.zeros((16, 128), F32)
    for c in range(B_HEADS // 2):
        ch = jnp.broadcast_to(row[:, c * 128:(c + 1) * 128], (16, 128))
        rolled = pltpu.roll(ch, 64, axis=1)
        for hh in range(2):
            h = 2 * c + hh
            kv = h // B_GROUP
            lm = (l < 64) if kv == 0 else (l >= 64)
            acc = jnp.where((r == h) & lm, ch if hh == kv else rolled, acc)
    return acc * SCALE


def _sample_b1_body(pt_ref, row_ref, pos_ref, w1_ref, w2_ref, mm_ref, pages_ref, oc_ref, sel_ref,
                    buf, sem, imp_sc, *, li, n_pages, ns, n_layers_b):
    n = pl.program_id(0)
    past = n_pages * PAGE_SIZE
    n_cmp = past // CMP_STRIDE
    col0 = li * 512

    def page_copy(page, p, slot, t):
        return pltpu.make_async_copy(pages_ref.at[page, :, pl.ds(col0 + t * 128, 128)],
                                     buf.at[slot, t, pl.ds(p * PAGE_SIZE, PAGE_SIZE), :], sem.at[slot])

    def fetch(nn, slot):
        def body(p, c):
            page = pt_ref[nn * n_pages + p]
            page_copy(page, p, slot, 0).start()
            page_copy(page, p, slot, 1).start()
            return c
        lax.fori_loop(0, n_pages, body, 0)

    @pl.when(n == 0)
    def _():
        fetch(0, 0)

    @pl.when(n + 1 < ns)
    def _():
        fetch(n + 1, (n + 1) % 2)

    slot = n % 2

    def wbody(p, c):
        page_copy(0, p, slot, 0).wait()
        page_copy(0, p, slot, 1).wait()
        return c
    lax.fori_loop(0, n_pages, wbody, 0)

    load = lambda t, l: buf[slot, t, pl.ds(l, n_cmp, stride=CMP_STRIDE), :]
    ck, cv = _compress_rows(load, pos_ref, w1_ref, w2_ref, n_cmp)

    q16 = _q16(row_ref[0]).astype(BF16)
    slope = _row_consts(16, SLOPES_B)
    s = lax.dot_general(q16, ck.astype(BF16), NT, preferred_element_type=F32)
    cend = CMP_STRIDE * _iota((1, n_cmp), 1) + (CMP_LEN - 1)
    p = _masked_softmax_rows(s - slope * (past - cend).astype(F32), cend <= past)
    oc_ref[0] = jnp.dot(p.astype(BF16), cv.astype(BF16), preferred_element_type=F32)

    r16 = _iota((16, n_cmp), 0)
    ps0 = jnp.sum(jnp.where(r16 < B_GROUP, p, 0.0), axis=0, keepdims=True)
    ps1 = jnp.sum(jnp.where((r16 >= B_GROUP) & (r16 < B_HEADS), p, 0.0), axis=0, keepdims=True)
    psum = jnp.concatenate([ps0, ps1, jnp.zeros((6, n_cmp), F32)], axis=0)
    mm = mm_ref[...]
    imp = sum(jnp.dot(t, mm, preferred_element_type=F32) for t in _split3(psum))
    blk = _iota((8, 256), 1)
    cur = past // SEL_BLOCK
    forced = (blk == 0) | (blk == cur) | (blk == cur - 1)
    imp_sc[n] = jnp.where(blk > cur, -jnp.inf, jnp.where(forced, FORCE_SCORE, imp))

    @pl.when(n == ns - 1)
    def _():
        impa = imp_sc[...]
        blkf = _iota(impa.shape, 2).astype(F32)
        lane = _iota((ns, 8, 128), 2)
        out = jnp.zeros((ns, 8, 128), F32)
        for r in range(SEL_TOPK):
            mx = jnp.max(impa, axis=-1, keepdims=True)
            idx = jnp.min(jnp.where(impa == mx, blkf, 1e9), axis=-1, keepdims=True)
            impa = jnp.where(blkf == idx, -jnp.inf, impa)
            out = jnp.where(lane == r, idx, out)
        sel_ref[...] = out.astype(jnp.int32)


def _sample_b1(page_table, proj_s, posw, w1bd, w2bd, mm, pages3, li):
    ns, n_pages = page_table.shape
    past = n_pages * PAGE_SIZE
    n_cmp = past // CMP_STRIDE
    row3 = proj_s.reshape(ns, 1, W_IN_B_PAD)
    full = lambda a: pl.BlockSpec(a.shape, lambda b, pt: (0,) * a.ndim)
    grid_spec = pltpu.PrefetchScalarGridSpec(
        num_scalar_prefetch=1, grid=(ns,),
        in_specs=[pl.BlockSpec((1, 1, W_IN_B_PAD), lambda b, pt: (b, 0, 0)), full(posw), full(w1bd), full(w2bd),
                  full(mm), pl.BlockSpec(memory_space=pl.ANY)],
        out_specs=[pl.BlockSpec((1, 16, 128), lambda b, pt: (b, 0, 0)),
                   pl.BlockSpec((ns, 8, 128), lambda b, pt: (0, 0, 0))],
        scratch_shapes=[pltpu.VMEM((2, 2, past, 128), F32), pltpu.SemaphoreType.DMA((2,)),
                        pltpu.VMEM((ns, 8, 256), F32)])
    return pl.pallas_call(
        functools.partial(_sample_b1_body, li=li, n_pages=n_pages, ns=ns, n_layers_b=pages3.shape[2] // 512),
        grid_spec=grid_spec,
        out_shape=[jax.ShapeDtypeStruct((ns, 16, 128), F32), jax.ShapeDtypeStruct((ns, 8, 128), jnp.int32)],
        compiler_params=_params(1), name="sample_b1",
    )(page_table.reshape(-1), row3, posw, w1bd, w2bd, mm, pages3)


def _sample_b2_body(pt_ref, sf_ref, row_ref, oc_ref, sel_ref, win_ref, mkv_ref, e16_ref, pages_ref, z_ref,
                    buf, sem, *, li, n_pages, ns):
    n = pl.program_id(0)
    past = n_pages * PAGE_SIZE
    n_blk = past // SEL_BLOCK
    per_page = PAGE_SIZE // SEL_BLOCK
    n_sel = B_KV * SEL_TOPK
    col0 = li * 512 + 256

    def blk_copy(page, r0, i, slot):
        return pltpu.make_async_copy(pages_ref.at[page, pl.ds(r0, SEL_BLOCK), pl.ds(col0, 256)],
                                     buf.at[slot, i], sem.at[slot])

    def fetch(nn, slot):
        def body(i, c):
            j = jnp.minimum(sf_ref[nn * n_sel + i], n_blk - 1)
            page = pt_ref[nn * n_pages + j // per_page]
            r0 = pl.multiple_of((j % per_page) * SEL_BLOCK, SEL_BLOCK)
            blk_copy(page, r0, i, slot).start()
            return c
        lax.fori_loop(0, n_sel, body, 0)

    @pl.when(n == 0)
    def _():
        fetch(0, 0)

    @pl.when(n + 1 < ns)
    def _():
        fetch(n + 1, (n + 1) % 2)

    slot = n % 2

    def wbody(i, c):
        blk_copy(0, 0, i, slot).wait()
        return c
    lax.fori_loop(0, n_sel, wbody, 0)

    row = row_ref[0]
    q16f = _q16(row)
    q16 = q16f.astype(BF16)
    q16r = q16.astype(F32)
    slope = _row_consts(16, SLOPES_B)
    r16 = _iota((16, 128), 0)

    def new_key(col):
        kn = _bf(row[:, col:col + 128])
        return jnp.sum(q16r * kn, axis=-1, keepdims=True)

    def attend(s, ok, s_new, vals, v_new):
        s = jnp.where(ok, s, NEG)
        m = jnp.maximum(jnp.max(s, axis=-1, keepdims=True), s_new)
        e = jnp.where(ok, jnp.exp(s - m), 0.0)
        e_new = jnp.exp(s_new - m)
        den = jnp.sum(e, axis=-1, keepdims=True) + e_new
        return (jnp.dot((e / den).astype(BF16), vals, preferred_element_type=F32)
                + _bf(e_new / den) * _bf(v_new))

    n_keys = SEL_TOPK * SEL_BLOCK
    selexp = jnp.dot(sel_ref[0].astype(F32).astype(BF16), e16_ref[...], preferred_element_type=F32)
    kpos = selexp * SEL_BLOCK + jnp.bitwise_and(_iota((8, n_keys), 1), SEL_BLOCK - 1).astype(F32)
    s_new = new_key(1024)
    v_new = row[:, 1152:1280]
    os16 = jnp.zeros((16, 128), F32)
    for kv in range(B_KV):
        kg = buf[slot, kv * SEL_TOPK:(kv + 1) * SEL_TOPK].reshape(n_keys, 256)
        dist = past - kpos[kv:kv + 1]
        s = lax.dot_general(q16, kg[:, 0:128].astype(BF16), NT, preferred_element_type=F32) - slope * dist
        o = attend(s, dist >= 1.0, s_new, kg[:, 128:256].astype(BF16), v_new)
        rows_kv = (r16 < B_GROUP) if kv == 0 else (r16 >= B_GROUP)
        os16 = jnp.where(rows_kv, o, os16)

    win = win_ref[0]
    lb = win.shape[0]
    dw = (lb - _iota((1, lb), 1)).astype(F32)
    s = lax.dot_general(q16, win[:, 0:128].astype(BF16), NT, preferred_element_type=F32) - slope * dw
    ow16 = attend(s, dw <= float(WIN_B), new_key(1280), win[:, 128:256].astype(BF16), row[:, 1408:1536])

    sg = jnp.broadcast_to(_sigmoid(row[:, 2816:2944]), (16, 128))
    l16 = _iota((16, 128), 1)
    gate = lambda b: jnp.sum(jnp.where(l16 == r16 * 3 + b, sg, 0.0), axis=-1, keepdims=True)
    out16 = gate(0) * oc_ref[0] + gate(1) * os16 + gate(2) * ow16
    lane1 = _iota((1, 128), 1)
    mix = _place_heads([out16[h:h + 1, :] for h in range(B_HEADS)], lane1)

    cx = _sample_cross(row[:, 2304:2560], mkv_ref[0])
    z_ref[0] = jnp.concatenate([mix * _silu(row[:, 1536:2304]), cx * _silu(row[:, 2560:2816])], axis=1)


def _sample_b2(page_table, sel, proj_s, oc, win, mkv, e16, pages3, li):
    ns, n_pages = page_table.shape
    row3 = proj_s.reshape(ns, 1, W_IN_B_PAD)
    full = lambda a: pl.BlockSpec(a.shape, lambda b, pt, sf: (0,) * a.ndim)
    per = lambda a: pl.BlockSpec((1,) + a.shape[1:], lambda b, pt, sf: (b,) + (0,) * (a.ndim - 1))
    grid_spec = pltpu.PrefetchScalarGridSpec(
        num_scalar_prefetch=2, grid=(ns,),
        in_specs=[per(row3), per(oc), per(sel), per(win), per(mkv), full(e16), pl.BlockSpec(memory_space=pl.ANY)],
        out_specs=pl.BlockSpec((1, 1, B_WIDTH + X_WIDTH), lambda b, pt, sf: (b, 0, 0)),
        scratch_shapes=[pltpu.VMEM((2, B_KV * SEL_TOPK, SEL_BLOCK, 256), F32), pltpu.SemaphoreType.DMA((2,))])
    return pl.pallas_call(
        functools.partial(_sample_b2_body, li=li, n_pages=n_pages, ns=ns),
        grid_spec=grid_spec,
        out_shape=jax.ShapeDtypeStruct((ns, 1, B_WIDTH + X_WIDTH), F32),
        compiler_params=_params(1), name="sample_b2",
    )(page_table.reshape(-1), sel[:, :B_KV, :SEL_TOPK].reshape(-1), row3, oc, sel, win, mkv, e16, pages3
      ).reshape(ns, B_WIDTH + X_WIDTH)


def _importance_matrix(n_cmp, n_cols):
    c = np.arange(n_cmp)[:, None]
    j = np.arange(n_cols)[None, :]
    per = SEL_BLOCK // CMP_STRIDE
    m = ((c >= per * j) & (c <= per * j + per - 1)).astype(np.float32)
    m = m + ((c + 1 >= per * j) & (c + 1 <= per * j + per - 1)).astype(np.float32)
    m[n_cmp - 1, :] = 0.0
    return m


def _gate_expand():
    eg = np.zeros((128, 3 * B_WIDTH), np.float32)
    for h in range(B_HEADS):
        for b in range(3):
            eg[h * 3 + b, b * B_WIDTH + h * HEAD_DIM:b * B_WIDTH + (h + 1) * HEAD_DIM] = 1.0
    return eg


def _block_expand(n_rows, n_keys):
    return (np.arange(n_keys)[None, :] // SEL_BLOCK == np.arange(n_rows)[:, None]).astype(np.float32)


def _compress_weights(cmp_pos, cmp_w1, cmp_w2):
    eye = jnp.eye(B_KV, dtype=F32)
    posw = jnp.concatenate([cmp_pos, cmp_pos], axis=-1)
    w1 = cmp_w1.reshape(2, CMP_LEN, HEAD_DIM, CMP_HIDDEN)
    w1bd = jnp.einsum('tlek,jm->tljemk', w1, eye).reshape(2, CMP_LEN, 2 * HEAD_DIM, 2 * CMP_HIDDEN)
    w2bd = jnp.einsum('tke,jm->tjkme', cmp_w2, eye).reshape(2, 2 * CMP_HIDDEN, 2 * HEAD_DIM)
    return posw, w1bd.astype(BF16), w2bd.astype(BF16)


def _permute_w_in_b(w):
    d = w.shape[0]
    return jnp.concatenate([w[:, :1536], w[:, 1572:W_IN_B], w[:, 1536:1572],
                            jnp.zeros((d, W_IN_B_PAD - W_IN_B), w.dtype)], axis=1)


def kernel(x_prompt, x_sample, cache_mem_kv, cache_a_w128_kv, cache_a_w512_kv, cache_a_w2048_kv, cache_b_pages,
           cache_b_win_kv, page_table, mem_prompt, norm_pre, norm_post, norm_mem, w_mem_kv, w_in_a, w_out_a,
           w_in_b, w_out_b, cmp_pos, cmp_w1, cmp_w2):
    n, s_len, d = x_prompt.shape
    ns = x_sample.shape[0]
    depth = norm_pre.shape[0]
    n_pool, page_size, n_lb = cache_b_pages.shape[:3]
    n_pages = page_table.shape[1]
    past = n_pages * page_size
    assert d == D_MODEL and x_sample.shape[1] == 1 and page_size == PAGE_SIZE
    assert s_len % 2048 == 0 and past % 2048 == 0 and s_len >= 2048 and past >= 2048
    caches_a = (cache_a_w128_kv, cache_a_w512_kv, cache_a_w2048_kv)
    for c, (win, _) in zip(caches_a, A_PATTERNS):
        assert c.shape[2] == win
    assert cache_b_win_kv.shape[2] == WIN_B

    tm = 512
    n_cmp_p = s_len // CMP_STRIDE
    n_cmp_s = past // CMP_STRIDE
    et3 = jnp.asarray(_block_expand(128, s_len).reshape(128, s_len // Q_BLOCK, Q_BLOCK).transpose(1, 2, 0), BF16)
    slope_lanes = jnp.asarray(_slope_lanes(), F32)
    mt = jnp.asarray(_importance_matrix(n_cmp_p, 128).T, BF16)
    mm = jnp.asarray(_importance_matrix(n_cmp_s, 256), BF16)
    eg = jnp.asarray(_gate_expand(), BF16)
    e16 = jnp.asarray(_block_expand(128, SEL_TOPK * SEL_BLOCK), BF16)
    pages3 = cache_b_pages.reshape(n_pool, page_size, n_lb * 512)

    xp = x_prompt
    xs = x_sample.reshape(ns, d)
    mem2 = mem_prompt.reshape(n * N_MEM, d)
    mem_new = []
    a_p = [[] for _ in A_PATTERNS]
    a_s = [[] for _ in A_PATTERNS]
    b_p, b_s, bw_p, bw_s = [], [], [], []
    for i in range(depth):
        li = i // 2
        mkv_p = _rms_proj(mem2, norm_mem[i], w_mem_kv[i].astype(BF16), tm=N_MEM).reshape(n, N_MEM, 2 * X_WIDTH)
        mem_new.append(mkv_p.reshape(n, N_MEM, 2, 4, HEAD_DIM))
        mkv_s = cache_mem_kv[i].reshape(ns, N_MEM, 2 * X_WIDTH)
        if i % 2 == 0:
            w_in = w_in_a[li].astype(BF16)
            w_out = w_out_a[li].astype(BF16)
            proj_p = _rms_proj(xp.reshape(n * s_len, d), norm_pre[i], w_in, tm=tm).reshape(n, s_len, W_IN_A)
            proj_s = _rms_proj(xs, norm_pre[i], w_in, tm=ns)
            ols = []
            for g, (win, dil) in enumerate(A_PATTERNS):
                o, l = _a_prompt_group(proj_p.reshape(n, s_len // dil, dil * W_IN_A), g, dil, n, s_len)
                ols += [o, l]
                kv_p = proj_p[:, s_len - win:, 768:2304].reshape(n, win, 2, 3, 4, HEAD_DIM)[:, :, :, g]
                a_p[g].append(kv_p)
                a_s[g].append(proj_s[:, 768:2304].reshape(ns, 1, 2, 3, 4, HEAD_DIM)[:, :, :, g])
            xp = _finish_a(xp, ols, proj_p, mkv_p, w_out, norm_post[i], tm=256)
            z = _sample_a(proj_s, [c[li] for c in caches_a], mkv_s)
            xs = _tail(xs, z, w_out, norm_post[i])
        else:
            w_in = _permute_w_in_b(w_in_b[li]).astype(BF16)
            w_out = w_out_b[li].astype(BF16)
            posw, w1bd, w2bd = _compress_weights(cmp_pos[li], cmp_w1[li], cmp_w2[li])
            proj_p = _rms_proj(xp.reshape(n * s_len, d), norm_pre[i], w_in, tm=tm).reshape(n, s_len, W_IN_B_PAD)
            proj_s = _rms_proj(xs, norm_pre[i], w_in, tm=ns)
            cmpd = _compress_prompt(proj_p, posw, w1bd, w2bd)
            pos = jnp.arange(s_len, dtype=jnp.int32)
            cend = CMP_STRIDE * jnp.arange(n_cmp_p, dtype=jnp.int32) + (CMP_LEN - 1)
            kvs = proj_p[:, :, 1024:1536].astype(BF16)
            mix = _nsa_prompt(proj_p, _keys_with_pos(cmpd[:, :, 0:128], cend), cmpd[:, :, 128:256].transpose(0, 2, 1),
                              _keys_with_pos(kvs[:, :, 256:384], pos), _values_by_group(kvs[:, :, 384:512]),
                              _keys_with_pos(kvs[:, :, 0:128], pos), _values_by_group(kvs[:, :, 128:256]),
                              et3, mt, eg, slope_lanes)
            xp = _finish_b(xp, mix, proj_p, mkv_p, w_out, norm_post[i], tm=256)
            oc, sel = _sample_b1(page_table, proj_s, posw, w1bd, w2bd, mm, pages3, li)
            win = cache_b_win_kv[li].reshape(ns, WIN_B, 256)
            z = _sample_b2(page_table, sel, proj_s, oc, win, mkv_s, e16, pages3, li)
            xs = _tail(xs, z, w_out, norm_post[i])
            b_p.append(proj_p[:, :, 768:1280].reshape(n, s_len, 4, B_KV, HEAD_DIM))
            bw_p.append(proj_p[:, s_len - WIN_B:, 1280:1536].reshape(n, WIN_B, 2, B_KV, HEAD_DIM))
            b_s.append(proj_s[:, 768:1280].reshape(ns, 1, 4, B_KV, HEAD_DIM))
            bw_s.append(proj_s[:, 1280:1536].reshape(ns, 1, 2, B_KV, HEAD_DIM))
    return (xp, xs.reshape(ns, 1, d), jnp.stack(mem_new, axis=0),
            jnp.stack(a_p[0], axis=0), jnp.stack(a_p[1], axis=0), jnp.stack(a_p[2], axis=0),
            jnp.stack(b_p, axis=2), jnp.stack(bw_p, axis=0),
            jnp.stack(a_s[0], axis=0), jnp.stack(a_s[1], axis=0), jnp.stack(a_s[2], axis=0),
            jnp.stack(b_s, axis=2), jnp.stack(bw_s, axis=0))
```

```python
import functools

import numpy as np
import jax
import jax.numpy as jnp
from jax import lax
from jax.experimental import pallas as pl
from jax.experimental.pallas import tpu as pltpu

F32 = jnp.float32
BF16 = jnp.bfloat16

D_MODEL = 1024
HEAD_DIM = 64
SCALE = HEAD_DIM ** -0.5
RMS_EPS = 1e-6
N_MEM = 256
X_WIDTH = 256
A_PATTERNS = ((128, 1), (512, 4), (2048, 16))
A_WIDTH = 256
W_IN_A = 3072
B_HEADS = 12
B_KV = 2
B_GROUP = 6
B_WIDTH = 768
W_IN_B = 2852
W_IN_B_PAD = 2944
CMP_LEN = 32
CMP_STRIDE = 16
CMP_HIDDEN = 128
SEL_BLOCK = 64
SEL_TOPK = 16
WIN_B = 512
Q_BLOCK = 128
FORCE_SCORE = 1e4
PAGE_SIZE = 128
NEG = -1e30
GROUPS_PER_CHUNK = 4
VMEM_LIMIT = 56 * 1024 * 1024

NT = (((1,), (1,)), ((), ()))


def _alibi(n):
    k = np.arange(1, n + 1, dtype=np.float32)
    return [float(v) for v in np.float32(2.0) ** (np.float32(-8.0) * k / np.float32(n))]


SLOPES_A = _alibi(12)
SLOPES_B = _alibi(12)


def _params(n_axes):
    return pltpu.CompilerParams(dimension_semantics=("arbitrary",) * n_axes, vmem_limit_bytes=VMEM_LIMIT)


def _sigmoid(x):
    return 1.0 / (1.0 + jnp.exp(-x))


def _silu(x):
    return x * _sigmoid(x)


def _iota(shape, dim):
    return lax.broadcasted_iota(jnp.int32, shape, dim)


def _split3(x):
    hi = x.astype(BF16)
    r1 = x - hi.astype(F32)
    mid = r1.astype(BF16)
    lo = (r1 - mid.astype(F32)).astype(BF16)
    return hi, mid, lo


def _rms_proj_body(x_ref, g_ref, w_ref, o_ref):
    x = x_ref[...]
    y = x * lax.rsqrt(jnp.mean(x * x, axis=-1, keepdims=True) + RMS_EPS)
    y = (y * g_ref[...]).astype(BF16)
    o_ref[...] = jnp.dot(y, w_ref[...], preferred_element_type=F32)


def _rms_proj(x, g, w, tm):
    m, d = x.shape
    n = w.shape[1]
    return pl.pallas_call(
        _rms_proj_body,
        grid=(m // tm,),
        in_specs=[pl.BlockSpec((tm, d), lambda i: (i, 0)),
                  pl.BlockSpec((1, d), lambda i: (0, 0)),
                  pl.BlockSpec((d, n), lambda i: (0, 0))],
        out_specs=pl.BlockSpec((tm, n), lambda i: (i, 0)),
        out_shape=jax.ShapeDtypeStruct((m, n), F32),
        compiler_params=_params(1),
        name="rms_proj",
    )(x, g.reshape(1, d), w)


A_SPAN = 2048


def _a_prompt_body(*refs, dil, slopes):
    ins, outs = refs[:10], refs[10:]
    t = pl.program_id(1)
    blk_rows = 128 * dil
    n_ub = A_SPAN // blk_rows
    i = _iota((128, 256), 0)
    j = _iota((128, 256), 1)
    back = 128 + i - j
    in_band = (back >= 0) & (back <= 128)
    dist = (back * dil).astype(F32)
    lane = _iota((128, 128), 1)

    def rows(ref, start):
        return ref[0, pl.ds(start, 128, stride=dil), :] if dil > 1 else ref[0, pl.ds(start, 128), :]

    def block(pair, ub, r, first):
        q_ref, k_ref, kp_ref, v_ref, vp_ref = ins[pair * 5:(pair + 1) * 5]
        o_ref, l_ref = outs[pair * 2:(pair + 1) * 2]
        start = ub * blk_rows + r
        if first:
            k_prev, v_prev = rows(kp_ref, r), rows(vp_ref, r)
            valid = in_band & (j >= jnp.where(t > 0, 0, 128))
        else:
            k_prev, v_prev = rows(k_ref, start - blk_rows), rows(v_ref, start - blk_rows)
            valid = in_band
        qp = rows(q_ref, start)
        kp = jnp.concatenate([k_prev, rows(k_ref, start)], axis=0).astype(BF16)
        vp = jnp.concatenate([v_prev, rows(v_ref, start)], axis=0).astype(BF16)
        o_pair = None
        l_pair = None
        for hh in range(2):
            hm = (lane < 64) if hh == 0 else (lane >= 64)
            qm = jnp.where(hm, qp, 0.0).astype(BF16)
            s = lax.dot_general(qm, kp, NT, preferred_element_type=F32) * SCALE
            s = jnp.where(valid, s - slopes[pair * 2 + hh] * dist, NEG)
            m = jnp.max(s, axis=-1, keepdims=True)
            e = jnp.exp(s - m)
            den = jnp.sum(e, axis=-1, keepdims=True)
            oh = jnp.dot((e * (1.0 / den)).astype(BF16), vp, preferred_element_type=F32)
            lh = jnp.broadcast_to(m + jnp.log(den), (128, 128))
            o_pair = oh if hh == 0 else jnp.where(lane < 64, o_pair, oh)
            l_pair = lh if hh == 0 else jnp.where(lane < 64, l_pair, lh)
        if dil > 1:
            o_ref[0, pl.ds(start, 128, stride=dil), :] = o_pair
            l_ref[0, pl.ds(start, 128, stride=dil), :] = l_pair
        else:
            o_ref[0, pl.ds(start, 128), :] = o_pair
            l_ref[0, pl.ds(start, 128), :] = l_pair

    def run(count, fn):
        if count == 0:
            return
        unroll = next(c for c in (4, 5, 3, 2, 1) if count % c == 0)

        def body(it, c):
            for k in range(unroll):
                fn(it * unroll + k)
            return c
        lax.fori_loop(0, count // unroll, body, 0)

    for pair in range(2):
        run(dil, lambda r, pair=pair: block(pair, 0, r, True))
        run((n_ub - 1) * dil, lambda idx, pair=pair: block(pair, 1 + idx // dil, idx % dil, False))


def _a_prompt_group(proj, g, dil):
    n, s_len, _ = proj.shape
    blk_rows = 128 * dil
    per_span = A_SPAN // blk_rows
    body = functools.partial(_a_prompt_body, dil=dil, slopes=tuple(SLOPES_A[g * 4:(g + 1) * 4]))
    cur = lambda col: pl.BlockSpec((1, A_SPAN, 128), lambda b, t: (b, t, col))
    prev = lambda col: pl.BlockSpec((1, blk_rows, 128), lambda b, t: (b, jnp.maximum(t * per_span - 1, 0), col))
    in_specs = []
    for pair in range(2):
        qc, kc, vc = 2 * g + pair, 6 + 2 * g + pair, 12 + 2 * g + pair
        in_specs += [cur(qc), cur(kc), prev(kc), cur(vc), prev(vc)]
    out_spec = pl.BlockSpec((1, A_SPAN, 128), lambda b, t: (b, t, 0))
    return pl.pallas_call(
        body,
        grid=(n, s_len // A_SPAN),
        in_specs=in_specs,
        out_specs=[out_spec] * 4,
        out_shape=[jax.ShapeDtypeStruct((n, s_len, 128), F32)] * 4,
        compiler_params=_params(2),
        name=f"a_prompt_g{g}",
    )(*([proj] * 10))


def _cross_rows(qx, kx, vx):
    tm = qx.shape[0]
    lane = _iota((tm, 128), 1)
    outs = []
    for pair in range(2):
        sl = slice(pair * 128, (pair + 1) * 128)
        qp, kp, vp = qx[:, sl], kx[:, sl], vx[:, sl]
        o_pair = None
        for hh in range(2):
            hm = (lane < 64) if hh == 0 else (lane >= 64)
            qm = jnp.where(hm, qp, 0.0).astype(BF16)
            s = lax.dot_general(qm, kp, NT, preferred_element_type=F32) * SCALE
            m = jnp.max(s, axis=-1, keepdims=True)
            e = jnp.exp(s - m)
            p = (e / jnp.sum(e, axis=-1, keepdims=True)).astype(BF16)
            oh = jnp.dot(p, vp, preferred_element_type=F32)
            o_pair = oh if hh == 0 else jnp.where(lane < 64, o_pair, oh)
        outs.append(o_pair)
    return jnp.concatenate(outs, axis=1)


def _out_norm_residual(x, z, w, g):
    y = jnp.dot(z.astype(BF16), w, preferred_element_type=F32)
    y = y * lax.rsqrt(jnp.mean(y * y, axis=-1, keepdims=True) + RMS_EPS)
    return x + y * g


def _finish_a_body(x_ref, *refs):
    gm_ref, qx_ref, gx_ref, mkv_ref, w_ref, g_ref, out_ref = refs[12:]
    mixes = []
    for pair in range(2):
        os_ = [refs[4 * g + 2 * pair][0] for g in range(3)]
        ls_ = [refs[4 * g + 2 * pair + 1][0] for g in range(3)]
        m = jnp.maximum(jnp.maximum(ls_[0], ls_[1]), ls_[2])
        es = [jnp.exp(l - m) for l in ls_]
        mixes.append((es[0] * os_[0] + es[1] * os_[1] + es[2] * os_[2]) / (es[0] + es[1] + es[2]))
    mix = jnp.concatenate(mixes, axis=1)
    mkv = mkv_ref[0]
    cx = _cross_rows(qx_ref[0], mkv[:, :X_WIDTH].astype(BF16), mkv[:, X_WIDTH:].astype(BF16))
    z = jnp.concatenate([mix * _silu(gm_ref[0]), cx * _silu(gx_ref[0])], axis=1)
    out_ref[0] = _out_norm_residual(x_ref[0], z, w_ref[...], g_ref[...])


def _finish_b_body(x_ref, mix_ref, gm_ref, qx_ref, gx_ref, mkv_ref, w_ref, g_ref, out_ref):
    mkv = mkv_ref[0]
    cx = _cross_rows(qx_ref[0], mkv[:, :X_WIDTH].astype(BF16), mkv[:, X_WIDTH:].astype(BF16))
    z = jnp.concatenate([mix_ref[0] * _silu(gm_ref[0]), cx * _silu(gx_ref[0])], axis=1)
    out_ref[0] = _out_norm_residual(x_ref[0], z, w_ref[...], g_ref[...])


def _finish_a(x, ols, proj, mkv, w_out, g_post, tm):
    n, s_len, d = x.shape
    row = lambda w, c: pl.BlockSpec((1, tm, w), lambda b, t: (b, t, c))
    in_specs = ([row(d, 0)] + [row(128, 0)] * 12 + [row(256, 9), row(256, 10), row(256, 11)]
                + [pl.BlockSpec((1, N_MEM, 2 * X_WIDTH), lambda b, t: (b, 0, 0)),
                   pl.BlockSpec(w_out.shape, lambda b, t: (0, 0)),
                   pl.BlockSpec((1, d), lambda b, t: (0, 0))])
    return pl.pallas_call(
        _finish_a_body, grid=(n, s_len // tm), in_specs=in_specs, out_specs=row(d, 0),
        out_shape=jax.ShapeDtypeStruct((n, s_len, d), F32), compiler_params=_params(2), name="finish_a",
    )(x, *ols, proj, proj, proj, mkv, w_out, g_post.reshape(1, d))


def _finish_b(x, mix, proj, mkv, w_out, g_post, tm):
    n, s_len, d = x.shape
    row = lambda w, c: pl.BlockSpec((1, tm, w), lambda b, t: (b, t, c))
    in_specs = [row(d, 0), row(B_WIDTH, 0), row(B_WIDTH, 2), row(256, 9), row(256, 10),
                pl.BlockSpec((1, N_MEM, 2 * X_WIDTH), lambda b, t: (b, 0, 0)),
                pl.BlockSpec(w_out.shape, lambda b, t: (0, 0)),
                pl.BlockSpec((1, d), lambda b, t: (0, 0))]
    return pl.pallas_call(
        _finish_b_body, grid=(n, s_len // tm), in_specs=in_specs, out_specs=row(d, 0),
        out_shape=jax.ShapeDtypeStruct((n, s_len, d), F32), compiler_params=_params(2), name="finish_b",
    )(x, mix, proj, proj, proj, mkv, w_out, g_post.reshape(1, d))


def _tail_body(x_ref, z_ref, w_ref, g_ref, out_ref):
    out_ref[...] = _out_norm_residual(x_ref[...], z_ref[...], w_ref[...], g_ref[...])


def _tail(x, z, w_out, g_post):
    m, d = x.shape
    full = lambda a: pl.BlockSpec(a.shape, lambda i: (0,) * a.ndim)
    g2 = g_post.reshape(1, d)
    return pl.pallas_call(
        _tail_body, grid=(1,), in_specs=[full(x), full(z), full(w_out), full(g2)], out_specs=full(x),
        out_shape=jax.ShapeDtypeStruct((m, d), F32), compiler_params=_params(1), name="sample_tail",
    )(x, z, w_out, g2)


def _compress_rows(load_rows, pos_ref, w1_ref, w2_ref, n_cmp):
    outs = []
    for t in range(2):
        a = jnp.zeros((n_cmp, 2 * CMP_HIDDEN), F32)
        b = jnp.zeros((n_cmp, 2 * CMP_HIDDEN), F32)
        for l in range(CMP_STRIDE):
            y = load_rows(t, l)
            ya = (y + pos_ref[t, l:l + 1, :]).astype(BF16)
            yb = (y + pos_ref[t, l + CMP_STRIDE:l + CMP_STRIDE + 1, :]).astype(BF16)
            a = a + jnp.dot(ya, w1_ref[t, l], preferred_element_type=F32)
            b = b + jnp.dot(yb, w1_ref[t, l + CMP_STRIDE], preferred_element_type=F32)
        h = a + pltpu.roll(b, n_cmp - 1, axis=0)
        outs.append(jnp.dot(_silu(h).astype(BF16), w2_ref[t], preferred_element_type=F32))
    return outs


def _compress_body(k_ref, v_ref, pos_ref, w1_ref, w2_ref, o_ref, *, n_cmp):
    refs = (k_ref, v_ref)
    load = lambda t, l: refs[t][0, pl.ds(l, n_cmp, stride=CMP_STRIDE), :]
    ck, cv = _compress_rows(load, pos_ref, w1_ref, w2_ref, n_cmp)
    o_ref[0, :, 0:128] = ck.astype(BF16)
    o_ref[0, :, 128:256] = cv.astype(BF16)


def _compress_prompt(proj, posw, w1bd, w2bd):
    n, s_len, _ = proj.shape
    n_cmp = s_len // CMP_STRIDE
    full = lambda a: pl.BlockSpec(a.shape, lambda b: (0,) * a.ndim)
    return pl.pallas_call(
        functools.partial(_compress_body, n_cmp=n_cmp), grid=(n,),
        in_specs=[pl.BlockSpec((1, s_len, 128), lambda b: (b, 0, 6)), pl.BlockSpec((1, s_len, 128), lambda b: (b, 0, 7)),
                  full(posw), full(w1bd), full(w2bd)],
        out_specs=pl.BlockSpec((1, n_cmp, 256), lambda b: (b, 0, 0)),
        out_shape=jax.ShapeDtypeStruct((n, n_cmp, 256), BF16), compiler_params=_params(1), name="compress_prompt",
    )(proj, proj, posw, w1bd, w2bd)


def _place_heads(tiles, lane):
    chunks = []
    for c in range(B_HEADS // 2):
        t0, t1 = tiles[2 * c], tiles[2 * c + 1]
        if (2 * c) // B_GROUP == 1:
            t0 = pltpu.roll(t0, 64, axis=1)
        if (2 * c + 1) // B_GROUP == 0:
            t1 = pltpu.roll(t1, 64, axis=1)
        chunks.append(jnp.where(lane < 64, t0, t1))
    return jnp.concatenate(chunks, axis=1)


def _masked_softmax_rows(s, ok):
    s = jnp.where(ok, s, NEG)
    m = jnp.max(s, axis=-1, keepdims=True)
    e = jnp.where(ok, jnp.exp(s - m), 0.0)
    den = jnp.maximum(jnp.sum(e, axis=-1, keepdims=True), 1e-30)
    return e * (1.0 / den)


def _ratio_of_halves(a):
    return a * (1.0 / pltpu.roll(a, 64, axis=1))


def _nsa_prompt_body(q_ref, gt_ref, kc_ref, vct_ref, kw_ref, vwt_ref, ks_ref, vst_ref, et_ref, mt_ref, eg_ref,
                     sl_ref, out_ref, q6_sc, m_sc, acc_sc, sel_sc, words_sm, idx_sm, *, s_len):
    qb = pl.program_id(1)
    qstart = qb * Q_BLOCK
    n_cmp = s_len // CMP_STRIDE
    q = q_ref[0] * SCALE
    lane = _iota((Q_BLOCK, 128), 1)
    tq_row = qstart + _iota((1, Q_BLOCK), 1)
    oc_t, os_t, ow_t = [None] * B_HEADS, [None] * B_HEADS, [None] * B_HEADS

    for kv in range(B_KV):
        row_kv = (_iota((128, 128), 0) < 64) if kv == 0 else (_iota((128, 128), 0) >= 64)
        for g in range(B_GROUP):
            h = kv * B_GROUP + g
            ch = q[:, (h // 2) * 128:(h // 2 + 1) * 128]
            if h % 2 == 1:
                ch = pltpu.roll(ch, 64, axis=1)
            q6_sc[g * 128:(g + 1) * 128, :] = jnp.where(lane < 64, ch, sl_ref[h:h + 1, :]).astype(BF16)
        q6 = q6_sc[...]

        s_t = lax.dot_general(kc_ref[0, kv], q6, NT, preferred_element_type=F32)
        cok = (CMP_STRIDE * _iota((n_cmp, Q_BLOCK), 0) + (CMP_LEN - 1)) <= tq_row
        q_ok = tq_row >= (CMP_LEN - 1)
        psum = jnp.zeros((n_cmp, Q_BLOCK), F32)
        ps = []
        for g in range(B_GROUP):
            s = jnp.where(cok, s_t[:, g * 128:(g + 1) * 128], NEG)
            e = jnp.exp(s - jnp.max(s, axis=0, keepdims=True))
            p = e * jnp.where(q_ok, 1.0 / jnp.sum(e, axis=0, keepdims=True), 0.0)
            psum = psum + p
            ps.append(p.astype(BF16))
        oc = jnp.dot(vct_ref[0], jnp.concatenate(ps, axis=1), preferred_element_type=F32)
        for g in range(B_GROUP):
            oc_t[kv * B_GROUP + g] = oc[:, g * 128:(g + 1) * 128].T

        kparts, vparts, pparts = [], [], []
        n_wb = WIN_B // Q_BLOCK + 1
        for wb in range(n_wb):
            b_raw = qb - (n_wb - 1) + wb
            b = jnp.maximum(b_raw, 0)
            r0 = pl.multiple_of(b * Q_BLOCK, Q_BLOCK)
            kparts.append(kw_ref[0, kv, pl.ds(r0, Q_BLOCK), :])
            vparts.append(jnp.where(row_kv, vwt_ref[0, b], 1.0).astype(BF16))
            pparts.append(jnp.where(b_raw >= 0, r0, s_len) + _iota((128, Q_BLOCK), 0))
        s_t = lax.dot_general(jnp.concatenate(kparts, axis=0), q6, NT, preferred_element_type=F32)
        dw = tq_row - jnp.concatenate(pparts, axis=0)
        wok = (dw >= 0) & (dw <= WIN_B)
        ps = []
        for g in range(B_GROUP):
            s = jnp.where(wok, s_t[:, g * 128:(g + 1) * 128], NEG)
            ps.append(jnp.exp(s - jnp.max(s, axis=0, keepdims=True)).astype(BF16))
        ow = jnp.dot(jnp.concatenate(vparts, axis=1), jnp.concatenate(ps, axis=1), preferred_element_type=F32)
        for g in range(B_GROUP):
            ow_t[kv * B_GROUP + g] = _ratio_of_halves(ow[:, g * 128:(g + 1) * 128].T)

        mt = mt_ref[...]
        imp = sum(jnp.dot(mt, t, preferred_element_type=F32) for t in _split3(psum))
        blk = _iota((128, Q_BLOCK), 0)
        ql = _iota((128, Q_BLOCK), 1)
        cur = jnp.where(ql >= SEL_BLOCK, qb * 2 + 1, qb * 2)
        forced = (blk == 0) | (blk == cur) | (blk == cur - 1)
        imp = jnp.where(blk > cur, -jnp.inf, jnp.where(forced, FORCE_SCORE, imp))
        blkf = blk.astype(F32)
        sel = jnp.zeros((128, Q_BLOCK), F32)
        for _ in range(SEL_TOPK):
            mx = jnp.max(imp, axis=0, keepdims=True)
            idx = jnp.min(jnp.where(imp == mx, blkf, 1e9), axis=0, keepdims=True)
            hit = blkf == idx
            sel = jnp.where(hit, 1.0, sel)
            imp = jnp.where(hit, -jnp.inf, imp)
        sel_sc[...] = sel.astype(BF16)

        blk_col = _iota((128, 1), 0)
        weight = lax.shift_left(jnp.ones((128, 1), jnp.int32), blk_col & 15).astype(F32)
        contrib = jnp.max(sel, axis=1, keepdims=True) * weight
        for w in range(8):
            words_sm[w] = jnp.sum(contrib[16 * w:16 * (w + 1), :]).astype(jnp.int32)

        m_sc[...] = jnp.full(m_sc.shape, NEG, F32)
        acc_sc[...] = jnp.zeros(acc_sc.shape, F32)

        def scan(gi, cnt):
            bits = (words_sm[gi >> 3] >> ((gi & 7) * 2)) & 3
            idx_sm[cnt] = gi
            return cnt + jnp.where(bits != 0, 1, 0)

        cnt = lax.fori_loop(0, qb + 1, scan, 0)
        for j in range(GROUPS_PER_CHUNK - 1):
            idx_sm[cnt + j] = -1

        def chunk(c, carry):
            kts, vts, hits, kposs = [], [], [], []
            for j in range(GROUPS_PER_CHUNK):
                gi_raw = idx_sm[c * GROUPS_PER_CHUNK + j]
                gi = jnp.maximum(gi_raw, 0)
                k0 = pl.multiple_of(gi * Q_BLOCK, Q_BLOCK)
                kts.append(ks_ref[0, kv, pl.ds(k0, Q_BLOCK), :])
                vts.append(jnp.where(row_kv, vst_ref[0, gi], 1.0).astype(BF16))
                hits.append(jnp.dot(et_ref[gi], sel_sc[...], preferred_element_type=F32))
                kposs.append(jnp.where(gi_raw >= 0, k0, s_len) + _iota((128, 128), 0))
            s_t = lax.dot_general(jnp.concatenate(kts, axis=0), q6_sc[...], NT,
                                  preferred_element_type=F32)
            ok = ((jnp.concatenate(hits, axis=0) > 0.5)
                  & (jnp.concatenate(kposs, axis=0) <= qstart + _iota((GROUPS_PER_CHUNK * 128, 128), 1)))
            m_old = m_sc[...]
            m_new, ps = [], []
            for g in range(B_GROUP):
                cols = slice(g * 128, (g + 1) * 128)
                s = jnp.where(ok, s_t[:, cols], NEG)
                mg = jnp.maximum(m_old[:, cols], jnp.max(s, axis=0, keepdims=True))
                ps.append(jnp.exp(s - mg).astype(BF16))
                m_new.append(mg)
            m_new = jnp.concatenate(m_new, axis=1)
            pv = jnp.dot(jnp.concatenate(vts, axis=1), jnp.concatenate(ps, axis=1), preferred_element_type=F32)
            acc_sc[...] = jnp.exp(m_old - m_new) * acc_sc[...] + pv
            m_sc[...] = m_new
            return carry

        lax.fori_loop(0, (cnt + GROUPS_PER_CHUNK - 1) // GROUPS_PER_CHUNK, chunk, 0)
        for g in range(B_GROUP):
            os_t[kv * B_GROUP + g] = _ratio_of_halves(acc_sc[:, g * 128:(g + 1) * 128].T)

    sg = _sigmoid(gt_ref[0])
    eg = eg_ref[...]
    gexp = sum(jnp.dot(t, eg, preferred_element_type=F32) for t in _split3(sg))
    out_ref[0] = (gexp[:, 0:B_WIDTH] * _place_heads(oc_t, lane)
                  + gexp[:, B_WIDTH:2 * B_WIDTH] * _place_heads(os_t, lane)
                  + gexp[:, 2 * B_WIDTH:] * _place_heads(ow_t, lane))


def _nsa_prompt(proj, kc, vc_t, kw, vw_t, ks, vs_t, et3, mt, eg, slope_lanes):
    n, s_len, _ = proj.shape
    n_cmp = s_len // CMP_STRIDE
    n_grp = s_len // Q_BLOCK
    full = lambda a: pl.BlockSpec(a.shape, lambda b, t: (0,) * a.ndim)
    per_n = lambda a: pl.BlockSpec((1,) + a.shape[1:], lambda b, t: (b,) + (0,) * (a.ndim - 1))
    return pl.pallas_call(
        functools.partial(_nsa_prompt_body, s_len=s_len),
        grid=(n, n_grp),
        in_specs=[pl.BlockSpec((1, Q_BLOCK, B_WIDTH), lambda b, t: (b, t, 0)),
                  pl.BlockSpec((1, Q_BLOCK, 128), lambda b, t: (b, t, 22)),
                  per_n(kc), per_n(vc_t), per_n(kw), per_n(vw_t), per_n(ks), per_n(vs_t),
                  full(et3), full(mt), full(eg), full(slope_lanes)],
        out_specs=pl.BlockSpec((1, Q_BLOCK, B_WIDTH), lambda b, t: (b, t, 0)),
        out_shape=jax.ShapeDtypeStruct((n, s_len, B_WIDTH), F32),
        scratch_shapes=[pltpu.VMEM((B_GROUP * Q_BLOCK, 128), BF16), pltpu.VMEM((1, B_GROUP * Q_BLOCK), F32),
                        pltpu.VMEM((128, B_GROUP * Q_BLOCK), F32), pltpu.VMEM((128, Q_BLOCK), BF16),
                        pltpu.SMEM((8,), jnp.int32), pltpu.SMEM((n_grp + GROUPS_PER_CHUNK,), jnp.int32)],
        compiler_params=_params(2), name="nsa_prompt",
    )(proj, proj, kc, vc_t, kw, vw_t, ks, vs_t, et3, mt, eg, slope_lanes)


POS_LANE = 64


def _slope_lanes():
    out = np.zeros((B_HEADS, 128), np.float32)
    for h, s in enumerate(SLOPES_B):
        hi = np.float32(np.asarray(s, np.float32).astype(jnp.bfloat16))
        mid = np.float32(np.asarray(np.float32(s) - hi, np.float32).astype(jnp.bfloat16))
        lo = np.float32(np.asarray(np.float32(s) - hi - mid, np.float32).astype(jnp.bfloat16))
        out[h, POS_LANE:POS_LANE + 6] = [hi, mid, lo, hi, mid, lo]
    return out


def _keys_with_pos(k2, pos):
    n, n_keys, _ = k2.shape
    lo = (pos % Q_BLOCK).astype(BF16)[None, :, None]
    hi = (pos - pos % Q_BLOCK).astype(BF16)[None, :, None]
    tail = jnp.concatenate([jnp.broadcast_to(lo, (n, n_keys, 3)), jnp.broadcast_to(hi, (n, n_keys, 3)),
                            jnp.zeros((n, n_keys, 128 - POS_LANE - 6), BF16)], axis=-1)
    return jnp.stack([jnp.concatenate([k2[..., kv * 64:(kv + 1) * 64], tail], axis=-1) for kv in range(B_KV)], axis=1)


def _values_by_group(v2):
    n, s_len, _ = v2.shape
    return v2.reshape(n, s_len // Q_BLOCK, Q_BLOCK, 128).transpose(0, 1, 3, 2)


def _heads_rows(vec, n_rows, width):
    r = _iota((n_rows, width), 0)
    l = _iota((n_rows, width), 1)
    hm = (l >= r * HEAD_DIM) & (l < r * HEAD_DIM + HEAD_DIM)
    return jnp.where(hm, jnp.broadcast_to(vec, (n_rows, width)), 0.0), hm


def _bf(x):
    return x.astype(BF16).astype(F32)


def _row_consts(n_rows, vals):
    r = _iota((n_rows, 1), 0)
    out = jnp.zeros((n_rows, 1), F32)
    for i, v in enumerate(vals):
        out = jnp.where(r == i, v, out)
    return out


def _rows_last(cache):
    nd = cache.ndim
    return cache.transpose(tuple(range(nd - 4)) + (nd - 3, nd - 2, nd - 1, nd - 4))


def _kv_t(ref, t):
    x = ref[0, 0, t]
    return x.reshape(x.shape[0] * x.shape[1], x.shape[2]).astype(BF16)


def _sample_cross(qx_row, k_t, v_t):
    q8, hm = _heads_rows(qx_row, 8, X_WIDTH)
    s = jnp.dot(q8.astype(BF16), k_t, preferred_element_type=F32) * SCALE
    e = jnp.exp(s - jnp.max(s, axis=-1, keepdims=True))
    p = (e * (1.0 / jnp.sum(e, axis=-1, keepdims=True))).astype(BF16)
    o8 = lax.dot_general(p, v_t, NT, preferred_element_type=F32)
    return jnp.sum(jnp.where(hm, o8, 0.0), axis=0, keepdims=True)


def _sample_a_body(row_ref, c0_ref, c1_ref, c2_ref, mkv_ref, z_ref):
    row = row_ref[0]
    outs, lses = [], []
    hm = None
    for g, (win, dil) in enumerate(A_PATTERNS):
        cref = (c0_ref, c1_ref, c2_ref)[g]
        q8, hm = _heads_rows(row[:, g * 256:(g + 1) * 256], 8, A_WIDTH)
        knew = row[:, 768 + g * 256:768 + (g + 1) * 256]
        vnew = row[:, 1536 + g * 256:1536 + (g + 1) * 256]
        q8b = q8.astype(BF16)
        slope = _row_consts(8, SLOPES_A[g * 4:(g + 1) * 4])
        s = jnp.dot(q8b, _kv_t(cref, 0), preferred_element_type=F32) * SCALE
        r = _iota((8, win), 1)
        s = jnp.where((r & (dil - 1)) == 0, s - slope * (win - r).astype(F32), NEG)
        s_new = jnp.sum(q8b.astype(F32) * _bf(knew), axis=-1, keepdims=True) * SCALE
        m = jnp.maximum(jnp.max(s, axis=-1, keepdims=True), s_new)
        e = jnp.exp(s - m)
        e_new = jnp.exp(s_new - m)
        den = jnp.sum(e, axis=-1, keepdims=True) + e_new
        inv = 1.0 / den
        o8 = (lax.dot_general((e * inv).astype(BF16), _kv_t(cref, 1), NT, preferred_element_type=F32)
              + _bf(e_new * inv) * _bf(vnew))
        outs.append(o8)
        lses.append(m + jnp.log(den))
    mx = jnp.maximum(jnp.maximum(lses[0], lses[1]), lses[2])
    ws = [jnp.exp(l - mx) for l in lses]
    mix8 = (ws[0] * outs[0] + ws[1] * outs[1] + ws[2] * outs[2]) / (ws[0] + ws[1] + ws[2])
    mix = jnp.sum(jnp.where(hm, mix8, 0.0), axis=0, keepdims=True)
    cx = _sample_cross(row[:, 2560:2816], _kv_t(mkv_ref, 0), _kv_t(mkv_ref, 1))
    z_ref[0] = jnp.concatenate([mix * _silu(row[:, 2304:2560]), cx * _silu(row[:, 2816:3072])], axis=1)


def _layer_block(cache_t, layer):
    return pl.BlockSpec((1, 1) + cache_t.shape[2:], lambda b, *_: (layer, b, 0, 0, 0, 0))


def _sample_a(proj_s, caches_t, mem_t, li, i):
    ns = proj_s.shape[0]
    row3 = proj_s.reshape(ns, 1, W_IN_A)
    return pl.pallas_call(
        _sample_a_body, grid=(ns,),
        in_specs=[pl.BlockSpec((1, 1, W_IN_A), lambda b: (b, 0, 0))] + [_layer_block(c, li) for c in caches_t]
                 + [_layer_block(mem_t, i)],
        out_specs=pl.BlockSpec((1, 1, A_WIDTH + X_WIDTH), lambda b: (b, 0, 0)),
        out_shape=jax.ShapeDtypeStruct((ns, 1, A_WIDTH + X_WIDTH), F32), compiler_params=_params(1), name="sample_a",
    )(row3, *caches_t, mem_t).reshape(ns, A_WIDTH + X_WIDTH)


def _q16(row):
    r = _iota((16, 128), 0)
    l = _iota((16, 128), 1)
    acc = jnp.zeros((16, 128), F32)
    for c in range(B_HEADS // 2):
        ch = jnp.broadcast_to(row[:, c * 128:(c + 1) * 128], (16, 128))
        rolled = pltpu.roll(ch, 64, axis=1)
        for hh in range(2):
            h = 2 * c + hh
            kv = h // B_GROUP
            lm = (l < 64) if kv == 0 else (l >= 64)
            acc = jnp.where((r == h) & lm, ch if hh == kv else rolled, acc)
    return acc * SCALE


def _sample_b1_body(pt_ref, row_ref, pos_ref, w1_ref, w2_ref, mm_ref, pages_ref, oc_ref, sel_ref,
                    buf, rows_sc, sem, imp_sc, *, li, n_pages, ns):
    n = pl.program_id(0)
    past = n_pages * PAGE_SIZE
    n_cmp = past // CMP_STRIDE

    def page_copy(page, p, slot):
        return pltpu.make_async_copy(pages_ref.at[page, li, pl.ds(0, 2)], buf.at[slot, p], sem.at[slot])

    def fetch(nn, slot):
        def body(p, c):
            page_copy(pt_ref[nn * n_pages + p], p, slot).start()
            return c
        lax.fori_loop(0, n_pages, body, 0)

    @pl.when(n == 0)
    def _():
        fetch(0, 0)

    @pl.when(n + 1 < ns)
    def _():
        fetch(n + 1, (n + 1) % 2)

    slot = n % 2

    def wbody(p, c):
        page_copy(0, p, slot).wait()
        return c
    lax.fori_loop(0, n_pages, wbody, 0)

    def to_rows(p, c):
        r0 = pl.multiple_of(p * PAGE_SIZE, PAGE_SIZE)
        for t in range(2):
            rows_sc[t, pl.ds(r0, PAGE_SIZE), :] = buf[slot, p, t].reshape(2 * HEAD_DIM, PAGE_SIZE).T
        return c
    lax.fori_loop(0, n_pages, to_rows, 0)

    load = lambda t, l: rows_sc[t, pl.ds(l, n_cmp, stride=CMP_STRIDE), :]
    ck, cv = _compress_rows(load, pos_ref, w1_ref, w2_ref, n_cmp)

    q16 = _q16(row_ref[0]).astype(BF16)
    slope = _row_consts(16, SLOPES_B)
    s = lax.dot_general(q16, ck.astype(BF16), NT, preferred_element_type=F32)
    cend = CMP_STRIDE * _iota((1, n_cmp), 1) + (CMP_LEN - 1)
    p = _masked_softmax_rows(s - slope * (past - cend).astype(F32), cend <= past)
    oc_ref[0] = jnp.dot(p.astype(BF16), cv.astype(BF16), preferred_element_type=F32)

    r16 = _iota((16, n_cmp), 0)
    ps0 = jnp.sum(jnp.where(r16 < B_GROUP, p, 0.0), axis=0, keepdims=True)
    ps1 = jnp.sum(jnp.where((r16 >= B_GROUP) & (r16 < B_HEADS), p, 0.0), axis=0, keepdims=True)
    psum = jnp.concatenate([ps0, ps1, jnp.zeros((6, n_cmp), F32)], axis=0)
    mm = mm_ref[...]
    imp = sum(jnp.dot(t, mm, preferred_element_type=F32) for t in _split3(psum))
    blk = _iota((8, 256), 1)
    cur = past // SEL_BLOCK
    forced = (blk == 0) | (blk == cur) | (blk == cur - 1)
    imp_sc[n] = jnp.where(blk > cur, -jnp.inf, jnp.where(forced, FORCE_SCORE, imp))

    @pl.when(n == ns - 1)
    def _():
        impa = imp_sc[...]
        blkf = _iota(impa.shape, 2).astype(F32)
        lane = _iota((ns, 8, 128), 2)
        out = jnp.zeros((ns, 8, 128), F32)
        for r in range(SEL_TOPK):
            mx = jnp.max(impa, axis=-1, keepdims=True)
            idx = jnp.min(jnp.where(impa == mx, blkf, 1e9), axis=-1, keepdims=True)
            impa = jnp.where(blkf == idx, -jnp.inf, impa)
            out = jnp.where(lane == r, idx, out)
        sel_ref[...] = out.astype(jnp.int32)


def _sample_b1(page_table, proj_s, posw, w1bd, w2bd, mm, pages_t, li):
    ns, n_pages = page_table.shape
    past = n_pages * PAGE_SIZE
    row3 = proj_s.reshape(ns, 1, W_IN_B_PAD)
    full = lambda a: pl.BlockSpec(a.shape, lambda b, pt: (0,) * a.ndim)
    grid_spec = pltpu.PrefetchScalarGridSpec(
        num_scalar_prefetch=1, grid=(ns,),
        in_specs=[pl.BlockSpec((1, 1, W_IN_B_PAD), lambda b, pt: (b, 0, 0)), full(posw), full(w1bd), full(w2bd),
                  full(mm), pl.BlockSpec(memory_space=pl.ANY)],
        out_specs=[pl.BlockSpec((1, 16, 128), lambda b, pt: (b, 0, 0)),
                   pl.BlockSpec((ns, 8, 128), lambda b, pt: (0, 0, 0))],
        scratch_shapes=[pltpu.VMEM((2, n_pages, 2, B_KV, HEAD_DIM, PAGE_SIZE), F32),
                        pltpu.VMEM((2, past, 128), F32), pltpu.SemaphoreType.DMA((2,)),
                        pltpu.VMEM((ns, 8, 256), F32)])
    return pl.pallas_call(
        functools.partial(_sample_b1_body, li=li, n_pages=n_pages, ns=ns),
        grid_spec=grid_spec,
        out_shape=[jax.ShapeDtypeStruct((ns, 16, 128), F32), jax.ShapeDtypeStruct((ns, 8, 128), jnp.int32)],
        compiler_params=_params(1), name="sample_b1",
    )(page_table.reshape(-1), row3, posw, w1bd, w2bd, mm, pages_t)


def _sample_b2_body(pt_ref, sf_ref, row_ref, oc_ref, sel_ref, win_ref, mkv_ref, e16_ref, pages_ref, z_ref,
                    buf, sem, *, li, n_pages, ns):
    n = pl.program_id(0)
    past = n_pages * PAGE_SIZE
    n_blk = past // SEL_BLOCK
    per_page = PAGE_SIZE // SEL_BLOCK
    n_sel = B_KV * SEL_TOPK

    def blk_copies(page, kv, r, slot):
        return [pltpu.make_async_copy(pages_ref.at[page, li, 2 + t, kv],
                                      buf.at[slot, t, kv, :, pl.ds(r * PAGE_SIZE, PAGE_SIZE)], sem.at[slot])
                for t in range(2)]

    def fetch(nn, slot):
        for kv in range(B_KV):
            for r in range(SEL_TOPK):
                j = jnp.minimum(sf_ref[nn * n_sel + kv * SEL_TOPK + r], n_blk - 1)
                for cp in blk_copies(pt_ref[nn * n_pages + j // per_page], kv, r, slot):
                    cp.start()

    @pl.when(n == 0)
    def _():
        fetch(0, 0)

    @pl.when(n + 1 < ns)
    def _():
        fetch(n + 1, (n + 1) % 2)

    slot = n % 2
    for kv in range(B_KV):
        for r in range(SEL_TOPK):
            for cp in blk_copies(0, kv, r, slot):
                cp.wait()

    row = row_ref[0]
    q16f = _q16(row)
    q16 = q16f.astype(BF16)
    q16r = q16.astype(F32)
    slope = _row_consts(16, SLOPES_B)
    r16 = _iota((16, 128), 0)

    def new_key(col):
        kn = _bf(row[:, col:col + 128])
        return jnp.sum(q16r * kn, axis=-1, keepdims=True)

    def attend(s, s_new, v_t):
        m = jnp.maximum(jnp.max(s, axis=-1, keepdims=True), s_new)
        e = jnp.exp(s - m)
        e_new = jnp.exp(s_new - m)
        inv = 1.0 / (jnp.sum(e, axis=-1, keepdims=True) + e_new)
        return lax.dot_general((e * inv).astype(BF16), v_t, NT, preferred_element_type=F32), _bf(e_new * inv)

    n_keys = SEL_TOPK * PAGE_SIZE
    jv = jnp.dot(sel_ref[0].astype(F32).astype(BF16), e16_ref[...], preferred_element_type=F32).astype(jnp.int32)
    in_page = jnp.bitwise_and(_iota((8, n_keys), 1), PAGE_SIZE - 1)
    ok_sel = ((in_page >> 6) == (jv & (per_page - 1))) & (jv < n_blk)
    dist_sel = (past - ((jv >> 1) * PAGE_SIZE + in_page)).astype(F32)
    q64 = jnp.where(r16[:, :HEAD_DIM] < B_GROUP, q16f[:, :HEAD_DIM], q16f[:, HEAD_DIM:]).astype(BF16)
    s_new = new_key(1024)
    outs, p_news = [], []
    for kv in range(B_KV):
        s = jnp.dot(q64, buf[slot, 0, kv].astype(BF16), preferred_element_type=F32) - slope * dist_sel[kv:kv + 1]
        o, p_new = attend(jnp.where(ok_sel[kv:kv + 1], s, NEG), s_new, buf[slot, 1, kv].astype(BF16))
        outs.append(o)
        p_news.append(p_new)
    p_new = jnp.where(_iota((16, 1), 0) < B_GROUP, p_news[0], p_news[1])
    os16 = jnp.concatenate(outs, axis=1) + p_new * _bf(row[:, 1152:1280])

    lb = win_ref.shape[-1]
    dw = (lb - _iota((1, lb), 1)).astype(F32)
    s = jnp.dot(q16, _kv_t(win_ref, 0), preferred_element_type=F32) - slope * dw
    ow16, p_new = attend(s, new_key(1280), _kv_t(win_ref, 1))
    ow16 = ow16 + p_new * _bf(row[:, 1408:1536])

    sg = jnp.broadcast_to(_sigmoid(row[:, 2816:2944]), (16, 128))
    l16 = _iota((16, 128), 1)
    gate = lambda b: jnp.sum(jnp.where(l16 == r16 * 3 + b, sg, 0.0), axis=-1, keepdims=True)
    out16 = gate(0) * oc_ref[0] + gate(1) * os16 + gate(2) * ow16
    lane1 = _iota((1, 128), 1)
    mix = _place_heads([out16[h:h + 1, :] for h in range(B_HEADS)], lane1)

    cx = _sample_cross(row[:, 2304:2560], _kv_t(mkv_ref, 0), _kv_t(mkv_ref, 1))
    z_ref[0] = jnp.concatenate([mix * _silu(row[:, 1536:2304]), cx * _silu(row[:, 2560:2816])], axis=1)


def _sample_b2(page_table, sel, proj_s, oc, win_t, mem_t, e16, pages_t, li, i):
    ns, n_pages = page_table.shape
    row3 = proj_s.reshape(ns, 1, W_IN_B_PAD)
    full = lambda a: pl.BlockSpec(a.shape, lambda b, pt, sf: (0,) * a.ndim)
    per = lambda a: pl.BlockSpec((1,) + a.shape[1:], lambda b, pt, sf: (b,) + (0,) * (a.ndim - 1))
    grid_spec = pltpu.PrefetchScalarGridSpec(
        num_scalar_prefetch=2, grid=(ns,),
        in_specs=[per(row3), per(oc), per(sel), _layer_block(win_t, li), _layer_block(mem_t, i), full(e16),
                  pl.BlockSpec(memory_space=pl.ANY)],
        out_specs=pl.BlockSpec((1, 1, B_WIDTH + X_WIDTH), lambda b, pt, sf: (b, 0, 0)),
        scratch_shapes=[pltpu.VMEM((2, 2, B_KV, HEAD_DIM, SEL_TOPK * PAGE_SIZE), F32), pltpu.SemaphoreType.DMA((2,))])
    return pl.pallas_call(
        functools.partial(_sample_b2_body, li=li, n_pages=n_pages, ns=ns),
        grid_spec=grid_spec,
        out_shape=jax.ShapeDtypeStruct((ns, 1, B_WIDTH + X_WIDTH), F32),
        compiler_params=_params(1), name="sample_b2",
    )(page_table.reshape(-1), sel[:, :B_KV, :SEL_TOPK].reshape(-1), row3, oc, sel, win_t, mem_t, e16, pages_t
      ).reshape(ns, B_WIDTH + X_WIDTH)


def _importance_matrix(n_cmp, n_cols):
    c = np.arange(n_cmp)[:, None]
    j = np.arange(n_cols)[None, :]
    per = SEL_BLOCK // CMP_STRIDE
    m = ((c >= per * j) & (c <= per * j + per - 1)).astype(np.float32)
    m = m + ((c + 1 >= per * j) & (c + 1 <= per * j + per - 1)).astype(np.float32)
    m[n_cmp - 1, :] = 0.0
    return m


def _gate_expand():
    eg = np.zeros((128, 3 * B_WIDTH), np.float32)
    for h in range(B_HEADS):
        for b in range(3):
            eg[h * 3 + b, b * B_WIDTH + h * HEAD_DIM:b * B_WIDTH + (h + 1) * HEAD_DIM] = 1.0
    return eg


def _block_expand(n_rows, n_keys):
    return (np.arange(n_keys)[None, :] // SEL_BLOCK == np.arange(n_rows)[:, None]).astype(np.float32)


def _compress_weights(cmp_pos, cmp_w1, cmp_w2):
    eye = jnp.eye(B_KV, dtype=F32)
    posw = jnp.concatenate([cmp_pos, cmp_pos], axis=-1)
    w1 = cmp_w1.reshape(2, CMP_LEN, HEAD_DIM, CMP_HIDDEN)
    w1bd = jnp.einsum('tlek,jm->tljemk', w1, eye).reshape(2, CMP_LEN, 2 * HEAD_DIM, 2 * CMP_HIDDEN)
    w2bd = jnp.einsum('tke,jm->tjkme', cmp_w2, eye).reshape(2, 2 * CMP_HIDDEN, 2 * HEAD_DIM)
    return posw, w1bd.astype(BF16), w2bd.astype(BF16)


def _permute_w_in_b(w):
    d = w.shape[0]
    return jnp.concatenate([w[:, :1536], w[:, 1572:W_IN_B], w[:, 1536:1572],
                            jnp.zeros((d, W_IN_B_PAD - W_IN_B), w.dtype)], axis=1)


def kernel(x_prompt, x_sample, cache_mem_kv, cache_a_w128_kv, cache_a_w512_kv, cache_a_w2048_kv, cache_b_pages,
           cache_b_win_kv, page_table, mem_prompt, norm_pre, norm_post, norm_mem, w_mem_kv, w_in_a, w_out_a,
           w_in_b, w_out_b, cmp_pos, cmp_w1, cmp_w2):
    n, s_len, d = x_prompt.shape
    ns = x_sample.shape[0]
    depth = norm_pre.shape[0]
    n_pool, page_size, n_lb = cache_b_pages.shape[:3]
    n_pages = page_table.shape[1]
    past = n_pages * page_size
    assert d == D_MODEL and x_sample.shape[1] == 1 and page_size == PAGE_SIZE
    assert s_len % 2048 == 0 and past % 2048 == 0 and s_len >= 2048 and past >= 2048
    caches_a = (cache_a_w128_kv, cache_a_w512_kv, cache_a_w2048_kv)
    for c, (win, _) in zip(caches_a, A_PATTERNS):
        assert c.shape[2] == win
    assert cache_b_win_kv.shape[2] == WIN_B

    tm = 512
    n_cmp_p = s_len // CMP_STRIDE
    n_cmp_s = past // CMP_STRIDE
    et3 = jnp.asarray(_block_expand(128, s_len).reshape(128, s_len // Q_BLOCK, Q_BLOCK).transpose(1, 2, 0), BF16)
    slope_lanes = jnp.asarray(_slope_lanes(), F32)
    mt = jnp.asarray(_importance_matrix(n_cmp_p, 128).T, BF16)
    mm = jnp.asarray(_importance_matrix(n_cmp_s, 256), BF16)
    eg = jnp.asarray(_gate_expand(), BF16)
    e16 = jnp.asarray((np.arange(SEL_TOPK * PAGE_SIZE)[None, :] // PAGE_SIZE == np.arange(128)[:, None]), BF16)
    pages_t = cache_b_pages.transpose(0, 2, 3, 4, 5, 1)
    caches_a_t = [_rows_last(c) for c in caches_a]
    mem_t = _rows_last(cache_mem_kv)
    win_t = _rows_last(cache_b_win_kv)

    xp = x_prompt
    xs = x_sample.reshape(ns, d)
    mem2 = mem_prompt.reshape(n * N_MEM, d)
    mem_new = []
    a_p = [[] for _ in A_PATTERNS]
    a_s = [[] for _ in A_PATTERNS]
    b_p, b_s, bw_p, bw_s = [], [], [], []
    for i in range(depth):
        li = i // 2
        mkv_p = _rms_proj(mem2, norm_mem[i], w_mem_kv[i].astype(BF16), tm=N_MEM).reshape(n, N_MEM, 2 * X_WIDTH)
        mem_new.append(mkv_p.reshape(n, N_MEM, 2, 4, HEAD_DIM))
        if i % 2 == 0:
            w_in = w_in_a[li].astype(BF16)
            w_out = w_out_a[li].astype(BF16)
            proj_p = _rms_proj(xp.reshape(n * s_len, d), norm_pre[i], w_in, tm=tm).reshape(n, s_len, W_IN_A)
            proj_s = _rms_proj(xs, norm_pre[i], w_in, tm=ns)
            ols = []
            for g, (win, dil) in enumerate(A_PATTERNS):
                ols += _a_prompt_group(proj_p, g, dil)
                kv_p = proj_p[:, s_len - win:, 768:2304].reshape(n, win, 2, 3, 4, HEAD_DIM)[:, :, :, g]
                a_p[g].append(kv_p)
                a_s[g].append(proj_s[:, 768:2304].reshape(ns, 1, 2, 3, 4, HEAD_DIM)[:, :, :, g])
            xp = _finish_a(xp, ols, proj_p, mkv_p, w_out, norm_post[i], tm=256)
            z = _sample_a(proj_s, caches_a_t, mem_t, li, i)
            xs = _tail(xs, z, w_out, norm_post[i])
        else:
            w_in = _permute_w_in_b(w_in_b[li]).astype(BF16)
            w_out = w_out_b[li].astype(BF16)
            posw, w1bd, w2bd = _compress_weights(cmp_pos[li], cmp_w1[li], cmp_w2[li])
            proj_p = _rms_proj(xp.reshape(n * s_len, d), norm_pre[i], w_in, tm=tm).reshape(n, s_len, W_IN_B_PAD)
            proj_s = _rms_proj(xs, norm_pre[i], w_in, tm=ns)
            cmpd = _compress_prompt(proj_p, posw, w1bd, w2bd)
            pos = jnp.arange(s_len, dtype=jnp.int32)
            cend = CMP_STRIDE * jnp.arange(n_cmp_p, dtype=jnp.int32) + (CMP_LEN - 1)
            kvs = proj_p[:, :, 1024:1536].astype(BF16)
            mix = _nsa_prompt(proj_p, _keys_with_pos(cmpd[:, :, 0:128], cend), cmpd[:, :, 128:256].transpose(0, 2, 1),
                              _keys_with_pos(kvs[:, :, 256:384], pos), _values_by_group(kvs[:, :, 384:512]),
                              _keys_with_pos(kvs[:, :, 0:128], pos), _values_by_group(kvs[:, :, 128:256]),
                              et3, mt, eg, slope_lanes)
            xp = _finish_b(xp, mix, proj_p, mkv_p, w_out, norm_post[i], tm=256)
            oc, sel = _sample_b1(page_table, proj_s, posw, w1bd, w2bd, mm, pages_t, li)
            z = _sample_b2(page_table, sel, proj_s, oc, win_t, mem_t, e16, pages_t, li, i)
            xs = _tail(xs, z, w_out, norm_post[i])
            b_p.append(proj_p[:, :, 768:1280].reshape(n, s_len, 4, B_KV, HEAD_DIM))
            bw_p.append(proj_p[:, s_len - WIN_B:, 1280:1536].reshape(n, WIN_B, 2, B_KV, HEAD_DIM))
            b_s.append(proj_s[:, 768:1280].reshape(ns, 1, 4, B_KV, HEAD_DIM))
            bw_s.append(proj_s[:, 1280:1536].reshape(ns, 1, 2, B_KV, HEAD_DIM))
    return (xp, xs.reshape(ns, 1, d), jnp.stack(mem_new, axis=0),
            jnp.stack(a_p[0], axis=0), jnp.stack(a_p[1], axis=0), jnp.stack(a_p[2], axis=0),
            jnp.stack(b_p, axis=2), jnp.stack(bw_p, axis=0),
            jnp.stack(a_s[0], axis=0), jnp.stack(a_s[1], axis=0), jnp.stack(a_s[2], axis=0),
            jnp.stack(b_s, axis=2), jnp.stack(bw_s, axis=0))
```

```python
import functools

import numpy as np
import jax
import jax.numpy as jnp
from jax import lax
from jax.experimental import pallas as pl
from jax.experimental.pallas import tpu as pltpu

F32 = jnp.float32
BF16 = jnp.bfloat16

D_MODEL = 1024
HEAD_DIM = 64
SCALE = HEAD_DIM ** -0.5
LOG2E = 1.4426950408889634
RMS_EPS = 1e-6
N_MEM = 256
X_WIDTH = 256
A_PATTERNS = ((128, 1), (512, 4), (2048, 16))
A_WIDTH = 256
W_IN_A = 3072
B_HEADS = 12
B_KV = 2
B_GROUP = 6
B_WIDTH = 768
W_IN_B = 2852
W_IN_B_PAD = 2944
CMP_LEN = 32
CMP_STRIDE = 16
CMP_HIDDEN = 128
SEL_BLOCK = 64
SEL_TOPK = 16
WIN_B = 512
Q_BLOCK = 128
FORCE_SCORE = 1e4
PAGE_SIZE = 128
NEG = -1e30
GROUPS_PER_CHUNK = 4
VMEM_LIMIT = 56 * 1024 * 1024

NT = (((1,), (1,)), ((), ()))


def _alibi(n):
    k = np.arange(1, n + 1, dtype=np.float32)
    return [float(v) for v in np.float32(2.0) ** (np.float32(-8.0) * k / np.float32(n))]


SLOPES_A = _alibi(12)
SLOPES_B = _alibi(12)


def _params(n_axes):
    return pltpu.CompilerParams(dimension_semantics=("arbitrary",) * n_axes, vmem_limit_bytes=VMEM_LIMIT)


def _sigmoid(x):
    return 1.0 / (1.0 + jnp.exp(-x))


def _silu(x):
    return x * _sigmoid(x)


def _iota(shape, dim):
    return lax.broadcasted_iota(jnp.int32, shape, dim)


def _split3(x):
    hi = x.astype(BF16)
    r1 = x - hi.astype(F32)
    mid = r1.astype(BF16)
    lo = (r1 - mid.astype(F32)).astype(BF16)
    return hi, mid, lo


def _rms_proj_body(x_ref, g_ref, w_ref, o_ref, *side_ref, side):
    x = x_ref[...]
    y = x * lax.rsqrt(jnp.mean(x * x, axis=-1, keepdims=True) + RMS_EPS)
    y = (y * g_ref[...]).astype(BF16)
    o = jnp.dot(y, w_ref[...], preferred_element_type=F32)
    o_ref[...] = o
    if side is not None:
        side_ref[0][...] = o[:, side[0]:side[0] + side[1]].astype(BF16)


def _rms_proj(x, g, w, tm, side=None):
    m, d = x.shape
    n = w.shape[1]
    out_specs = [pl.BlockSpec((tm, n), lambda i: (i, 0))]
    out_shape = [jax.ShapeDtypeStruct((m, n), F32)]
    if side is not None:
        out_specs.append(pl.BlockSpec((tm, side[1]), lambda i: (i, 0)))
        out_shape.append(jax.ShapeDtypeStruct((m, side[1]), BF16))
    outs = pl.pallas_call(
        functools.partial(_rms_proj_body, side=side),
        grid=(m // tm,),
        in_specs=[pl.BlockSpec((tm, d), lambda i: (i, 0)),
                  pl.BlockSpec((1, d), lambda i: (0, 0)),
                  pl.BlockSpec((d, n), lambda i: (0, 0))],
        out_specs=out_specs,
        out_shape=out_shape,
        compiler_params=_params(1),
        name="rms_proj",
    )(x, g.reshape(1, d), w)
    return outs[0] if side is None else outs


A_SPAN = 2048


def _a_prompt_body(*refs, dil, slopes):
    ins, outs = refs[:10], refs[10:]
    t = pl.program_id(1)
    blk_rows = 128 * dil
    n_ub = A_SPAN // blk_rows
    i = _iota((128, 256), 0)
    j = _iota((128, 256), 1)
    back = 128 + i - j
    in_band = (back >= 0) & (back <= 128)
    dist = (back * dil).astype(F32)
    lane = _iota((128, 128), 1)

    def rows(ref, start):
        return ref[0, pl.ds(start, 128, stride=dil), :] if dil > 1 else ref[0, pl.ds(start, 128), :]

    def block(pair, ub, r, first):
        q_ref, k_ref, kp_ref, v_ref, vp_ref = ins[pair * 5:(pair + 1) * 5]
        o_ref, l_ref = outs[pair * 2:(pair + 1) * 2]
        start = ub * blk_rows + r
        if first:
            k_prev, v_prev = rows(kp_ref, r), rows(vp_ref, r)
            valid = in_band & (j >= jnp.where(t > 0, 0, 128))
        else:
            k_prev, v_prev = rows(k_ref, start - blk_rows), rows(v_ref, start - blk_rows)
            valid = in_band
        qp = rows(q_ref, start)
        kp = jnp.concatenate([k_prev, rows(k_ref, start)], axis=0).astype(BF16)
        vp = jnp.concatenate([v_prev, rows(v_ref, start)], axis=0).astype(BF16)
        o_pair = None
        l_pair = None
        for hh in range(2):
            hm = (lane < 64) if hh == 0 else (lane >= 64)
            qm = jnp.where(hm, qp, 0.0).astype(BF16)
            s = lax.dot_general(qm, kp, NT, preferred_element_type=F32) * SCALE
            s = jnp.where(valid, s - slopes[pair * 2 + hh] * dist, NEG)
            m = jnp.max(s, axis=-1, keepdims=True)
            e = jnp.exp(s - m)
            den = jnp.sum(e, axis=-1, keepdims=True)
            oh = jnp.dot((e * (1.0 / den)).astype(BF16), vp, preferred_element_type=F32)
            lh = jnp.broadcast_to(m + jnp.log(den), (128, 128))
            o_pair = oh if hh == 0 else jnp.where(lane < 64, o_pair, oh)
            l_pair = lh if hh == 0 else jnp.where(lane < 64, l_pair, lh)
        if dil > 1:
            o_ref[0, pl.ds(start, 128, stride=dil), :] = o_pair
            l_ref[0, pl.ds(start, 128, stride=dil), :] = l_pair
        else:
            o_ref[0, pl.ds(start, 128), :] = o_pair
            l_ref[0, pl.ds(start, 128), :] = l_pair

    def run(count, fn):
        if count == 0:
            return
        unroll = next(c for c in (4, 5, 3, 2, 1) if count % c == 0)

        def body(it, c):
            for k in range(unroll):
                fn(it * unroll + k)
            return c
        lax.fori_loop(0, count // unroll, body, 0)

    for pair in range(2):
        run(dil, lambda r, pair=pair: block(pair, 0, r, True))
        run((n_ub - 1) * dil, lambda idx, pair=pair: block(pair, 1 + idx // dil, idx % dil, False))


def _a_prompt_group(proj, g, dil):
    n, s_len, _ = proj.shape
    blk_rows = 128 * dil
    per_span = A_SPAN // blk_rows
    body = functools.partial(_a_prompt_body, dil=dil, slopes=tuple(SLOPES_A[g * 4:(g + 1) * 4]))
    cur = lambda col: pl.BlockSpec((1, A_SPAN, 128), lambda b, t: (b, t, col))
    prev = lambda col: pl.BlockSpec((1, blk_rows, 128), lambda b, t: (b, jnp.maximum(t * per_span - 1, 0), col))
    in_specs = []
    for pair in range(2):
        qc, kc, vc = 2 * g + pair, 6 + 2 * g + pair, 12 + 2 * g + pair
        in_specs += [cur(qc), cur(kc), prev(kc), cur(vc), prev(vc)]
    out_spec = pl.BlockSpec((1, A_SPAN, 128), lambda b, t: (b, t, 0))
    return pl.pallas_call(
        body,
        grid=(n, s_len // A_SPAN),
        in_specs=in_specs,
        out_specs=[out_spec] * 4,
        out_shape=[jax.ShapeDtypeStruct((n, s_len, 128), F32)] * 4,
        compiler_params=_params(2),
        name=f"a_prompt_g{g}",
    )(*([proj] * 10))


def _cross_rows(qx, kx, vx):
    tm = qx.shape[0]
    lane = _iota((tm, 128), 1)
    outs = []
    for pair in range(2):
        sl = slice(pair * 128, (pair + 1) * 128)
        qp, kp, vp = qx[:, sl], kx[:, sl], vx[:, sl]
        o_pair = None
        for hh in range(2):
            hm = (lane < 64) if hh == 0 else (lane >= 64)
            qm = jnp.where(hm, qp, 0.0).astype(BF16)
            s = lax.dot_general(qm, kp, NT, preferred_element_type=F32) * SCALE
            m = jnp.max(s, axis=-1, keepdims=True)
            e = jnp.exp(s - m)
            p = (e / jnp.sum(e, axis=-1, keepdims=True)).astype(BF16)
            oh = jnp.dot(p, vp, preferred_element_type=F32)
            o_pair = oh if hh == 0 else jnp.where(lane < 64, o_pair, oh)
        outs.append(o_pair)
    return jnp.concatenate(outs, axis=1)


def _out_norm_residual(x, z, w, g):
    y = jnp.dot(z.astype(BF16), w, preferred_element_type=F32)
    y = y * lax.rsqrt(jnp.mean(y * y, axis=-1, keepdims=True) + RMS_EPS)
    return x + y * g


def _finish_a_body(x_ref, *refs):
    gm_ref, qx_ref, gx_ref, mkv_ref, w_ref, g_ref, out_ref = refs[12:]
    mixes = []
    for pair in range(2):
        os_ = [refs[4 * g + 2 * pair][0] for g in range(3)]
        ls_ = [refs[4 * g + 2 * pair + 1][0] for g in range(3)]
        m = jnp.maximum(jnp.maximum(ls_[0], ls_[1]), ls_[2])
        es = [jnp.exp(l - m) for l in ls_]
        mixes.append((es[0] * os_[0] + es[1] * os_[1] + es[2] * os_[2]) / (es[0] + es[1] + es[2]))
    mix = jnp.concatenate(mixes, axis=1)
    mkv = mkv_ref[0]
    cx = _cross_rows(qx_ref[0], mkv[:, :X_WIDTH].astype(BF16), mkv[:, X_WIDTH:].astype(BF16))
    z = jnp.concatenate([mix * _silu(gm_ref[0]), cx * _silu(gx_ref[0])], axis=1)
    out_ref[0] = _out_norm_residual(x_ref[0], z, w_ref[...], g_ref[...])


def _finish_b_body(x_ref, mix_ref, gm_ref, qx_ref, gx_ref, mkv_ref, w_ref, g_ref, out_ref):
    mkv = mkv_ref[0]
    cx = _cross_rows(qx_ref[0], mkv[:, :X_WIDTH].astype(BF16), mkv[:, X_WIDTH:].astype(BF16))
    z = jnp.concatenate([mix_ref[0] * _silu(gm_ref[0]), cx * _silu(gx_ref[0])], axis=1)
    out_ref[0] = _out_norm_residual(x_ref[0], z, w_ref[...], g_ref[...])


def _finish_a(x, ols, proj, mkv, w_out, g_post, tm):
    n, s_len, d = x.shape
    row = lambda w, c: pl.BlockSpec((1, tm, w), lambda b, t: (b, t, c))
    in_specs = ([row(d, 0)] + [row(128, 0)] * 12 + [row(256, 9), row(256, 10), row(256, 11)]
                + [pl.BlockSpec((1, N_MEM, 2 * X_WIDTH), lambda b, t: (b, 0, 0)),
                   pl.BlockSpec(w_out.shape, lambda b, t: (0, 0)),
                   pl.BlockSpec((1, d), lambda b, t: (0, 0))])
    return pl.pallas_call(
        _finish_a_body, grid=(n, s_len // tm), in_specs=in_specs, out_specs=row(d, 0),
        out_shape=jax.ShapeDtypeStruct((n, s_len, d), F32), compiler_params=_params(2), name="finish_a",
    )(x, *ols, proj, proj, proj, mkv, w_out, g_post.reshape(1, d))


def _finish_b(x, mix, proj, mkv, w_out, g_post, tm):
    n, s_len, d = x.shape
    row = lambda w, c: pl.BlockSpec((1, tm, w), lambda b, t: (b, t, c))
    in_specs = [row(d, 0), row(B_WIDTH, 0), row(B_WIDTH, 2), row(256, 9), row(256, 10),
                pl.BlockSpec((1, N_MEM, 2 * X_WIDTH), lambda b, t: (b, 0, 0)),
                pl.BlockSpec(w_out.shape, lambda b, t: (0, 0)),
                pl.BlockSpec((1, d), lambda b, t: (0, 0))]
    return pl.pallas_call(
        _finish_b_body, grid=(n, s_len // tm), in_specs=in_specs, out_specs=row(d, 0),
        out_shape=jax.ShapeDtypeStruct((n, s_len, d), F32), compiler_params=_params(2), name="finish_b",
    )(x, mix, proj, proj, proj, mkv, w_out, g_post.reshape(1, d))


def _tail_body(x_ref, z_ref, w_ref, g_ref, out_ref):
    out_ref[...] = _out_norm_residual(x_ref[...], z_ref[...], w_ref[...], g_ref[...])


def _tail(x, z, w_out, g_post):
    m, d = x.shape
    full = lambda a: pl.BlockSpec(a.shape, lambda i: (0,) * a.ndim)
    g2 = g_post.reshape(1, d)
    return pl.pallas_call(
        _tail_body, grid=(1,), in_specs=[full(x), full(z), full(w_out), full(g2)], out_specs=full(x),
        out_shape=jax.ShapeDtypeStruct((m, d), F32), compiler_params=_params(1), name="sample_tail",
    )(x, z, w_out, g2)


def _compress_rows(load_rows, pos_ref, w1_ref, w2_ref, n_cmp):
    outs = []
    for t in range(2):
        a = jnp.zeros((n_cmp, 2 * CMP_HIDDEN), F32)
        b = jnp.zeros((n_cmp, 2 * CMP_HIDDEN), F32)
        for l in range(0, CMP_STRIDE, 2):
            y0, y1 = load_rows(t, l), load_rows(t, l + 1)
            pa = pos_ref[t, l:l + 2, :]
            pb = pos_ref[t, l + CMP_STRIDE:l + CMP_STRIDE + 2, :]
            ya = jnp.concatenate([y0 + pa[0:1], y1 + pa[1:2]], axis=1).astype(BF16)
            yb = jnp.concatenate([y0 + pb[0:1], y1 + pb[1:2]], axis=1).astype(BF16)
            a = a + jnp.dot(ya, w1_ref[t, l // 2], preferred_element_type=F32)
            b = b + jnp.dot(yb, w1_ref[t, (l + CMP_STRIDE) // 2], preferred_element_type=F32)
        h = a + pltpu.roll(b, n_cmp - 1, axis=0)
        outs.append(jnp.dot(_silu(h).astype(BF16), w2_ref[t], preferred_element_type=F32))
    return outs


def _compress_body(k_ref, v_ref, pos_ref, w1_ref, w2_ref, o_ref, *, n_cmp):
    refs = (k_ref, v_ref)
    load = lambda t, l: refs[t][0, pl.ds(l, n_cmp, stride=CMP_STRIDE), :]
    ck, cv = _compress_rows(load, pos_ref, w1_ref, w2_ref, n_cmp)
    o_ref[0, :, 0:128] = ck.astype(BF16)
    o_ref[0, :, 128:256] = cv.astype(BF16)


def _compress_prompt(proj, posw, w1bd, w2bd):
    n, s_len, _ = proj.shape
    n_cmp = s_len // CMP_STRIDE
    full = lambda a: pl.BlockSpec(a.shape, lambda b: (0,) * a.ndim)
    return pl.pallas_call(
        functools.partial(_compress_body, n_cmp=n_cmp), grid=(n,),
        in_specs=[pl.BlockSpec((1, s_len, 128), lambda b: (b, 0, 6)), pl.BlockSpec((1, s_len, 128), lambda b: (b, 0, 7)),
                  full(posw), full(w1bd), full(w2bd)],
        out_specs=pl.BlockSpec((1, n_cmp, 256), lambda b: (b, 0, 0)),
        out_shape=jax.ShapeDtypeStruct((n, n_cmp, 256), BF16), compiler_params=_params(1), name="compress_prompt",
    )(proj, proj, posw, w1bd, w2bd)


def _place_heads(tiles, lane):
    chunks = []
    for c in range(B_HEADS // 2):
        t0, t1 = tiles[2 * c], tiles[2 * c + 1]
        if (2 * c) // B_GROUP == 1:
            t0 = pltpu.roll(t0, 64, axis=1)
        if (2 * c + 1) // B_GROUP == 0:
            t1 = pltpu.roll(t1, 64, axis=1)
        chunks.append(jnp.where(lane < 64, t0, t1))
    return jnp.concatenate(chunks, axis=1)


def _masked_softmax_rows(s, ok):
    s = jnp.where(ok, s, NEG)
    m = jnp.max(s, axis=-1, keepdims=True)
    e = jnp.where(ok, jnp.exp(s - m), 0.0)
    den = jnp.maximum(jnp.sum(e, axis=-1, keepdims=True), 1e-30)
    return e * (1.0 / den)


def _ratio_of_halves(a):
    return a * (1.0 / pltpu.roll(a, 64, axis=1))


def _nsa_prompt_body(q_ref, gt_ref, kc_ref, vct_ref, kw_ref, vwt_ref, ks_ref, vst_ref, et_ref, mt_ref, eg_ref,
                     sl_ref, out_ref, q6_sc, m_sc, acc_sc, sel_sc, words_sm, idx_sm, *, s_len):
    qb = pl.program_id(1)
    qstart = qb * Q_BLOCK
    n_cmp = s_len // CMP_STRIDE
    q = q_ref[0] * (SCALE * LOG2E)
    lane = _iota((Q_BLOCK, 128), 1)
    tq_row = qstart + _iota((1, Q_BLOCK), 1)
    oc_t, os_t, ow_t = [None] * B_HEADS, [None] * B_HEADS, [None] * B_HEADS

    for kv in range(B_KV):
        row_kv = (_iota((128, 128), 0) < 64) if kv == 0 else (_iota((128, 128), 0) >= 64)
        for g in range(B_GROUP):
            h = kv * B_GROUP + g
            ch = q[:, (h // 2) * 128:(h // 2 + 1) * 128]
            if h % 2 == 1:
                ch = pltpu.roll(ch, 64, axis=1)
            q6_sc[g * 128:(g + 1) * 128, :] = jnp.where(lane < 64, ch, sl_ref[h:h + 1, :]).astype(BF16)
        q6 = q6_sc[...]

        s_t = lax.dot_general(kc_ref[0, kv], q6, NT, preferred_element_type=F32)
        cok = (CMP_STRIDE * _iota((n_cmp, Q_BLOCK), 0) + (CMP_LEN - 1)) <= tq_row
        q_ok = tq_row >= (CMP_LEN - 1)
        psum = jnp.zeros((n_cmp, Q_BLOCK), F32)
        ps = []
        for g in range(B_GROUP):
            s = jnp.where(cok, s_t[:, g * 128:(g + 1) * 128], NEG)
            e = jnp.exp2(s - jnp.max(s, axis=0, keepdims=True))
            p = e * jnp.where(q_ok, 1.0 / jnp.sum(e, axis=0, keepdims=True), 0.0)
            psum = psum + p
            ps.append(p.astype(BF16))
        oc = jnp.dot(vct_ref[0], jnp.concatenate(ps, axis=1), preferred_element_type=F32)
        for g in range(B_GROUP):
            oc_t[kv * B_GROUP + g] = oc[:, g * 128:(g + 1) * 128].T

        kparts, vparts, pparts = [], [], []
        n_wb = WIN_B // Q_BLOCK + 1
        for wb in range(n_wb):
            b_raw = qb - (n_wb - 1) + wb
            b = jnp.maximum(b_raw, 0)
            r0 = pl.multiple_of(b * Q_BLOCK, Q_BLOCK)
            kparts.append(kw_ref[0, kv, pl.ds(r0, Q_BLOCK), :])
            vparts.append(jnp.where(row_kv, vwt_ref[0, b], 1.0).astype(BF16))
            pparts.append(jnp.where(b_raw >= 0, r0, s_len) + _iota((128, Q_BLOCK), 0))
        s_t = lax.dot_general(jnp.concatenate(kparts, axis=0), q6, NT, preferred_element_type=F32)
        dw = tq_row - jnp.concatenate(pparts, axis=0)
        wok = (dw >= 0) & (dw <= WIN_B)
        ps = []
        for g in range(B_GROUP):
            s = jnp.where(wok, s_t[:, g * 128:(g + 1) * 128], NEG)
            ps.append(jnp.exp2(s - jnp.max(s, axis=0, keepdims=True)).astype(BF16))
        ow = jnp.dot(jnp.concatenate(vparts, axis=1), jnp.concatenate(ps, axis=1), preferred_element_type=F32)
        for g in range(B_GROUP):
            ow_t[kv * B_GROUP + g] = _ratio_of_halves(ow[:, g * 128:(g + 1) * 128].T)

        mt = mt_ref[...]
        imp = sum(jnp.dot(mt, t, preferred_element_type=F32) for t in _split3(psum))
        blk = _iota((128, Q_BLOCK), 0)
        ql = _iota((128, Q_BLOCK), 1)
        cur = jnp.where(ql >= SEL_BLOCK, qb * 2 + 1, qb * 2)
        forced = (blk == 0) | (blk == cur) | (blk == cur - 1)
        imp = jnp.where((blk > cur) | forced, -jnp.inf, imp)
        blkf = blk.astype(F32)
        sel = jnp.where(forced, 1.0, 0.0)
        for _ in range(SEL_TOPK - 3):
            mx = jnp.max(imp, axis=0, keepdims=True)
            idx = jnp.min(jnp.where(imp == mx, blkf, 1e9), axis=0, keepdims=True)
            hit = blkf == idx
            sel = jnp.where(hit, 1.0, sel)
            imp = jnp.where(hit, -jnp.inf, imp)
        sel_sc[...] = sel.astype(BF16)

        blk_col = _iota((128, 1), 0)
        weight = lax.shift_left(jnp.ones((128, 1), jnp.int32), blk_col & 15).astype(F32)
        contrib = jnp.max(sel, axis=1, keepdims=True) * weight
        for w in range(8):
            words_sm[w] = jnp.sum(contrib[16 * w:16 * (w + 1), :]).astype(jnp.int32)

        m_sc[...] = jnp.full(m_sc.shape, NEG, F32)
        acc_sc[...] = jnp.zeros(acc_sc.shape, F32)

        def scan(gi, cnt):
            bits = (words_sm[gi >> 3] >> ((gi & 7) * 2)) & 3
            idx_sm[cnt] = gi
            return cnt + jnp.where(bits != 0, 1, 0)

        cnt = lax.fori_loop(0, qb + 1, scan, 0)
        for j in range(GROUPS_PER_CHUNK - 1):
            idx_sm[cnt + j] = -1

        def chunk(c, carry):
            kts, vts, hits, kposs = [], [], [], []
            for j in range(GROUPS_PER_CHUNK):
                gi_raw = idx_sm[c * GROUPS_PER_CHUNK + j]
                gi = jnp.maximum(gi_raw, 0)
                k0 = pl.multiple_of(gi * Q_BLOCK, Q_BLOCK)
                kts.append(ks_ref[0, kv, pl.ds(k0, Q_BLOCK), :])
                vts.append(jnp.where(row_kv, vst_ref[0, gi], 1.0).astype(BF16))
                hits.append(jnp.dot(et_ref[gi], sel_sc[...], preferred_element_type=F32))
                kposs.append(jnp.where(gi_raw >= 0, k0, s_len) + _iota((128, 128), 0))
            s_t = lax.dot_general(jnp.concatenate(kts, axis=0), q6_sc[...], NT,
                                  preferred_element_type=F32)
            ok = ((jnp.concatenate(hits, axis=0) > 0.5)
                  & (jnp.concatenate(kposs, axis=0) <= qstart + _iota((GROUPS_PER_CHUNK * 128, 128), 1)))
            m_old = m_sc[...]
            m_new, ps = [], []
            for g in range(B_GROUP):
                cols = slice(g * 128, (g + 1) * 128)
                s = jnp.where(ok, s_t[:, cols], NEG)
                mg = jnp.maximum(m_old[:, cols], jnp.max(s, axis=0, keepdims=True))
                ps.append(jnp.exp2(s - mg).astype(BF16))
                m_new.append(mg)
            m_new = jnp.concatenate(m_new, axis=1)
            pv = jnp.dot(jnp.concatenate(vts, axis=1), jnp.concatenate(ps, axis=1), preferred_element_type=F32)
            acc_sc[...] = jnp.exp2(m_old - m_new) * acc_sc[...] + pv
            m_sc[...] = m_new
            return carry

        lax.fori_loop(0, (cnt + GROUPS_PER_CHUNK - 1) // GROUPS_PER_CHUNK, chunk, 0)
        for g in range(B_GROUP):
            os_t[kv * B_GROUP + g] = _ratio_of_halves(acc_sc[:, g * 128:(g + 1) * 128].T)

    sg = _sigmoid(gt_ref[0])
    eg = eg_ref[...]
    gexp = sum(jnp.dot(t, eg, preferred_element_type=F32) for t in _split3(sg))
    out_ref[0] = (gexp[:, 0:B_WIDTH] * _place_heads(oc_t, lane)
                  + gexp[:, B_WIDTH:2 * B_WIDTH] * _place_heads(os_t, lane)
                  + gexp[:, 2 * B_WIDTH:] * _place_heads(ow_t, lane))


def _nsa_prompt(proj, kc, vc_t, kw, vw_t, ks, vs_t, et3, mt, eg, slope_lanes):
    n, s_len, _ = proj.shape
    n_cmp = s_len // CMP_STRIDE
    n_grp = s_len // Q_BLOCK
    full = lambda a: pl.BlockSpec(a.shape, lambda b, t: (0,) * a.ndim)
    per_n = lambda a: pl.BlockSpec((1,) + a.shape[1:], lambda b, t: (b,) + (0,) * (a.ndim - 1))
    return pl.pallas_call(
        functools.partial(_nsa_prompt_body, s_len=s_len),
        grid=(n, n_grp),
        in_specs=[pl.BlockSpec((1, Q_BLOCK, B_WIDTH), lambda b, t: (b, t, 0)),
                  pl.BlockSpec((1, Q_BLOCK, 128), lambda b, t: (b, t, 22)),
                  per_n(kc), per_n(vc_t), per_n(kw), per_n(vw_t), per_n(ks), per_n(vs_t),
                  full(et3), full(mt), full(eg), full(slope_lanes)],
        out_specs=pl.BlockSpec((1, Q_BLOCK, B_WIDTH), lambda b, t: (b, t, 0)),
        out_shape=jax.ShapeDtypeStruct((n, s_len, B_WIDTH), F32),
        scratch_shapes=[pltpu.VMEM((B_GROUP * Q_BLOCK, 128), BF16), pltpu.VMEM((1, B_GROUP * Q_BLOCK), F32),
                        pltpu.VMEM((128, B_GROUP * Q_BLOCK), F32), pltpu.VMEM((128, Q_BLOCK), BF16),
                        pltpu.SMEM((8,), jnp.int32), pltpu.SMEM((n_grp + GROUPS_PER_CHUNK,), jnp.int32)],
        compiler_params=_params(2), name="nsa_prompt",
    )(proj, proj, kc, vc_t, kw, vw_t, ks, vs_t, et3, mt, eg, slope_lanes)


POS_LANE = 64


def _slope_lanes():
    out = np.zeros((B_HEADS, 128), np.float32)
    for h, slope in enumerate(SLOPES_B):
        s = np.float32(np.float64(slope) * LOG2E)
        hi = np.float32(np.asarray(s, np.float32).astype(jnp.bfloat16))
        mid = np.float32(np.asarray(np.float32(s) - hi, np.float32).astype(jnp.bfloat16))
        lo = np.float32(np.asarray(np.float32(s) - hi - mid, np.float32).astype(jnp.bfloat16))
        out[h, POS_LANE:POS_LANE + 6] = [hi, mid, lo, hi, mid, lo]
    return out


def _keys_with_pos(k2, pos):
    n, n_keys, _ = k2.shape
    lo = (pos % Q_BLOCK).astype(BF16)[None, :, None]
    hi = (pos - pos % Q_BLOCK).astype(BF16)[None, :, None]
    tail = jnp.concatenate([jnp.broadcast_to(lo, (n, n_keys, 3)), jnp.broadcast_to(hi, (n, n_keys, 3)),
                            jnp.zeros((n, n_keys, 128 - POS_LANE - 6), BF16)], axis=-1)
    return jnp.stack([jnp.concatenate([k2[..., kv * 64:(kv + 1) * 64], tail], axis=-1) for kv in range(B_KV)], axis=1)


def _values_by_group(v2):
    n, s_len, _ = v2.shape
    return v2.reshape(n, s_len // Q_BLOCK, Q_BLOCK, 128).transpose(0, 1, 3, 2)


def _heads_rows(vec, n_rows, width):
    r = _iota((n_rows, width), 0)
    l = _iota((n_rows, width), 1)
    hm = (l >= r * HEAD_DIM) & (l < r * HEAD_DIM + HEAD_DIM)
    return jnp.where(hm, jnp.broadcast_to(vec, (n_rows, width)), 0.0), hm


def _bf(x):
    return x.astype(BF16).astype(F32)


def _row_consts(n_rows, vals):
    r = _iota((n_rows, 1), 0)
    out = jnp.zeros((n_rows, 1), F32)
    for i, v in enumerate(vals):
        out = jnp.where(r == i, v, out)
    return out


def _rows_last(cache):
    nd = cache.ndim
    return cache.transpose(tuple(range(nd - 4)) + (nd - 3, nd - 2, nd - 1, nd - 4))


def _kv_t(ref, t):
    x = ref[0, 0, t]
    return x.reshape(x.shape[0] * x.shape[1], x.shape[2]).astype(BF16)


def _sample_cross(qx_row, k_t, v_t):
    q8, hm = _heads_rows(qx_row, 8, X_WIDTH)
    s = jnp.dot(q8.astype(BF16), k_t, preferred_element_type=F32) * SCALE
    e = jnp.exp(s - jnp.max(s, axis=-1, keepdims=True))
    p = (e * (1.0 / jnp.sum(e, axis=-1, keepdims=True))).astype(BF16)
    o8 = lax.dot_general(p, v_t, NT, preferred_element_type=F32)
    return jnp.sum(jnp.where(hm, o8, 0.0), axis=0, keepdims=True)


def _sample_a_body(row_ref, c0_ref, c1_ref, c2_ref, mkv_ref, z_ref):
    row = row_ref[0]
    outs, lses = [], []
    hm = None
    for g, (win, dil) in enumerate(A_PATTERNS):
        cref = (c0_ref, c1_ref, c2_ref)[g]
        q8, hm = _heads_rows(row[:, g * 256:(g + 1) * 256], 8, A_WIDTH)
        knew = row[:, 768 + g * 256:768 + (g + 1) * 256]
        vnew = row[:, 1536 + g * 256:1536 + (g + 1) * 256]
        q8b = q8.astype(BF16)
        slope = _row_consts(8, SLOPES_A[g * 4:(g + 1) * 4])
        s = jnp.dot(q8b, _kv_t(cref, 0), preferred_element_type=F32) * SCALE
        r = _iota((8, win), 1)
        s = jnp.where((r & (dil - 1)) == 0, s - slope * (win - r).astype(F32), NEG)
        s_new = jnp.sum(q8b.astype(F32) * _bf(knew), axis=-1, keepdims=True) * SCALE
        m = jnp.maximum(jnp.max(s, axis=-1, keepdims=True), s_new)
        e = jnp.exp(s - m)
        e_new = jnp.exp(s_new - m)
        den = jnp.sum(e, axis=-1, keepdims=True) + e_new
        inv = 1.0 / den
        o8 = (lax.dot_general((e * inv).astype(BF16), _kv_t(cref, 1), NT, preferred_element_type=F32)
              + _bf(e_new * inv) * _bf(vnew))
        outs.append(o8)
        lses.append(m + jnp.log(den))
    mx = jnp.maximum(jnp.maximum(lses[0], lses[1]), lses[2])
    ws = [jnp.exp(l - mx) for l in lses]
    mix8 = (ws[0] * outs[0] + ws[1] * outs[1] + ws[2] * outs[2]) / (ws[0] + ws[1] + ws[2])
    mix = jnp.sum(jnp.where(hm, mix8, 0.0), axis=0, keepdims=True)
    cx = _sample_cross(row[:, 2560:2816], _kv_t(mkv_ref, 0), _kv_t(mkv_ref, 1))
    z_ref[0] = jnp.concatenate([mix * _silu(row[:, 2304:2560]), cx * _silu(row[:, 2816:3072])], axis=1)


def _layer_block(cache_t, layer):
    return pl.BlockSpec((1, 1) + cache_t.shape[2:], lambda b, *_: (layer, b, 0, 0, 0, 0))


def _sample_a(proj_s, caches_t, mem_t, li, i):
    ns = proj_s.shape[0]
    row3 = proj_s.reshape(ns, 1, W_IN_A)
    return pl.pallas_call(
        _sample_a_body, grid=(ns,),
        in_specs=[pl.BlockSpec((1, 1, W_IN_A), lambda b: (b, 0, 0))] + [_layer_block(c, li) for c in caches_t]
                 + [_layer_block(mem_t, i)],
        out_specs=pl.BlockSpec((1, 1, A_WIDTH + X_WIDTH), lambda b: (b, 0, 0)),
        out_shape=jax.ShapeDtypeStruct((ns, 1, A_WIDTH + X_WIDTH), F32), compiler_params=_params(1), name="sample_a",
    )(row3, *caches_t, mem_t).reshape(ns, A_WIDTH + X_WIDTH)


def _q16(row):
    r = _iota((16, 128), 0)
    l = _iota((16, 128), 1)
    acc = jnp.zeros((16, 128), F32)
    for c in range(B_HEADS // 2):
        ch = jnp.broadcast_to(row[:, c * 128:(c + 1) * 128], (16, 128))
        rolled = pltpu.roll(ch, 64, axis=1)
        for hh in range(2):
            h = 2 * c + hh
            kv = h // B_GROUP
            lm = (l < 64) if kv == 0 else (l >= 64)
            acc = jnp.where((r == h) & lm, ch if hh == kv else rolled, acc)
    return acc * SCALE


def _sample_b1_body(pt_ref, row_ref, pos_ref, w1_ref, w2_ref, mm_ref, pages_ref, oc_ref, sel_ref,
                    buf, rows_sc, sem, imp_sc, *, li, n_pages, ns):
    n = pl.program_id(0)
    past = n_pages * PAGE_SIZE
    n_cmp = past // CMP_STRIDE

    def page_copy(page, p, slot):
        return pltpu.make_async_copy(pages_ref.at[page, li, pl.ds(0, 2)], buf.at[slot, p], sem.at[slot])

    def fetch(nn, slot):
        def body(p, c):
            page_copy(pt_ref[nn * n_pages + p], p, slot).start()
            return c
        lax.fori_loop(0, n_pages, body, 0)

    @pl.when(n == 0)
    def _():
        fetch(0, 0)

    @pl.when(n + 1 < ns)
    def _():
        fetch(n + 1, (n + 1) % 2)

    slot = n % 2

    def wbody(p, c):
        page_copy(0, p, slot).wait()
        return c
    lax.fori_loop(0, n_pages, wbody, 0)

    pages_per_step = 8

    def to_rows(it, c):
        for k in range(pages_per_step):
            p = it * pages_per_step + k
            r0 = pl.multiple_of(p * PAGE_SIZE, PAGE_SIZE)
            for t in range(2):
                rows_sc[t, pl.ds(r0, PAGE_SIZE), :] = buf[slot, p, t].reshape(2 * HEAD_DIM, PAGE_SIZE).T
        return c
    lax.fori_loop(0, n_pages // pages_per_step, to_rows, 0)

    load = lambda t, l: rows_sc[t, pl.ds(l, n_cmp, stride=CMP_STRIDE), :]
    ck, cv = _compress_rows(load, pos_ref, w1_ref, w2_ref, n_cmp)

    q16 = _q16(row_ref[0]).astype(BF16)
    slope = _row_consts(16, SLOPES_B)
    s = lax.dot_general(q16, ck.astype(BF16), NT, preferred_element_type=F32)
    cend = CMP_STRIDE * _iota((1, n_cmp), 1) + (CMP_LEN - 1)
    p = _masked_softmax_rows(s - slope * (past - cend).astype(F32), cend <= past)
    oc_ref[0] = jnp.dot(p.astype(BF16), cv.astype(BF16), preferred_element_type=F32)

    r16 = _iota((16, n_cmp), 0)
    ps0 = jnp.sum(jnp.where(r16 < B_GROUP, p, 0.0), axis=0, keepdims=True)
    ps1 = jnp.sum(jnp.where((r16 >= B_GROUP) & (r16 < B_HEADS), p, 0.0), axis=0, keepdims=True)
    psum = jnp.concatenate([ps0, ps1, jnp.zeros((6, n_cmp), F32)], axis=0)
    mm = mm_ref[...]
    imp = sum(jnp.dot(t, mm, preferred_element_type=F32) for t in _split3(psum))
    blk = _iota((8, 256), 1)
    cur = past // SEL_BLOCK
    forced = (blk == 0) | (blk == cur) | (blk == cur - 1)
    imp_sc[n] = jnp.where(blk > cur, -jnp.inf, jnp.where(forced, FORCE_SCORE, imp))

    @pl.when(n == ns - 1)
    def _():
        impa = imp_sc[...]
        blkf = _iota(impa.shape, 2).astype(F32)
        lane = _iota((ns, 8, 128), 2)
        out = jnp.zeros((ns, 8, 128), F32)
        for r in range(SEL_TOPK):
            mx = jnp.max(impa, axis=-1, keepdims=True)
            idx = jnp.min(jnp.where(impa == mx, blkf, 1e9), axis=-1, keepdims=True)
            impa = jnp.where(blkf == idx, -jnp.inf, impa)
            out = jnp.where(lane == r, idx, out)
        sel_ref[...] = out.astype(jnp.int32)


def _sample_b1(page_table, proj_s, posw, w1bd, w2bd, mm, pages_t, li):
    ns, n_pages = page_table.shape
    past = n_pages * PAGE_SIZE
    row3 = proj_s.reshape(ns, 1, W_IN_B_PAD)
    full = lambda a: pl.BlockSpec(a.shape, lambda b, pt: (0,) * a.ndim)
    grid_spec = pltpu.PrefetchScalarGridSpec(
        num_scalar_prefetch=1, grid=(ns,),
        in_specs=[pl.BlockSpec((1, 1, W_IN_B_PAD), lambda b, pt: (b, 0, 0)), full(posw), full(w1bd), full(w2bd),
                  full(mm), pl.BlockSpec(memory_space=pl.ANY)],
        out_specs=[pl.BlockSpec((1, 16, 128), lambda b, pt: (b, 0, 0)),
                   pl.BlockSpec((ns, 8, 128), lambda b, pt: (0, 0, 0))],
        scratch_shapes=[pltpu.VMEM((2, n_pages, 2, B_KV, HEAD_DIM, PAGE_SIZE), F32),
                        pltpu.VMEM((2, past, 128), F32), pltpu.SemaphoreType.DMA((2,)),
                        pltpu.VMEM((ns, 8, 256), F32)])
    return pl.pallas_call(
        functools.partial(_sample_b1_body, li=li, n_pages=n_pages, ns=ns),
        grid_spec=grid_spec,
        out_shape=[jax.ShapeDtypeStruct((ns, 16, 128), F32), jax.ShapeDtypeStruct((ns, 8, 128), jnp.int32)],
        compiler_params=_params(1), name="sample_b1",
    )(page_table.reshape(-1), row3, posw, w1bd, w2bd, mm, pages_t)


def _sample_b2_body(pt_ref, sf_ref, row_ref, oc_ref, sel_ref, win_ref, mkv_ref, e16_ref, pages_ref, z_ref,
                    buf, sem, *, li, n_pages, ns):
    n = pl.program_id(0)
    past = n_pages * PAGE_SIZE
    n_blk = past // SEL_BLOCK
    per_page = PAGE_SIZE // SEL_BLOCK
    n_sel = B_KV * SEL_TOPK

    def blk_copies(page, kv, r, slot):
        return [pltpu.make_async_copy(pages_ref.at[page, li, 2 + t, kv],
                                      buf.at[slot, t, kv, :, pl.ds(r * PAGE_SIZE, PAGE_SIZE)], sem.at[slot])
                for t in range(2)]

    def fetch(nn, slot):
        for kv in range(B_KV):
            for r in range(SEL_TOPK):
                j = jnp.minimum(sf_ref[nn * n_sel + kv * SEL_TOPK + r], n_blk - 1)
                for cp in blk_copies(pt_ref[nn * n_pages + j // per_page], kv, r, slot):
                    cp.start()

    @pl.when(n == 0)
    def _():
        fetch(0, 0)

    @pl.when(n + 1 < ns)
    def _():
        fetch(n + 1, (n + 1) % 2)

    slot = n % 2
    for kv in range(B_KV):
        for r in range(SEL_TOPK):
            for cp in blk_copies(0, kv, r, slot):
                cp.wait()

    row = row_ref[0]
    q16f = _q16(row)
    q16 = q16f.astype(BF16)
    q16r = q16.astype(F32)
    slope = _row_consts(16, SLOPES_B)
    r16 = _iota((16, 128), 0)

    def new_key(col):
        kn = _bf(row[:, col:col + 128])
        return jnp.sum(q16r * kn, axis=-1, keepdims=True)

    def attend(s, s_new, v_t):
        m = jnp.maximum(jnp.max(s, axis=-1, keepdims=True), s_new)
        e = jnp.exp(s - m)
        e_new = jnp.exp(s_new - m)
        inv = 1.0 / (jnp.sum(e, axis=-1, keepdims=True) + e_new)
        return lax.dot_general((e * inv).astype(BF16), v_t, NT, preferred_element_type=F32), _bf(e_new * inv)

    n_keys = SEL_TOPK * PAGE_SIZE
    jv = jnp.dot(sel_ref[0].astype(F32).astype(BF16), e16_ref[...], preferred_element_type=F32).astype(jnp.int32)
    in_page = jnp.bitwise_and(_iota((8, n_keys), 1), PAGE_SIZE - 1)
    ok_sel = ((in_page >> 6) == (jv & (per_page - 1))) & (jv < n_blk)
    dist_sel = (past - ((jv >> 1) * PAGE_SIZE + in_page)).astype(F32)
    q64 = jnp.where(r16[:, :HEAD_DIM] < B_GROUP, q16f[:, :HEAD_DIM], q16f[:, HEAD_DIM:]).astype(BF16)
    s_new = new_key(1024)
    outs, p_news = [], []
    for kv in range(B_KV):
        s = jnp.dot(q64, buf[slot, 0, kv].astype(BF16), preferred_element_type=F32) - slope * dist_sel[kv:kv + 1]
        o, p_new = attend(jnp.where(ok_sel[kv:kv + 1], s, NEG), s_new, buf[slot, 1, kv].astype(BF16))
        outs.append(o)
        p_news.append(p_new)
    p_new = jnp.where(_iota((16, 1), 0) < B_GROUP, p_news[0], p_news[1])
    os16 = jnp.concatenate(outs, axis=1) + p_new * _bf(row[:, 1152:1280])

    lb = win_ref.shape[-1]
    dw = (lb - _iota((1, lb), 1)).astype(F32)
    s = jnp.dot(q16, _kv_t(win_ref, 0), preferred_element_type=F32) - slope * dw
    ow16, p_new = attend(s, new_key(1280), _kv_t(win_ref, 1))
    ow16 = ow16 + p_new * _bf(row[:, 1408:1536])

    sg = jnp.broadcast_to(_sigmoid(row[:, 2816:2944]), (16, 128))
    l16 = _iota((16, 128), 1)
    gate = lambda b: jnp.sum(jnp.where(l16 == r16 * 3 + b, sg, 0.0), axis=-1, keepdims=True)
    out16 = gate(0) * oc_ref[0] + gate(1) * os16 + gate(2) * ow16
    lane1 = _iota((1, 128), 1)
    mix = _place_heads([out16[h:h + 1, :] for h in range(B_HEADS)], lane1)

    cx = _sample_cross(row[:, 2304:2560], _kv_t(mkv_ref, 0), _kv_t(mkv_ref, 1))
    z_ref[0] = jnp.concatenate([mix * _silu(row[:, 1536:2304]), cx * _silu(row[:, 2560:2816])], axis=1)


def _sample_b2(page_table, sel, proj_s, oc, win_t, mem_t, e16, pages_t, li, i):
    ns, n_pages = page_table.shape
    row3 = proj_s.reshape(ns, 1, W_IN_B_PAD)
    full = lambda a: pl.BlockSpec(a.shape, lambda b, pt, sf: (0,) * a.ndim)
    per = lambda a: pl.BlockSpec((1,) + a.shape[1:], lambda b, pt, sf: (b,) + (0,) * (a.ndim - 1))
    grid_spec = pltpu.PrefetchScalarGridSpec(
        num_scalar_prefetch=2, grid=(ns,),
        in_specs=[per(row3), per(oc), per(sel), _layer_block(win_t, li), _layer_block(mem_t, i), full(e16),
                  pl.BlockSpec(memory_space=pl.ANY)],
        out_specs=pl.BlockSpec((1, 1, B_WIDTH + X_WIDTH), lambda b, pt, sf: (b, 0, 0)),
        scratch_shapes=[pltpu.VMEM((2, 2, B_KV, HEAD_DIM, SEL_TOPK * PAGE_SIZE), F32), pltpu.SemaphoreType.DMA((2,))])
    return pl.pallas_call(
        functools.partial(_sample_b2_body, li=li, n_pages=n_pages, ns=ns),
        grid_spec=grid_spec,
        out_shape=jax.ShapeDtypeStruct((ns, 1, B_WIDTH + X_WIDTH), F32),
        compiler_params=_params(1), name="sample_b2",
    )(page_table.reshape(-1), sel[:, :B_KV, :SEL_TOPK].reshape(-1), row3, oc, sel, win_t, mem_t, e16, pages_t
      ).reshape(ns, B_WIDTH + X_WIDTH)


def _importance_matrix(n_cmp, n_cols):
    c = np.arange(n_cmp)[:, None]
    j = np.arange(n_cols)[None, :]
    per = SEL_BLOCK // CMP_STRIDE
    m = ((c >= per * j) & (c <= per * j + per - 1)).astype(np.float32)
    m = m + ((c + 1 >= per * j) & (c + 1 <= per * j + per - 1)).astype(np.float32)
    m[n_cmp - 1, :] = 0.0
    return m


def _gate_expand():
    eg = np.zeros((128, 3 * B_WIDTH), np.float32)
    for h in range(B_HEADS):
        for b in range(3):
            eg[h * 3 + b, b * B_WIDTH + h * HEAD_DIM:b * B_WIDTH + (h + 1) * HEAD_DIM] = 1.0
    return eg


def _block_expand(n_rows, n_keys):
    return (np.arange(n_keys)[None, :] // SEL_BLOCK == np.arange(n_rows)[:, None]).astype(np.float32)


def _compress_weights(cmp_pos, cmp_w1, cmp_w2):
    eye = jnp.eye(B_KV, dtype=F32)
    posw = jnp.concatenate([cmp_pos, cmp_pos], axis=-1)
    w1 = cmp_w1.reshape(2, CMP_LEN, HEAD_DIM, CMP_HIDDEN)
    w1bd = jnp.einsum('tlek,jm->tljemk', w1, eye).reshape(2, CMP_LEN // 2, 4 * HEAD_DIM, 2 * CMP_HIDDEN)
    w2bd = jnp.einsum('tke,jm->tjkme', cmp_w2, eye).reshape(2, 2 * CMP_HIDDEN, 2 * HEAD_DIM)
    return posw, w1bd.astype(BF16), w2bd.astype(BF16)


def _permute_w_in_b(w):
    d = w.shape[0]
    return jnp.concatenate([w[:, :1536], w[:, 1572:W_IN_B], w[:, 1536:1572],
                            jnp.zeros((d, W_IN_B_PAD - W_IN_B), w.dtype)], axis=1)


def kernel(x_prompt, x_sample, cache_mem_kv, cache_a_w128_kv, cache_a_w512_kv, cache_a_w2048_kv, cache_b_pages,
           cache_b_win_kv, page_table, mem_prompt, norm_pre, norm_post, norm_mem, w_mem_kv, w_in_a, w_out_a,
           w_in_b, w_out_b, cmp_pos, cmp_w1, cmp_w2):
    n, s_len, d = x_prompt.shape
    ns = x_sample.shape[0]
    depth = norm_pre.shape[0]
    n_pool, page_size, n_lb = cache_b_pages.shape[:3]
    n_pages = page_table.shape[1]
    past = n_pages * page_size
    assert d == D_MODEL and x_sample.shape[1] == 1 and page_size == PAGE_SIZE
    assert s_len % 2048 == 0 and past % 2048 == 0 and s_len >= 2048 and past >= 2048
    caches_a = (cache_a_w128_kv, cache_a_w512_kv, cache_a_w2048_kv)
    for c, (win, _) in zip(caches_a, A_PATTERNS):
        assert c.shape[2] == win
    assert cache_b_win_kv.shape[2] == WIN_B

    tm = 512
    n_cmp_p = s_len // CMP_STRIDE
    n_cmp_s = past // CMP_STRIDE
    et3 = jnp.asarray(_block_expand(128, s_len).reshape(128, s_len // Q_BLOCK, Q_BLOCK).transpose(1, 2, 0), BF16)
    slope_lanes = jnp.asarray(_slope_lanes(), F32)
    mt = jnp.asarray(_importance_matrix(n_cmp_p, 128).T, BF16)
    mm = jnp.asarray(_importance_matrix(n_cmp_s, 256), BF16)
    eg = jnp.asarray(_gate_expand(), BF16)
    e16 = jnp.asarray((np.arange(SEL_TOPK * PAGE_SIZE)[None, :] // PAGE_SIZE == np.arange(128)[:, None]), BF16)
    pages_t = cache_b_pages.transpose(0, 2, 3, 4, 5, 1)
    caches_a_t = [_rows_last(c) for c in caches_a]
    mem_t = _rows_last(cache_mem_kv)
    win_t = _rows_last(cache_b_win_kv)

    xp = x_prompt
    xs = x_sample.reshape(ns, d)
    mem2 = mem_prompt.reshape(n * N_MEM, d)
    mem_new = []
    a_p = [[] for _ in A_PATTERNS]
    a_s = [[] for _ in A_PATTERNS]
    b_p, b_s, bw_p, bw_s = [], [], [], []
    for i in range(depth):
        li = i // 2
        mkv_p = _rms_proj(mem2, norm_mem[i], w_mem_kv[i].astype(BF16), tm=N_MEM).reshape(n, N_MEM, 2 * X_WIDTH)
        mem_new.append(mkv_p.reshape(n, N_MEM, 2, 4, HEAD_DIM))
        if i % 2 == 0:
            w_in = w_in_a[li].astype(BF16)
            w_out = w_out_a[li].astype(BF16)
            proj_p = _rms_proj(xp.reshape(n * s_len, d), norm_pre[i], w_in, tm=tm).reshape(n, s_len, W_IN_A)
            proj_s = _rms_proj(xs, norm_pre[i], w_in, tm=ns)
            ols = []
            for g, (win, dil) in enumerate(A_PATTERNS):
                ols += _a_prompt_group(proj_p, g, dil)
                kv_p = proj_p[:, s_len - win:, 768:2304].reshape(n, win, 2, 3, 4, HEAD_DIM)[:, :, :, g]
                a_p[g].append(kv_p)
                a_s[g].append(proj_s[:, 768:2304].reshape(ns, 1, 2, 3, 4, HEAD_DIM)[:, :, :, g])
            xp = _finish_a(xp, ols, proj_p, mkv_p, w_out, norm_post[i], tm=256)
            z = _sample_a(proj_s, caches_a_t, mem_t, li, i)
            xs = _tail(xs, z, w_out, norm_post[i])
        else:
            w_in = _permute_w_in_b(w_in_b[li]).astype(BF16)
            w_out = w_out_b[li].astype(BF16)
            posw, w1bd, w2bd = _compress_weights(cmp_pos[li], cmp_w1[li], cmp_w2[li])
            proj_p, kvs = _rms_proj(xp.reshape(n * s_len, d), norm_pre[i], w_in, tm=tm, side=(1024, 512))
            proj_p = proj_p.reshape(n, s_len, W_IN_B_PAD)
            kvs = kvs.reshape(n, s_len, 512)
            proj_s = _rms_proj(xs, norm_pre[i], w_in, tm=ns)
            cmpd = _compress_prompt(proj_p, posw, w1bd, w2bd)
            pos = jnp.arange(s_len, dtype=jnp.int32)
            cend = CMP_STRIDE * jnp.arange(n_cmp_p, dtype=jnp.int32) + (CMP_LEN - 1)
            mix = _nsa_prompt(proj_p, _keys_with_pos(cmpd[:, :, 0:128], cend), cmpd[:, :, 128:256].transpose(0, 2, 1),
                              _keys_with_pos(kvs[:, :, 256:384], pos), _values_by_group(kvs[:, :, 384:512]),
                              _keys_with_pos(kvs[:, :, 0:128], pos), _values_by_group(kvs[:, :, 128:256]),
                              et3, mt, eg, slope_lanes)
            xp = _finish_b(xp, mix, proj_p, mkv_p, w_out, norm_post[i], tm=256)
            oc, sel = _sample_b1(page_table, proj_s, posw, w1bd, w2bd, mm, pages_t, li)
            z = _sample_b2(page_table, sel, proj_s, oc, win_t, mem_t, e16, pages_t, li, i)
            xs = _tail(xs, z, w_out, norm_post[i])
            b_p.append(proj_p[:, :, 768:1280].reshape(n, s_len, 4, B_KV, HEAD_DIM))
            bw_p.append(proj_p[:, s_len - WIN_B:, 1280:1536].reshape(n, WIN_B, 2, B_KV, HEAD_DIM))
            b_s.append(proj_s[:, 768:1280].reshape(ns, 1, 4, B_KV, HEAD_DIM))
            bw_s.append(proj_s[:, 1280:1536].reshape(ns, 1, 2, B_KV, HEAD_DIM))
    return (xp, xs.reshape(ns, 1, d), jnp.stack(mem_new, axis=0),
            jnp.stack(a_p[0], axis=0), jnp.stack(a_p[1], axis=0), jnp.stack(a_p[2], axis=0),
            jnp.stack(b_p, axis=2), jnp.stack(bw_p, axis=0),
            jnp.stack(a_s[0], axis=0), jnp.stack(a_s[1], axis=0), jnp.stack(a_s[2], axis=0),
            jnp.stack(b_s, axis=2), jnp.stack(bw_s, axis=0))
```

```python
import functools

import numpy as np
import jax
import jax.numpy as jnp
from jax import lax
from jax.experimental import pallas as pl
from jax.experimental.pallas import tpu as pltpu

F32 = jnp.float32
BF16 = jnp.bfloat16

D_MODEL = 1024
HEAD_DIM = 64
SCALE = HEAD_DIM ** -0.5
LOG2E = 1.4426950408889634
RMS_EPS = 1e-6
N_MEM = 256
X_WIDTH = 256
A_PATTERNS = ((128, 1), (512, 4), (2048, 16))
A_WIDTH = 256
W_IN_A = 3072
B_HEADS = 12
B_KV = 2
B_GROUP = 6
B_WIDTH = 768
W_IN_B = 2852
W_IN_B_PAD = 2944
CMP_LEN = 32
CMP_STRIDE = 16
CMP_HIDDEN = 128
SEL_BLOCK = 64
SEL_TOPK = 16
WIN_B = 512
Q_BLOCK = 128
FORCE_SCORE = 1e4
PAGE_SIZE = 128
NEG = -1e30
GROUPS_PER_CHUNK = 4
VMEM_LIMIT = 56 * 1024 * 1024

NT = (((1,), (1,)), ((), ()))


def _alibi(n):
    k = np.arange(1, n + 1, dtype=np.float32)
    return [float(v) for v in np.float32(2.0) ** (np.float32(-8.0) * k / np.float32(n))]


SLOPES_A = _alibi(12)
SLOPES_B = _alibi(12)


def _params(n_axes):
    return pltpu.CompilerParams(dimension_semantics=("arbitrary",) * n_axes, vmem_limit_bytes=VMEM_LIMIT)


def _sigmoid(x):
    return 1.0 / (1.0 + jnp.exp(-x))


def _silu(x):
    return x * _sigmoid(x)


def _iota(shape, dim):
    return lax.broadcasted_iota(jnp.int32, shape, dim)


def _split3(x):
    hi = x.astype(BF16)
    r1 = x - hi.astype(F32)
    mid = r1.astype(BF16)
    lo = (r1 - mid.astype(F32)).astype(BF16)
    return hi, mid, lo


def _rms_proj_body(x_ref, g_ref, w_ref, o_ref, *side_ref, side):
    x = x_ref[...]
    y = x * lax.rsqrt(jnp.mean(x * x, axis=-1, keepdims=True) + RMS_EPS)
    y = (y * g_ref[...]).astype(BF16)
    o = jnp.dot(y, w_ref[...], preferred_element_type=F32)
    o_ref[...] = o
    if side is not None:
        side_ref[0][...] = o[:, side[0]:side[0] + side[1]].astype(BF16)


def _rms_proj(x, g, w, tm, side=None):
    m, d = x.shape
    n = w.shape[1]
    out_specs = [pl.BlockSpec((tm, n), lambda i: (i, 0))]
    out_shape = [jax.ShapeDtypeStruct((m, n), F32)]
    if side is not None:
        out_specs.append(pl.BlockSpec((tm, side[1]), lambda i: (i, 0)))
        out_shape.append(jax.ShapeDtypeStruct((m, side[1]), BF16))
    outs = pl.pallas_call(
        functools.partial(_rms_proj_body, side=side),
        grid=(m // tm,),
        in_specs=[pl.BlockSpec((tm, d), lambda i: (i, 0)),
                  pl.BlockSpec((1, d), lambda i: (0, 0)),
                  pl.BlockSpec((d, n), lambda i: (0, 0))],
        out_specs=out_specs,
        out_shape=out_shape,
        compiler_params=_params(1),
        name="rms_proj",
    )(x, g.reshape(1, d), w)
    return outs[0] if side is None else outs


A_SPAN = 2048


def _a_prompt_body(*refs, dil, slopes):
    ins, outs = refs[:10], refs[10:]
    t = pl.program_id(1)
    blk_rows = 128 * dil
    n_ub = A_SPAN // blk_rows
    i = _iota((128, 256), 0)
    j = _iota((128, 256), 1)
    back = 128 + i - j
    in_band = (back >= 0) & (back <= 128)
    dist = (back * dil).astype(F32)
    lane = _iota((128, 128), 1)

    def rows(ref, start):
        return ref[0, pl.ds(start, 128, stride=dil), :] if dil > 1 else ref[0, pl.ds(start, 128), :]

    def block(pair, ub, r, first):
        q_ref, k_ref, kp_ref, v_ref, vp_ref = ins[pair * 5:(pair + 1) * 5]
        o_ref, l_ref = outs[pair * 2:(pair + 1) * 2]
        start = ub * blk_rows + r
        if first:
            k_prev, v_prev = rows(kp_ref, r), rows(vp_ref, r)
            valid = in_band & (j >= jnp.where(t > 0, 0, 128))
        else:
            k_prev, v_prev = rows(k_ref, start - blk_rows), rows(v_ref, start - blk_rows)
            valid = in_band
        qp = rows(q_ref, start)
        kp = jnp.concatenate([k_prev, rows(k_ref, start)], axis=0).astype(BF16)
        vp = jnp.concatenate([v_prev, rows(v_ref, start)], axis=0).astype(BF16)
        o_pair = None
        l_pair = None
        for hh in range(2):
            hm = (lane < 64) if hh == 0 else (lane >= 64)
            qm = jnp.where(hm, qp, 0.0).astype(BF16)
            s = lax.dot_general(qm, kp, NT, preferred_element_type=F32) * SCALE
            s = jnp.where(valid, s - slopes[pair * 2 + hh] * dist, NEG)
            m = jnp.max(s, axis=-1, keepdims=True)
            e = jnp.exp(s - m)
            den = jnp.sum(e, axis=-1, keepdims=True)
            oh = jnp.dot((e * (1.0 / den)).astype(BF16), vp, preferred_element_type=F32)
            lh = jnp.broadcast_to(m + jnp.log(den), (128, 128))
            o_pair = oh if hh == 0 else jnp.where(lane < 64, o_pair, oh)
            l_pair = lh if hh == 0 else jnp.where(lane < 64, l_pair, lh)
        if dil > 1:
            o_ref[0, pl.ds(start, 128, stride=dil), :] = o_pair
            l_ref[0, pl.ds(start, 128, stride=dil), :] = l_pair
        else:
            o_ref[0, pl.ds(start, 128), :] = o_pair
            l_ref[0, pl.ds(start, 128), :] = l_pair

    def run(count, fn):
        if count == 0:
            return
        unroll = next(c for c in (4, 5, 3, 2, 1) if count % c == 0)

        def body(it, c):
            for k in range(unroll):
                fn(it * unroll + k)
            return c
        lax.fori_loop(0, count // unroll, body, 0)

    for pair in range(2):
        run(dil, lambda r, pair=pair: block(pair, 0, r, True))
        run((n_ub - 1) * dil, lambda idx, pair=pair: block(pair, 1 + idx // dil, idx % dil, False))


def _a_prompt_group(proj, g, dil):
    n, s_len, _ = proj.shape
    blk_rows = 128 * dil
    per_span = A_SPAN // blk_rows
    body = functools.partial(_a_prompt_body, dil=dil, slopes=tuple(SLOPES_A[g * 4:(g + 1) * 4]))
    cur = lambda col: pl.BlockSpec((1, A_SPAN, 128), lambda b, t: (b, t, col))
    prev = lambda col: pl.BlockSpec((1, blk_rows, 128), lambda b, t: (b, jnp.maximum(t * per_span - 1, 0), col))
    in_specs = []
    for pair in range(2):
        qc, kc, vc = 2 * g + pair, 6 + 2 * g + pair, 12 + 2 * g + pair
        in_specs += [cur(qc), cur(kc), prev(kc), cur(vc), prev(vc)]
    out_spec = pl.BlockSpec((1, A_SPAN, 128), lambda b, t: (b, t, 0))
    return pl.pallas_call(
        body,
        grid=(n, s_len // A_SPAN),
        in_specs=in_specs,
        out_specs=[out_spec] * 4,
        out_shape=[jax.ShapeDtypeStruct((n, s_len, 128), F32)] * 4,
        compiler_params=_params(2),
        name=f"a_prompt_g{g}",
    )(*([proj] * 10))


def _cross_rows(qx, kx, vx):
    tm = qx.shape[0]
    lane = _iota((tm, 128), 1)
    outs = []
    for pair in range(2):
        sl = slice(pair * 128, (pair + 1) * 128)
        qp, kp, vp = qx[:, sl], kx[:, sl], vx[:, sl]
        o_pair = None
        for hh in range(2):
            hm = (lane < 64) if hh == 0 else (lane >= 64)
            qm = jnp.where(hm, qp, 0.0).astype(BF16)
            s = lax.dot_general(qm, kp, NT, preferred_element_type=F32) * SCALE
            m = jnp.max(s, axis=-1, keepdims=True)
            e = jnp.exp(s - m)
            p = (e / jnp.sum(e, axis=-1, keepdims=True)).astype(BF16)
            oh = jnp.dot(p, vp, preferred_element_type=F32)
            o_pair = oh if hh == 0 else jnp.where(lane < 64, o_pair, oh)
        outs.append(o_pair)
    return jnp.concatenate(outs, axis=1)


def _out_norm_residual(x, z, w, g):
    y = jnp.dot(z.astype(BF16), w, preferred_element_type=F32)
    y = y * lax.rsqrt(jnp.mean(y * y, axis=-1, keepdims=True) + RMS_EPS)
    return x + y * g


def _finish_a_body(x_ref, *refs):
    gm_ref, qx_ref, gx_ref, mkv_ref, w_ref, g_ref, out_ref = refs[12:]
    mixes = []
    for pair in range(2):
        os_ = [refs[4 * g + 2 * pair][0] for g in range(3)]
        ls_ = [refs[4 * g + 2 * pair + 1][0] for g in range(3)]
        m = jnp.maximum(jnp.maximum(ls_[0], ls_[1]), ls_[2])
        es = [jnp.exp(l - m) for l in ls_]
        mixes.append((es[0] * os_[0] + es[1] * os_[1] + es[2] * os_[2]) / (es[0] + es[1] + es[2]))
    mix = jnp.concatenate(mixes, axis=1)
    mkv = mkv_ref[0]
    cx = _cross_rows(qx_ref[0], mkv[:, :X_WIDTH].astype(BF16), mkv[:, X_WIDTH:].astype(BF16))
    z = jnp.concatenate([mix * _silu(gm_ref[0]), cx * _silu(gx_ref[0])], axis=1)
    out_ref[0] = _out_norm_residual(x_ref[0], z, w_ref[...], g_ref[...])


def _finish_b_body(x_ref, mix_ref, gm_ref, qx_ref, gx_ref, mkv_ref, w_ref, g_ref, out_ref):
    mkv = mkv_ref[0]
    cx = _cross_rows(qx_ref[0], mkv[:, :X_WIDTH].astype(BF16), mkv[:, X_WIDTH:].astype(BF16))
    z = jnp.concatenate([mix_ref[0] * _silu(gm_ref[0]), cx * _silu(gx_ref[0])], axis=1)
    out_ref[0] = _out_norm_residual(x_ref[0], z, w_ref[...], g_ref[...])


def _finish_a(x, ols, proj, mkv, w_out, g_post, tm):
    n, s_len, d = x.shape
    row = lambda w, c: pl.BlockSpec((1, tm, w), lambda b, t: (b, t, c))
    in_specs = ([row(d, 0)] + [row(128, 0)] * 12 + [row(256, 9), row(256, 10), row(256, 11)]
                + [pl.BlockSpec((1, N_MEM, 2 * X_WIDTH), lambda b, t: (b, 0, 0)),
                   pl.BlockSpec(w_out.shape, lambda b, t: (0, 0)),
                   pl.BlockSpec((1, d), lambda b, t: (0, 0))])
    return pl.pallas_call(
        _finish_a_body, grid=(n, s_len // tm), in_specs=in_specs, out_specs=row(d, 0),
        out_shape=jax.ShapeDtypeStruct((n, s_len, d), F32), compiler_params=_params(2), name="finish_a",
    )(x, *ols, proj, proj, proj, mkv, w_out, g_post.reshape(1, d))


def _finish_b(x, mix, proj, mkv, w_out, g_post, tm):
    n, s_len, d = x.shape
    row = lambda w, c: pl.BlockSpec((1, tm, w), lambda b, t: (b, t, c))
    in_specs = [row(d, 0), row(B_WIDTH, 0), row(B_WIDTH, 2), row(256, 9), row(256, 10),
                pl.BlockSpec((1, N_MEM, 2 * X_WIDTH), lambda b, t: (b, 0, 0)),
                pl.BlockSpec(w_out.shape, lambda b, t: (0, 0)),
                pl.BlockSpec((1, d), lambda b, t: (0, 0))]
    return pl.pallas_call(
        _finish_b_body, grid=(n, s_len // tm), in_specs=in_specs, out_specs=row(d, 0),
        out_shape=jax.ShapeDtypeStruct((n, s_len, d), F32), compiler_params=_params(2), name="finish_b",
    )(x, mix, proj, proj, proj, mkv, w_out, g_post.reshape(1, d))


def _tail_body(x_ref, z_ref, w_ref, g_ref, out_ref):
    out_ref[...] = _out_norm_residual(x_ref[...], z_ref[...], w_ref[...], g_ref[...])


def _tail(x, z, w_out, g_post):
    m, d = x.shape
    full = lambda a: pl.BlockSpec(a.shape, lambda i: (0,) * a.ndim)
    g2 = g_post.reshape(1, d)
    return pl.pallas_call(
        _tail_body, grid=(1,), in_specs=[full(x), full(z), full(w_out), full(g2)], out_specs=full(x),
        out_shape=jax.ShapeDtypeStruct((m, d), F32), compiler_params=_params(1), name="sample_tail",
    )(x, z, w_out, g2)


def _compress_rows(load_rows, pos_ref, w1_ref, w2_ref, n_cmp):
    outs = []
    for t in range(2):
        a = jnp.zeros((n_cmp, 2 * CMP_HIDDEN), F32)
        b = jnp.zeros((n_cmp, 2 * CMP_HIDDEN), F32)
        for l in range(0, CMP_STRIDE, 2):
            y0, y1 = load_rows(t, l), load_rows(t, l + 1)
            pa = pos_ref[t, l:l + 2, :]
            pb = pos_ref[t, l + CMP_STRIDE:l + CMP_STRIDE + 2, :]
            ya = jnp.concatenate([y0 + pa[0:1], y1 + pa[1:2]], axis=1).astype(BF16)
            yb = jnp.concatenate([y0 + pb[0:1], y1 + pb[1:2]], axis=1).astype(BF16)
            a = a + jnp.dot(ya, w1_ref[t, l // 2], preferred_element_type=F32)
            b = b + jnp.dot(yb, w1_ref[t, (l + CMP_STRIDE) // 2], preferred_element_type=F32)
        h = a + pltpu.roll(b, n_cmp - 1, axis=0)
        outs.append(jnp.dot(_silu(h).astype(BF16), w2_ref[t], preferred_element_type=F32))
    return outs


def _compress_body(k_ref, v_ref, pos_ref, w1_ref, w2_ref, o_ref, *, n_cmp):
    refs = (k_ref, v_ref)
    load = lambda t, l: refs[t][0, pl.ds(l, n_cmp, stride=CMP_STRIDE), :]
    ck, cv = _compress_rows(load, pos_ref, w1_ref, w2_ref, n_cmp)
    o_ref[0, :, 0:128] = ck.astype(BF16)
    o_ref[0, :, 128:256] = cv.astype(BF16)


def _compress_prompt(proj, posw, w1bd, w2bd):
    n, s_len, _ = proj.shape
    n_cmp = s_len // CMP_STRIDE
    full = lambda a: pl.BlockSpec(a.shape, lambda b: (0,) * a.ndim)
    return pl.pallas_call(
        functools.partial(_compress_body, n_cmp=n_cmp), grid=(n,),
        in_specs=[pl.BlockSpec((1, s_len, 128), lambda b: (b, 0, 6)), pl.BlockSpec((1, s_len, 128), lambda b: (b, 0, 7)),
                  full(posw), full(w1bd), full(w2bd)],
        out_specs=pl.BlockSpec((1, n_cmp, 256), lambda b: (b, 0, 0)),
        out_shape=jax.ShapeDtypeStruct((n, n_cmp, 256), BF16), compiler_params=_params(1), name="compress_prompt",
    )(proj, proj, posw, w1bd, w2bd)


def _place_heads(tiles, lane):
    chunks = []
    for c in range(B_HEADS // 2):
        t0, t1 = tiles[2 * c], tiles[2 * c + 1]
        if (2 * c) // B_GROUP == 1:
            t0 = pltpu.roll(t0, 64, axis=1)
        if (2 * c + 1) // B_GROUP == 0:
            t1 = pltpu.roll(t1, 64, axis=1)
        chunks.append(jnp.where(lane < 64, t0, t1))
    return jnp.concatenate(chunks, axis=1)


def _masked_softmax_rows(s, ok):
    s = jnp.where(ok, s, NEG)
    m = jnp.max(s, axis=-1, keepdims=True)
    e = jnp.where(ok, jnp.exp(s - m), 0.0)
    den = jnp.maximum(jnp.sum(e, axis=-1, keepdims=True), 1e-30)
    return e * (1.0 / den)


ONES_ROWS = 16


def _values_and_ones(v_ref, idx, kv):
    v = v_ref[idx + (slice(kv * HEAD_DIM, (kv + 1) * HEAD_DIM), slice(None))]
    return jnp.concatenate([v, jnp.ones((ONES_ROWS, v.shape[1]), BF16)], axis=0)


def _head_tile(num, den, kv):
    x = num if den is None else num * (1.0 / den)
    z = jnp.zeros_like(x)
    return jnp.concatenate([x, z] if kv == 0 else [z, x], axis=0).T


def _nsa_prompt_body(q_ref, gt_ref, kc_ref, vct_ref, kw_ref, vwt_ref, ks_ref, vst_ref, mt_ref, eg_ref,
                     sl_ref, out_ref, q6_sc, m_sc, acc_sc, sel_sc, words_sm, idx_sm, *, s_len):
    qb = pl.program_id(1)
    qstart = qb * Q_BLOCK
    n_cmp = s_len // CMP_STRIDE
    q = q_ref[0] * (SCALE * LOG2E)
    lane = _iota((Q_BLOCK, 128), 1)
    tq_row = qstart + _iota((1, Q_BLOCK), 1)
    oc_t, os_t, ow_t = [None] * B_HEADS, [None] * B_HEADS, [None] * B_HEADS

    sub = _iota((128, 128), 0)
    psums = []

    for kv in range(B_KV):
        for g in range(B_GROUP):
            h = kv * B_GROUP + g
            ch = q[:, (h // 2) * 128:(h // 2 + 1) * 128]
            if h % 2 == 1:
                ch = pltpu.roll(ch, 64, axis=1)
            q6_sc[kv, g * 128:(g + 1) * 128, :] = jnp.where(lane < 64, ch, sl_ref[h:h + 1, :]).astype(BF16)

    cok = (CMP_STRIDE * _iota((n_cmp, Q_BLOCK), 0) + (CMP_LEN - 1)) <= tq_row
    q_ok = tq_row >= (CMP_LEN - 1)
    n_wb = WIN_B // Q_BLOCK + 1

    def cmp_scores(kv):
        return lax.dot_general(kc_ref[0, kv], q6_sc[kv], NT, preferred_element_type=F32)

    def cmp_finish(kv, s_t):
        psum = jnp.zeros((n_cmp, Q_BLOCK), F32)
        ps = []
        for g in range(B_GROUP):
            s = jnp.where(cok, s_t[:, g * 128:(g + 1) * 128], NEG)
            e = jnp.exp2(s - jnp.max(s, axis=0, keepdims=True))
            p = e * jnp.where(q_ok, 1.0 / jnp.sum(e, axis=0, keepdims=True), 0.0)
            psum = psum + p
            ps.append(p.astype(BF16))
        oc = jnp.dot(vct_ref[0, kv * HEAD_DIM:(kv + 1) * HEAD_DIM, :], jnp.concatenate(ps, axis=1),
                     preferred_element_type=F32)
        for g in range(B_GROUP):
            oc_t[kv * B_GROUP + g] = _head_tile(oc[:, g * 128:(g + 1) * 128], None, kv)
        psums.append(psum)

    def win_scores(kv):
        kparts, vparts, pparts = [], [], []
        for wb in range(n_wb):
            b_raw = qb - (n_wb - 1) + wb
            b = jnp.maximum(b_raw, 0)
            r0 = pl.multiple_of(b * Q_BLOCK, Q_BLOCK)
            kparts.append(kw_ref[0, kv, pl.ds(r0, Q_BLOCK), :])
            vparts.append(_values_and_ones(vwt_ref, (0, b), kv))
            pparts.append(jnp.where(b_raw >= 0, r0, s_len) + _iota((128, Q_BLOCK), 0))
        s_t = lax.dot_general(jnp.concatenate(kparts, axis=0), q6_sc[kv], NT, preferred_element_type=F32)
        return s_t, jnp.concatenate(vparts, axis=1), jnp.concatenate(pparts, axis=0)

    def win_finish(kv, s_t, v_t, kpos):
        dw = tq_row - kpos
        wok = (dw >= 0) & (dw <= WIN_B)
        ps = []
        for g in range(B_GROUP):
            s = jnp.where(wok, s_t[:, g * 128:(g + 1) * 128], NEG)
            ps.append(jnp.exp2(s - jnp.max(s, axis=0, keepdims=True)).astype(BF16))
        ow = jnp.dot(v_t, jnp.concatenate(ps, axis=1), preferred_element_type=F32)
        for g in range(B_GROUP):
            cols = slice(g * 128, (g + 1) * 128)
            ow_t[kv * B_GROUP + g] = _head_tile(ow[0:HEAD_DIM, cols], ow[HEAD_DIM:HEAD_DIM + 1, cols], kv)

    sc0 = cmp_scores(0)
    sw0 = win_scores(0)
    cmp_finish(0, sc0)
    sc1 = cmp_scores(1)
    win_finish(0, *sw0)
    sw1 = win_scores(1)
    cmp_finish(1, sc1)
    win_finish(1, *sw1)

    mt = mt_ref[...]
    blk = _iota((128, Q_BLOCK), 0)
    ql = _iota((128, Q_BLOCK), 1)
    cur = jnp.where(ql >= SEL_BLOCK, qb * 2 + 1, qb * 2)
    forced = (blk == 0) | (blk == cur) | (blk == cur - 1)
    blkf = blk.astype(F32)
    imps = [jnp.where((blk > cur) | forced, -jnp.inf,
                      sum(jnp.dot(mt, t, preferred_element_type=F32) for t in _split3(psum))) for psum in psums]
    sels = [jnp.where(forced, 1.0, 0.0)] * B_KV
    for _ in range(SEL_TOPK - 3):
        for kv in range(B_KV):
            mx = jnp.max(imps[kv], axis=0, keepdims=True)
            idx = jnp.min(jnp.where(imps[kv] == mx, blkf, 1e9), axis=0, keepdims=True)
            hit = blkf == idx
            sels[kv] = jnp.where(hit, 1.0, sels[kv])
            imps[kv] = jnp.where(hit, -jnp.inf, imps[kv])

    blk_col = _iota((128, 1), 0)
    weight = lax.shift_left(jnp.ones((128, 1), jnp.int32), blk_col & 15).astype(F32)
    for kv in range(B_KV):
        sel_sc[kv] = sels[kv]
        contrib = jnp.max(sels[kv], axis=1, keepdims=True) * weight
        for w in range(8):
            words_sm[kv * 8 + w] = jnp.sum(contrib[16 * w:16 * (w + 1), :]).astype(jnp.int32)

    m_sc[...] = jnp.full(m_sc.shape, NEG, F32)
    acc_sc[...] = jnp.zeros(acc_sc.shape, F32)
    list_len = idx_sm.shape[0] // B_KV
    cnts = []
    for kv in range(B_KV):
        def scan(gi, cnt, kv=kv):
            bits = (words_sm[kv * 8 + (gi >> 3)] >> ((gi & 7) * 2)) & 3
            idx_sm[kv * list_len + cnt] = gi
            return cnt + jnp.where(bits != 0, 1, 0)
        cnts.append(lax.fori_loop(0, qb + 1, scan, 0))
    n_chunks = (jnp.maximum(cnts[0], cnts[1]) + GROUPS_PER_CHUNK - 1) // GROUPS_PER_CHUNK
    for kv in range(B_KV):
        def pad(i, c, kv=kv):
            idx_sm[kv * list_len + i] = -1
            return c
        lax.fori_loop(cnts[kv], n_chunks * GROUPS_PER_CHUNK, pad, 0)

    sub8 = _iota((8, Q_BLOCK), 0)

    def group_hits(kv, gi):
        rows8 = sel_sc[kv, pl.ds(pl.multiple_of((gi >> 2) * 8, 8), 8), :]
        r = (gi & 3) * 2
        lo = jnp.sum(jnp.where(sub8 == r, rows8, 0.0), axis=0, keepdims=True)
        hi = jnp.sum(jnp.where(sub8 == r + 1, rows8, 0.0), axis=0, keepdims=True)
        return jnp.where(sub < SEL_BLOCK, lo, hi)

    def chunk_scores(c, kv):
        kts, vts, hits, kposs = [], [], [], []
        for j in range(GROUPS_PER_CHUNK):
            gi_raw = idx_sm[kv * list_len + c * GROUPS_PER_CHUNK + j]
            gi = jnp.maximum(gi_raw, 0)
            k0 = pl.multiple_of(gi * Q_BLOCK, Q_BLOCK)
            kts.append(ks_ref[0, kv, pl.ds(k0, Q_BLOCK), :])
            vts.append(_values_and_ones(vst_ref, (0, gi), kv))
            hits.append(group_hits(kv, gi))
            kposs.append(jnp.where(gi_raw >= 0, k0, s_len) + _iota((128, 128), 0))
        s_t = lax.dot_general(jnp.concatenate(kts, axis=0), q6_sc[kv], NT, preferred_element_type=F32)
        ok = ((jnp.concatenate(hits, axis=0) > 0.5)
              & (jnp.concatenate(kposs, axis=0) <= qstart + _iota((GROUPS_PER_CHUNK * 128, 128), 1)))
        return s_t, ok, jnp.concatenate(vts, axis=1)

    def chunk_update(kv, s_t, ok, v_t):
        m_old = m_sc[kv]
        m_new, ps = [], []
        for g in range(B_GROUP):
            cols = slice(g * 128, (g + 1) * 128)
            s = jnp.where(ok, s_t[:, cols], NEG)
            mg = jnp.maximum(m_old[:, cols], jnp.max(s, axis=0, keepdims=True))
            ps.append(jnp.exp2(s - mg).astype(BF16))
            m_new.append(mg)
        m_new = jnp.concatenate(m_new, axis=1)
        pv = jnp.dot(v_t, jnp.concatenate(ps, axis=1), preferred_element_type=F32)
        acc_sc[kv] = jnp.exp2(m_old - m_new) * acc_sc[kv] + pv
        m_sc[kv] = m_new

    def chunk(c, carry):
        first = chunk_scores(c, 0)
        second = chunk_scores(c, 1)
        chunk_update(0, *first)
        chunk_update(1, *second)
        return carry

    lax.fori_loop(0, n_chunks, chunk, 0)
    for kv in range(B_KV):
        for g in range(B_GROUP):
            cols = slice(g * 128, (g + 1) * 128)
            os_t[kv * B_GROUP + g] = _head_tile(acc_sc[kv, 0:HEAD_DIM, cols], acc_sc[kv, HEAD_DIM:HEAD_DIM + 1, cols], kv)

    sg = _sigmoid(gt_ref[0])
    eg = eg_ref[...]
    gexp = sum(jnp.dot(t, eg, preferred_element_type=F32) for t in _split3(sg))
    out_ref[0] = (gexp[:, 0:B_WIDTH] * _place_heads(oc_t, lane)
                  + gexp[:, B_WIDTH:2 * B_WIDTH] * _place_heads(os_t, lane)
                  + gexp[:, 2 * B_WIDTH:] * _place_heads(ow_t, lane))


def _nsa_prompt(proj, kc, vc_t, kw, vw_t, ks, vs_t, mt, eg, slope_lanes):
    n, s_len, _ = proj.shape
    n_cmp = s_len // CMP_STRIDE
    n_grp = s_len // Q_BLOCK
    full = lambda a: pl.BlockSpec(a.shape, lambda b, t: (0,) * a.ndim)
    per_n = lambda a: pl.BlockSpec((1,) + a.shape[1:], lambda b, t: (b,) + (0,) * (a.ndim - 1))
    return pl.pallas_call(
        functools.partial(_nsa_prompt_body, s_len=s_len),
        grid=(n, n_grp),
        in_specs=[pl.BlockSpec((1, Q_BLOCK, B_WIDTH), lambda b, t: (b, t, 0)),
                  pl.BlockSpec((1, Q_BLOCK, 128), lambda b, t: (b, t, 22)),
                  per_n(kc), per_n(vc_t), per_n(kw), per_n(vw_t), per_n(ks), per_n(vs_t),
                  full(mt), full(eg), full(slope_lanes)],
        out_specs=pl.BlockSpec((1, Q_BLOCK, B_WIDTH), lambda b, t: (b, t, 0)),
        out_shape=jax.ShapeDtypeStruct((n, s_len, B_WIDTH), F32),
        scratch_shapes=[pltpu.VMEM((B_KV, B_GROUP * Q_BLOCK, 128), BF16), pltpu.VMEM((B_KV, 1, B_GROUP * Q_BLOCK), F32),
                        pltpu.VMEM((B_KV, HEAD_DIM + ONES_ROWS, B_GROUP * Q_BLOCK), F32),
                        pltpu.VMEM((B_KV, 128, Q_BLOCK), F32), pltpu.SMEM((B_KV * 8,), jnp.int32),
                        pltpu.SMEM((B_KV * (n_grp + GROUPS_PER_CHUNK),), jnp.int32)],
        compiler_params=_params(2), name="nsa_prompt",
    )(proj, proj, kc, vc_t, kw, vw_t, ks, vs_t, mt, eg, slope_lanes)


POS_LANE = 64


def _slope_lanes():
    out = np.zeros((B_HEADS, 128), np.float32)
    for h, slope in enumerate(SLOPES_B):
        s = np.float32(np.float64(slope) * LOG2E)
        hi = np.float32(np.asarray(s, np.float32).astype(jnp.bfloat16))
        mid = np.float32(np.asarray(np.float32(s) - hi, np.float32).astype(jnp.bfloat16))
        lo = np.float32(np.asarray(np.float32(s) - hi - mid, np.float32).astype(jnp.bfloat16))
        out[h, POS_LANE:POS_LANE + 6] = [hi, mid, lo, hi, mid, lo]
    return out


def _keys_with_pos(k2, pos):
    n, n_keys, _ = k2.shape
    lo = (pos % Q_BLOCK).astype(BF16)[None, :, None]
    hi = (pos - pos % Q_BLOCK).astype(BF16)[None, :, None]
    tail = jnp.concatenate([jnp.broadcast_to(lo, (n, n_keys, 3)), jnp.broadcast_to(hi, (n, n_keys, 3)),
                            jnp.zeros((n, n_keys, 128 - POS_LANE - 6), BF16)], axis=-1)
    return jnp.stack([jnp.concatenate([k2[..., kv * 64:(kv + 1) * 64], tail], axis=-1) for kv in range(B_KV)], axis=1)


def _values_by_group(v2):
    n, s_len, _ = v2.shape
    return v2.reshape(n, s_len // Q_BLOCK, Q_BLOCK, 128).transpose(0, 1, 3, 2)


def _heads_rows(vec, n_rows, width):
    r = _iota((n_rows, width), 0)
    l = _iota((n_rows, width), 1)
    hm = (l >= r * HEAD_DIM) & (l < r * HEAD_DIM + HEAD_DIM)
    return jnp.where(hm, jnp.broadcast_to(vec, (n_rows, width)), 0.0), hm


def _bf(x):
    return x.astype(BF16).astype(F32)


def _row_consts(n_rows, vals):
    r = _iota((n_rows, 1), 0)
    out = jnp.zeros((n_rows, 1), F32)
    for i, v in enumerate(vals):
        out = jnp.where(r == i, v, out)
    return out


def _rows_last(cache):
    nd = cache.ndim
    return cache.transpose(tuple(range(nd - 4)) + (nd - 3, nd - 2, nd - 1, nd - 4))


def _kv_t(ref, t):
    x = ref[0, 0, t]
    return x.reshape(x.shape[0] * x.shape[1], x.shape[2]).astype(BF16)


def _sample_cross(qx_row, k_t, v_t):
    q8, hm = _heads_rows(qx_row, 8, X_WIDTH)
    s = jnp.dot(q8.astype(BF16), k_t, preferred_element_type=F32) * SCALE
    e = jnp.exp(s - jnp.max(s, axis=-1, keepdims=True))
    p = (e * (1.0 / jnp.sum(e, axis=-1, keepdims=True))).astype(BF16)
    o8 = lax.dot_general(p, v_t, NT, preferred_element_type=F32)
    return jnp.sum(jnp.where(hm, o8, 0.0), axis=0, keepdims=True)


def _sample_a_body(row_ref, c0_ref, c1_ref, c2_ref, mkv_ref, z_ref):
    row = row_ref[0]
    outs, lses = [], []
    hm = None
    for g, (win, dil) in enumerate(A_PATTERNS):
        cref = (c0_ref, c1_ref, c2_ref)[g]
        q8, hm = _heads_rows(row[:, g * 256:(g + 1) * 256], 8, A_WIDTH)
        knew = row[:, 768 + g * 256:768 + (g + 1) * 256]
        vnew = row[:, 1536 + g * 256:1536 + (g + 1) * 256]
        q8b = q8.astype(BF16)
        slope = _row_consts(8, SLOPES_A[g * 4:(g + 1) * 4])
        s = jnp.dot(q8b, _kv_t(cref, 0), preferred_element_type=F32) * SCALE
        r = _iota((8, win), 1)
        s = jnp.where((r & (dil - 1)) == 0, s - slope * (win - r).astype(F32), NEG)
        s_new = jnp.sum(q8b.astype(F32) * _bf(knew), axis=-1, keepdims=True) * SCALE
        m = jnp.maximum(jnp.max(s, axis=-1, keepdims=True), s_new)
        e = jnp.exp(s - m)
        e_new = jnp.exp(s_new - m)
        den = jnp.sum(e, axis=-1, keepdims=True) + e_new
        inv = 1.0 / den
        o8 = (lax.dot_general((e * inv).astype(BF16), _kv_t(cref, 1), NT, preferred_element_type=F32)
              + _bf(e_new * inv) * _bf(vnew))
        outs.append(o8)
        lses.append(m + jnp.log(den))
    mx = jnp.maximum(jnp.maximum(lses[0], lses[1]), lses[2])
    ws = [jnp.exp(l - mx) for l in lses]
    mix8 = (ws[0] * outs[0] + ws[1] * outs[1] + ws[2] * outs[2]) / (ws[0] + ws[1] + ws[2])
    mix = jnp.sum(jnp.where(hm, mix8, 0.0), axis=0, keepdims=True)
    cx = _sample_cross(row[:, 2560:2816], _kv_t(mkv_ref, 0), _kv_t(mkv_ref, 1))
    z_ref[0] = jnp.concatenate([mix * _silu(row[:, 2304:2560]), cx * _silu(row[:, 2816:3072])], axis=1)


def _layer_block(cache_t, layer):
    return pl.BlockSpec((1, 1) + cache_t.shape[2:], lambda b, *_: (layer, b, 0, 0, 0, 0))


def _sample_a(proj_s, caches_t, mem_t, li, i):
    ns = proj_s.shape[0]
    row3 = proj_s.reshape(ns, 1, W_IN_A)
    return pl.pallas_call(
        _sample_a_body, grid=(ns,),
        in_specs=[pl.BlockSpec((1, 1, W_IN_A), lambda b: (b, 0, 0))] + [_layer_block(c, li) for c in caches_t]
                 + [_layer_block(mem_t, i)],
        out_specs=pl.BlockSpec((1, 1, A_WIDTH + X_WIDTH), lambda b: (b, 0, 0)),
        out_shape=jax.ShapeDtypeStruct((ns, 1, A_WIDTH + X_WIDTH), F32), compiler_params=_params(1), name="sample_a",
    )(row3, *caches_t, mem_t).reshape(ns, A_WIDTH + X_WIDTH)


def _q16(row):
    r = _iota((16, 128), 0)
    l = _iota((16, 128), 1)
    acc = jnp.zeros((16, 128), F32)
    for c in range(B_HEADS // 2):
        ch = jnp.broadcast_to(row[:, c * 128:(c + 1) * 128], (16, 128))
        rolled = pltpu.roll(ch, 64, axis=1)
        for hh in range(2):
            h = 2 * c + hh
            kv = h // B_GROUP
            lm = (l < 64) if kv == 0 else (l >= 64)
            acc = jnp.where((r == h) & lm, ch if hh == kv else rolled, acc)
    return acc * SCALE


def _sample_b1_body(pt_ref, row_ref, pos_ref, w1_ref, w2_ref, mm_ref, pages_ref, oc_ref, sel_ref,
                    buf, rows_sc, sem, imp_sc, *, li, n_pages, ns):
    n = pl.program_id(0)
    past = n_pages * PAGE_SIZE
    n_cmp = past // CMP_STRIDE

    def page_copy(page, p, slot):
        return pltpu.make_async_copy(pages_ref.at[page, li, pl.ds(0, 2)], buf.at[slot, p], sem.at[slot])

    def fetch(nn, slot):
        def body(p, c):
            page_copy(pt_ref[nn * n_pages + p], p, slot).start()
            return c
        lax.fori_loop(0, n_pages, body, 0)

    @pl.when(n == 0)
    def _():
        fetch(0, 0)

    @pl.when(n + 1 < ns)
    def _():
        fetch(n + 1, (n + 1) % 2)

    slot = n % 2

    def wbody(p, c):
        page_copy(0, p, slot).wait()
        return c
    lax.fori_loop(0, n_pages, wbody, 0)

    pages_per_step = 8

    def to_rows(it, c):
        for k in range(pages_per_step):
            p = it * pages_per_step + k
            r0 = pl.multiple_of(p * PAGE_SIZE, PAGE_SIZE)
            for t in range(2):
                rows_sc[t, pl.ds(r0, PAGE_SIZE), :] = buf[slot, p, t].reshape(2 * HEAD_DIM, PAGE_SIZE).T
        return c
    lax.fori_loop(0, n_pages // pages_per_step, to_rows, 0)

    load = lambda t, l: rows_sc[t, pl.ds(l, n_cmp, stride=CMP_STRIDE), :]
    ck, cv = _compress_rows(load, pos_ref, w1_ref, w2_ref, n_cmp)

    q16 = _q16(row_ref[0]).astype(BF16)
    slope = _row_consts(16, SLOPES_B)
    s = lax.dot_general(q16, ck.astype(BF16), NT, preferred_element_type=F32)
    cend = CMP_STRIDE * _iota((1, n_cmp), 1) + (CMP_LEN - 1)
    p = _masked_softmax_rows(s - slope * (past - cend).astype(F32), cend <= past)
    oc_ref[0] = jnp.dot(p.astype(BF16), cv.astype(BF16), preferred_element_type=F32)

    r16 = _iota((16, n_cmp), 0)
    ps0 = jnp.sum(jnp.where(r16 < B_GROUP, p, 0.0), axis=0, keepdims=True)
    ps1 = jnp.sum(jnp.where((r16 >= B_GROUP) & (r16 < B_HEADS), p, 0.0), axis=0, keepdims=True)
    psum = jnp.concatenate([ps0, ps1, jnp.zeros((6, n_cmp), F32)], axis=0)
    mm = mm_ref[...]
    imp = sum(jnp.dot(t, mm, preferred_element_type=F32) for t in _split3(psum))
    blk = _iota((8, 256), 1)
    cur = past // SEL_BLOCK
    forced = (blk == 0) | (blk == cur) | (blk == cur - 1)
    imp_sc[n] = jnp.where(blk > cur, -jnp.inf, jnp.where(forced, FORCE_SCORE, imp))

    @pl.when(n == ns - 1)
    def _():
        impa = imp_sc[...]
        blkf = _iota(impa.shape, 2).astype(F32)
        lane = _iota((ns, 8, 128), 2)
        out = jnp.zeros((ns, 8, 128), F32)
        for r in range(SEL_TOPK):
            mx = jnp.max(impa, axis=-1, keepdims=True)
            idx = jnp.min(jnp.where(impa == mx, blkf, 1e9), axis=-1, keepdims=True)
            impa = jnp.where(blkf == idx, -jnp.inf, impa)
            out = jnp.where(lane == r, idx, out)
        sel_ref[...] = out.astype(jnp.int32)


def _sample_b1(page_table, proj_s, posw, w1bd, w2bd, mm, pages_t, li):
    ns, n_pages = page_table.shape
    past = n_pages * PAGE_SIZE
    row3 = proj_s.reshape(ns, 1, W_IN_B_PAD)
    full = lambda a: pl.BlockSpec(a.shape, lambda b, pt: (0,) * a.ndim)
    grid_spec = pltpu.PrefetchScalarGridSpec(
        num_scalar_prefetch=1, grid=(ns,),
        in_specs=[pl.BlockSpec((1, 1, W_IN_B_PAD), lambda b, pt: (b, 0, 0)), full(posw), full(w1bd), full(w2bd),
                  full(mm), pl.BlockSpec(memory_space=pl.ANY)],
        out_specs=[pl.BlockSpec((1, 16, 128), lambda b, pt: (b, 0, 0)),
                   pl.BlockSpec((ns, 8, 128), lambda b, pt: (0, 0, 0))],
        scratch_shapes=[pltpu.VMEM((2, n_pages, 2, B_KV, HEAD_DIM, PAGE_SIZE), F32),
                        pltpu.VMEM((2, past, 128), F32), pltpu.SemaphoreType.DMA((2,)),
                        pltpu.VMEM((ns, 8, 256), F32)])
    return pl.pallas_call(
        functools.partial(_sample_b1_body, li=li, n_pages=n_pages, ns=ns),
        grid_spec=grid_spec,
        out_shape=[jax.ShapeDtypeStruct((ns, 16, 128), F32), jax.ShapeDtypeStruct((ns, 8, 128), jnp.int32)],
        compiler_params=_params(1), name="sample_b1",
    )(page_table.reshape(-1), row3, posw, w1bd, w2bd, mm, pages_t)


def _sample_b2_body(pt_ref, sf_ref, row_ref, oc_ref, sel_ref, win_ref, mkv_ref, e16_ref, pages_ref, z_ref,
                    buf, sem, *, li, n_pages, ns):
    n = pl.program_id(0)
    past = n_pages * PAGE_SIZE
    n_blk = past // SEL_BLOCK
    per_page = PAGE_SIZE // SEL_BLOCK
    n_sel = B_KV * SEL_TOPK

    def blk_copies(page, kv, r, slot):
        return [pltpu.make_async_copy(pages_ref.at[page, li, 2 + t, kv],
                                      buf.at[slot, t, kv, :, pl.ds(r * PAGE_SIZE, PAGE_SIZE)], sem.at[slot])
                for t in range(2)]

    def fetch(nn, slot):
        for kv in range(B_KV):
            for r in range(SEL_TOPK):
                j = jnp.minimum(sf_ref[nn * n_sel + kv * SEL_TOPK + r], n_blk - 1)
                for cp in blk_copies(pt_ref[nn * n_pages + j // per_page], kv, r, slot):
                    cp.start()

    @pl.when(n == 0)
    def _():
        fetch(0, 0)

    @pl.when(n + 1 < ns)
    def _():
        fetch(n + 1, (n + 1) % 2)

    slot = n % 2
    for kv in range(B_KV):
        for r in range(SEL_TOPK):
            for cp in blk_copies(0, kv, r, slot):
                cp.wait()

    row = row_ref[0]
    q16f = _q16(row)
    q16 = q16f.astype(BF16)
    q16r = q16.astype(F32)
    slope = _row_consts(16, SLOPES_B)
    r16 = _iota((16, 128), 0)

    def new_key(col):
        kn = _bf(row[:, col:col + 128])
        return jnp.sum(q16r * kn, axis=-1, keepdims=True)

    def attend(s, s_new, v_t):
        m = jnp.maximum(jnp.max(s, axis=-1, keepdims=True), s_new)
        e = jnp.exp(s - m)
        e_new = jnp.exp(s_new - m)
        inv = 1.0 / (jnp.sum(e, axis=-1, keepdims=True) + e_new)
        return lax.dot_general((e * inv).astype(BF16), v_t, NT, preferred_element_type=F32), _bf(e_new * inv)

    n_keys = SEL_TOPK * PAGE_SIZE
    jv = jnp.dot(sel_ref[0].astype(F32).astype(BF16), e16_ref[...], preferred_element_type=F32).astype(jnp.int32)
    in_page = jnp.bitwise_and(_iota((8, n_keys), 1), PAGE_SIZE - 1)
    ok_sel = ((in_page >> 6) == (jv & (per_page - 1))) & (jv < n_blk)
    dist_sel = (past - ((jv >> 1) * PAGE_SIZE + in_page)).astype(F32)
    q64 = jnp.where(r16[:, :HEAD_DIM] < B_GROUP, q16f[:, :HEAD_DIM], q16f[:, HEAD_DIM:]).astype(BF16)
    s_new = new_key(1024)
    outs, p_news = [], []
    for kv in range(B_KV):
        s = jnp.dot(q64, buf[slot, 0, kv].astype(BF16), preferred_element_type=F32) - slope * dist_sel[kv:kv + 1]
        o, p_new = attend(jnp.where(ok_sel[kv:kv + 1], s, NEG), s_new, buf[slot, 1, kv].astype(BF16))
        outs.append(o)
        p_news.append(p_new)
    p_new = jnp.where(_iota((16, 1), 0) < B_GROUP, p_news[0], p_news[1])
    os16 = jnp.concatenate(outs, axis=1) + p_new * _bf(row[:, 1152:1280])

    lb = win_ref.shape[-1]
    dw = (lb - _iota((1, lb), 1)).astype(F32)
    s = jnp.dot(q16, _kv_t(win_ref, 0), preferred_element_type=F32) - slope * dw
    ow16, p_new = attend(s, new_key(1280), _kv_t(win_ref, 1))
    ow16 = ow16 + p_new * _bf(row[:, 1408:1536])

    sg = jnp.broadcast_to(_sigmoid(row[:, 2816:2944]), (16, 128))
    l16 = _iota((16, 128), 1)
    gate = lambda b: jnp.sum(jnp.where(l16 == r16 * 3 + b, sg, 0.0), axis=-1, keepdims=True)
    out16 = gate(0) * oc_ref[0] + gate(1) * os16 + gate(2) * ow16
    lane1 = _iota((1, 128), 1)
    mix = _place_heads([out16[h:h + 1, :] for h in range(B_HEADS)], lane1)

    cx = _sample_cross(row[:, 2304:2560], _kv_t(mkv_ref, 0), _kv_t(mkv_ref, 1))
    z_ref[0] = jnp.concatenate([mix * _silu(row[:, 1536:2304]), cx * _silu(row[:, 2560:2816])], axis=1)


def _sample_b2(page_table, sel, proj_s, oc, win_t, mem_t, e16, pages_t, li, i):
    ns, n_pages = page_table.shape
    row3 = proj_s.reshape(ns, 1, W_IN_B_PAD)
    full = lambda a: pl.BlockSpec(a.shape, lambda b, pt, sf: (0,) * a.ndim)
    per = lambda a: pl.BlockSpec((1,) + a.shape[1:], lambda b, pt, sf: (b,) + (0,) * (a.ndim - 1))
    grid_spec = pltpu.PrefetchScalarGridSpec(
        num_scalar_prefetch=2, grid=(ns,),
        in_specs=[per(row3), per(oc), per(sel), _layer_block(win_t, li), _layer_block(mem_t, i), full(e16),
                  pl.BlockSpec(memory_space=pl.ANY)],
        out_specs=pl.BlockSpec((1, 1, B_WIDTH + X_WIDTH), lambda b, pt, sf: (b, 0, 0)),
        scratch_shapes=[pltpu.VMEM((2, 2, B_KV, HEAD_DIM, SEL_TOPK * PAGE_SIZE), F32), pltpu.SemaphoreType.DMA((2,))])
    return pl.pallas_call(
        functools.partial(_sample_b2_body, li=li, n_pages=n_pages, ns=ns),
        grid_spec=grid_spec,
        out_shape=jax.ShapeDtypeStruct((ns, 1, B_WIDTH + X_WIDTH), F32),
        compiler_params=_params(1), name="sample_b2",
    )(page_table.reshape(-1), sel[:, :B_KV, :SEL_TOPK].reshape(-1), row3, oc, sel, win_t, mem_t, e16, pages_t
      ).reshape(ns, B_WIDTH + X_WIDTH)


def _importance_matrix(n_cmp, n_cols):
    c = np.arange(n_cmp)[:, None]
    j = np.arange(n_cols)[None, :]
    per = SEL_BLOCK // CMP_STRIDE
    m = ((c >= per * j) & (c <= per * j + per - 1)).astype(np.float32)
    m = m + ((c + 1 >= per * j) & (c + 1 <= per * j + per - 1)).astype(np.float32)
    m[n_cmp - 1, :] = 0.0
    return m


def _gate_expand():
    eg = np.zeros((128, 3 * B_WIDTH), np.float32)
    for h in range(B_HEADS):
        for b in range(3):
            eg[h * 3 + b, b * B_WIDTH + h * HEAD_DIM:b * B_WIDTH + (h + 1) * HEAD_DIM] = 1.0
    return eg


def _block_expand(n_rows, n_keys):
    return (np.arange(n_keys)[None, :] // SEL_BLOCK == np.arange(n_rows)[:, None]).astype(np.float32)


def _compress_weights(cmp_pos, cmp_w1, cmp_w2):
    eye = jnp.eye(B_KV, dtype=F32)
    posw = jnp.concatenate([cmp_pos, cmp_pos], axis=-1)
    w1 = cmp_w1.reshape(2, CMP_LEN, HEAD_DIM, CMP_HIDDEN)
    w1bd = jnp.einsum('tlek,jm->tljemk', w1, eye).reshape(2, CMP_LEN // 2, 4 * HEAD_DIM, 2 * CMP_HIDDEN)
    w2bd = jnp.einsum('tke,jm->tjkme', cmp_w2, eye).reshape(2, 2 * CMP_HIDDEN, 2 * HEAD_DIM)
    return posw, w1bd.astype(BF16), w2bd.astype(BF16)


def _permute_w_in_b(w):
    d = w.shape[0]
    return jnp.concatenate([w[:, :1536], w[:, 1572:W_IN_B], w[:, 1536:1572],
                            jnp.zeros((d, W_IN_B_PAD - W_IN_B), w.dtype)], axis=1)


def kernel(x_prompt, x_sample, cache_mem_kv, cache_a_w128_kv, cache_a_w512_kv, cache_a_w2048_kv, cache_b_pages,
           cache_b_win_kv, page_table, mem_prompt, norm_pre, norm_post, norm_mem, w_mem_kv, w_in_a, w_out_a,
           w_in_b, w_out_b, cmp_pos, cmp_w1, cmp_w2):
    n, s_len, d = x_prompt.shape
    ns = x_sample.shape[0]
    depth = norm_pre.shape[0]
    n_pool, page_size, n_lb = cache_b_pages.shape[:3]
    n_pages = page_table.shape[1]
    past = n_pages * page_size
    assert d == D_MODEL and x_sample.shape[1] == 1 and page_size == PAGE_SIZE
    assert s_len % 2048 == 0 and past % 2048 == 0 and s_len >= 2048 and past >= 2048
    caches_a = (cache_a_w128_kv, cache_a_w512_kv, cache_a_w2048_kv)
    for c, (win, _) in zip(caches_a, A_PATTERNS):
        assert c.shape[2] == win
    assert cache_b_win_kv.shape[2] == WIN_B

    tm = 512
    n_cmp_p = s_len // CMP_STRIDE
    n_cmp_s = past // CMP_STRIDE
    slope_lanes = jnp.asarray(_slope_lanes(), F32)
    mt = jnp.asarray(_importance_matrix(n_cmp_p, 128).T, BF16)
    mm = jnp.asarray(_importance_matrix(n_cmp_s, 256), BF16)
    eg = jnp.asarray(_gate_expand(), BF16)
    e16 = jnp.asarray((np.arange(SEL_TOPK * PAGE_SIZE)[None, :] // PAGE_SIZE == np.arange(128)[:, None]), BF16)
    pages_t = cache_b_pages.transpose(0, 2, 3, 4, 5, 1)
    caches_a_t = [_rows_last(c) for c in caches_a]
    mem_t = _rows_last(cache_mem_kv)
    win_t = _rows_last(cache_b_win_kv)

    xp = x_prompt
    xs = x_sample.reshape(ns, d)
    mem2 = mem_prompt.reshape(n * N_MEM, d)
    mem_new = []
    a_p = [[] for _ in A_PATTERNS]
    a_s = [[] for _ in A_PATTERNS]
    b_p, b_s, bw_p, bw_s = [], [], [], []
    for i in range(depth):
        li = i // 2
        mkv_p = _rms_proj(mem2, norm_mem[i], w_mem_kv[i].astype(BF16), tm=N_MEM).reshape(n, N_MEM, 2 * X_WIDTH)
        mem_new.append(mkv_p.reshape(n, N_MEM, 2, 4, HEAD_DIM))
        if i % 2 == 0:
            w_in = w_in_a[li].astype(BF16)
            w_out = w_out_a[li].astype(BF16)
            proj_p = _rms_proj(xp.reshape(n * s_len, d), norm_pre[i], w_in, tm=tm).reshape(n, s_len, W_IN_A)
            proj_s = _rms_proj(xs, norm_pre[i], w_in, tm=ns)
            ols = []
            for g, (win, dil) in enumerate(A_PATTERNS):
                ols += _a_prompt_group(proj_p, g, dil)
                kv_p = proj_p[:, s_len - win:, 768:2304].reshape(n, win, 2, 3, 4, HEAD_DIM)[:, :, :, g]
                a_p[g].append(kv_p)
                a_s[g].append(proj_s[:, 768:2304].reshape(ns, 1, 2, 3, 4, HEAD_DIM)[:, :, :, g])
            xp = _finish_a(xp, ols, proj_p, mkv_p, w_out, norm_post[i], tm=256)
            z = _sample_a(proj_s, caches_a_t, mem_t, li, i)
            xs = _tail(xs, z, w_out, norm_post[i])
        else:
            w_in = _permute_w_in_b(w_in_b[li]).astype(BF16)
            w_out = w_out_b[li].astype(BF16)
            posw, w1bd, w2bd = _compress_weights(cmp_pos[li], cmp_w1[li], cmp_w2[li])
            proj_p, kvs = _rms_proj(xp.reshape(n * s_len, d), norm_pre[i], w_in, tm=tm, side=(1024, 512))
            proj_p = proj_p.reshape(n, s_len, W_IN_B_PAD)
            kvs = kvs.reshape(n, s_len, 512)
            proj_s = _rms_proj(xs, norm_pre[i], w_in, tm=ns)
            cmpd = _compress_prompt(proj_p, posw, w1bd, w2bd)
            pos = jnp.arange(s_len, dtype=jnp.int32)
            cend = CMP_STRIDE * jnp.arange(n_cmp_p, dtype=jnp.int32) + (CMP_LEN - 1)
            mix = _nsa_prompt(proj_p, _keys_with_pos(cmpd[:, :, 0:128], cend), cmpd[:, :, 128:256].transpose(0, 2, 1),
                              _keys_with_pos(kvs[:, :, 256:384], pos), _values_by_group(kvs[:, :, 384:512]),
                              _keys_with_pos(kvs[:, :, 0:128], pos), _values_by_group(kvs[:, :, 128:256]),
                              mt, eg, slope_lanes)
            xp = _finish_b(xp, mix, proj_p, mkv_p, w_out, norm_post[i], tm=256)
            oc, sel = _sample_b1(page_table, proj_s, posw, w1bd, w2bd, mm, pages_t, li)
            z = _sample_b2(page_table, sel, proj_s, oc, win_t, mem_t, e16, pages_t, li, i)
            xs = _tail(xs, z, w_out, norm_post[i])
            b_p.append(proj_p[:, :, 768:1280].reshape(n, s_len, 4, B_KV, HEAD_DIM))
            bw_p.append(proj_p[:, s_len - WIN_B:, 1280:1536].reshape(n, WIN_B, 2, B_KV, HEAD_DIM))
            b_s.append(proj_s[:, 768:1280].reshape(ns, 1, 4, B_KV, HEAD_DIM))
            bw_s.append(proj_s[:, 1280:1536].reshape(ns, 1, 2, B_KV, HEAD_DIM))
    return (xp, xs.reshape(ns, 1, d), jnp.stack(mem_new, axis=0),
            jnp.stack(a_p[0], axis=0), jnp.stack(a_p[1], axis=0), jnp.stack(a_p[2], axis=0),
            jnp.stack(b_p, axis=2), jnp.stack(bw_p, axis=0),
            jnp.stack(a_s[0], axis=0), jnp.stack(a_s[1], axis=0), jnp.stack(a_s[2], axis=0),
            jnp.stack(b_s, axis=2), jnp.stack(bw_s, axis=0))
```

```python
import functools

import numpy as np
import jax
import jax.numpy as jnp
from jax import lax
from jax.experimental import pallas as pl
from jax.experimental.pallas import tpu as pltpu

F32 = jnp.float32
BF16 = jnp.bfloat16

D_MODEL = 1024
HEAD_DIM = 64
SCALE = HEAD_DIM ** -0.5
LOG2E = 1.4426950408889634
RMS_EPS = 1e-6
N_MEM = 256
X_WIDTH = 256
A_PATTERNS = ((128, 1), (512, 4), (2048, 16))
A_WIDTH = 256
W_IN_A = 3072
B_HEADS = 12
B_KV = 2
B_GROUP = 6
B_WIDTH = 768
W_IN_B = 2852
W_IN_B_PAD = 2944
CMP_LEN = 32
CMP_STRIDE = 16
CMP_HIDDEN = 128
SEL_BLOCK = 64
SEL_TOPK = 16
WIN_B = 512
Q_BLOCK = 128
FORCE_SCORE = 1e4
PAGE_SIZE = 128
NEG = -1e30
GROUPS_PER_CHUNK = 4
VMEM_LIMIT = 56 * 1024 * 1024

NT = (((1,), (1,)), ((), ()))


def _alibi(n):
    k = np.arange(1, n + 1, dtype=np.float32)
    return [float(v) for v in np.float32(2.0) ** (np.float32(-8.0) * k / np.float32(n))]


SLOPES_A = _alibi(12)
SLOPES_B = _alibi(12)


def _params(n_axes):
    return pltpu.CompilerParams(dimension_semantics=("arbitrary",) * n_axes, vmem_limit_bytes=VMEM_LIMIT)


def _sigmoid(x):
    return 1.0 / (1.0 + jnp.exp(-x))


def _silu(x):
    return x * _sigmoid(x)


def _iota(shape, dim):
    return lax.broadcasted_iota(jnp.int32, shape, dim)


def _split3(x):
    hi = x.astype(BF16)
    r1 = x - hi.astype(F32)
    mid = r1.astype(BF16)
    lo = (r1 - mid.astype(F32)).astype(BF16)
    return hi, mid, lo


def _rms_proj_body(x_ref, g_ref, w_ref, o_ref, *side_ref, side):
    x = x_ref[...]
    y = x * lax.rsqrt(jnp.mean(x * x, axis=-1, keepdims=True) + RMS_EPS)
    y = (y * g_ref[...]).astype(BF16)
    o = jnp.dot(y, w_ref[...], preferred_element_type=F32)
    o_ref[...] = o
    if side is not None:
        side_ref[0][...] = o[:, side[0]:side[0] + side[1]].astype(BF16)


def _rms_proj(x, g, w, tm, side=None):
    m, d = x.shape
    n = w.shape[1]
    out_specs = [pl.BlockSpec((tm, n), lambda i: (i, 0))]
    out_shape = [jax.ShapeDtypeStruct((m, n), F32)]
    if side is not None:
        out_specs.append(pl.BlockSpec((tm, side[1]), lambda i: (i, 0)))
        out_shape.append(jax.ShapeDtypeStruct((m, side[1]), BF16))
    outs = pl.pallas_call(
        functools.partial(_rms_proj_body, side=side),
        grid=(m // tm,),
        in_specs=[pl.BlockSpec((tm, d), lambda i: (i, 0)),
                  pl.BlockSpec((1, d), lambda i: (0, 0)),
                  pl.BlockSpec((d, n), lambda i: (0, 0))],
        out_specs=out_specs,
        out_shape=out_shape,
        compiler_params=_params(1),
        name="rms_proj",
    )(x, g.reshape(1, d), w)
    return outs[0] if side is None else outs


A_SPAN = 2048


def _a_prompt_body(*refs, dil, slopes):
    ins, outs = refs[:10], refs[10:]
    t = pl.program_id(1)
    blk_rows = 128 * dil
    n_ub = A_SPAN // blk_rows
    i = _iota((128, 256), 0)
    j = _iota((128, 256), 1)
    back = 128 + i - j
    in_band = (back >= 0) & (back <= 128)
    dist = (back * dil).astype(F32)
    lane = _iota((128, 128), 1)

    def rows(ref, start):
        return ref[0, pl.ds(start, 128, stride=dil), :] if dil > 1 else ref[0, pl.ds(start, 128), :]

    def block(pair, ub, r, first):
        q_ref, k_ref, kp_ref, v_ref, vp_ref = ins[pair * 5:(pair + 1) * 5]
        o_ref, l_ref = outs[pair * 2:(pair + 1) * 2]
        start = ub * blk_rows + r
        if first:
            k_prev, v_prev = rows(kp_ref, r), rows(vp_ref, r)
            valid = in_band & (j >= jnp.where(t > 0, 0, 128))
        else:
            k_prev, v_prev = rows(k_ref, start - blk_rows), rows(v_ref, start - blk_rows)
            valid = in_band
        qp = rows(q_ref, start)
        kp = jnp.concatenate([k_prev, rows(k_ref, start)], axis=0).astype(BF16)
        vp = jnp.concatenate([v_prev, rows(v_ref, start)], axis=0).astype(BF16)
        o_pair = None
        l_pair = None
        for hh in range(2):
            hm = (lane < 64) if hh == 0 else (lane >= 64)
            qm = jnp.where(hm, qp, 0.0).astype(BF16)
            s = lax.dot_general(qm, kp, NT, preferred_element_type=F32) * SCALE
            s = jnp.where(valid, s - slopes[pair * 2 + hh] * dist, NEG)
            m = jnp.max(s, axis=-1, keepdims=True)
            e = jnp.exp(s - m)
            den = jnp.sum(e, axis=-1, keepdims=True)
            oh = jnp.dot((e * (1.0 / den)).astype(BF16), vp, preferred_element_type=F32)
            lh = jnp.broadcast_to(m + jnp.log(den), (128, 128))
            o_pair = oh if hh == 0 else jnp.where(lane < 64, o_pair, oh)
            l_pair = lh if hh == 0 else jnp.where(lane < 64, l_pair, lh)
        if dil > 1:
            o_ref[0, pl.ds(start, 128, stride=dil), :] = o_pair
            l_ref[0, pl.ds(start, 128, stride=dil), :] = l_pair
        else:
            o_ref[0, pl.ds(start, 128), :] = o_pair
            l_ref[0, pl.ds(start, 128), :] = l_pair

    def run(count, fn):
        if count == 0:
            return
        unroll = next(c for c in (4, 5, 3, 2, 1) if count % c == 0)

        def body(it, c):
            for k in range(unroll):
                fn(it * unroll + k)
            return c
        lax.fori_loop(0, count // unroll, body, 0)

    for pair in range(2):
        run(dil, lambda r, pair=pair: block(pair, 0, r, True))
        run((n_ub - 1) * dil, lambda idx, pair=pair: block(pair, 1 + idx // dil, idx % dil, False))


def _a_prompt_group(proj, g, dil):
    n, s_len, _ = proj.shape
    blk_rows = 128 * dil
    per_span = A_SPAN // blk_rows
    body = functools.partial(_a_prompt_body, dil=dil, slopes=tuple(SLOPES_A[g * 4:(g + 1) * 4]))
    cur = lambda col: pl.BlockSpec((1, A_SPAN, 128), lambda b, t: (b, t, col))
    prev = lambda col: pl.BlockSpec((1, blk_rows, 128), lambda b, t: (b, jnp.maximum(t * per_span - 1, 0), col))
    in_specs = []
    for pair in range(2):
        qc, kc, vc = 2 * g + pair, 6 + 2 * g + pair, 12 + 2 * g + pair
        in_specs += [cur(qc), cur(kc), prev(kc), cur(vc), prev(vc)]
    out_spec = pl.BlockSpec((1, A_SPAN, 128), lambda b, t: (b, t, 0))
    return pl.pallas_call(
        body,
        grid=(n, s_len // A_SPAN),
        in_specs=in_specs,
        out_specs=[out_spec] * 4,
        out_shape=[jax.ShapeDtypeStruct((n, s_len, 128), F32)] * 4,
        compiler_params=_params(2),
        name=f"a_prompt_g{g}",
    )(*([proj] * 10))


def _cross_rows(qx, kx, vx):
    tm = qx.shape[0]
    lane = _iota((tm, 128), 1)
    outs = []
    for pair in range(2):
        sl = slice(pair * 128, (pair + 1) * 128)
        qp, kp, vp = qx[:, sl], kx[:, sl], vx[:, sl]
        o_pair = None
        for hh in range(2):
            hm = (lane < 64) if hh == 0 else (lane >= 64)
            qm = jnp.where(hm, qp, 0.0).astype(BF16)
            s = lax.dot_general(qm, kp, NT, preferred_element_type=F32) * SCALE
            m = jnp.max(s, axis=-1, keepdims=True)
            e = jnp.exp(s - m)
            p = (e / jnp.sum(e, axis=-1, keepdims=True)).astype(BF16)
            oh = jnp.dot(p, vp, preferred_element_type=F32)
            o_pair = oh if hh == 0 else jnp.where(lane < 64, o_pair, oh)
        outs.append(o_pair)
    return jnp.concatenate(outs, axis=1)


def _out_norm_residual(x, z, w, g):
    y = jnp.dot(z.astype(BF16), w, preferred_element_type=F32)
    y = y * lax.rsqrt(jnp.mean(y * y, axis=-1, keepdims=True) + RMS_EPS)
    return x + y * g


def _finish_a_body(x_ref, *refs):
    gm_ref, qx_ref, gx_ref, mkv_ref, w_ref, g_ref, out_ref = refs[12:]
    mixes = []
    for pair in range(2):
        os_ = [refs[4 * g + 2 * pair][0] for g in range(3)]
        ls_ = [refs[4 * g + 2 * pair + 1][0] for g in range(3)]
        m = jnp.maximum(jnp.maximum(ls_[0], ls_[1]), ls_[2])
        es = [jnp.exp(l - m) for l in ls_]
        mixes.append((es[0] * os_[0] + es[1] * os_[1] + es[2] * os_[2]) / (es[0] + es[1] + es[2]))
    mix = jnp.concatenate(mixes, axis=1)
    mkv = mkv_ref[0]
    cx = _cross_rows(qx_ref[0], mkv[:, :X_WIDTH].astype(BF16), mkv[:, X_WIDTH:].astype(BF16))
    z = jnp.concatenate([mix * _silu(gm_ref[0]), cx * _silu(gx_ref[0])], axis=1)
    out_ref[0] = _out_norm_residual(x_ref[0], z, w_ref[...], g_ref[...])


def _finish_b_body(x_ref, mix_ref, gm_ref, qx_ref, gx_ref, mkv_ref, w_ref, g_ref, out_ref):
    mkv = mkv_ref[0]
    cx = _cross_rows(qx_ref[0], mkv[:, :X_WIDTH].astype(BF16), mkv[:, X_WIDTH:].astype(BF16))
    z = jnp.concatenate([mix_ref[0] * _silu(gm_ref[0]), cx * _silu(gx_ref[0])], axis=1)
    out_ref[0] = _out_norm_residual(x_ref[0], z, w_ref[...], g_ref[...])


def _finish_a(x, ols, proj, mkv, w_out, g_post, tm):
    n, s_len, d = x.shape
    row = lambda w, c: pl.BlockSpec((1, tm, w), lambda b, t: (b, t, c))
    in_specs = ([row(d, 0)] + [row(128, 0)] * 12 + [row(256, 9), row(256, 10), row(256, 11)]
                + [pl.BlockSpec((1, N_MEM, 2 * X_WIDTH), lambda b, t: (b, 0, 0)),
                   pl.BlockSpec(w_out.shape, lambda b, t: (0, 0)),
                   pl.BlockSpec((1, d), lambda b, t: (0, 0))])
    return pl.pallas_call(
        _finish_a_body, grid=(n, s_len // tm), in_specs=in_specs, out_specs=row(d, 0),
        out_shape=jax.ShapeDtypeStruct((n, s_len, d), F32), compiler_params=_params(2), name="finish_a",
    )(x, *ols, proj, proj, proj, mkv, w_out, g_post.reshape(1, d))


def _finish_b(x, mix, proj, mkv, w_out, g_post, tm):
    n, s_len, d = x.shape
    row = lambda w, c: pl.BlockSpec((1, tm, w), lambda b, t: (b, t, c))
    in_specs = [row(d, 0), row(B_WIDTH, 0), row(B_WIDTH, 2), row(256, 9), row(256, 10),
                pl.BlockSpec((1, N_MEM, 2 * X_WIDTH), lambda b, t: (b, 0, 0)),
                pl.BlockSpec(w_out.shape, lambda b, t: (0, 0)),
                pl.BlockSpec((1, d), lambda b, t: (0, 0))]
    return pl.pallas_call(
        _finish_b_body, grid=(n, s_len // tm), in_specs=in_specs, out_specs=row(d, 0),
        out_shape=jax.ShapeDtypeStruct((n, s_len, d), F32), compiler_params=_params(2), name="finish_b",
    )(x, mix, proj, proj, proj, mkv, w_out, g_post.reshape(1, d))


def _tail_body(x_ref, z_ref, w_ref, g_ref, out_ref):
    out_ref[...] = _out_norm_residual(x_ref[...], z_ref[...], w_ref[...], g_ref[...])


def _tail(x, z, w_out, g_post):
    m, d = x.shape
    full = lambda a: pl.BlockSpec(a.shape, lambda i: (0,) * a.ndim)
    g2 = g_post.reshape(1, d)
    return pl.pallas_call(
        _tail_body, grid=(1,), in_specs=[full(x), full(z), full(w_out), full(g2)], out_specs=full(x),
        out_shape=jax.ShapeDtypeStruct((m, d), F32), compiler_params=_params(1), name="sample_tail",
    )(x, z, w_out, g2)


def _compress_rows(load_rows, pos_ref, w1_ref, w2_ref, n_cmp):
    outs = []
    half = 2 * CMP_HIDDEN
    for t in range(2):
        y = jnp.concatenate([load_rows(t, l).astype(BF16) for l in range(CMP_STRIDE)], axis=1)
        ab = jnp.dot(y, w1_ref[t], preferred_element_type=F32) + pos_ref[t]
        h = ab[:, :half] + pltpu.roll(ab[:, half:], n_cmp - 1, axis=0)
        outs.append(jnp.dot(_silu(h).astype(BF16), w2_ref[t], preferred_element_type=F32))
    return outs


def _compress_body(k_ref, v_ref, pos_ref, w1_ref, w2_ref, o_ref, *, n_cmp):
    refs = (k_ref, v_ref)
    load = lambda t, l: refs[t][0, pl.ds(l, n_cmp, stride=CMP_STRIDE), :]
    ck, cv = _compress_rows(load, pos_ref, w1_ref, w2_ref, n_cmp)
    o_ref[0, :, 0:128] = ck.astype(BF16)
    o_ref[0, :, 128:256] = cv.astype(BF16)


def _compress_prompt(proj, posw, w1bd, w2bd):
    n, s_len, _ = proj.shape
    n_cmp = s_len // CMP_STRIDE
    full = lambda a: pl.BlockSpec(a.shape, lambda b: (0,) * a.ndim)
    return pl.pallas_call(
        functools.partial(_compress_body, n_cmp=n_cmp), grid=(n,),
        in_specs=[pl.BlockSpec((1, s_len, 128), lambda b: (b, 0, 6)), pl.BlockSpec((1, s_len, 128), lambda b: (b, 0, 7)),
                  full(posw), full(w1bd), full(w2bd)],
        out_specs=pl.BlockSpec((1, n_cmp, 256), lambda b: (b, 0, 0)),
        out_shape=jax.ShapeDtypeStruct((n, n_cmp, 256), BF16), compiler_params=_params(1), name="compress_prompt",
    )(proj, proj, posw, w1bd, w2bd)


def _place_heads(tiles, lane):
    chunks = []
    for c in range(B_HEADS // 2):
        t0, t1 = tiles[2 * c], tiles[2 * c + 1]
        if (2 * c) // B_GROUP == 1:
            t0 = pltpu.roll(t0, 64, axis=1)
        if (2 * c + 1) // B_GROUP == 0:
            t1 = pltpu.roll(t1, 64, axis=1)
        chunks.append(jnp.where(lane < 64, t0, t1))
    return jnp.concatenate(chunks, axis=1)


def _masked_softmax_rows(s, ok):
    s = jnp.where(ok, s, NEG)
    m = jnp.max(s, axis=-1, keepdims=True)
    e = jnp.where(ok, jnp.exp(s - m), 0.0)
    den = jnp.maximum(jnp.sum(e, axis=-1, keepdims=True), 1e-30)
    return e * (1.0 / den)


ONES_ROWS = 16


def _values_and_ones(v_ref, idx, kv):
    v = v_ref[idx + (slice(kv * HEAD_DIM, (kv + 1) * HEAD_DIM), slice(None))]
    return jnp.concatenate([v, jnp.ones((ONES_ROWS, v.shape[1]), BF16)], axis=0)


def _head_tile(num, den, kv):
    x = num if den is None else num * (1.0 / den)
    z = jnp.zeros_like(x)
    return jnp.concatenate([x, z] if kv == 0 else [z, x], axis=0).T


def _nsa_prompt_body(q_ref, gt_ref, kc_ref, vct_ref, kw_ref, vwt_ref, ks_ref, vst_ref, mt_ref, eg_ref,
                     sl_ref, out_ref, q6_sc, m_sc, acc_sc, sel_sc, words_sm, idx_sm, *, s_len):
    qb = pl.program_id(1)
    qstart = qb * Q_BLOCK
    n_cmp = s_len // CMP_STRIDE
    q = q_ref[0] * (SCALE * LOG2E)
    lane = _iota((Q_BLOCK, 128), 1)
    tq_row = qstart + _iota((1, Q_BLOCK), 1)
    oc_t, os_t, ow_t = [None] * B_HEADS, [None] * B_HEADS, [None] * B_HEADS

    sub = _iota((128, 128), 0)
    psums = []

    for kv in range(B_KV):
        for g in range(B_GROUP):
            h = kv * B_GROUP + g
            ch = q[:, (h // 2) * 128:(h // 2 + 1) * 128]
            if h % 2 == 1:
                ch = pltpu.roll(ch, 64, axis=1)
            q6_sc[kv, g * 128:(g + 1) * 128, :] = jnp.where(lane < 64, ch, sl_ref[h:h + 1, :]).astype(BF16)

    cok = (CMP_STRIDE * _iota((n_cmp, Q_BLOCK), 0) + (CMP_LEN - 1)) <= tq_row
    q_ok = tq_row >= (CMP_LEN - 1)
    n_wb = WIN_B // Q_BLOCK + 1

    def cmp_scores(kv):
        return lax.dot_general(kc_ref[0, kv], q6_sc[kv], NT, preferred_element_type=F32)

    def cmp_finish(kv, s_t):
        psum = jnp.zeros((n_cmp, Q_BLOCK), F32)
        ps = []
        for g in range(B_GROUP):
            s = jnp.where(cok, s_t[:, g * 128:(g + 1) * 128], NEG)
            e = jnp.exp2(s - jnp.max(s, axis=0, keepdims=True))
            p = e * jnp.where(q_ok, 1.0 / jnp.sum(e, axis=0, keepdims=True), 0.0)
            psum = psum + p
            ps.append(p.astype(BF16))
        oc = jnp.dot(vct_ref[0, kv * HEAD_DIM:(kv + 1) * HEAD_DIM, :], jnp.concatenate(ps, axis=1),
                     preferred_element_type=F32)
        for g in range(B_GROUP):
            oc_t[kv * B_GROUP + g] = _head_tile(oc[:, g * 128:(g + 1) * 128], None, kv)
        psums.append(psum)

    def win_scores(kv):
        kparts, vparts, pparts = [], [], []
        for wb in range(n_wb):
            b_raw = qb - (n_wb - 1) + wb
            b = jnp.maximum(b_raw, 0)
            r0 = pl.multiple_of(b * Q_BLOCK, Q_BLOCK)
            kparts.append(kw_ref[0, kv, pl.ds(r0, Q_BLOCK), :])
            vparts.append(_values_and_ones(vwt_ref, (0, b), kv))
            pparts.append(jnp.where(b_raw >= 0, r0, s_len) + _iota((128, Q_BLOCK), 0))
        s_t = lax.dot_general(jnp.concatenate(kparts, axis=0), q6_sc[kv], NT, preferred_element_type=F32)
        return s_t, jnp.concatenate(vparts, axis=1), jnp.concatenate(pparts, axis=0)

    def win_finish(kv, s_t, v_t, kpos):
        dw = tq_row - kpos
        wok = (dw >= 0) & (dw <= WIN_B)
        ps = []
        for g in range(B_GROUP):
            s = jnp.where(wok, s_t[:, g * 128:(g + 1) * 128], NEG)
            ps.append(jnp.exp2(s - jnp.max(s, axis=0, keepdims=True)).astype(BF16))
        ow = jnp.dot(v_t, jnp.concatenate(ps, axis=1), preferred_element_type=F32)
        for g in range(B_GROUP):
            cols = slice(g * 128, (g + 1) * 128)
            ow_t[kv * B_GROUP + g] = _head_tile(ow[0:HEAD_DIM, cols], ow[HEAD_DIM:HEAD_DIM + 1, cols], kv)

    sc0 = cmp_scores(0)
    sw0 = win_scores(0)
    cmp_finish(0, sc0)
    sc1 = cmp_scores(1)
    win_finish(0, *sw0)
    sw1 = win_scores(1)
    cmp_finish(1, sc1)
    win_finish(1, *sw1)

    mt = mt_ref[...]
    blk = _iota((128, Q_BLOCK), 0)
    ql = _iota((128, Q_BLOCK), 1)
    cur = jnp.where(ql >= SEL_BLOCK, qb * 2 + 1, qb * 2)
    forced = (blk == 0) | (blk == cur) | (blk == cur - 1)
    blkf = blk.astype(F32)
    imps = [jnp.where((blk > cur) | forced, -jnp.inf,
                      sum(jnp.dot(mt, t, preferred_element_type=F32) for t in _split3(psum))) for psum in psums]
    sels = [jnp.where(forced, 1.0, 0.0)] * B_KV
    for _ in range(SEL_TOPK - 3):
        for kv in range(B_KV):
            mx = jnp.max(imps[kv], axis=0, keepdims=True)
            idx = jnp.min(jnp.where(imps[kv] == mx, blkf, 1e9), axis=0, keepdims=True)
            hit = blkf == idx
            sels[kv] = jnp.where(hit, 1.0, sels[kv])
            imps[kv] = jnp.where(hit, -jnp.inf, imps[kv])

    blk_col = _iota((128, 1), 0)
    weight = lax.shift_left(jnp.ones((128, 1), jnp.int32), blk_col & 15).astype(F32)
    for kv in range(B_KV):
        sel_sc[kv] = sels[kv]
        contrib = jnp.max(sels[kv], axis=1, keepdims=True) * weight
        for w in range(8):
            words_sm[kv * 8 + w] = jnp.sum(contrib[16 * w:16 * (w + 1), :]).astype(jnp.int32)

    m_sc[...] = jnp.full(m_sc.shape, NEG, F32)
    acc_sc[...] = jnp.zeros(acc_sc.shape, F32)
    list_len = idx_sm.shape[0] // B_KV
    cnts = []
    n_grp = s_len // Q_BLOCK
    for kv in range(B_KV):
        cnt = jnp.int32(0)
        for w in range((n_grp + 7) // 8):
            word = words_sm[kv * 8 + w]
            for j in range(min(8, n_grp - 8 * w)):
                idx_sm[kv * list_len + cnt] = 8 * w + j
                cnt = cnt + jnp.where(((word >> (2 * j)) & 3) != 0, 1, 0)
        cnts.append(cnt)
    n_chunks = (jnp.maximum(cnts[0], cnts[1]) + GROUPS_PER_CHUNK - 1) // GROUPS_PER_CHUNK
    for kv in range(B_KV):
        def pad(i, c, kv=kv):
            idx_sm[kv * list_len + i] = -1
            return c
        lax.fori_loop(cnts[kv], n_chunks * GROUPS_PER_CHUNK, pad, 0)

    sub8 = _iota((8, Q_BLOCK), 0)

    def group_hits(kv, gi):
        rows8 = sel_sc[kv, pl.ds(pl.multiple_of((gi >> 2) * 8, 8), 8), :]
        r = (gi & 3) * 2
        lo = jnp.sum(jnp.where(sub8 == r, rows8, 0.0), axis=0, keepdims=True)
        hi = jnp.sum(jnp.where(sub8 == r + 1, rows8, 0.0), axis=0, keepdims=True)
        return jnp.where(sub < SEL_BLOCK, lo, hi)

    def chunk_scores(c, kv):
        kts, vts, hits, kposs = [], [], [], []
        for j in range(GROUPS_PER_CHUNK):
            gi_raw = idx_sm[kv * list_len + c * GROUPS_PER_CHUNK + j]
            gi = jnp.maximum(gi_raw, 0)
            k0 = pl.multiple_of(gi * Q_BLOCK, Q_BLOCK)
            kts.append(ks_ref[0, kv, pl.ds(k0, Q_BLOCK), :])
            vts.append(_values_and_ones(vst_ref, (0, gi), kv))
            hits.append(group_hits(kv, gi))
            kposs.append(jnp.where(gi_raw >= 0, k0, s_len) + _iota((128, 128), 0))
        s_t = lax.dot_general(jnp.concatenate(kts, axis=0), q6_sc[kv], NT, preferred_element_type=F32)
        ok = ((jnp.concatenate(hits, axis=0) > 0.5)
              & (jnp.concatenate(kposs, axis=0) <= qstart + _iota((GROUPS_PER_CHUNK * 128, 128), 1)))
        return s_t, ok, jnp.concatenate(vts, axis=1)

    def chunk_update(kv, s_t, ok, v_t):
        m_old = m_sc[kv]
        m_new, ps = [], []
        for g in range(B_GROUP):
            cols = slice(g * 128, (g + 1) * 128)
            s = jnp.where(ok, s_t[:, cols], NEG)
            mg = jnp.maximum(m_old[:, cols], jnp.max(s, axis=0, keepdims=True))
            ps.append(jnp.exp2(s - mg).astype(BF16))
            m_new.append(mg)
        m_new = jnp.concatenate(m_new, axis=1)
        pv = jnp.dot(v_t, jnp.concatenate(ps, axis=1), preferred_element_type=F32)
        acc_sc[kv] = jnp.exp2(m_old - m_new) * acc_sc[kv] + pv
        m_sc[kv] = m_new

    def chunk(c, carry):
        first = chunk_scores(c, 0)
        second = chunk_scores(c, 1)
        chunk_update(0, *first)
        chunk_update(1, *second)
        return carry

    lax.fori_loop(0, n_chunks, chunk, 0)
    for kv in range(B_KV):
        for g in range(B_GROUP):
            cols = slice(g * 128, (g + 1) * 128)
            os_t[kv * B_GROUP + g] = _head_tile(acc_sc[kv, 0:HEAD_DIM, cols], acc_sc[kv, HEAD_DIM:HEAD_DIM + 1, cols], kv)

    sg = _sigmoid(gt_ref[0])
    eg = eg_ref[...]
    gexp = sum(jnp.dot(t, eg, preferred_element_type=F32) for t in _split3(sg))
    out_ref[0] = (gexp[:, 0:B_WIDTH] * _place_heads(oc_t, lane)
                  + gexp[:, B_WIDTH:2 * B_WIDTH] * _place_heads(os_t, lane)
                  + gexp[:, 2 * B_WIDTH:] * _place_heads(ow_t, lane))


def _nsa_prompt(proj, kc, vc_t, kw, vw_t, ks, vs_t, mt, eg, slope_lanes):
    n, s_len, _ = proj.shape
    n_cmp = s_len // CMP_STRIDE
    n_grp = s_len // Q_BLOCK
    full = lambda a: pl.BlockSpec(a.shape, lambda b, t: (0,) * a.ndim)
    per_n = lambda a: pl.BlockSpec((1,) + a.shape[1:], lambda b, t: (b,) + (0,) * (a.ndim - 1))
    return pl.pallas_call(
        functools.partial(_nsa_prompt_body, s_len=s_len),
        grid=(n, n_grp),
        in_specs=[pl.BlockSpec((1, Q_BLOCK, B_WIDTH), lambda b, t: (b, t, 0)),
                  pl.BlockSpec((1, Q_BLOCK, 128), lambda b, t: (b, t, 22)),
                  per_n(kc), per_n(vc_t), per_n(kw), per_n(vw_t), per_n(ks), per_n(vs_t),
                  full(mt), full(eg), full(slope_lanes)],
        out_specs=pl.BlockSpec((1, Q_BLOCK, B_WIDTH), lambda b, t: (b, t, 0)),
        out_shape=jax.ShapeDtypeStruct((n, s_len, B_WIDTH), F32),
        scratch_shapes=[pltpu.VMEM((B_KV, B_GROUP * Q_BLOCK, 128), BF16), pltpu.VMEM((B_KV, 1, B_GROUP * Q_BLOCK), F32),
                        pltpu.VMEM((B_KV, HEAD_DIM + ONES_ROWS, B_GROUP * Q_BLOCK), F32),
                        pltpu.VMEM((B_KV, 128, Q_BLOCK), F32), pltpu.SMEM((B_KV * 8,), jnp.int32),
                        pltpu.SMEM((B_KV * (n_grp + GROUPS_PER_CHUNK),), jnp.int32)],
        compiler_params=_params(2), name="nsa_prompt",
    )(proj, proj, kc, vc_t, kw, vw_t, ks, vs_t, mt, eg, slope_lanes)


POS_LANE = 64


def _slope_lanes():
    out = np.zeros((B_HEADS, 128), np.float32)
    for h, slope in enumerate(SLOPES_B):
        s = np.float32(np.float64(slope) * LOG2E)
        hi = np.float32(np.asarray(s, np.float32).astype(jnp.bfloat16))
        mid = np.float32(np.asarray(np.float32(s) - hi, np.float32).astype(jnp.bfloat16))
        lo = np.float32(np.asarray(np.float32(s) - hi - mid, np.float32).astype(jnp.bfloat16))
        out[h, POS_LANE:POS_LANE + 6] = [hi, mid, lo, hi, mid, lo]
    return out


def _keys_with_pos(k2, pos):
    n, n_keys, _ = k2.shape
    lo = (pos % Q_BLOCK).astype(BF16)[None, :, None]
    hi = (pos - pos % Q_BLOCK).astype(BF16)[None, :, None]
    tail = jnp.concatenate([jnp.broadcast_to(lo, (n, n_keys, 3)), jnp.broadcast_to(hi, (n, n_keys, 3)),
                            jnp.zeros((n, n_keys, 128 - POS_LANE - 6), BF16)], axis=-1)
    return jnp.stack([jnp.concatenate([k2[..., kv * 64:(kv + 1) * 64], tail], axis=-1) for kv in range(B_KV)], axis=1)


def _values_by_group(v2):
    n, s_len, _ = v2.shape
    return v2.reshape(n, s_len // Q_BLOCK, Q_BLOCK, 128).transpose(0, 1, 3, 2)


def _heads_rows(vec, n_rows, width):
    r = _iota((n_rows, width), 0)
    l = _iota((n_rows, width), 1)
    hm = (l >= r * HEAD_DIM) & (l < r * HEAD_DIM + HEAD_DIM)
    return jnp.where(hm, jnp.broadcast_to(vec, (n_rows, width)), 0.0), hm


def _bf(x):
    return x.astype(BF16).astype(F32)


def _row_consts(n_rows, vals):
    r = _iota((n_rows, 1), 0)
    out = jnp.zeros((n_rows, 1), F32)
    for i, v in enumerate(vals):
        out = jnp.where(r == i, v, out)
    return out


def _rows_last(cache):
    nd = cache.ndim
    return cache.transpose(tuple(range(nd - 4)) + (nd - 3, nd - 2, nd - 1, nd - 4))


def _kv_t(ref, t):
    x = ref[0, 0, t]
    return x.reshape(x.shape[0] * x.shape[1], x.shape[2]).astype(BF16)


def _sample_cross(qx_row, k_t, v_t):
    q8, hm = _heads_rows(qx_row, 8, X_WIDTH)
    s = jnp.dot(q8.astype(BF16), k_t, preferred_element_type=F32) * SCALE
    e = jnp.exp(s - jnp.max(s, axis=-1, keepdims=True))
    p = (e * (1.0 / jnp.sum(e, axis=-1, keepdims=True))).astype(BF16)
    o8 = lax.dot_general(p, v_t, NT, preferred_element_type=F32)
    return jnp.sum(jnp.where(hm, o8, 0.0), axis=0, keepdims=True)


def _sample_a_body(row_ref, c0_ref, c1_ref, c2_ref, mkv_ref, z_ref):
    row = row_ref[0]
    outs, lses = [], []
    hm = None
    for g, (win, dil) in enumerate(A_PATTERNS):
        cref = (c0_ref, c1_ref, c2_ref)[g]
        q8, hm = _heads_rows(row[:, g * 256:(g + 1) * 256], 8, A_WIDTH)
        knew = row[:, 768 + g * 256:768 + (g + 1) * 256]
        vnew = row[:, 1536 + g * 256:1536 + (g + 1) * 256]
        q8b = q8.astype(BF16)
        slope = _row_consts(8, SLOPES_A[g * 4:(g + 1) * 4])
        s = jnp.dot(q8b, _kv_t(cref, 0), preferred_element_type=F32) * SCALE
        r = _iota((8, win), 1)
        s = jnp.where((r & (dil - 1)) == 0, s - slope * (win - r).astype(F32), NEG)
        s_new = jnp.sum(q8b.astype(F32) * _bf(knew), axis=-1, keepdims=True) * SCALE
        m = jnp.maximum(jnp.max(s, axis=-1, keepdims=True), s_new)
        e = jnp.exp(s - m)
        e_new = jnp.exp(s_new - m)
        den = jnp.sum(e, axis=-1, keepdims=True) + e_new
        inv = 1.0 / den
        o8 = (lax.dot_general((e * inv).astype(BF16), _kv_t(cref, 1), NT, preferred_element_type=F32)
              + _bf(e_new * inv) * _bf(vnew))
        outs.append(o8)
        lses.append(m + jnp.log(den))
    mx = jnp.maximum(jnp.maximum(lses[0], lses[1]), lses[2])
    ws = [jnp.exp(l - mx) for l in lses]
    mix8 = (ws[0] * outs[0] + ws[1] * outs[1] + ws[2] * outs[2]) / (ws[0] + ws[1] + ws[2])
    mix = jnp.sum(jnp.where(hm, mix8, 0.0), axis=0, keepdims=True)
    cx = _sample_cross(row[:, 2560:2816], _kv_t(mkv_ref, 0), _kv_t(mkv_ref, 1))
    z_ref[0] = jnp.concatenate([mix * _silu(row[:, 2304:2560]), cx * _silu(row[:, 2816:3072])], axis=1)


def _layer_block(cache_t, layer):
    return pl.BlockSpec((1, 1) + cache_t.shape[2:], lambda b, *_: (layer, b, 0, 0, 0, 0))


def _sample_a(proj_s, caches_t, mem_t, li, i):
    ns = proj_s.shape[0]
    row3 = proj_s.reshape(ns, 1, W_IN_A)
    return pl.pallas_call(
        _sample_a_body, grid=(ns,),
        in_specs=[pl.BlockSpec((1, 1, W_IN_A), lambda b: (b, 0, 0))] + [_layer_block(c, li) for c in caches_t]
                 + [_layer_block(mem_t, i)],
        out_specs=pl.BlockSpec((1, 1, A_WIDTH + X_WIDTH), lambda b: (b, 0, 0)),
        out_shape=jax.ShapeDtypeStruct((ns, 1, A_WIDTH + X_WIDTH), F32), compiler_params=_params(1), name="sample_a",
    )(row3, *caches_t, mem_t).reshape(ns, A_WIDTH + X_WIDTH)


def _q16(row):
    r = _iota((16, 128), 0)
    l = _iota((16, 128), 1)
    acc = jnp.zeros((16, 128), F32)
    for c in range(B_HEADS // 2):
        ch = jnp.broadcast_to(row[:, c * 128:(c + 1) * 128], (16, 128))
        rolled = pltpu.roll(ch, 64, axis=1)
        for hh in range(2):
            h = 2 * c + hh
            kv = h // B_GROUP
            lm = (l < 64) if kv == 0 else (l >= 64)
            acc = jnp.where((r == h) & lm, ch if hh == kv else rolled, acc)
    return acc * SCALE


def _sample_b1_body(pt_ref, row_ref, pos_ref, w1_ref, w2_ref, mm_ref, pages_ref, oc_ref, sel_ref,
                    buf, rows_sc, sem, imp_sc, *, li, n_pages, ns):
    n = pl.program_id(0)
    past = n_pages * PAGE_SIZE
    n_cmp = past // CMP_STRIDE

    def page_copy(page, p, slot):
        return pltpu.make_async_copy(pages_ref.at[page, li, pl.ds(0, 2)], buf.at[slot, p], sem.at[slot])

    def fetch(nn, slot):
        def body(p, c):
            page_copy(pt_ref[nn * n_pages + p], p, slot).start()
            return c
        lax.fori_loop(0, n_pages, body, 0)

    @pl.when(n == 0)
    def _():
        fetch(0, 0)

    @pl.when(n + 1 < ns)
    def _():
        fetch(n + 1, (n + 1) % 2)

    slot = n % 2

    def wbody(p, c):
        page_copy(0, p, slot).wait()
        return c
    lax.fori_loop(0, n_pages, wbody, 0)

    pages_per_step = 8

    def to_rows(it, c):
        for k in range(pages_per_step):
            p = it * pages_per_step + k
            r0 = pl.multiple_of(p * PAGE_SIZE, PAGE_SIZE)
            for t in range(2):
                rows_sc[t, pl.ds(r0, PAGE_SIZE), :] = buf[slot, p, t].reshape(2 * HEAD_DIM, PAGE_SIZE).T
        return c
    lax.fori_loop(0, n_pages // pages_per_step, to_rows, 0)

    load = lambda t, l: rows_sc[t, pl.ds(l, n_cmp, stride=CMP_STRIDE), :]
    ck, cv = _compress_rows(load, pos_ref, w1_ref, w2_ref, n_cmp)

    q16 = _q16(row_ref[0]).astype(BF16)
    slope = _row_consts(16, SLOPES_B)
    s = lax.dot_general(q16, ck.astype(BF16), NT, preferred_element_type=F32)
    cend = CMP_STRIDE * _iota((1, n_cmp), 1) + (CMP_LEN - 1)
    p = _masked_softmax_rows(s - slope * (past - cend).astype(F32), cend <= past)
    oc_ref[0] = jnp.dot(p.astype(BF16), cv.astype(BF16), preferred_element_type=F32)

    r16 = _iota((16, n_cmp), 0)
    ps0 = jnp.sum(jnp.where(r16 < B_GROUP, p, 0.0), axis=0, keepdims=True)
    ps1 = jnp.sum(jnp.where((r16 >= B_GROUP) & (r16 < B_HEADS), p, 0.0), axis=0, keepdims=True)
    psum = jnp.concatenate([ps0, ps1, jnp.zeros((6, n_cmp), F32)], axis=0)
    mm = mm_ref[...]
    imp = sum(jnp.dot(t, mm, preferred_element_type=F32) for t in _split3(psum))
    blk = _iota((8, 256), 1)
    cur = past // SEL_BLOCK
    forced = (blk == 0) | (blk == cur) | (blk == cur - 1)
    imp_sc[n] = jnp.where(blk > cur, -jnp.inf, jnp.where(forced, FORCE_SCORE, imp))

    @pl.when(n == ns - 1)
    def _():
        impa = imp_sc[...]
        blkf = _iota(impa.shape, 2).astype(F32)
        lane = _iota((ns, 8, 128), 2)
        out = jnp.zeros((ns, 8, 128), F32)
        for r in range(SEL_TOPK):
            mx = jnp.max(impa, axis=-1, keepdims=True)
            idx = jnp.min(jnp.where(impa == mx, blkf, 1e9), axis=-1, keepdims=True)
            impa = jnp.where(blkf == idx, -jnp.inf, impa)
            out = jnp.where(lane == r, idx, out)
        sel_ref[...] = out.astype(jnp.int32)


def _sample_b1(page_table, proj_s, posw, w1bd, w2bd, mm, pages_t, li):
    ns, n_pages = page_table.shape
    past = n_pages * PAGE_SIZE
    row3 = proj_s.reshape(ns, 1, W_IN_B_PAD)
    full = lambda a: pl.BlockSpec(a.shape, lambda b, pt: (0,) * a.ndim)
    grid_spec = pltpu.PrefetchScalarGridSpec(
        num_scalar_prefetch=1, grid=(ns,),
        in_specs=[pl.BlockSpec((1, 1, W_IN_B_PAD), lambda b, pt: (b, 0, 0)), full(posw), full(w1bd), full(w2bd),
                  full(mm), pl.BlockSpec(memory_space=pl.ANY)],
        out_specs=[pl.BlockSpec((1, 16, 128), lambda b, pt: (b, 0, 0)),
                   pl.BlockSpec((ns, 8, 128), lambda b, pt: (0, 0, 0))],
        scratch_shapes=[pltpu.VMEM((2, n_pages, 2, B_KV, HEAD_DIM, PAGE_SIZE), F32),
                        pltpu.VMEM((2, past, 128), F32), pltpu.SemaphoreType.DMA((2,)),
                        pltpu.VMEM((ns, 8, 256), F32)])
    return pl.pallas_call(
        functools.partial(_sample_b1_body, li=li, n_pages=n_pages, ns=ns),
        grid_spec=grid_spec,
        out_shape=[jax.ShapeDtypeStruct((ns, 16, 128), F32), jax.ShapeDtypeStruct((ns, 8, 128), jnp.int32)],
        compiler_params=_params(1), name="sample_b1",
    )(page_table.reshape(-1), row3, posw, w1bd, w2bd, mm, pages_t)


def _sample_b2_body(pt_ref, sf_ref, row_ref, oc_ref, sel_ref, win_ref, mkv_ref, e16_ref, pages_ref, z_ref,
                    buf, sem, *, li, n_pages, ns):
    n = pl.program_id(0)
    past = n_pages * PAGE_SIZE
    n_blk = past // SEL_BLOCK
    per_page = PAGE_SIZE // SEL_BLOCK
    n_sel = B_KV * SEL_TOPK

    def blk_copies(page, kv, r, slot):
        return [pltpu.make_async_copy(pages_ref.at[page, li, 2 + t, kv],
                                      buf.at[slot, t, kv, :, pl.ds(r * PAGE_SIZE, PAGE_SIZE)], sem.at[slot])
                for t in range(2)]

    def fetch(nn, slot):
        for kv in range(B_KV):
            for r in range(SEL_TOPK):
                j = jnp.minimum(sf_ref[nn * n_sel + kv * SEL_TOPK + r], n_blk - 1)
                for cp in blk_copies(pt_ref[nn * n_pages + j // per_page], kv, r, slot):
                    cp.start()

    @pl.when(n == 0)
    def _():
        fetch(0, 0)

    @pl.when(n + 1 < ns)
    def _():
        fetch(n + 1, (n + 1) % 2)

    slot = n % 2
    for kv in range(B_KV):
        for r in range(SEL_TOPK):
            for cp in blk_copies(0, kv, r, slot):
                cp.wait()

    row = row_ref[0]
    q16f = _q16(row)
    q16 = q16f.astype(BF16)
    q16r = q16.astype(F32)
    slope = _row_consts(16, SLOPES_B)
    r16 = _iota((16, 128), 0)

    def new_key(col):
        kn = _bf(row[:, col:col + 128])
        return jnp.sum(q16r * kn, axis=-1, keepdims=True)

    def attend(s, s_new, v_t):
        m = jnp.maximum(jnp.max(s, axis=-1, keepdims=True), s_new)
        e = jnp.exp(s - m)
        e_new = jnp.exp(s_new - m)
        inv = 1.0 / (jnp.sum(e, axis=-1, keepdims=True) + e_new)
        return lax.dot_general((e * inv).astype(BF16), v_t, NT, preferred_element_type=F32), _bf(e_new * inv)

    n_keys = SEL_TOPK * PAGE_SIZE
    jv = jnp.dot(sel_ref[0].astype(F32).astype(BF16), e16_ref[...], preferred_element_type=F32).astype(jnp.int32)
    in_page = jnp.bitwise_and(_iota((8, n_keys), 1), PAGE_SIZE - 1)
    ok_sel = ((in_page >> 6) == (jv & (per_page - 1))) & (jv < n_blk)
    dist_sel = (past - ((jv >> 1) * PAGE_SIZE + in_page)).astype(F32)
    q64 = jnp.where(r16[:, :HEAD_DIM] < B_GROUP, q16f[:, :HEAD_DIM], q16f[:, HEAD_DIM:]).astype(BF16)
    s_new = new_key(1024)
    outs, p_news = [], []
    for kv in range(B_KV):
        s = jnp.dot(q64, buf[slot, 0, kv].astype(BF16), preferred_element_type=F32) - slope * dist_sel[kv:kv + 1]
        o, p_new = attend(jnp.where(ok_sel[kv:kv + 1], s, NEG), s_new, buf[slot, 1, kv].astype(BF16))
        outs.append(o)
        p_news.append(p_new)
    p_new = jnp.where(_iota((16, 1), 0) < B_GROUP, p_news[0], p_news[1])
    os16 = jnp.concatenate(outs, axis=1) + p_new * _bf(row[:, 1152:1280])

    lb = win_ref.shape[-1]
    dw = (lb - _iota((1, lb), 1)).astype(F32)
    s = jnp.dot(q16, _kv_t(win_ref, 0), preferred_element_type=F32) - slope * dw
    ow16, p_new = attend(s, new_key(1280), _kv_t(win_ref, 1))
    ow16 = ow16 + p_new * _bf(row[:, 1408:1536])

    sg = jnp.broadcast_to(_sigmoid(row[:, 2816:2944]), (16, 128))
    l16 = _iota((16, 128), 1)
    gate = lambda b: jnp.sum(jnp.where(l16 == r16 * 3 + b, sg, 0.0), axis=-1, keepdims=True)
    out16 = gate(0) * oc_ref[0] + gate(1) * os16 + gate(2) * ow16
    lane1 = _iota((1, 128), 1)
    mix = _place_heads([out16[h:h + 1, :] for h in range(B_HEADS)], lane1)

    cx = _sample_cross(row[:, 2304:2560], _kv_t(mkv_ref, 0), _kv_t(mkv_ref, 1))
    z_ref[0] = jnp.concatenate([mix * _silu(row[:, 1536:2304]), cx * _silu(row[:, 2560:2816])], axis=1)


def _sample_b2(page_table, sel, proj_s, oc, win_t, mem_t, e16, pages_t, li, i):
    ns, n_pages = page_table.shape
    row3 = proj_s.reshape(ns, 1, W_IN_B_PAD)
    full = lambda a: pl.BlockSpec(a.shape, lambda b, pt, sf: (0,) * a.ndim)
    per = lambda a: pl.BlockSpec((1,) + a.shape[1:], lambda b, pt, sf: (b,) + (0,) * (a.ndim - 1))
    grid_spec = pltpu.PrefetchScalarGridSpec(
        num_scalar_prefetch=2, grid=(ns,),
        in_specs=[per(row3), per(oc), per(sel), _layer_block(win_t, li), _layer_block(mem_t, i), full(e16),
                  pl.BlockSpec(memory_space=pl.ANY)],
        out_specs=pl.BlockSpec((1, 1, B_WIDTH + X_WIDTH), lambda b, pt, sf: (b, 0, 0)),
        scratch_shapes=[pltpu.VMEM((2, 2, B_KV, HEAD_DIM, SEL_TOPK * PAGE_SIZE), F32), pltpu.SemaphoreType.DMA((2,))])
    return pl.pallas_call(
        functools.partial(_sample_b2_body, li=li, n_pages=n_pages, ns=ns),
        grid_spec=grid_spec,
        out_shape=jax.ShapeDtypeStruct((ns, 1, B_WIDTH + X_WIDTH), F32),
        compiler_params=_params(1), name="sample_b2",
    )(page_table.reshape(-1), sel[:, :B_KV, :SEL_TOPK].reshape(-1), row3, oc, sel, win_t, mem_t, e16, pages_t
      ).reshape(ns, B_WIDTH + X_WIDTH)


def _importance_matrix(n_cmp, n_cols):
    c = np.arange(n_cmp)[:, None]
    j = np.arange(n_cols)[None, :]
    per = SEL_BLOCK // CMP_STRIDE
    m = ((c >= per * j) & (c <= per * j + per - 1)).astype(np.float32)
    m = m + ((c + 1 >= per * j) & (c + 1 <= per * j + per - 1)).astype(np.float32)
    m[n_cmp - 1, :] = 0.0
    return m


def _gate_expand():
    eg = np.zeros((128, 3 * B_WIDTH), np.float32)
    for h in range(B_HEADS):
        for b in range(3):
            eg[h * 3 + b, b * B_WIDTH + h * HEAD_DIM:b * B_WIDTH + (h + 1) * HEAD_DIM] = 1.0
    return eg


def _block_expand(n_rows, n_keys):
    return (np.arange(n_keys)[None, :] // SEL_BLOCK == np.arange(n_rows)[:, None]).astype(np.float32)


def _compress_weights(cmp_pos, cmp_w1, cmp_w2):
    eye = jnp.eye(B_KV, dtype=F32)
    posw = jnp.concatenate([cmp_pos, cmp_pos], axis=-1)
    w1 = cmp_w1.reshape(2, CMP_LEN, HEAD_DIM, CMP_HIDDEN)
    w1bd = jnp.einsum('tlek,jm->tljemk', w1, eye).reshape(2, CMP_LEN, 2 * HEAD_DIM, 2 * CMP_HIDDEN)
    w1cat = jnp.concatenate([w1bd[:, :CMP_STRIDE].reshape(2, CMP_STRIDE * 2 * HEAD_DIM, 2 * CMP_HIDDEN),
                             w1bd[:, CMP_STRIDE:].reshape(2, CMP_STRIDE * 2 * HEAD_DIM, 2 * CMP_HIDDEN)], axis=-1)
    pos_h = jnp.einsum('tlr,tlrh->tlh', posw, w1bd, precision=lax.Precision.HIGHEST)
    bias = jnp.concatenate([pos_h[:, :CMP_STRIDE].sum(axis=1), pos_h[:, CMP_STRIDE:].sum(axis=1)], axis=-1)
    w2bd = jnp.einsum('tke,jm->tjkme', cmp_w2, eye).reshape(2, 2 * CMP_HIDDEN, 2 * HEAD_DIM)
    return bias[:, None, :], w1cat.astype(BF16), w2bd.astype(BF16)


def _permute_w_in_b(w):
    d = w.shape[0]
    return jnp.concatenate([w[:, :1536], w[:, 1572:W_IN_B], w[:, 1536:1572],
                            jnp.zeros((d, W_IN_B_PAD - W_IN_B), w.dtype)], axis=1)


def kernel(x_prompt, x_sample, cache_mem_kv, cache_a_w128_kv, cache_a_w512_kv, cache_a_w2048_kv, cache_b_pages,
           cache_b_win_kv, page_table, mem_prompt, norm_pre, norm_post, norm_mem, w_mem_kv, w_in_a, w_out_a,
           w_in_b, w_out_b, cmp_pos, cmp_w1, cmp_w2):
    n, s_len, d = x_prompt.shape
    ns = x_sample.shape[0]
    depth = norm_pre.shape[0]
    n_pool, page_size, n_lb = cache_b_pages.shape[:3]
    n_pages = page_table.shape[1]
    past = n_pages * page_size
    assert d == D_MODEL and x_sample.shape[1] == 1 and page_size == PAGE_SIZE
    assert s_len % 2048 == 0 and past % 2048 == 0 and s_len >= 2048 and past >= 2048
    caches_a = (cache_a_w128_kv, cache_a_w512_kv, cache_a_w2048_kv)
    for c, (win, _) in zip(caches_a, A_PATTERNS):
        assert c.shape[2] == win
    assert cache_b_win_kv.shape[2] == WIN_B

    tm = 512
    n_cmp_p = s_len // CMP_STRIDE
    n_cmp_s = past // CMP_STRIDE
    slope_lanes = jnp.asarray(_slope_lanes(), F32)
    mt = jnp.asarray(_importance_matrix(n_cmp_p, 128).T, BF16)
    mm = jnp.asarray(_importance_matrix(n_cmp_s, 256), BF16)
    eg = jnp.asarray(_gate_expand(), BF16)
    e16 = jnp.asarray((np.arange(SEL_TOPK * PAGE_SIZE)[None, :] // PAGE_SIZE == np.arange(128)[:, None]), BF16)
    pages_t = cache_b_pages.transpose(0, 2, 3, 4, 5, 1)
    caches_a_t = [_rows_last(c) for c in caches_a]
    mem_t = _rows_last(cache_mem_kv)
    win_t = _rows_last(cache_b_win_kv)

    xp = x_prompt
    xs = x_sample.reshape(ns, d)
    mem2 = mem_prompt.reshape(n * N_MEM, d)
    mem_new = []
    a_p = [[] for _ in A_PATTERNS]
    a_s = [[] for _ in A_PATTERNS]
    b_p, b_s, bw_p, bw_s = [], [], [], []
    for i in range(depth):
        li = i // 2
        mkv_p = _rms_proj(mem2, norm_mem[i], w_mem_kv[i].astype(BF16), tm=N_MEM).reshape(n, N_MEM, 2 * X_WIDTH)
        mem_new.append(mkv_p.reshape(n, N_MEM, 2, 4, HEAD_DIM))
        if i % 2 == 0:
            w_in = w_in_a[li].astype(BF16)
            w_out = w_out_a[li].astype(BF16)
            proj_p = _rms_proj(xp.reshape(n * s_len, d), norm_pre[i], w_in, tm=tm).reshape(n, s_len, W_IN_A)
            proj_s = _rms_proj(xs, norm_pre[i], w_in, tm=ns)
            ols = []
            for g, (win, dil) in enumerate(A_PATTERNS):
                ols += _a_prompt_group(proj_p, g, dil)
                kv_p = proj_p[:, s_len - win:, 768:2304].reshape(n, win, 2, 3, 4, HEAD_DIM)[:, :, :, g]
                a_p[g].append(kv_p)
                a_s[g].append(proj_s[:, 768:2304].reshape(ns, 1, 2, 3, 4, HEAD_DIM)[:, :, :, g])
            xp = _finish_a(xp, ols, proj_p, mkv_p, w_out, norm_post[i], tm=tm)
            z = _sample_a(proj_s, caches_a_t, mem_t, li, i)
            xs = _tail(xs, z, w_out, norm_post[i])
        else:
            w_in = _permute_w_in_b(w_in_b[li]).astype(BF16)
            w_out = w_out_b[li].astype(BF16)
            posw, w1bd, w2bd = _compress_weights(cmp_pos[li], cmp_w1[li], cmp_w2[li])
            proj_p, kvs = _rms_proj(xp.reshape(n * s_len, d), norm_pre[i], w_in, tm=tm, side=(1024, 512))
            proj_p = proj_p.reshape(n, s_len, W_IN_B_PAD)
            kvs = kvs.reshape(n, s_len, 512)
            proj_s = _rms_proj(xs, norm_pre[i], w_in, tm=ns)
            cmpd = _compress_prompt(proj_p, posw, w1bd, w2bd)
            pos = jnp.arange(s_len, dtype=jnp.int32)
            cend = CMP_STRIDE * jnp.arange(n_cmp_p, dtype=jnp.int32) + (CMP_LEN - 1)
            mix = _nsa_prompt(proj_p, _keys_with_pos(cmpd[:, :, 0:128], cend), cmpd[:, :, 128:256].transpose(0, 2, 1),
                              _keys_with_pos(kvs[:, :, 256:384], pos), _values_by_group(kvs[:, :, 384:512]),
                              _keys_with_pos(kvs[:, :, 0:128], pos), _values_by_group(kvs[:, :, 128:256]),
                              mt, eg, slope_lanes)
            xp = _finish_b(xp, mix, proj_p, mkv_p, w_out, norm_post[i], tm=tm)
            oc, sel = _sample_b1(page_table, proj_s, posw, w1bd, w2bd, mm, pages_t, li)
            z = _sample_b2(page_table, sel, proj_s, oc, win_t, mem_t, e16, pages_t, li, i)
            xs = _tail(xs, z, w_out, norm_post[i])
            b_p.append(proj_p[:, :, 768:1280].reshape(n, s_len, 4, B_KV, HEAD_DIM))
            bw_p.append(proj_p[:, s_len - WIN_B:, 1280:1536].reshape(n, WIN_B, 2, B_KV, HEAD_DIM))
            b_s.append(proj_s[:, 768:1280].reshape(ns, 1, 4, B_KV, HEAD_DIM))
            bw_s.append(proj_s[:, 1280:1536].reshape(ns, 1, 2, B_KV, HEAD_DIM))
    return (xp, xs.reshape(ns, 1, d), jnp.stack(mem_new, axis=0),
            jnp.stack(a_p[0], axis=0), jnp.stack(a_p[1], axis=0), jnp.stack(a_p[2], axis=0),
            jnp.stack(b_p, axis=2), jnp.stack(bw_p, axis=0),
            jnp.stack(a_s[0], axis=0), jnp.stack(a_s[1], axis=0), jnp.stack(a_s[2], axis=0),
            jnp.stack(b_s, axis=2), jnp.stack(bw_s, axis=0))
```

```python
import functools

import numpy as np
import jax
import jax.numpy as jnp
from jax import lax
from jax.experimental import pallas as pl
from jax.experimental.pallas import tpu as pltpu

F32 = jnp.float32
BF16 = jnp.bfloat16

D_MODEL = 1024
HEAD_DIM = 64
SCALE = HEAD_DIM ** -0.5
LOG2E = 1.4426950408889634
RMS_EPS = 1e-6
N_MEM = 256
X_WIDTH = 256
A_PATTERNS = ((128, 1), (512, 4), (2048, 16))
A_WIDTH = 256
W_IN_A = 3072
B_HEADS = 12
B_KV = 2
B_GROUP = 6
B_WIDTH = 768
W_IN_B = 2852
W_IN_B_PAD = 2944
CMP_LEN = 32
CMP_STRIDE = 16
CMP_HIDDEN = 128
SEL_BLOCK = 64
SEL_TOPK = 16
WIN_B = 512
Q_BLOCK = 128
FORCE_SCORE = 1e4
PAGE_SIZE = 128
NEG = -1e30
GROUPS_PER_CHUNK = 4
V7X_VMEM_BYTES = 64 * 1024 * 1024
VMEM_LIMIT = V7X_VMEM_BYTES * 7 // 8
ROW_TILE = 512

NT = (((1,), (1,)), ((), ()))


def _alibi(n):
    k = np.arange(1, n + 1, dtype=np.float32)
    return [float(v) for v in np.float32(2.0) ** (np.float32(-8.0) * k / np.float32(n))]


SLOPES_A = _alibi(12)
SLOPES_B = _alibi(12)


def _params(n_axes):
    return pltpu.CompilerParams(dimension_semantics=("arbitrary",) * n_axes, vmem_limit_bytes=VMEM_LIMIT)


def _sigmoid(x):
    return 1.0 / (1.0 + jnp.exp(-x))


def _silu(x):
    return x * _sigmoid(x)


def _iota(shape, dim):
    return lax.broadcasted_iota(jnp.int32, shape, dim)


def _split3(x):
    hi = x.astype(BF16)
    r1 = x - hi.astype(F32)
    mid = r1.astype(BF16)
    lo = (r1 - mid.astype(F32)).astype(BF16)
    return hi, mid, lo


def _rms_proj_body(x_ref, g_ref, w_ref, o_ref, *side_ref, side):
    x = x_ref[...]
    y = x * lax.rsqrt(jnp.mean(x * x, axis=-1, keepdims=True) + RMS_EPS)
    y = (y * g_ref[...]).astype(BF16)
    o = jnp.dot(y, w_ref[...], preferred_element_type=F32)
    o_ref[...] = o
    if side is not None:
        side_ref[0][...] = o[:, side[0]:side[0] + side[1]].astype(BF16)


def _rms_proj(x, g, w, tm, side=None):
    m, d = x.shape
    n = w.shape[1]
    out_specs = [pl.BlockSpec((tm, n), lambda i: (i, 0))]
    out_shape = [jax.ShapeDtypeStruct((m, n), F32)]
    if side is not None:
        out_specs.append(pl.BlockSpec((tm, side[1]), lambda i: (i, 0)))
        out_shape.append(jax.ShapeDtypeStruct((m, side[1]), BF16))
    outs = pl.pallas_call(
        functools.partial(_rms_proj_body, side=side),
        grid=(m // tm,),
        in_specs=[pl.BlockSpec((tm, d), lambda i: (i, 0)),
                  pl.BlockSpec((1, d), lambda i: (0, 0)),
                  pl.BlockSpec((d, n), lambda i: (0, 0))],
        out_specs=out_specs,
        out_shape=out_shape,
        compiler_params=_params(1),
        name="rms_proj",
    )(x, g.reshape(1, d), w)
    return outs[0] if side is None else outs


A_SPAN = 2048


def _a_prompt_body(*refs, dil, slopes):
    ins, outs = refs[:10], refs[10:]
    t = pl.program_id(1)
    blk_rows = 128 * dil
    n_ub = A_SPAN // blk_rows
    i = _iota((128, 256), 0)
    j = _iota((128, 256), 1)
    back = 128 + i - j
    in_band = (back >= 0) & (back <= 128)
    dist = (back * dil).astype(F32)
    lane = _iota((128, 128), 1)

    def rows(ref, start):
        return ref[0, pl.ds(start, 128, stride=dil), :] if dil > 1 else ref[0, pl.ds(start, 128), :]

    def block(pair, ub, r, first):
        q_ref, k_ref, kp_ref, v_ref, vp_ref = ins[pair * 5:(pair + 1) * 5]
        o_ref, l_ref = outs[pair * 2:(pair + 1) * 2]
        start = ub * blk_rows + r
        if first:
            k_prev, v_prev = rows(kp_ref, r), rows(vp_ref, r)
            valid = in_band & (j >= jnp.where(t > 0, 0, 128))
        else:
            k_prev, v_prev = rows(k_ref, start - blk_rows), rows(v_ref, start - blk_rows)
            valid = in_band
        qp = rows(q_ref, start)
        kp = jnp.concatenate([k_prev, rows(k_ref, start)], axis=0).astype(BF16)
        vp = jnp.concatenate([v_prev, rows(v_ref, start)], axis=0).astype(BF16)
        o_pair = None
        l_pair = None
        for hh in range(2):
            hm = (lane < 64) if hh == 0 else (lane >= 64)
            qm = jnp.where(hm, qp, 0.0).astype(BF16)
            s = lax.dot_general(qm, kp, NT, preferred_element_type=F32) * SCALE
            s = jnp.where(valid, s - slopes[pair * 2 + hh] * dist, NEG)
            m = jnp.max(s, axis=-1, keepdims=True)
            e = jnp.exp(s - m)
            den = jnp.sum(e, axis=-1, keepdims=True)
            oh = jnp.dot((e * (1.0 / den)).astype(BF16), vp, preferred_element_type=F32)
            lh = jnp.broadcast_to(m + jnp.log(den), (128, 128))
            o_pair = oh if hh == 0 else jnp.where(lane < 64, o_pair, oh)
            l_pair = lh if hh == 0 else jnp.where(lane < 64, l_pair, lh)
        if dil > 1:
            o_ref[0, pl.ds(start, 128, stride=dil), :] = o_pair
            l_ref[0, pl.ds(start, 128, stride=dil), :] = l_pair
        else:
            o_ref[0, pl.ds(start, 128), :] = o_pair
            l_ref[0, pl.ds(start, 128), :] = l_pair

    def run(count, fn):
        if count == 0:
            return
        unroll = next(c for c in (8, 6, 5, 4, 3, 2, 1) if count % c == 0)

        def body(it, c):
            for k in range(unroll):
                fn(it * unroll + k)
            return c
        lax.fori_loop(0, count // unroll, body, 0)

    for pair in range(2):
        run(dil, lambda r, pair=pair: block(pair, 0, r, True))
        run((n_ub - 1) * dil, lambda idx, pair=pair: block(pair, 1 + idx // dil, idx % dil, False))


def _a_prompt_group(proj, g, dil):
    n, s_len, _ = proj.shape
    blk_rows = 128 * dil
    per_span = A_SPAN // blk_rows
    body = functools.partial(_a_prompt_body, dil=dil, slopes=tuple(SLOPES_A[g * 4:(g + 1) * 4]))
    cur = lambda col: pl.BlockSpec((1, A_SPAN, 128), lambda b, t: (b, t, col))
    prev = lambda col: pl.BlockSpec((1, blk_rows, 128), lambda b, t: (b, jnp.maximum(t * per_span - 1, 0), col))
    in_specs = []
    for pair in range(2):
        qc, kc, vc = 2 * g + pair, 6 + 2 * g + pair, 12 + 2 * g + pair
        in_specs += [cur(qc), cur(kc), prev(kc), cur(vc), prev(vc)]
    out_spec = pl.BlockSpec((1, A_SPAN, 128), lambda b, t: (b, t, 0))
    return pl.pallas_call(
        body,
        grid=(n, s_len // A_SPAN),
        in_specs=in_specs,
        out_specs=[out_spec] * 4,
        out_shape=[jax.ShapeDtypeStruct((n, s_len, 128), F32)] * 4,
        compiler_params=_params(2),
        name=f"a_prompt_g{g}",
    )(*([proj] * 10))


def _cross_rows(qx, kx, vx):
    tm = qx.shape[0]
    lane = _iota((tm, 128), 1)
    outs = []
    for pair in range(2):
        sl = slice(pair * 128, (pair + 1) * 128)
        qp, kp, vp = qx[:, sl], kx[:, sl], vx[:, sl]
        o_pair = None
        for hh in range(2):
            hm = (lane < 64) if hh == 0 else (lane >= 64)
            qm = jnp.where(hm, qp, 0.0).astype(BF16)
            s = lax.dot_general(qm, kp, NT, preferred_element_type=F32) * SCALE
            m = jnp.max(s, axis=-1, keepdims=True)
            e = jnp.exp(s - m)
            p = (e / jnp.sum(e, axis=-1, keepdims=True)).astype(BF16)
            oh = jnp.dot(p, vp, preferred_element_type=F32)
            o_pair = oh if hh == 0 else jnp.where(lane < 64, o_pair, oh)
        outs.append(o_pair)
    return jnp.concatenate(outs, axis=1)


def _out_norm_residual(x, z, w, g):
    y = jnp.dot(z.astype(BF16), w, preferred_element_type=F32)
    y = y * lax.rsqrt(jnp.mean(y * y, axis=-1, keepdims=True) + RMS_EPS)
    return x + y * g


def _finish_a_body(x_ref, *refs):
    gm_ref, qx_ref, gx_ref, mkv_ref, w_ref, g_ref, out_ref = refs[12:]
    mixes = []
    for pair in range(2):
        os_ = [refs[4 * g + 2 * pair][0] for g in range(3)]
        ls_ = [refs[4 * g + 2 * pair + 1][0] for g in range(3)]
        m = jnp.maximum(jnp.maximum(ls_[0], ls_[1]), ls_[2])
        es = [jnp.exp(l - m) for l in ls_]
        mixes.append((es[0] * os_[0] + es[1] * os_[1] + es[2] * os_[2]) / (es[0] + es[1] + es[2]))
    mix = jnp.concatenate(mixes, axis=1)
    mkv = mkv_ref[0]
    cx = _cross_rows(qx_ref[0], mkv[:, :X_WIDTH].astype(BF16), mkv[:, X_WIDTH:].astype(BF16))
    z = jnp.concatenate([mix * _silu(gm_ref[0]), cx * _silu(gx_ref[0])], axis=1)
    out_ref[0] = _out_norm_residual(x_ref[0], z, w_ref[...], g_ref[...])


def _finish_b_body(x_ref, mix_ref, gm_ref, qx_ref, gx_ref, mkv_ref, w_ref, g_ref, out_ref):
    mkv = mkv_ref[0]
    cx = _cross_rows(qx_ref[0], mkv[:, :X_WIDTH].astype(BF16), mkv[:, X_WIDTH:].astype(BF16))
    z = jnp.concatenate([mix_ref[0] * _silu(gm_ref[0]), cx * _silu(gx_ref[0])], axis=1)
    out_ref[0] = _out_norm_residual(x_ref[0], z, w_ref[...], g_ref[...])


def _finish_a(x, ols, proj, mkv, w_out, g_post, tm):
    n, s_len, d = x.shape
    row = lambda w, c: pl.BlockSpec((1, tm, w), lambda b, t: (b, t, c))
    in_specs = ([row(d, 0)] + [row(128, 0)] * 12 + [row(256, 9), row(256, 10), row(256, 11)]
                + [pl.BlockSpec((1, N_MEM, 2 * X_WIDTH), lambda b, t: (b, 0, 0)),
                   pl.BlockSpec(w_out.shape, lambda b, t: (0, 0)),
                   pl.BlockSpec((1, d), lambda b, t: (0, 0))])
    return pl.pallas_call(
        _finish_a_body, grid=(n, s_len // tm), in_specs=in_specs, out_specs=row(d, 0),
        out_shape=jax.ShapeDtypeStruct((n, s_len, d), F32), compiler_params=_params(2), name="finish_a",
    )(x, *ols, proj, proj, proj, mkv, w_out, g_post.reshape(1, d))


def _finish_b(x, mix, proj, mkv, w_out, g_post, tm):
    n, s_len, d = x.shape
    row = lambda w, c: pl.BlockSpec((1, tm, w), lambda b, t: (b, t, c))
    in_specs = [row(d, 0), row(B_WIDTH, 0), row(B_WIDTH, 2), row(256, 9), row(256, 10),
                pl.BlockSpec((1, N_MEM, 2 * X_WIDTH), lambda b, t: (b, 0, 0)),
                pl.BlockSpec(w_out.shape, lambda b, t: (0, 0)),
                pl.BlockSpec((1, d), lambda b, t: (0, 0))]
    return pl.pallas_call(
        _finish_b_body, grid=(n, s_len // tm), in_specs=in_specs, out_specs=row(d, 0),
        out_shape=jax.ShapeDtypeStruct((n, s_len, d), F32), compiler_params=_params(2), name="finish_b",
    )(x, mix, proj, proj, proj, mkv, w_out, g_post.reshape(1, d))


def _tail_body(x_ref, z_ref, w_ref, g_ref, out_ref):
    out_ref[...] = _out_norm_residual(x_ref[...], z_ref[...], w_ref[...], g_ref[...])


def _tail(x, z, w_out, g_post):
    m, d = x.shape
    full = lambda a: pl.BlockSpec(a.shape, lambda i: (0,) * a.ndim)
    g2 = g_post.reshape(1, d)
    return pl.pallas_call(
        _tail_body, grid=(1,), in_specs=[full(x), full(z), full(w_out), full(g2)], out_specs=full(x),
        out_shape=jax.ShapeDtypeStruct((m, d), F32), compiler_params=_params(1), name="sample_tail",
    )(x, z, w_out, g2)


def _compress_rows(load_rows, pos_ref, w1_ref, w2_ref, n_cmp, between=None):
    outs = []
    half = 2 * CMP_HIDDEN
    for t in range(2):
        y = jnp.concatenate([load_rows(t, l).astype(BF16) for l in range(CMP_STRIDE)], axis=1)
        ab = jnp.dot(y, w1_ref[t], preferred_element_type=F32) + pos_ref[t]
        h = ab[:, :half] + pltpu.roll(ab[:, half:], n_cmp - 1, axis=0)
        outs.append(jnp.dot(_silu(h).astype(BF16), w2_ref[t], preferred_element_type=F32))
        if between is not None:
            between(t)
    return outs


def _compress_body(k_ref, v_ref, pos_ref, w1_ref, w2_ref, o_ref, *, n_cmp):
    refs = (k_ref, v_ref)
    load = lambda t, l: refs[t][0, pl.ds(l, n_cmp, stride=CMP_STRIDE), :]
    ck, cv = _compress_rows(load, pos_ref, w1_ref, w2_ref, n_cmp)
    o_ref[0, :, 0:128] = ck.astype(BF16)
    o_ref[0, :, 128:256] = cv.astype(BF16)


def _compress_prompt(proj, posw, w1bd, w2bd):
    n, s_len, _ = proj.shape
    n_cmp = s_len // CMP_STRIDE
    full = lambda a: pl.BlockSpec(a.shape, lambda b: (0,) * a.ndim)
    return pl.pallas_call(
        functools.partial(_compress_body, n_cmp=n_cmp), grid=(n,),
        in_specs=[pl.BlockSpec((1, s_len, 128), lambda b: (b, 0, 6)), pl.BlockSpec((1, s_len, 128), lambda b: (b, 0, 7)),
                  full(posw), full(w1bd), full(w2bd)],
        out_specs=pl.BlockSpec((1, n_cmp, 256), lambda b: (b, 0, 0)),
        out_shape=jax.ShapeDtypeStruct((n, n_cmp, 256), BF16), compiler_params=_params(1), name="compress_prompt",
    )(proj, proj, posw, w1bd, w2bd)


def _place_heads(tiles, lane):
    chunks = []
    for c in range(B_HEADS // 2):
        t0, t1 = tiles[2 * c], tiles[2 * c + 1]
        if (2 * c) // B_GROUP == 1:
            t0 = pltpu.roll(t0, 64, axis=1)
        if (2 * c + 1) // B_GROUP == 0:
            t1 = pltpu.roll(t1, 64, axis=1)
        chunks.append(jnp.where(lane < 64, t0, t1))
    return jnp.concatenate(chunks, axis=1)


def _masked_softmax_rows(s, ok):
    s = jnp.where(ok, s, NEG)
    m = jnp.max(s, axis=-1, keepdims=True)
    e = jnp.where(ok, jnp.exp(s - m), 0.0)
    den = jnp.maximum(jnp.sum(e, axis=-1, keepdims=True), 1e-30)
    return e * (1.0 / den)


ONES_ROWS = 16


def _values_and_ones(v_ref, idx, kv):
    v = v_ref[idx + (slice(kv * HEAD_DIM, (kv + 1) * HEAD_DIM), slice(None))]
    return jnp.concatenate([v, jnp.ones((ONES_ROWS, v.shape[1]), BF16)], axis=0)


def _head_tile(num, den, kv):
    x = num if den is None else num * (1.0 / den)
    z = jnp.zeros_like(x)
    return jnp.concatenate([x, z] if kv == 0 else [z, x], axis=0).T


def _nsa_prompt_body(q_ref, gt_ref, kc_ref, vct_ref, kw_ref, vwt_ref, ks_ref, vst_ref, mt_ref, eg_ref,
                     sl_ref, out_ref, q6_sc, m_sc, acc_sc, sel_sc, words_sm, idx_sm, *, s_len):
    qb = pl.program_id(1)
    qstart = qb * Q_BLOCK
    n_cmp = s_len // CMP_STRIDE
    q = q_ref[0] * (SCALE * LOG2E)
    lane = _iota((Q_BLOCK, 128), 1)
    tq_row = qstart + _iota((1, Q_BLOCK), 1)
    oc_t, os_t, ow_t = [None] * B_HEADS, [None] * B_HEADS, [None] * B_HEADS

    sub = _iota((128, 128), 0)
    psums = []

    for kv in range(B_KV):
        for g in range(B_GROUP):
            h = kv * B_GROUP + g
            ch = q[:, (h // 2) * 128:(h // 2 + 1) * 128]
            if h % 2 == 1:
                ch = pltpu.roll(ch, 64, axis=1)
            q6_sc[kv, g * 128:(g + 1) * 128, :] = jnp.where(lane < 64, ch, sl_ref[h:h + 1, :]).astype(BF16)

    cok = (CMP_STRIDE * _iota((n_cmp, Q_BLOCK), 0) + (CMP_LEN - 1)) <= tq_row
    q_ok = tq_row >= (CMP_LEN - 1)
    n_wb = WIN_B // Q_BLOCK + 1

    def cmp_scores(kv):
        return lax.dot_general(kc_ref[0, kv], q6_sc[kv], NT, preferred_element_type=F32)

    def cmp_finish(kv, s_t):
        psum = jnp.zeros((n_cmp, Q_BLOCK), F32)
        ps = []
        for g in range(B_GROUP):
            s = jnp.where(cok, s_t[:, g * 128:(g + 1) * 128], NEG)
            e = jnp.exp2(s - jnp.max(s, axis=0, keepdims=True))
            p = e * jnp.where(q_ok, 1.0 / jnp.sum(e, axis=0, keepdims=True), 0.0)
            psum = psum + p
            ps.append(p.astype(BF16))
        oc = jnp.dot(vct_ref[0, kv * HEAD_DIM:(kv + 1) * HEAD_DIM, :], jnp.concatenate(ps, axis=1),
                     preferred_element_type=F32)
        for g in range(B_GROUP):
            oc_t[kv * B_GROUP + g] = _head_tile(oc[:, g * 128:(g + 1) * 128], None, kv)
        psums.append(psum)

    def win_scores(kv):
        kparts, vparts, pparts = [], [], []
        for wb in range(n_wb):
            b_raw = qb - (n_wb - 1) + wb
            b = jnp.maximum(b_raw, 0)
            r0 = pl.multiple_of(b * Q_BLOCK, Q_BLOCK)
            kparts.append(kw_ref[0, kv, pl.ds(r0, Q_BLOCK), :])
            vparts.append(_values_and_ones(vwt_ref, (0, b), kv))
            pparts.append(jnp.where(b_raw >= 0, r0, s_len) + _iota((128, Q_BLOCK), 0))
        s_t = lax.dot_general(jnp.concatenate(kparts, axis=0), q6_sc[kv], NT, preferred_element_type=F32)
        return s_t, jnp.concatenate(vparts, axis=1), jnp.concatenate(pparts, axis=0)

    def win_finish(kv, s_t, v_t, kpos):
        dw = tq_row - kpos
        wok = (dw >= 0) & (dw <= WIN_B)
        ps = []
        for g in range(B_GROUP):
            s = jnp.where(wok, s_t[:, g * 128:(g + 1) * 128], NEG)
            ps.append(jnp.exp2(s - jnp.max(s, axis=0, keepdims=True)).astype(BF16))
        ow = jnp.dot(v_t, jnp.concatenate(ps, axis=1), preferred_element_type=F32)
        for g in range(B_GROUP):
            cols = slice(g * 128, (g + 1) * 128)
            ow_t[kv * B_GROUP + g] = _head_tile(ow[0:HEAD_DIM, cols], ow[HEAD_DIM:HEAD_DIM + 1, cols], kv)

    sc0 = cmp_scores(0)
    sw0 = win_scores(0)
    cmp_finish(0, sc0)
    sc1 = cmp_scores(1)
    win_finish(0, *sw0)
    sw1 = win_scores(1)
    cmp_finish(1, sc1)
    win_finish(1, *sw1)

    mt = mt_ref[...]
    blk = _iota((128, Q_BLOCK), 0)
    ql = _iota((128, Q_BLOCK), 1)
    cur = jnp.where(ql >= SEL_BLOCK, qb * 2 + 1, qb * 2)
    forced = (blk == 0) | (blk == cur) | (blk == cur - 1)
    blkf = blk.astype(F32)
    imps = [jnp.where((blk > cur) | forced, -jnp.inf,
                      sum(jnp.dot(mt, t, preferred_element_type=F32) for t in _split3(psum))) for psum in psums]
    sels = [jnp.where(forced, 1.0, 0.0)] * B_KV
    for _ in range(SEL_TOPK - 3):
        for kv in range(B_KV):
            mx = jnp.max(imps[kv], axis=0, keepdims=True)
            idx = jnp.min(jnp.where(imps[kv] == mx, blkf, 1e9), axis=0, keepdims=True)
            hit = blkf == idx
            sels[kv] = jnp.where(hit, 1.0, sels[kv])
            imps[kv] = jnp.where(hit, -jnp.inf, imps[kv])

    blk_col = _iota((128, 1), 0)
    weight = lax.shift_left(jnp.ones((128, 1), jnp.int32), blk_col & 15).astype(F32)
    for kv in range(B_KV):
        sel_sc[kv] = sels[kv]
        contrib = jnp.max(sels[kv], axis=1, keepdims=True) * weight
        for w in range(8):
            words_sm[kv * 8 + w] = jnp.sum(contrib[16 * w:16 * (w + 1), :]).astype(jnp.int32)

    m_sc[...] = jnp.full(m_sc.shape, NEG, F32)
    acc_sc[...] = jnp.zeros(acc_sc.shape, F32)
    list_len = idx_sm.shape[0] // B_KV
    cnts = []
    n_grp = s_len // Q_BLOCK
    for kv in range(B_KV):
        cnt = jnp.int32(0)
        for w in range((n_grp + 7) // 8):
            word = words_sm[kv * 8 + w]
            for j in range(min(8, n_grp - 8 * w)):
                idx_sm[kv * list_len + cnt] = 8 * w + j
                cnt = cnt + jnp.where(((word >> (2 * j)) & 3) != 0, 1, 0)
        cnts.append(cnt)
    n_chunks = (jnp.maximum(cnts[0], cnts[1]) + GROUPS_PER_CHUNK - 1) // GROUPS_PER_CHUNK
    for kv in range(B_KV):
        def pad(i, c, kv=kv):
            idx_sm[kv * list_len + i] = -1
            return c
        lax.fori_loop(cnts[kv], n_chunks * GROUPS_PER_CHUNK, pad, 0)

    sub8 = _iota((8, Q_BLOCK), 0)

    def group_hits(kv, gi):
        rows8 = sel_sc[kv, pl.ds(pl.multiple_of((gi >> 2) * 8, 8), 8), :]
        r = (gi & 3) * 2
        lo = jnp.sum(jnp.where(sub8 == r, rows8, 0.0), axis=0, keepdims=True)
        hi = jnp.sum(jnp.where(sub8 == r + 1, rows8, 0.0), axis=0, keepdims=True)
        return jnp.where(sub < SEL_BLOCK, lo, hi)

    def chunk_scores(c, kv):
        kts, vts, hits, kposs = [], [], [], []
        for j in range(GROUPS_PER_CHUNK):
            gi_raw = idx_sm[kv * list_len + c * GROUPS_PER_CHUNK + j]
            gi = jnp.maximum(gi_raw, 0)
            k0 = pl.multiple_of(gi * Q_BLOCK, Q_BLOCK)
            kts.append(ks_ref[0, kv, pl.ds(k0, Q_BLOCK), :])
            vts.append(_values_and_ones(vst_ref, (0, gi), kv))
            hits.append(group_hits(kv, gi))
            kposs.append(jnp.where(gi_raw >= 0, k0, s_len) + _iota((128, 128), 0))
        s_t = lax.dot_general(jnp.concatenate(kts, axis=0), q6_sc[kv], NT, preferred_element_type=F32)
        ok = ((jnp.concatenate(hits, axis=0) > 0.5)
              & (jnp.concatenate(kposs, axis=0) <= qstart + _iota((GROUPS_PER_CHUNK * 128, 128), 1)))
        return s_t, ok, jnp.concatenate(vts, axis=1)

    def chunk_update(kv, s_t, ok, v_t):
        m_old = m_sc[kv]
        m_new, ps = [], []
        for g in range(B_GROUP):
            cols = slice(g * 128, (g + 1) * 128)
            s = jnp.where(ok, s_t[:, cols], NEG)
            mg = jnp.maximum(m_old[:, cols], jnp.max(s, axis=0, keepdims=True))
            ps.append(jnp.exp2(s - mg).astype(BF16))
            m_new.append(mg)
        m_new = jnp.concatenate(m_new, axis=1)
        pv = jnp.dot(v_t, jnp.concatenate(ps, axis=1), preferred_element_type=F32)
        acc_sc[kv] = jnp.exp2(m_old - m_new) * acc_sc[kv] + pv
        m_sc[kv] = m_new

    def chunk(c, carry):
        first = chunk_scores(c, 0)
        second = chunk_scores(c, 1)
        chunk_update(0, *first)
        chunk_update(1, *second)
        return carry

    lax.fori_loop(0, n_chunks, chunk, 0)
    for kv in range(B_KV):
        for g in range(B_GROUP):
            cols = slice(g * 128, (g + 1) * 128)
            os_t[kv * B_GROUP + g] = _head_tile(acc_sc[kv, 0:HEAD_DIM, cols], acc_sc[kv, HEAD_DIM:HEAD_DIM + 1, cols], kv)

    sg = _sigmoid(gt_ref[0])
    eg = eg_ref[...]
    gexp = sum(jnp.dot(t, eg, preferred_element_type=F32) for t in _split3(sg))
    out_ref[0] = (gexp[:, 0:B_WIDTH] * _place_heads(oc_t, lane)
                  + gexp[:, B_WIDTH:2 * B_WIDTH] * _place_heads(os_t, lane)
                  + gexp[:, 2 * B_WIDTH:] * _place_heads(ow_t, lane))


def _nsa_prompt(proj, kc, vc_t, kw, vw_t, ks, vs_t, mt, eg, slope_lanes):
    n, s_len, _ = proj.shape
    n_cmp = s_len // CMP_STRIDE
    n_grp = s_len // Q_BLOCK
    full = lambda a: pl.BlockSpec(a.shape, lambda b, t: (0,) * a.ndim)
    per_n = lambda a: pl.BlockSpec((1,) + a.shape[1:], lambda b, t: (b,) + (0,) * (a.ndim - 1))
    return pl.pallas_call(
        functools.partial(_nsa_prompt_body, s_len=s_len),
        grid=(n, n_grp),
        in_specs=[pl.BlockSpec((1, Q_BLOCK, B_WIDTH), lambda b, t: (b, t, 0)),
                  pl.BlockSpec((1, Q_BLOCK, 128), lambda b, t: (b, t, 22)),
                  per_n(kc), per_n(vc_t), per_n(kw), per_n(vw_t), per_n(ks), per_n(vs_t),
                  full(mt), full(eg), full(slope_lanes)],
        out_specs=pl.BlockSpec((1, Q_BLOCK, B_WIDTH), lambda b, t: (b, t, 0)),
        out_shape=jax.ShapeDtypeStruct((n, s_len, B_WIDTH), F32),
        scratch_shapes=[pltpu.VMEM((B_KV, B_GROUP * Q_BLOCK, 128), BF16), pltpu.VMEM((B_KV, 1, B_GROUP * Q_BLOCK), F32),
                        pltpu.VMEM((B_KV, HEAD_DIM + ONES_ROWS, B_GROUP * Q_BLOCK), F32),
                        pltpu.VMEM((B_KV, 128, Q_BLOCK), F32), pltpu.SMEM((B_KV * 8,), jnp.int32),
                        pltpu.SMEM((B_KV * (n_grp + GROUPS_PER_CHUNK),), jnp.int32)],
        compiler_params=_params(2), name="nsa_prompt",
    )(proj, proj, kc, vc_t, kw, vw_t, ks, vs_t, mt, eg, slope_lanes)


POS_LANE = 64


def _slope_lanes():
    out = np.zeros((B_HEADS, 128), np.float32)
    for h, slope in enumerate(SLOPES_B):
        s = np.float32(np.float64(slope) * LOG2E)
        hi = np.float32(np.asarray(s, np.float32).astype(jnp.bfloat16))
        mid = np.float32(np.asarray(np.float32(s) - hi, np.float32).astype(jnp.bfloat16))
        lo = np.float32(np.asarray(np.float32(s) - hi - mid, np.float32).astype(jnp.bfloat16))
        out[h, POS_LANE:POS_LANE + 6] = [hi, mid, lo, hi, mid, lo]
    return out


def _keys_with_pos(k2, pos):
    n, n_keys, _ = k2.shape
    lo = (pos % Q_BLOCK).astype(BF16)[None, :, None]
    hi = (pos - pos % Q_BLOCK).astype(BF16)[None, :, None]
    tail = jnp.concatenate([jnp.broadcast_to(lo, (n, n_keys, 3)), jnp.broadcast_to(hi, (n, n_keys, 3)),
                            jnp.zeros((n, n_keys, 128 - POS_LANE - 6), BF16)], axis=-1)
    return jnp.stack([jnp.concatenate([k2[..., kv * 64:(kv + 1) * 64], tail], axis=-1) for kv in range(B_KV)], axis=1)


def _values_by_group(v2):
    n, s_len, _ = v2.shape
    return v2.reshape(n, s_len // Q_BLOCK, Q_BLOCK, 128).transpose(0, 1, 3, 2)


def _heads_rows(vec, n_rows, width):
    r = _iota((n_rows, width), 0)
    l = _iota((n_rows, width), 1)
    hm = (l >= r * HEAD_DIM) & (l < r * HEAD_DIM + HEAD_DIM)
    return jnp.where(hm, jnp.broadcast_to(vec, (n_rows, width)), 0.0), hm


def _bf(x):
    return x.astype(BF16).astype(F32)


def _row_consts(n_rows, vals):
    r = _iota((n_rows, 1), 0)
    out = jnp.zeros((n_rows, 1), F32)
    for i, v in enumerate(vals):
        out = jnp.where(r == i, v, out)
    return out


def _rows_last(cache):
    nd = cache.ndim
    return cache.transpose(tuple(range(nd - 4)) + (nd - 3, nd - 2, nd - 1, nd - 4))


def _kv_t(ref, t):
    x = ref[0, 0, t]
    return x.reshape(x.shape[0] * x.shape[1], x.shape[2]).astype(BF16)


def _sample_cross(qx_row, k_t, v_t):
    q8, hm = _heads_rows(qx_row, 8, X_WIDTH)
    s = jnp.dot(q8.astype(BF16), k_t, preferred_element_type=F32) * SCALE
    e = jnp.exp(s - jnp.max(s, axis=-1, keepdims=True))
    p = (e * (1.0 / jnp.sum(e, axis=-1, keepdims=True))).astype(BF16)
    o8 = lax.dot_general(p, v_t, NT, preferred_element_type=F32)
    return jnp.sum(jnp.where(hm, o8, 0.0), axis=0, keepdims=True)


def _sample_a_body(row_ref, c0_ref, c1_ref, c2_ref, mkv_ref, z_ref):
    row = row_ref[0]
    outs, lses = [], []
    hm = None
    for g, (win, dil) in enumerate(A_PATTERNS):
        cref = (c0_ref, c1_ref, c2_ref)[g]
        q8, hm = _heads_rows(row[:, g * 256:(g + 1) * 256], 8, A_WIDTH)
        knew = row[:, 768 + g * 256:768 + (g + 1) * 256]
        vnew = row[:, 1536 + g * 256:1536 + (g + 1) * 256]
        q8b = q8.astype(BF16)
        slope = _row_consts(8, SLOPES_A[g * 4:(g + 1) * 4])
        s = jnp.dot(q8b, _kv_t(cref, 0), preferred_element_type=F32) * SCALE
        r = _iota((8, win), 1)
        s = jnp.where((r & (dil - 1)) == 0, s - slope * (win - r).astype(F32), NEG)
        s_new = jnp.sum(q8b.astype(F32) * _bf(knew), axis=-1, keepdims=True) * SCALE
        m = jnp.maximum(jnp.max(s, axis=-1, keepdims=True), s_new)
        e = jnp.exp(s - m)
        e_new = jnp.exp(s_new - m)
        den = jnp.sum(e, axis=-1, keepdims=True) + e_new
        inv = 1.0 / den
        o8 = (lax.dot_general((e * inv).astype(BF16), _kv_t(cref, 1), NT, preferred_element_type=F32)
              + _bf(e_new * inv) * _bf(vnew))
        outs.append(o8)
        lses.append(m + jnp.log(den))
    mx = jnp.maximum(jnp.maximum(lses[0], lses[1]), lses[2])
    ws = [jnp.exp(l - mx) for l in lses]
    mix8 = (ws[0] * outs[0] + ws[1] * outs[1] + ws[2] * outs[2]) / (ws[0] + ws[1] + ws[2])
    mix = jnp.sum(jnp.where(hm, mix8, 0.0), axis=0, keepdims=True)
    cx = _sample_cross(row[:, 2560:2816], _kv_t(mkv_ref, 0), _kv_t(mkv_ref, 1))
    z_ref[0] = jnp.concatenate([mix * _silu(row[:, 2304:2560]), cx * _silu(row[:, 2816:3072])], axis=1)


def _layer_block(cache_t, layer):
    return pl.BlockSpec((1, 1) + cache_t.shape[2:], lambda b, *_: (layer, b, 0, 0, 0, 0))


def _sample_a(proj_s, caches_t, mem_t, li, i):
    ns = proj_s.shape[0]
    row3 = proj_s.reshape(ns, 1, W_IN_A)
    return pl.pallas_call(
        _sample_a_body, grid=(ns,),
        in_specs=[pl.BlockSpec((1, 1, W_IN_A), lambda b: (b, 0, 0))] + [_layer_block(c, li) for c in caches_t]
                 + [_layer_block(mem_t, i)],
        out_specs=pl.BlockSpec((1, 1, A_WIDTH + X_WIDTH), lambda b: (b, 0, 0)),
        out_shape=jax.ShapeDtypeStruct((ns, 1, A_WIDTH + X_WIDTH), F32), compiler_params=_params(1), name="sample_a",
    )(row3, *caches_t, mem_t).reshape(ns, A_WIDTH + X_WIDTH)


def _q16(row):
    r = _iota((16, 128), 0)
    l = _iota((16, 128), 1)
    acc = jnp.zeros((16, 128), F32)
    for c in range(B_HEADS // 2):
        ch = jnp.broadcast_to(row[:, c * 128:(c + 1) * 128], (16, 128))
        rolled = pltpu.roll(ch, 64, axis=1)
        for hh in range(2):
            h = 2 * c + hh
            kv = h // B_GROUP
            lm = (l < 64) if kv == 0 else (l >= 64)
            acc = jnp.where((r == h) & lm, ch if hh == kv else rolled, acc)
    return acc * SCALE


def _sample_b1_body(pt_ref, row_ref, pos_ref, w1_ref, w2_ref, mm_ref, pages_ref, oc_ref, sel_ref,
                    buf_a, buf_b, rows_a, rows_b, sem, imp_sc, *, li, n_pages, ns):
    n = pl.program_id(0)
    past = n_pages * PAGE_SIZE
    n_cmp = past // CMP_STRIDE
    bufs, rows_bufs = (buf_a, buf_b), (rows_a, rows_b)

    def page_copy(page, p, s):
        return pltpu.make_async_copy(pages_ref.at[page, li, pl.ds(0, 2)], bufs[s].at[p], sem.at[s])

    def fetch(nn, s):
        def body(p, c):
            page_copy(pt_ref[nn * n_pages + p], p, s).start()
            return c
        lax.fori_loop(0, n_pages, body, 0)

    def wait(s):
        def body(p, c):
            page_copy(0, p, s).wait()
            return c
        lax.fori_loop(0, n_pages, body, 0)

    def to_rows(s, p):
        r0 = p * PAGE_SIZE if isinstance(p, int) else pl.multiple_of(p * PAGE_SIZE, PAGE_SIZE)
        for t in range(2):
            rows_bufs[s][t, pl.ds(r0, PAGE_SIZE), :] = bufs[s][p, t].reshape(2 * HEAD_DIM, PAGE_SIZE).T

    @pl.when(n == 0)
    def _():
        fetch(0, 0)
        fetch(1, 1)
        wait(0)

        def body(it, c):
            for k in range(8):
                to_rows(0, it * 8 + k)
            return c
        lax.fori_loop(0, n_pages // 8, body, 0)

    def stage(cur):
        nxt = 1 - cur

        @pl.when(n + 1 < ns)
        def _():
            wait(nxt)

        @pl.when(n + 2 < ns)
        def _():
            fetch(n + 2, cur)

        quarters = [range(q * n_pages // 4, (q + 1) * n_pages // 4) for q in range(4)]

        def next_rows(q):
            for p in quarters[q]:
                to_rows(nxt, p)

        next_rows(0)
        load = lambda t, l: rows_bufs[cur][t, pl.ds(l, n_cmp, stride=CMP_STRIDE), :]
        ck, cv = _compress_rows(load, pos_ref, w1_ref, w2_ref, n_cmp, between=lambda t: next_rows(1 + t))

        q16 = _q16(row_ref[0]).astype(BF16)
        slope = _row_consts(16, SLOPES_B)
        s = lax.dot_general(q16, ck.astype(BF16), NT, preferred_element_type=F32)
        cend = CMP_STRIDE * _iota((1, n_cmp), 1) + (CMP_LEN - 1)
        p = _masked_softmax_rows(s - slope * (past - cend).astype(F32), cend <= past)
        oc_ref[0] = jnp.dot(p.astype(BF16), cv.astype(BF16), preferred_element_type=F32)
        next_rows(3)

        r16 = _iota((16, n_cmp), 0)
        ps0 = jnp.sum(jnp.where(r16 < B_GROUP, p, 0.0), axis=0, keepdims=True)
        ps1 = jnp.sum(jnp.where((r16 >= B_GROUP) & (r16 < B_HEADS), p, 0.0), axis=0, keepdims=True)
        psum = jnp.concatenate([ps0, ps1, jnp.zeros((6, n_cmp), F32)], axis=0)
        mm = mm_ref[...]
        imp = sum(jnp.dot(t, mm, preferred_element_type=F32) for t in _split3(psum))
        blk = _iota((8, 256), 1)
        cur_blk = past // SEL_BLOCK
        forced = (blk == 0) | (blk == cur_blk) | (blk == cur_blk - 1)
        imp_sc[n] = jnp.where(blk > cur_blk, -jnp.inf, jnp.where(forced, FORCE_SCORE, imp))

    @pl.when(n % 2 == 0)
    def _():
        stage(0)

    @pl.when(n % 2 == 1)
    def _():
        stage(1)

    @pl.when(n == ns - 1)
    def _():
        impa = imp_sc[...]
        blkf = _iota(impa.shape, 2).astype(F32)
        lane = _iota((ns, 8, 128), 2)
        out = jnp.zeros((ns, 8, 128), F32)
        for r in range(SEL_TOPK):
            mx = jnp.max(impa, axis=-1, keepdims=True)
            idx = jnp.min(jnp.where(impa == mx, blkf, 1e9), axis=-1, keepdims=True)
            impa = jnp.where(blkf == idx, -jnp.inf, impa)
            out = jnp.where(lane == r, idx, out)
        sel_ref[...] = out.astype(jnp.int32)


def _sample_b1(page_table, proj_s, posw, w1bd, w2bd, mm, pages_t, li):
    ns, n_pages = page_table.shape
    past = n_pages * PAGE_SIZE
    row3 = proj_s.reshape(ns, 1, W_IN_B_PAD)
    full = lambda a: pl.BlockSpec(a.shape, lambda b, pt: (0,) * a.ndim)
    grid_spec = pltpu.PrefetchScalarGridSpec(
        num_scalar_prefetch=1, grid=(ns,),
        in_specs=[pl.BlockSpec((1, 1, W_IN_B_PAD), lambda b, pt: (b, 0, 0)), full(posw), full(w1bd), full(w2bd),
                  full(mm), pl.BlockSpec(memory_space=pl.ANY)],
        out_specs=[pl.BlockSpec((1, 16, 128), lambda b, pt: (b, 0, 0)),
                   pl.BlockSpec((ns, 8, 128), lambda b, pt: (0, 0, 0))],
        scratch_shapes=[pltpu.VMEM((n_pages, 2, B_KV, HEAD_DIM, PAGE_SIZE), F32)] * 2
                       + [pltpu.VMEM((2, past, 128), F32)] * 2
                       + [pltpu.SemaphoreType.DMA((2,)), pltpu.VMEM((ns, 8, 256), F32)])
    return pl.pallas_call(
        functools.partial(_sample_b1_body, li=li, n_pages=n_pages, ns=ns),
        grid_spec=grid_spec,
        out_shape=[jax.ShapeDtypeStruct((ns, 16, 128), F32), jax.ShapeDtypeStruct((ns, 8, 128), jnp.int32)],
        compiler_params=_params(1), name="sample_b1",
    )(page_table.reshape(-1), row3, posw, w1bd, w2bd, mm, pages_t)


def _sample_b2_body(pt_ref, sf_ref, row_ref, oc_ref, sel_ref, win_ref, mkv_ref, e16_ref, pages_ref, z_ref,
                    buf, sem, *, li, n_pages, ns):
    n = pl.program_id(0)
    past = n_pages * PAGE_SIZE
    n_blk = past // SEL_BLOCK
    per_page = PAGE_SIZE // SEL_BLOCK
    n_sel = B_KV * SEL_TOPK

    def blk_copies(page, kv, r, slot):
        return [pltpu.make_async_copy(pages_ref.at[page, li, pl.ds(2, 2), kv],
                                      buf.at[slot, :, kv, :, pl.ds(r * PAGE_SIZE, PAGE_SIZE)], sem.at[slot])]

    def fetch(nn, slot):
        for kv in range(B_KV):
            for r in range(SEL_TOPK):
                j = jnp.minimum(sf_ref[nn * n_sel + kv * SEL_TOPK + r], n_blk - 1)
                for cp in blk_copies(pt_ref[nn * n_pages + j // per_page], kv, r, slot):
                    cp.start()

    @pl.when(n == 0)
    def _():
        fetch(0, 0)

    @pl.when(n + 1 < ns)
    def _():
        fetch(n + 1, (n + 1) % 2)

    slot = n % 2
    for kv in range(B_KV):
        for r in range(SEL_TOPK):
            for cp in blk_copies(0, kv, r, slot):
                cp.wait()

    row = row_ref[0]
    q16f = _q16(row)
    q16 = q16f.astype(BF16)
    q16r = q16.astype(F32)
    slope = _row_consts(16, SLOPES_B)
    r16 = _iota((16, 128), 0)

    def new_key(col):
        kn = _bf(row[:, col:col + 128])
        return jnp.sum(q16r * kn, axis=-1, keepdims=True)

    def attend(s, s_new, v_t):
        m = jnp.maximum(jnp.max(s, axis=-1, keepdims=True), s_new)
        e = jnp.exp(s - m)
        e_new = jnp.exp(s_new - m)
        inv = 1.0 / (jnp.sum(e, axis=-1, keepdims=True) + e_new)
        return lax.dot_general((e * inv).astype(BF16), v_t, NT, preferred_element_type=F32), _bf(e_new * inv)

    n_keys = SEL_TOPK * PAGE_SIZE
    jv = jnp.dot(sel_ref[0].astype(F32).astype(BF16), e16_ref[...], preferred_element_type=F32).astype(jnp.int32)
    in_page = jnp.bitwise_and(_iota((8, n_keys), 1), PAGE_SIZE - 1)
    blk_shift = SEL_BLOCK.bit_length() - 1
    page_shift = per_page.bit_length() - 1
    ok_sel = ((in_page >> blk_shift) == (jv & (per_page - 1))) & (jv < n_blk)
    dist_sel = (past - ((jv >> page_shift) * PAGE_SIZE + in_page)).astype(F32)
    q64 = jnp.where(r16[:, :HEAD_DIM] < B_GROUP, q16f[:, :HEAD_DIM], q16f[:, HEAD_DIM:]).astype(BF16)
    s_new = new_key(1024)
    outs, p_news = [], []
    for kv in range(B_KV):
        s = jnp.dot(q64, buf[slot, 0, kv].astype(BF16), preferred_element_type=F32) - slope * dist_sel[kv:kv + 1]
        o, p_new = attend(jnp.where(ok_sel[kv:kv + 1], s, NEG), s_new, buf[slot, 1, kv].astype(BF16))
        outs.append(o)
        p_news.append(p_new)
    p_new = jnp.where(_iota((16, 1), 0) < B_GROUP, p_news[0], p_news[1])
    os16 = jnp.concatenate(outs, axis=1) + p_new * _bf(row[:, 1152:1280])

    lb = win_ref.shape[-1]
    dw = (lb - _iota((1, lb), 1)).astype(F32)
    s = jnp.dot(q16, _kv_t(win_ref, 0), preferred_element_type=F32) - slope * dw
    ow16, p_new = attend(s, new_key(1280), _kv_t(win_ref, 1))
    ow16 = ow16 + p_new * _bf(row[:, 1408:1536])

    sg = jnp.broadcast_to(_sigmoid(row[:, 2816:2944]), (16, 128))
    l16 = _iota((16, 128), 1)
    gate = lambda b: jnp.sum(jnp.where(l16 == r16 * 3 + b, sg, 0.0), axis=-1, keepdims=True)
    out16 = gate(0) * oc_ref[0] + gate(1) * os16 + gate(2) * ow16
    lane1 = _iota((1, 128), 1)
    mix = _place_heads([out16[h:h + 1, :] for h in range(B_HEADS)], lane1)

    cx = _sample_cross(row[:, 2304:2560], _kv_t(mkv_ref, 0), _kv_t(mkv_ref, 1))
    z_ref[0] = jnp.concatenate([mix * _silu(row[:, 1536:2304]), cx * _silu(row[:, 2560:2816])], axis=1)


def _sample_b2(page_table, sel, proj_s, oc, win_t, mem_t, e16, pages_t, li, i):
    ns, n_pages = page_table.shape
    row3 = proj_s.reshape(ns, 1, W_IN_B_PAD)
    full = lambda a: pl.BlockSpec(a.shape, lambda b, pt, sf: (0,) * a.ndim)
    per = lambda a: pl.BlockSpec((1,) + a.shape[1:], lambda b, pt, sf: (b,) + (0,) * (a.ndim - 1))
    grid_spec = pltpu.PrefetchScalarGridSpec(
        num_scalar_prefetch=2, grid=(ns,),
        in_specs=[per(row3), per(oc), per(sel), _layer_block(win_t, li), _layer_block(mem_t, i), full(e16),
                  pl.BlockSpec(memory_space=pl.ANY)],
        out_specs=pl.BlockSpec((1, 1, B_WIDTH + X_WIDTH), lambda b, pt, sf: (b, 0, 0)),
        scratch_shapes=[pltpu.VMEM((2, 2, B_KV, HEAD_DIM, SEL_TOPK * PAGE_SIZE), F32), pltpu.SemaphoreType.DMA((2,))])
    return pl.pallas_call(
        functools.partial(_sample_b2_body, li=li, n_pages=n_pages, ns=ns),
        grid_spec=grid_spec,
        out_shape=jax.ShapeDtypeStruct((ns, 1, B_WIDTH + X_WIDTH), F32),
        compiler_params=_params(1), name="sample_b2",
    )(page_table.reshape(-1), sel[:, :B_KV, :SEL_TOPK].reshape(-1), row3, oc, sel, win_t, mem_t, e16, pages_t
      ).reshape(ns, B_WIDTH + X_WIDTH)


def _importance_matrix(n_cmp, n_cols):
    c = np.arange(n_cmp)[:, None]
    j = np.arange(n_cols)[None, :]
    per = SEL_BLOCK // CMP_STRIDE
    m = ((c >= per * j) & (c <= per * j + per - 1)).astype(np.float32)
    m = m + ((c + 1 >= per * j) & (c + 1 <= per * j + per - 1)).astype(np.float32)
    m[n_cmp - 1, :] = 0.0
    return m


def _gate_expand():
    eg = np.zeros((128, 3 * B_WIDTH), np.float32)
    for h in range(B_HEADS):
        for b in range(3):
            eg[h * 3 + b, b * B_WIDTH + h * HEAD_DIM:b * B_WIDTH + (h + 1) * HEAD_DIM] = 1.0
    return eg


def _compress_weights(cmp_pos, cmp_w1, cmp_w2):
    eye = jnp.eye(B_KV, dtype=F32)
    posw = jnp.concatenate([cmp_pos, cmp_pos], axis=-1)
    w1 = cmp_w1.reshape(2, CMP_LEN, HEAD_DIM, CMP_HIDDEN)
    w1bd = jnp.einsum('tlek,jm->tljemk', w1, eye).reshape(2, CMP_LEN, 2 * HEAD_DIM, 2 * CMP_HIDDEN)
    w1cat = jnp.concatenate([w1bd[:, :CMP_STRIDE].reshape(2, CMP_STRIDE * 2 * HEAD_DIM, 2 * CMP_HIDDEN),
                             w1bd[:, CMP_STRIDE:].reshape(2, CMP_STRIDE * 2 * HEAD_DIM, 2 * CMP_HIDDEN)], axis=-1)
    pos_h = jnp.einsum('tlr,tlrh->tlh', posw, w1bd, precision=lax.Precision.HIGHEST)
    bias = jnp.concatenate([pos_h[:, :CMP_STRIDE].sum(axis=1), pos_h[:, CMP_STRIDE:].sum(axis=1)], axis=-1)
    w2bd = jnp.einsum('tke,jm->tjkme', cmp_w2, eye).reshape(2, 2 * CMP_HIDDEN, 2 * HEAD_DIM)
    return bias[:, None, :], w1cat.astype(BF16), w2bd.astype(BF16)


def _permute_w_in_b(w):
    d = w.shape[0]
    return jnp.concatenate([w[:, :1536], w[:, 1572:W_IN_B], w[:, 1536:1572],
                            jnp.zeros((d, W_IN_B_PAD - W_IN_B), w.dtype)], axis=1)


def kernel(x_prompt, x_sample, cache_mem_kv, cache_a_w128_kv, cache_a_w512_kv, cache_a_w2048_kv, cache_b_pages,
           cache_b_win_kv, page_table, mem_prompt, norm_pre, norm_post, norm_mem, w_mem_kv, w_in_a, w_out_a,
           w_in_b, w_out_b, cmp_pos, cmp_w1, cmp_w2):
    n, s_len, d = x_prompt.shape
    ns = x_sample.shape[0]
    depth = norm_pre.shape[0]
    page_size = cache_b_pages.shape[1]
    n_pages = page_table.shape[1]
    past = n_pages * page_size
    assert d == D_MODEL and x_sample.shape[1] == 1 and page_size == PAGE_SIZE
    assert s_len % A_SPAN == 0 and past % A_SPAN == 0
    caches_a = (cache_a_w128_kv, cache_a_w512_kv, cache_a_w2048_kv)
    for c, (win, _) in zip(caches_a, A_PATTERNS):
        assert c.shape[2] == win
    assert cache_b_win_kv.shape[2] == WIN_B

    tm = ROW_TILE
    n_cmp_p = s_len // CMP_STRIDE
    n_cmp_s = past // CMP_STRIDE
    slope_lanes = jnp.asarray(_slope_lanes(), F32)
    mt = jnp.asarray(_importance_matrix(n_cmp_p, 128).T, BF16)
    mm = jnp.asarray(_importance_matrix(n_cmp_s, 256), BF16)
    eg = jnp.asarray(_gate_expand(), BF16)
    e16 = jnp.asarray((np.arange(SEL_TOPK * PAGE_SIZE)[None, :] // PAGE_SIZE == np.arange(128)[:, None]), BF16)
    pages_t = cache_b_pages.transpose(0, 2, 3, 4, 5, 1)
    caches_a_t = [_rows_last(c) for c in caches_a]
    mem_t = _rows_last(cache_mem_kv)
    win_t = _rows_last(cache_b_win_kv)

    xp = x_prompt
    xs = x_sample.reshape(ns, d)
    mem2 = mem_prompt.reshape(n * N_MEM, d)
    mem_new = []
    a_p = [[] for _ in A_PATTERNS]
    a_s = [[] for _ in A_PATTERNS]
    b_p, b_s, bw_p, bw_s = [], [], [], []
    for i in range(depth):
        li = i // 2
        mkv_p = _rms_proj(mem2, norm_mem[i], w_mem_kv[i].astype(BF16), tm=N_MEM).reshape(n, N_MEM, 2 * X_WIDTH)
        mem_new.append(mkv_p.reshape(n, N_MEM, 2, 4, HEAD_DIM))
        if i % 2 == 0:
            w_in = w_in_a[li].astype(BF16)
            w_out = w_out_a[li].astype(BF16)
            proj_p = _rms_proj(xp.reshape(n * s_len, d), norm_pre[i], w_in, tm=tm).reshape(n, s_len, W_IN_A)
            proj_s = _rms_proj(xs, norm_pre[i], w_in, tm=ns)
            ols = []
            for g, (win, dil) in enumerate(A_PATTERNS):
                ols += _a_prompt_group(proj_p, g, dil)
                kv_p = proj_p[:, s_len - win:, 768:2304].reshape(n, win, 2, 3, 4, HEAD_DIM)[:, :, :, g]
                a_p[g].append(kv_p)
                a_s[g].append(proj_s[:, 768:2304].reshape(ns, 1, 2, 3, 4, HEAD_DIM)[:, :, :, g])
            xp = _finish_a(xp, ols, proj_p, mkv_p, w_out, norm_post[i], tm=tm)
            z = _sample_a(proj_s, caches_a_t, mem_t, li, i)
            xs = _tail(xs, z, w_out, norm_post[i])
        else:
            w_in = _permute_w_in_b(w_in_b[li]).astype(BF16)
            w_out = w_out_b[li].astype(BF16)
            posw, w1bd, w2bd = _compress_weights(cmp_pos[li], cmp_w1[li], cmp_w2[li])
            proj_p, kvs = _rms_proj(xp.reshape(n * s_len, d), norm_pre[i], w_in, tm=tm, side=(1024, 512))
            proj_p = proj_p.reshape(n, s_len, W_IN_B_PAD)
            kvs = kvs.reshape(n, s_len, 512)
            proj_s = _rms_proj(xs, norm_pre[i], w_in, tm=ns)
            cmpd = _compress_prompt(proj_p, posw, w1bd, w2bd)
            pos = jnp.arange(s_len, dtype=jnp.int32)
            cend = CMP_STRIDE * jnp.arange(n_cmp_p, dtype=jnp.int32) + (CMP_LEN - 1)
            mix = _nsa_prompt(proj_p, _keys_with_pos(cmpd[:, :, 0:128], cend), cmpd[:, :, 128:256].transpose(0, 2, 1),
                              _keys_with_pos(kvs[:, :, 256:384], pos), _values_by_group(kvs[:, :, 384:512]),
                              _keys_with_pos(kvs[:, :, 0:128], pos), _values_by_group(kvs[:, :, 128:256]),
                              mt, eg, slope_lanes)
            xp = _finish_b(xp, mix, proj_p, mkv_p, w_out, norm_post[i], tm=tm)
            oc, sel = _sample_b1(page_table, proj_s, posw, w1bd, w2bd, mm, pages_t, li)
            z = _sample_b2(page_table, sel, proj_s, oc, win_t, mem_t, e16, pages_t, li, i)
            xs = _tail(xs, z, w_out, norm_post[i])
            b_p.append(proj_p[:, :, 768:1280].reshape(n, s_len, 4, B_KV, HEAD_DIM))
            bw_p.append(proj_p[:, s_len - WIN_B:, 1280:1536].reshape(n, WIN_B, 2, B_KV, HEAD_DIM))
            b_s.append(proj_s[:, 768:1280].reshape(ns, 1, 4, B_KV, HEAD_DIM))
            bw_s.append(proj_s[:, 1280:1536].reshape(ns, 1, 2, B_KV, HEAD_DIM))
    return (xp, xs.reshape(ns, 1, d), jnp.stack(mem_new, axis=0),
            jnp.stack(a_p[0], axis=0), jnp.stack(a_p[1], axis=0), jnp.stack(a_p[2], axis=0),
            jnp.stack(b_p, axis=2), jnp.stack(bw_p, axis=0),
            jnp.stack(a_s[0], axis=0), jnp.stack(a_s[1], axis=0), jnp.stack(a_s[2], axis=0),
            jnp.stack(b_s, axis=2), jnp.stack(bw_s, axis=0))
```

```python
import functools

import numpy as np
import jax
import jax.numpy as jnp
from jax import lax
from jax.experimental import pallas as pl
from jax.experimental.pallas import tpu as pltpu

F32 = jnp.float32
BF16 = jnp.bfloat16

D_MODEL = 1024
HEAD_DIM = 64
SCALE = HEAD_DIM ** -0.5
LOG2E = 1.4426950408889634
RMS_EPS = 1e-6
N_MEM = 256
X_WIDTH = 256
A_PATTERNS = ((128, 1), (512, 4), (2048, 16))
A_WIDTH = 256
W_IN_A = 3072
B_HEADS = 12
B_KV = 2
B_GROUP = 6
B_WIDTH = 768
W_IN_B = 2852
W_IN_B_PAD = 2944
CMP_LEN = 32
CMP_STRIDE = 16
CMP_HIDDEN = 128
SEL_BLOCK = 64
SEL_TOPK = 16
WIN_B = 512
Q_BLOCK = 128
FORCE_SCORE = 1e4
PAGE_SIZE = 128
NEG = -1e30
GROUPS_PER_CHUNK = 4
V7X_VMEM_BYTES = 64 * 1024 * 1024
VMEM_LIMIT = V7X_VMEM_BYTES * 7 // 8
ROW_TILE = 512

NT = (((1,), (1,)), ((), ()))


def _alibi(n):
    k = np.arange(1, n + 1, dtype=np.float32)
    return [float(v) for v in np.float32(2.0) ** (np.float32(-8.0) * k / np.float32(n))]


SLOPES_A = _alibi(12)
SLOPES_B = _alibi(12)


def _params(n_axes):
    return pltpu.CompilerParams(dimension_semantics=("arbitrary",) * n_axes, vmem_limit_bytes=VMEM_LIMIT)


def _sigmoid(x):
    return 1.0 / (1.0 + jnp.exp(-x))


def _silu(x):
    return x * _sigmoid(x)


def _iota(shape, dim):
    return lax.broadcasted_iota(jnp.int32, shape, dim)


def _split3(x):
    hi = x.astype(BF16)
    r1 = x - hi.astype(F32)
    mid = r1.astype(BF16)
    lo = (r1 - mid.astype(F32)).astype(BF16)
    return hi, mid, lo


def _rms_proj_body(x_ref, g_ref, w_ref, o_ref, *side_ref, side):
    x = x_ref[...]
    y = x * lax.rsqrt(jnp.mean(x * x, axis=-1, keepdims=True) + RMS_EPS)
    y = (y * g_ref[...]).astype(BF16)
    o = jnp.dot(y, w_ref[...], preferred_element_type=F32)
    o_ref[...] = o
    if side is not None:
        side_ref[0][...] = o[:, side[0]:side[0] + side[1]].astype(BF16)


def _rms_proj(x, g, w, tm, side=None):
    m, d = x.shape
    n = w.shape[1]
    out_specs = [pl.BlockSpec((tm, n), lambda i: (i, 0))]
    out_shape = [jax.ShapeDtypeStruct((m, n), F32)]
    if side is not None:
        out_specs.append(pl.BlockSpec((tm, side[1]), lambda i: (i, 0)))
        out_shape.append(jax.ShapeDtypeStruct((m, side[1]), BF16))
    outs = pl.pallas_call(
        functools.partial(_rms_proj_body, side=side),
        grid=(m // tm,),
        in_specs=[pl.BlockSpec((tm, d), lambda i: (i, 0)),
                  pl.BlockSpec((1, d), lambda i: (0, 0)),
                  pl.BlockSpec((d, n), lambda i: (0, 0))],
        out_specs=out_specs,
        out_shape=out_shape,
        compiler_params=_params(1),
        name="rms_proj",
    )(x, g.reshape(1, d), w)
    return outs[0] if side is None else outs


A_SPAN = 2048


def _a_prompt_body(*refs, dil, slopes):
    ins, outs = refs[:10], refs[10:]
    t = pl.program_id(1)
    blk_rows = 128 * dil
    n_ub = A_SPAN // blk_rows
    i = _iota((128, 256), 0)
    j = _iota((128, 256), 1)
    back = 128 + i - j
    in_band = (back >= 0) & (back <= 128)
    dist = (back * dil).astype(F32)
    lane = _iota((128, 128), 1)

    def rows(ref, start):
        return ref[0, pl.ds(start, 128, stride=dil), :] if dil > 1 else ref[0, pl.ds(start, 128), :]

    def block(pair, ub, r, first):
        q_ref, k_ref, kp_ref, v_ref, vp_ref = ins[pair * 5:(pair + 1) * 5]
        o_ref, l_ref = outs[pair * 2:(pair + 1) * 2]
        start = ub * blk_rows + r
        if first:
            k_prev, v_prev = rows(kp_ref, r), rows(vp_ref, r)
            valid = in_band & (j >= jnp.where(t > 0, 0, 128))
        else:
            k_prev, v_prev = rows(k_ref, start - blk_rows), rows(v_ref, start - blk_rows)
            valid = in_band
        qp = rows(q_ref, start)
        kp = jnp.concatenate([k_prev, rows(k_ref, start)], axis=0).astype(BF16)
        vp = jnp.concatenate([v_prev, rows(v_ref, start)], axis=0).astype(BF16)
        o_pair = None
        l_pair = None
        for hh in range(2):
            hm = (lane < 64) if hh == 0 else (lane >= 64)
            qm = jnp.where(hm, qp, 0.0).astype(BF16)
            s = lax.dot_general(qm, kp, NT, preferred_element_type=F32) * SCALE
            s = jnp.where(valid, s - slopes[pair * 2 + hh] * dist, NEG)
            m = jnp.max(s, axis=-1, keepdims=True)
            e = jnp.exp(s - m)
            den = jnp.sum(e, axis=-1, keepdims=True)
            oh = jnp.dot((e * (1.0 / den)).astype(BF16), vp, preferred_element_type=F32)
            lh = jnp.broadcast_to(m + jnp.log(den), (128, 128))
            o_pair = oh if hh == 0 else jnp.where(lane < 64, o_pair, oh)
            l_pair = lh if hh == 0 else jnp.where(lane < 64, l_pair, lh)
        if dil > 1:
            o_ref[0, pl.ds(start, 128, stride=dil), :] = o_pair
            l_ref[0, pl.ds(start, 128, stride=dil), :] = l_pair
        else:
            o_ref[0, pl.ds(start, 128), :] = o_pair
            l_ref[0, pl.ds(start, 128), :] = l_pair

    def run(count, fn):
        if count == 0:
            return
        unroll = next(c for c in (8, 6, 5, 4, 3, 2, 1) if count % c == 0)

        def body(it, c):
            for k in range(unroll):
                fn(it * unroll + k)
            return c
        lax.fori_loop(0, count // unroll, body, 0)

    for pair in range(2):
        run(dil, lambda r, pair=pair: block(pair, 0, r, True))
        run((n_ub - 1) * dil, lambda idx, pair=pair: block(pair, 1 + idx // dil, idx % dil, False))


def _a_prompt_group(proj, g, dil):
    n, s_len, _ = proj.shape
    blk_rows = 128 * dil
    per_span = A_SPAN // blk_rows
    body = functools.partial(_a_prompt_body, dil=dil, slopes=tuple(SLOPES_A[g * 4:(g + 1) * 4]))
    cur = lambda col: pl.BlockSpec((1, A_SPAN, 128), lambda b, t: (b, t, col))
    prev = lambda col: pl.BlockSpec((1, blk_rows, 128), lambda b, t: (b, jnp.maximum(t * per_span - 1, 0), col))
    in_specs = []
    for pair in range(2):
        qc, kc, vc = 2 * g + pair, 6 + 2 * g + pair, 12 + 2 * g + pair
        in_specs += [cur(qc), cur(kc), prev(kc), cur(vc), prev(vc)]
    out_spec = pl.BlockSpec((1, A_SPAN, 128), lambda b, t: (b, t, 0))
    return pl.pallas_call(
        body,
        grid=(n, s_len // A_SPAN),
        in_specs=in_specs,
        out_specs=[out_spec] * 4,
        out_shape=[jax.ShapeDtypeStruct((n, s_len, 128), F32)] * 4,
        compiler_params=_params(2),
        name=f"a_prompt_g{g}",
    )(*([proj] * 10))


def _cross_rows(qx, kx, vx):
    tm = qx.shape[0]
    lane = _iota((tm, 128), 1)
    outs = []
    for pair in range(2):
        sl = slice(pair * 128, (pair + 1) * 128)
        qp, kp, vp = qx[:, sl], kx[:, sl], vx[:, sl]
        o_pair = None
        for hh in range(2):
            hm = (lane < 64) if hh == 0 else (lane >= 64)
            qm = jnp.where(hm, qp, 0.0).astype(BF16)
            s = lax.dot_general(qm, kp, NT, preferred_element_type=F32) * SCALE
            m = jnp.max(s, axis=-1, keepdims=True)
            e = jnp.exp(s - m)
            p = (e / jnp.sum(e, axis=-1, keepdims=True)).astype(BF16)
            oh = jnp.dot(p, vp, preferred_element_type=F32)
            o_pair = oh if hh == 0 else jnp.where(lane < 64, o_pair, oh)
        outs.append(o_pair)
    return jnp.concatenate(outs, axis=1)


def _out_norm_residual(x, z, w, g):
    y = jnp.dot(z.astype(BF16), w, preferred_element_type=F32)
    y = y * lax.rsqrt(jnp.mean(y * y, axis=-1, keepdims=True) + RMS_EPS)
    return x + y * g


def _finish_a_body(x_ref, *refs):
    gm_ref, qx_ref, gx_ref, mkv_ref, w_ref, g_ref, out_ref = refs[12:]
    mixes = []
    for pair in range(2):
        os_ = [refs[4 * g + 2 * pair][0] for g in range(3)]
        ls_ = [refs[4 * g + 2 * pair + 1][0] for g in range(3)]
        m = jnp.maximum(jnp.maximum(ls_[0], ls_[1]), ls_[2])
        es = [jnp.exp(l - m) for l in ls_]
        mixes.append((es[0] * os_[0] + es[1] * os_[1] + es[2] * os_[2]) / (es[0] + es[1] + es[2]))
    mix = jnp.concatenate(mixes, axis=1)
    mkv = mkv_ref[0]
    cx = _cross_rows(qx_ref[0], mkv[:, :X_WIDTH].astype(BF16), mkv[:, X_WIDTH:].astype(BF16))
    z = jnp.concatenate([mix * _silu(gm_ref[0]), cx * _silu(gx_ref[0])], axis=1)
    out_ref[0] = _out_norm_residual(x_ref[0], z, w_ref[...], g_ref[...])


def _finish_b_body(x_ref, mix_ref, gm_ref, qx_ref, gx_ref, mkv_ref, w_ref, g_ref, out_ref):
    mkv = mkv_ref[0]
    cx = _cross_rows(qx_ref[0], mkv[:, :X_WIDTH].astype(BF16), mkv[:, X_WIDTH:].astype(BF16))
    z = jnp.concatenate([mix_ref[0] * _silu(gm_ref[0]), cx * _silu(gx_ref[0])], axis=1)
    out_ref[0] = _out_norm_residual(x_ref[0], z, w_ref[...], g_ref[...])


def _finish_a(x, ols, proj, mkv, w_out, g_post, tm):
    n, s_len, d = x.shape
    row = lambda w, c: pl.BlockSpec((1, tm, w), lambda b, t: (b, t, c))
    in_specs = ([row(d, 0)] + [row(128, 0)] * 12 + [row(256, 9), row(256, 10), row(256, 11)]
                + [pl.BlockSpec((1, N_MEM, 2 * X_WIDTH), lambda b, t: (b, 0, 0)),
                   pl.BlockSpec(w_out.shape, lambda b, t: (0, 0)),
                   pl.BlockSpec((1, d), lambda b, t: (0, 0))])
    return pl.pallas_call(
        _finish_a_body, grid=(n, s_len // tm), in_specs=in_specs, out_specs=row(d, 0),
        out_shape=jax.ShapeDtypeStruct((n, s_len, d), F32), compiler_params=_params(2), name="finish_a",
    )(x, *ols, proj, proj, proj, mkv, w_out, g_post.reshape(1, d))


def _finish_b(x, mix, proj, mkv, w_out, g_post, tm):
    n, s_len, d = x.shape
    row = lambda w, c: pl.BlockSpec((1, tm, w), lambda b, t: (b, t, c))
    in_specs = [row(d, 0), row(B_WIDTH, 0), row(B_WIDTH, 2), row(256, 9), row(256, 10),
                pl.BlockSpec((1, N_MEM, 2 * X_WIDTH), lambda b, t: (b, 0, 0)),
                pl.BlockSpec(w_out.shape, lambda b, t: (0, 0)),
                pl.BlockSpec((1, d), lambda b, t: (0, 0))]
    return pl.pallas_call(
        _finish_b_body, grid=(n, s_len // tm), in_specs=in_specs, out_specs=row(d, 0),
        out_shape=jax.ShapeDtypeStruct((n, s_len, d), F32), compiler_params=_params(2), name="finish_b",
    )(x, mix, proj, proj, proj, mkv, w_out, g_post.reshape(1, d))


def _tail_body(x_ref, z_ref, w_ref, g_ref, out_ref):
    out_ref[...] = _out_norm_residual(x_ref[...], z_ref[...], w_ref[...], g_ref[...])


def _tail(x, z, w_out, g_post):
    m, d = x.shape
    full = lambda a: pl.BlockSpec(a.shape, lambda i: (0,) * a.ndim)
    g2 = g_post.reshape(1, d)
    return pl.pallas_call(
        _tail_body, grid=(1,), in_specs=[full(x), full(z), full(w_out), full(g2)], out_specs=full(x),
        out_shape=jax.ShapeDtypeStruct((m, d), F32), compiler_params=_params(1), name="sample_tail",
    )(x, z, w_out, g2)


def _compress_rows(load_rows, pos_ref, w1_ref, w2_ref, n_cmp, between=None):
    outs = []
    half = 2 * CMP_HIDDEN
    for t in range(2):
        y = jnp.concatenate([load_rows(t, l).astype(BF16) for l in range(CMP_STRIDE)], axis=1)
        ab = jnp.dot(y, w1_ref[t], preferred_element_type=F32) + pos_ref[t]
        h = ab[:, :half] + pltpu.roll(ab[:, half:], n_cmp - 1, axis=0)
        outs.append(jnp.dot(_silu(h).astype(BF16), w2_ref[t], preferred_element_type=F32))
        if between is not None:
            between(t)
    return outs


def _compress_body(k_ref, v_ref, pos_ref, w1_ref, w2_ref, o_ref, *, n_cmp):
    refs = (k_ref, v_ref)
    load = lambda t, l: refs[t][0, pl.ds(l, n_cmp, stride=CMP_STRIDE), :]
    ck, cv = _compress_rows(load, pos_ref, w1_ref, w2_ref, n_cmp)
    o_ref[0, :, 0:128] = ck.astype(BF16)
    o_ref[0, :, 128:256] = cv.astype(BF16)


def _compress_prompt(proj, posw, w1bd, w2bd):
    n, s_len, _ = proj.shape
    n_cmp = s_len // CMP_STRIDE
    full = lambda a: pl.BlockSpec(a.shape, lambda b: (0,) * a.ndim)
    return pl.pallas_call(
        functools.partial(_compress_body, n_cmp=n_cmp), grid=(n,),
        in_specs=[pl.BlockSpec((1, s_len, 128), lambda b: (b, 0, 6)), pl.BlockSpec((1, s_len, 128), lambda b: (b, 0, 7)),
                  full(posw), full(w1bd), full(w2bd)],
        out_specs=pl.BlockSpec((1, n_cmp, 256), lambda b: (b, 0, 0)),
        out_shape=jax.ShapeDtypeStruct((n, n_cmp, 256), BF16), compiler_params=_params(1), name="compress_prompt",
    )(proj, proj, posw, w1bd, w2bd)


def _place_heads(tiles, lane):
    chunks = []
    for c in range(B_HEADS // 2):
        t0, t1 = tiles[2 * c], tiles[2 * c + 1]
        if (2 * c) // B_GROUP == 1:
            t0 = pltpu.roll(t0, 64, axis=1)
        if (2 * c + 1) // B_GROUP == 0:
            t1 = pltpu.roll(t1, 64, axis=1)
        chunks.append(jnp.where(lane < 64, t0, t1))
    return jnp.concatenate(chunks, axis=1)


def _masked_softmax_rows(s, ok):
    s = jnp.where(ok, s, NEG)
    m = jnp.max(s, axis=-1, keepdims=True)
    e = jnp.where(ok, jnp.exp(s - m), 0.0)
    den = jnp.maximum(jnp.sum(e, axis=-1, keepdims=True), 1e-30)
    return e * (1.0 / den)


ONES_ROWS = 16


def _values_and_ones(v_ref, idx, kv):
    v = v_ref[idx + (slice(kv * HEAD_DIM, (kv + 1) * HEAD_DIM), slice(None))]
    return jnp.concatenate([v, jnp.ones((ONES_ROWS, v.shape[1]), BF16)], axis=0)


def _head_tile(num, den, kv):
    x = num if den is None else num * (1.0 / den)
    z = jnp.zeros_like(x)
    return jnp.concatenate([x, z] if kv == 0 else [z, x], axis=0).T


def _nsa_prompt_body(q_ref, gt_ref, kc_ref, vct_ref, kw_ref, vwt_ref, ks_ref, vst_ref, mt_ref, eg_ref,
                     sl_ref, out_ref, q6_sc, m_sc, acc_sc, sel_sc, words_sm, idx_sm, *, s_len):
    qb = pl.program_id(1)
    qstart = qb * Q_BLOCK
    n_cmp = s_len // CMP_STRIDE
    q = q_ref[0] * (SCALE * LOG2E)
    lane = _iota((Q_BLOCK, 128), 1)
    tq_row = qstart + _iota((1, Q_BLOCK), 1)
    oc_t, os_t, ow_t = [None] * B_HEADS, [None] * B_HEADS, [None] * B_HEADS

    sub = _iota((128, 128), 0)
    psums = []

    for kv in range(B_KV):
        for g in range(B_GROUP):
            h = kv * B_GROUP + g
            ch = q[:, (h // 2) * 128:(h // 2 + 1) * 128]
            if h % 2 == 1:
                ch = pltpu.roll(ch, 64, axis=1)
            q6_sc[kv, g * 128:(g + 1) * 128, :] = jnp.where(lane < 64, ch, sl_ref[h:h + 1, :]).astype(BF16)

    cok = (CMP_STRIDE * _iota((n_cmp, Q_BLOCK), 0) + (CMP_LEN - 1)) <= tq_row
    q_ok = tq_row >= (CMP_LEN - 1)
    n_wb = WIN_B // Q_BLOCK + 1

    def cmp_scores(kv):
        return lax.dot_general(kc_ref[0, kv], q6_sc[kv], NT, preferred_element_type=F32)

    def cmp_finish(kv, s_t):
        psum = jnp.zeros((n_cmp, Q_BLOCK), F32)
        ps = []
        for g in range(B_GROUP):
            s = jnp.where(cok, s_t[:, g * 128:(g + 1) * 128], NEG)
            e = jnp.exp2(s - jnp.max(s, axis=0, keepdims=True))
            p = e * jnp.where(q_ok, 1.0 / jnp.sum(e, axis=0, keepdims=True), 0.0)
            psum = psum + p
            ps.append(p.astype(BF16))
        oc = jnp.dot(vct_ref[0, kv * HEAD_DIM:(kv + 1) * HEAD_DIM, :], jnp.concatenate(ps, axis=1),
                     preferred_element_type=F32)
        for g in range(B_GROUP):
            oc_t[kv * B_GROUP + g] = _head_tile(oc[:, g * 128:(g + 1) * 128], None, kv)
        psums.append(psum)

    def win_scores(kv):
        kparts, vparts, pparts = [], [], []
        for wb in range(n_wb):
            b_raw = qb - (n_wb - 1) + wb
            b = jnp.maximum(b_raw, 0)
            r0 = pl.multiple_of(b * Q_BLOCK, Q_BLOCK)
            kparts.append(kw_ref[0, kv, pl.ds(r0, Q_BLOCK), :])
            vparts.append(_values_and_ones(vwt_ref, (0, b), kv))
            pparts.append(jnp.where(b_raw >= 0, r0, s_len) + _iota((128, Q_BLOCK), 0))
        s_t = lax.dot_general(jnp.concatenate(kparts, axis=0), q6_sc[kv], NT, preferred_element_type=F32)
        return s_t, jnp.concatenate(vparts, axis=1), jnp.concatenate(pparts, axis=0)

    def win_finish(kv, s_t, v_t, kpos):
        dw = tq_row - kpos
        wok = (dw >= 0) & (dw <= WIN_B)
        ps = []
        for g in range(B_GROUP):
            s = jnp.where(wok, s_t[:, g * 128:(g + 1) * 128], NEG)
            ps.append(jnp.exp2(s - jnp.max(s, axis=0, keepdims=True)).astype(BF16))
        ow = jnp.dot(v_t, jnp.concatenate(ps, axis=1), preferred_element_type=F32)
        for g in range(B_GROUP):
            cols = slice(g * 128, (g + 1) * 128)
            ow_t[kv * B_GROUP + g] = _head_tile(ow[0:HEAD_DIM, cols], ow[HEAD_DIM:HEAD_DIM + 1, cols], kv)

    sc0 = cmp_scores(0)
    sw0 = win_scores(0)
    cmp_finish(0, sc0)
    sc1 = cmp_scores(1)
    win_finish(0, *sw0)
    sw1 = win_scores(1)
    cmp_finish(1, sc1)
    win_finish(1, *sw1)

    mt = mt_ref[...]
    blk = _iota((128, Q_BLOCK), 0)
    ql = _iota((128, Q_BLOCK), 1)
    cur = jnp.where(ql >= SEL_BLOCK, qb * 2 + 1, qb * 2)
    forced = (blk == 0) | (blk == cur) | (blk == cur - 1)
    blkf = blk.astype(F32)
    imps = [jnp.where((blk > cur) | forced, -jnp.inf,
                      sum(jnp.dot(mt, t, preferred_element_type=F32) for t in _split3(psum))) for psum in psums]
    sels = [jnp.where(forced, 1.0, 0.0)] * B_KV
    for _ in range(SEL_TOPK - 3):
        for kv in range(B_KV):
            mx = jnp.max(imps[kv], axis=0, keepdims=True)
            idx = jnp.min(jnp.where(imps[kv] == mx, blkf, 1e9), axis=0, keepdims=True)
            hit = blkf == idx
            sels[kv] = jnp.where(hit, 1.0, sels[kv])
            imps[kv] = jnp.where(hit, -jnp.inf, imps[kv])

    blk_col = _iota((128, 1), 0)
    weight = lax.shift_left(jnp.ones((128, 1), jnp.int32), blk_col & 15).astype(F32)
    for kv in range(B_KV):
        sel_sc[kv] = sels[kv]
        contrib = jnp.max(sels[kv], axis=1, keepdims=True) * weight
        for w in range(8):
            words_sm[kv * 8 + w] = jnp.sum(contrib[16 * w:16 * (w + 1), :]).astype(jnp.int32)

    m_sc[...] = jnp.full(m_sc.shape, NEG, F32)
    acc_sc[...] = jnp.zeros(acc_sc.shape, F32)
    list_len = idx_sm.shape[0] // B_KV
    cnts = []
    n_grp = s_len // Q_BLOCK
    for kv in range(B_KV):
        cnt = jnp.int32(0)
        for w in range((n_grp + 7) // 8):
            word = words_sm[kv * 8 + w]
            for j in range(min(8, n_grp - 8 * w)):
                idx_sm[kv * list_len + cnt] = 8 * w + j
                cnt = cnt + jnp.where(((word >> (2 * j)) & 3) != 0, 1, 0)
        cnts.append(cnt)
    n_chunks = (jnp.maximum(cnts[0], cnts[1]) + GROUPS_PER_CHUNK - 1) // GROUPS_PER_CHUNK
    for kv in range(B_KV):
        def pad(i, c, kv=kv):
            idx_sm[kv * list_len + i] = -1
            return c
        lax.fori_loop(cnts[kv], n_chunks * GROUPS_PER_CHUNK, pad, 0)

    sub8 = _iota((8, Q_BLOCK), 0)

    def group_hits(kv, gi):
        rows8 = sel_sc[kv, pl.ds(pl.multiple_of((gi >> 2) * 8, 8), 8), :]
        r = (gi & 3) * 2
        lo = jnp.sum(jnp.where(sub8 == r, rows8, 0.0), axis=0, keepdims=True)
        hi = jnp.sum(jnp.where(sub8 == r + 1, rows8, 0.0), axis=0, keepdims=True)
        return jnp.where(sub < SEL_BLOCK, lo, hi)

    def chunk_scores(c, kv):
        kts, vts, hits, kposs = [], [], [], []
        for j in range(GROUPS_PER_CHUNK):
            gi_raw = idx_sm[kv * list_len + c * GROUPS_PER_CHUNK + j]
            gi = jnp.maximum(gi_raw, 0)
            k0 = pl.multiple_of(gi * Q_BLOCK, Q_BLOCK)
            kts.append(ks_ref[0, kv, pl.ds(k0, Q_BLOCK), :])
            vts.append(_values_and_ones(vst_ref, (0, gi), kv))
            hits.append(group_hits(kv, gi))
            kposs.append(jnp.where(gi_raw >= 0, k0, s_len) + _iota((128, 128), 0))
        s_t = lax.dot_general(jnp.concatenate(kts, axis=0), q6_sc[kv], NT, preferred_element_type=F32)
        ok = ((jnp.concatenate(hits, axis=0) > 0.5)
              & (jnp.concatenate(kposs, axis=0) <= qstart + _iota((GROUPS_PER_CHUNK * 128, 128), 1)))
        return s_t, ok, jnp.concatenate(vts, axis=1)

    def chunk_update(kv, s_t, ok, v_t):
        m_old = m_sc[kv]
        m_new, ps = [], []
        for g in range(B_GROUP):
            cols = slice(g * 128, (g + 1) * 128)
            s = jnp.where(ok, s_t[:, cols], NEG)
            mg = jnp.maximum(m_old[:, cols], jnp.max(s, axis=0, keepdims=True))
            ps.append(jnp.exp2(s - mg).astype(BF16))
            m_new.append(mg)
        m_new = jnp.concatenate(m_new, axis=1)
        pv = jnp.dot(v_t, jnp.concatenate(ps, axis=1), preferred_element_type=F32)
        acc_sc[kv] = jnp.exp2(m_old - m_new) * acc_sc[kv] + pv
        m_sc[kv] = m_new

    def chunk(c, carry):
        first = chunk_scores(c, 0)
        second = chunk_scores(c, 1)
        chunk_update(0, *first)
        chunk_update(1, *second)
        return carry

    lax.fori_loop(0, n_chunks, chunk, 0)
    for kv in range(B_KV):
        for g in range(B_GROUP):
            cols = slice(g * 128, (g + 1) * 128)
            os_t[kv * B_GROUP + g] = _head_tile(acc_sc[kv, 0:HEAD_DIM, cols], acc_sc[kv, HEAD_DIM:HEAD_DIM + 1, cols], kv)

    sg = _sigmoid(gt_ref[0])
    eg = eg_ref[...]
    gexp = sum(jnp.dot(t, eg, preferred_element_type=F32) for t in _split3(sg))
    out_ref[0] = (gexp[:, 0:B_WIDTH] * _place_heads(oc_t, lane)
                  + gexp[:, B_WIDTH:2 * B_WIDTH] * _place_heads(os_t, lane)
                  + gexp[:, 2 * B_WIDTH:] * _place_heads(ow_t, lane))


def _nsa_prompt(proj, kc, vc_t, kw, vw_t, ks, vs_t, mt, eg, slope_lanes):
    n, s_len, _ = proj.shape
    n_cmp = s_len // CMP_STRIDE
    n_grp = s_len // Q_BLOCK
    full = lambda a: pl.BlockSpec(a.shape, lambda b, t: (0,) * a.ndim)
    per_n = lambda a: pl.BlockSpec((1,) + a.shape[1:], lambda b, t: (b,) + (0,) * (a.ndim - 1))
    return pl.pallas_call(
        functools.partial(_nsa_prompt_body, s_len=s_len),
        grid=(n, n_grp),
        in_specs=[pl.BlockSpec((1, Q_BLOCK, B_WIDTH), lambda b, t: (b, t, 0)),
                  pl.BlockSpec((1, Q_BLOCK, 128), lambda b, t: (b, t, 22)),
                  per_n(kc), per_n(vc_t), per_n(kw), per_n(vw_t), per_n(ks), per_n(vs_t),
                  full(mt), full(eg), full(slope_lanes)],
        out_specs=pl.BlockSpec((1, Q_BLOCK, B_WIDTH), lambda b, t: (b, t, 0)),
        out_shape=jax.ShapeDtypeStruct((n, s_len, B_WIDTH), F32),
        scratch_shapes=[pltpu.VMEM((B_KV, B_GROUP * Q_BLOCK, 128), BF16), pltpu.VMEM((B_KV, 1, B_GROUP * Q_BLOCK), F32),
                        pltpu.VMEM((B_KV, HEAD_DIM + ONES_ROWS, B_GROUP * Q_BLOCK), F32),
                        pltpu.VMEM((B_KV, 128, Q_BLOCK), F32), pltpu.SMEM((B_KV * 8,), jnp.int32),
                        pltpu.SMEM((B_KV * (n_grp + GROUPS_PER_CHUNK),), jnp.int32)],
        compiler_params=_params(2), name="nsa_prompt",
    )(proj, proj, kc, vc_t, kw, vw_t, ks, vs_t, mt, eg, slope_lanes)


POS_LANE = 64


def _slope_lanes():
    out = np.zeros((B_HEADS, 128), np.float32)
    for h, slope in enumerate(SLOPES_B):
        s = np.float32(np.float64(slope) * LOG2E)
        hi = np.float32(np.asarray(s, np.float32).astype(jnp.bfloat16))
        mid = np.float32(np.asarray(np.float32(s) - hi, np.float32).astype(jnp.bfloat16))
        lo = np.float32(np.asarray(np.float32(s) - hi - mid, np.float32).astype(jnp.bfloat16))
        out[h, POS_LANE:POS_LANE + 6] = [hi, mid, lo, hi, mid, lo]
    return out


def _keys_with_pos(k2, pos):
    n, n_keys, _ = k2.shape
    lo = (pos % Q_BLOCK).astype(BF16)[None, :, None]
    hi = (pos - pos % Q_BLOCK).astype(BF16)[None, :, None]
    tail = jnp.concatenate([jnp.broadcast_to(lo, (n, n_keys, 3)), jnp.broadcast_to(hi, (n, n_keys, 3)),
                            jnp.zeros((n, n_keys, 128 - POS_LANE - 6), BF16)], axis=-1)
    return jnp.stack([jnp.concatenate([k2[..., kv * 64:(kv + 1) * 64], tail], axis=-1) for kv in range(B_KV)], axis=1)


def _values_by_group(v2):
    n, s_len, _ = v2.shape
    return v2.reshape(n, s_len // Q_BLOCK, Q_BLOCK, 128).transpose(0, 1, 3, 2)


def _heads_rows(vec, n_rows, width):
    r = _iota((n_rows, width), 0)
    l = _iota((n_rows, width), 1)
    hm = (l >= r * HEAD_DIM) & (l < r * HEAD_DIM + HEAD_DIM)
    return jnp.where(hm, jnp.broadcast_to(vec, (n_rows, width)), 0.0), hm


def _bf(x):
    return x.astype(BF16).astype(F32)


def _row_consts(n_rows, vals):
    r = _iota((n_rows, 1), 0)
    out = jnp.zeros((n_rows, 1), F32)
    for i, v in enumerate(vals):
        out = jnp.where(r == i, v, out)
    return out


def _rows_last(cache):
    nd = cache.ndim
    return cache.transpose(tuple(range(nd - 4)) + (nd - 3, nd - 2, nd - 1, nd - 4))


def _kv_t(ref, t):
    x = ref[0, 0, t]
    return x.reshape(x.shape[0] * x.shape[1], x.shape[2]).astype(BF16)


def _sample_cross(qx_row, k_t, v_t):
    q8, hm = _heads_rows(qx_row, 8, X_WIDTH)
    s = jnp.dot(q8.astype(BF16), k_t, preferred_element_type=F32) * SCALE
    e = jnp.exp(s - jnp.max(s, axis=-1, keepdims=True))
    p = (e * (1.0 / jnp.sum(e, axis=-1, keepdims=True))).astype(BF16)
    o8 = lax.dot_general(p, v_t, NT, preferred_element_type=F32)
    return jnp.sum(jnp.where(hm, o8, 0.0), axis=0, keepdims=True)


def _sample_a_body(row_ref, c0_ref, c1_ref, c2_ref, mkv_ref, z_ref):
    row = row_ref[0]
    outs, lses = [], []
    hm = None
    for g, (win, dil) in enumerate(A_PATTERNS):
        cref = (c0_ref, c1_ref, c2_ref)[g]
        q8, hm = _heads_rows(row[:, g * 256:(g + 1) * 256], 8, A_WIDTH)
        knew = row[:, 768 + g * 256:768 + (g + 1) * 256]
        vnew = row[:, 1536 + g * 256:1536 + (g + 1) * 256]
        q8b = q8.astype(BF16)
        slope = _row_consts(8, SLOPES_A[g * 4:(g + 1) * 4])
        s = jnp.dot(q8b, _kv_t(cref, 0), preferred_element_type=F32) * SCALE
        r = _iota((8, win), 1)
        s = jnp.where((r & (dil - 1)) == 0, s - slope * (win - r).astype(F32), NEG)
        s_new = jnp.sum(q8b.astype(F32) * _bf(knew), axis=-1, keepdims=True) * SCALE
        m = jnp.maximum(jnp.max(s, axis=-1, keepdims=True), s_new)
        e = jnp.exp(s - m)
        e_new = jnp.exp(s_new - m)
        den = jnp.sum(e, axis=-1, keepdims=True) + e_new
        inv = 1.0 / den
        o8 = (lax.dot_general((e * inv).astype(BF16), _kv_t(cref, 1), NT, preferred_element_type=F32)
              + _bf(e_new * inv) * _bf(vnew))
        outs.append(o8)
        lses.append(m + jnp.log(den))
    mx = jnp.maximum(jnp.maximum(lses[0], lses[1]), lses[2])
    ws = [jnp.exp(l - mx) for l in lses]
    mix8 = (ws[0] * outs[0] + ws[1] * outs[1] + ws[2] * outs[2]) / (ws[0] + ws[1] + ws[2])
    mix = jnp.sum(jnp.where(hm, mix8, 0.0), axis=0, keepdims=True)
    cx = _sample_cross(row[:, 2560:2816], _kv_t(mkv_ref, 0), _kv_t(mkv_ref, 1))
    z_ref[0] = jnp.concatenate([mix * _silu(row[:, 2304:2560]), cx * _silu(row[:, 2816:3072])], axis=1)


def _layer_block(cache_t, layer):
    return pl.BlockSpec((1, 1) + cache_t.shape[2:], lambda b, *_: (layer, b, 0, 0, 0, 0))


def _sample_a(proj_s, caches_t, mem_t, li, i):
    ns = proj_s.shape[0]
    row3 = proj_s.reshape(ns, 1, W_IN_A)
    return pl.pallas_call(
        _sample_a_body, grid=(ns,),
        in_specs=[pl.BlockSpec((1, 1, W_IN_A), lambda b: (b, 0, 0))] + [_layer_block(c, li) for c in caches_t]
                 + [_layer_block(mem_t, i)],
        out_specs=pl.BlockSpec((1, 1, A_WIDTH + X_WIDTH), lambda b: (b, 0, 0)),
        out_shape=jax.ShapeDtypeStruct((ns, 1, A_WIDTH + X_WIDTH), F32), compiler_params=_params(1), name="sample_a",
    )(row3, *caches_t, mem_t).reshape(ns, A_WIDTH + X_WIDTH)


def _q16(row):
    r = _iota((16, 128), 0)
    l = _iota((16, 128), 1)
    acc = jnp.zeros((16, 128), F32)
    for c in range(B_HEADS // 2):
        ch = jnp.broadcast_to(row[:, c * 128:(c + 1) * 128], (16, 128))
        rolled = pltpu.roll(ch, 64, axis=1)
        for hh in range(2):
            h = 2 * c + hh
            kv = h // B_GROUP
            lm = (l < 64) if kv == 0 else (l >= 64)
            acc = jnp.where((r == h) & lm, ch if hh == kv else rolled, acc)
    return acc * SCALE


def _sample_b1_body(pt_ref, row_ref, pos_ref, w1_ref, w2_ref, mm_ref, pages_ref, oc_ref, sel_ref,
                    buf_a, buf_b, rows_a, rows_b, sem, imp_sc, *, li, n_pages, ns):
    n = pl.program_id(0)
    past = n_pages * PAGE_SIZE
    n_cmp = past // CMP_STRIDE
    bufs, rows_bufs = (buf_a, buf_b), (rows_a, rows_b)

    def page_copy(page, p, s):
        return pltpu.make_async_copy(pages_ref.at[page, li, pl.ds(0, 2)], bufs[s].at[p], sem.at[s])

    def fetch(nn, s):
        def body(p, c):
            page_copy(pt_ref[nn * n_pages + p], p, s).start()
            return c
        lax.fori_loop(0, n_pages, body, 0)

    def wait(s):
        def body(p, c):
            page_copy(0, p, s).wait()
            return c
        lax.fori_loop(0, n_pages, body, 0)

    def to_rows(s, p):
        r0 = p * PAGE_SIZE if isinstance(p, int) else pl.multiple_of(p * PAGE_SIZE, PAGE_SIZE)
        for t in range(2):
            rows_bufs[s][t, pl.ds(r0, PAGE_SIZE), :] = bufs[s][p, t].reshape(2 * HEAD_DIM, PAGE_SIZE).T

    @pl.when(n == 0)
    def _():
        fetch(0, 0)
        fetch(1, 1)
        wait(0)

        def body(it, c):
            for k in range(8):
                to_rows(0, it * 8 + k)
            return c
        lax.fori_loop(0, n_pages // 8, body, 0)

    def stage(cur):
        nxt = 1 - cur

        @pl.when(n + 1 < ns)
        def _():
            wait(nxt)

        @pl.when(n + 2 < ns)
        def _():
            fetch(n + 2, cur)

        quarters = [range(q * n_pages // 4, (q + 1) * n_pages // 4) for q in range(4)]

        def next_rows(q):
            for p in quarters[q]:
                to_rows(nxt, p)

        next_rows(0)
        load = lambda t, l: rows_bufs[cur][t, pl.ds(l, n_cmp, stride=CMP_STRIDE), :]
        ck, cv = _compress_rows(load, pos_ref, w1_ref, w2_ref, n_cmp, between=lambda t: next_rows(1 + t))

        q16 = _q16(row_ref[0]).astype(BF16)
        slope = _row_consts(16, SLOPES_B)
        s = lax.dot_general(q16, ck.astype(BF16), NT, preferred_element_type=F32)
        cend = CMP_STRIDE * _iota((1, n_cmp), 1) + (CMP_LEN - 1)
        p = _masked_softmax_rows(s - slope * (past - cend).astype(F32), cend <= past)
        oc_ref[0] = jnp.dot(p.astype(BF16), cv.astype(BF16), preferred_element_type=F32)
        next_rows(3)

        r16 = _iota((16, n_cmp), 0)
        ps0 = jnp.sum(jnp.where(r16 < B_GROUP, p, 0.0), axis=0, keepdims=True)
        ps1 = jnp.sum(jnp.where((r16 >= B_GROUP) & (r16 < B_HEADS), p, 0.0), axis=0, keepdims=True)
        psum = jnp.concatenate([ps0, ps1, jnp.zeros((6, n_cmp), F32)], axis=0)
        mm = mm_ref[...]
        imp = sum(jnp.dot(t, mm, preferred_element_type=F32) for t in _split3(psum))
        blk = _iota((8, 256), 1)
        cur_blk = past // SEL_BLOCK
        forced = (blk == 0) | (blk == cur_blk) | (blk == cur_blk - 1)
        imp_sc[n] = jnp.where(blk > cur_blk, -jnp.inf, jnp.where(forced, FORCE_SCORE, imp))

    @pl.when(n % 2 == 0)
    def _():
        stage(0)

    @pl.when(n % 2 == 1)
    def _():
        stage(1)

    @pl.when(n == ns - 1)
    def _():
        impa = imp_sc[...]
        blkf = _iota(impa.shape, 2).astype(F32)
        lane = _iota((ns, 8, 128), 2)
        out = jnp.zeros((ns, 8, 128), F32)
        for r in range(SEL_TOPK):
            mx = jnp.max(impa, axis=-1, keepdims=True)
            idx = jnp.min(jnp.where(impa == mx, blkf, 1e9), axis=-1, keepdims=True)
            impa = jnp.where(blkf == idx, -jnp.inf, impa)
            out = jnp.where(lane == r, idx, out)
        sel_ref[...] = out.astype(jnp.int32)


def _sample_b1(page_table, proj_s, posw, w1bd, w2bd, mm, pages_t, li):
    ns, n_pages = page_table.shape
    past = n_pages * PAGE_SIZE
    row3 = proj_s.reshape(ns, 1, W_IN_B_PAD)
    full = lambda a: pl.BlockSpec(a.shape, lambda b, pt: (0,) * a.ndim)
    grid_spec = pltpu.PrefetchScalarGridSpec(
        num_scalar_prefetch=1, grid=(ns,),
        in_specs=[pl.BlockSpec((1, 1, W_IN_B_PAD), lambda b, pt: (b, 0, 0)), full(posw), full(w1bd), full(w2bd),
                  full(mm), pl.BlockSpec(memory_space=pl.ANY)],
        out_specs=[pl.BlockSpec((1, 16, 128), lambda b, pt: (b, 0, 0)),
                   pl.BlockSpec((ns, 8, 128), lambda b, pt: (0, 0, 0))],
        scratch_shapes=[pltpu.VMEM((n_pages, 2, B_KV, HEAD_DIM, PAGE_SIZE), F32)] * 2
                       + [pltpu.VMEM((2, past, 128), F32)] * 2
                       + [pltpu.SemaphoreType.DMA((2,)), pltpu.VMEM((ns, 8, 256), F32)])
    return pl.pallas_call(
        functools.partial(_sample_b1_body, li=li, n_pages=n_pages, ns=ns),
        grid_spec=grid_spec,
        out_shape=[jax.ShapeDtypeStruct((ns, 16, 128), F32), jax.ShapeDtypeStruct((ns, 8, 128), jnp.int32)],
        compiler_params=_params(1), name="sample_b1",
    )(page_table.reshape(-1), row3, posw, w1bd, w2bd, mm, pages_t)


def _sample_b2_body(pt_ref, sf_ref, row_ref, oc_ref, sel_ref, win_ref, mkv_ref, e16_ref, pages_ref, z_ref,
                    buf, sem, *, li, n_pages, ns):
    n = pl.program_id(0)
    past = n_pages * PAGE_SIZE
    n_blk = past // SEL_BLOCK
    per_page = PAGE_SIZE // SEL_BLOCK
    n_sel = B_KV * SEL_TOPK

    def blk_copies(page, kv, r, slot):
        return [pltpu.make_async_copy(pages_ref.at[page, li, 2 + t, kv],
                                      buf.at[slot, t, kv, :, pl.ds(r * PAGE_SIZE, PAGE_SIZE)], sem.at[slot])
                for t in range(2)]

    def fetch(nn, slot):
        for kv in range(B_KV):
            for r in range(SEL_TOPK):
                j = jnp.minimum(sf_ref[nn * n_sel + kv * SEL_TOPK + r], n_blk - 1)
                for cp in blk_copies(pt_ref[nn * n_pages + j // per_page], kv, r, slot):
                    cp.start()

    @pl.when(n == 0)
    def _():
        fetch(0, 0)

    @pl.when(n + 1 < ns)
    def _():
        fetch(n + 1, (n + 1) % 2)

    slot = n % 2
    for kv in range(B_KV):
        for r in range(SEL_TOPK):
            for cp in blk_copies(0, kv, r, slot):
                cp.wait()

    row = row_ref[0]
    q16f = _q16(row)
    q16 = q16f.astype(BF16)
    q16r = q16.astype(F32)
    slope = _row_consts(16, SLOPES_B)
    r16 = _iota((16, 128), 0)

    def new_key(col):
        kn = _bf(row[:, col:col + 128])
        return jnp.sum(q16r * kn, axis=-1, keepdims=True)

    def attend(s, s_new, v_t):
        m = jnp.maximum(jnp.max(s, axis=-1, keepdims=True), s_new)
        e = jnp.exp(s - m)
        e_new = jnp.exp(s_new - m)
        inv = 1.0 / (jnp.sum(e, axis=-1, keepdims=True) + e_new)
        return lax.dot_general((e * inv).astype(BF16), v_t, NT, preferred_element_type=F32), _bf(e_new * inv)

    n_keys = SEL_TOPK * PAGE_SIZE
    jv = jnp.dot(sel_ref[0].astype(F32).astype(BF16), e16_ref[...], preferred_element_type=F32).astype(jnp.int32)
    in_page = jnp.bitwise_and(_iota((8, n_keys), 1), PAGE_SIZE - 1)
    blk_shift = SEL_BLOCK.bit_length() - 1
    page_shift = per_page.bit_length() - 1
    ok_sel = ((in_page >> blk_shift) == (jv & (per_page - 1))) & (jv < n_blk)
    dist_sel = (past - ((jv >> page_shift) * PAGE_SIZE + in_page)).astype(F32)
    q64 = jnp.where(r16[:, :HEAD_DIM] < B_GROUP, q16f[:, :HEAD_DIM], q16f[:, HEAD_DIM:]).astype(BF16)
    s_new = new_key(1024)
    outs, p_news = [], []
    for kv in range(B_KV):
        s = jnp.dot(q64, buf[slot, 0, kv].astype(BF16), preferred_element_type=F32) - slope * dist_sel[kv:kv + 1]
        o, p_new = attend(jnp.where(ok_sel[kv:kv + 1], s, NEG), s_new, buf[slot, 1, kv].astype(BF16))
        outs.append(o)
        p_news.append(p_new)
    p_new = jnp.where(_iota((16, 1), 0) < B_GROUP, p_news[0], p_news[1])
    os16 = jnp.concatenate(outs, axis=1) + p_new * _bf(row[:, 1152:1280])

    lb = win_ref.shape[-1]
    dw = (lb - _iota((1, lb), 1)).astype(F32)
    s = jnp.dot(q16, _kv_t(win_ref, 0), preferred_element_type=F32) - slope * dw
    ow16, p_new = attend(s, new_key(1280), _kv_t(win_ref, 1))
    ow16 = ow16 + p_new * _bf(row[:, 1408:1536])

    sg = jnp.broadcast_to(_sigmoid(row[:, 2816:2944]), (16, 128))
    l16 = _iota((16, 128), 1)
    gate = lambda b: jnp.sum(jnp.where(l16 == r16 * 3 + b, sg, 0.0), axis=-1, keepdims=True)
    out16 = gate(0) * oc_ref[0] + gate(1) * os16 + gate(2) * ow16
    lane1 = _iota((1, 128), 1)
    mix = _place_heads([out16[h:h + 1, :] for h in range(B_HEADS)], lane1)

    cx = _sample_cross(row[:, 2304:2560], _kv_t(mkv_ref, 0), _kv_t(mkv_ref, 1))
    z_ref[0] = jnp.concatenate([mix * _silu(row[:, 1536:2304]), cx * _silu(row[:, 2560:2816])], axis=1)


def _sample_b2(page_table, sel, proj_s, oc, win_t, mem_t, e16, pages_t, li, i):
    ns, n_pages = page_table.shape
    row3 = proj_s.reshape(ns, 1, W_IN_B_PAD)
    full = lambda a: pl.BlockSpec(a.shape, lambda b, pt, sf: (0,) * a.ndim)
    per = lambda a: pl.BlockSpec((1,) + a.shape[1:], lambda b, pt, sf: (b,) + (0,) * (a.ndim - 1))
    grid_spec = pltpu.PrefetchScalarGridSpec(
        num_scalar_prefetch=2, grid=(ns,),
        in_specs=[per(row3), per(oc), per(sel), _layer_block(win_t, li), _layer_block(mem_t, i), full(e16),
                  pl.BlockSpec(memory_space=pl.ANY)],
        out_specs=pl.BlockSpec((1, 1, B_WIDTH + X_WIDTH), lambda b, pt, sf: (b, 0, 0)),
        scratch_shapes=[pltpu.VMEM((2, 2, B_KV, HEAD_DIM, SEL_TOPK * PAGE_SIZE), F32), pltpu.SemaphoreType.DMA((2,))])
    return pl.pallas_call(
        functools.partial(_sample_b2_body, li=li, n_pages=n_pages, ns=ns),
        grid_spec=grid_spec,
        out_shape=jax.ShapeDtypeStruct((ns, 1, B_WIDTH + X_WIDTH), F32),
        compiler_params=_params(1), name="sample_b2",
    )(page_table.reshape(-1), sel[:, :B_KV, :SEL_TOPK].reshape(-1), row3, oc, sel, win_t, mem_t, e16, pages_t
      ).reshape(ns, B_WIDTH + X_WIDTH)


def _importance_matrix(n_cmp, n_cols):
    c = np.arange(n_cmp)[:, None]
    j = np.arange(n_cols)[None, :]
    per = SEL_BLOCK // CMP_STRIDE
    m = ((c >= per * j) & (c <= per * j + per - 1)).astype(np.float32)
    m = m + ((c + 1 >= per * j) & (c + 1 <= per * j + per - 1)).astype(np.float32)
    m[n_cmp - 1, :] = 0.0
    return m


def _gate_expand():
    eg = np.zeros((128, 3 * B_WIDTH), np.float32)
    for h in range(B_HEADS):
        for b in range(3):
            eg[h * 3 + b, b * B_WIDTH + h * HEAD_DIM:b * B_WIDTH + (h + 1) * HEAD_DIM] = 1.0
    return eg


def _compress_weights(cmp_pos, cmp_w1, cmp_w2):
    eye = jnp.eye(B_KV, dtype=F32)
    posw = jnp.concatenate([cmp_pos, cmp_pos], axis=-1)
    w1 = cmp_w1.reshape(2, CMP_LEN, HEAD_DIM, CMP_HIDDEN)
    w1bd = jnp.einsum('tlek,jm->tljemk', w1, eye).reshape(2, CMP_LEN, 2 * HEAD_DIM, 2 * CMP_HIDDEN)
    w1cat = jnp.concatenate([w1bd[:, :CMP_STRIDE].reshape(2, CMP_STRIDE * 2 * HEAD_DIM, 2 * CMP_HIDDEN),
                             w1bd[:, CMP_STRIDE:].reshape(2, CMP_STRIDE * 2 * HEAD_DIM, 2 * CMP_HIDDEN)], axis=-1)
    pos_h = jnp.einsum('tlr,tlrh->tlh', posw, w1bd, precision=lax.Precision.HIGHEST)
    bias = jnp.concatenate([pos_h[:, :CMP_STRIDE].sum(axis=1), pos_h[:, CMP_STRIDE:].sum(axis=1)], axis=-1)
    w2bd = jnp.einsum('tke,jm->tjkme', cmp_w2, eye).reshape(2, 2 * CMP_HIDDEN, 2 * HEAD_DIM)
    return bias[:, None, :], w1cat.astype(BF16), w2bd.astype(BF16)


def _permute_w_in_b(w):
    d = w.shape[0]
    return jnp.concatenate([w[:, :1536], w[:, 1572:W_IN_B], w[:, 1536:1572],
                            jnp.zeros((d, W_IN_B_PAD - W_IN_B), w.dtype)], axis=1)


def kernel(x_prompt, x_sample, cache_mem_kv, cache_a_w128_kv, cache_a_w512_kv, cache_a_w2048_kv, cache_b_pages,
           cache_b_win_kv, page_table, mem_prompt, norm_pre, norm_post, norm_mem, w_mem_kv, w_in_a, w_out_a,
           w_in_b, w_out_b, cmp_pos, cmp_w1, cmp_w2):
    n, s_len, d = x_prompt.shape
    ns = x_sample.shape[0]
    depth = norm_pre.shape[0]
    page_size = cache_b_pages.shape[1]
    n_pages = page_table.shape[1]
    past = n_pages * page_size
    assert d == D_MODEL and x_sample.shape[1] == 1 and page_size == PAGE_SIZE
    assert ns >= 2
    assert s_len % A_SPAN == 0 and past % A_SPAN == 0
    caches_a = (cache_a_w128_kv, cache_a_w512_kv, cache_a_w2048_kv)
    for c, (win, _) in zip(caches_a, A_PATTERNS):
        assert c.shape[2] == win
    assert cache_b_win_kv.shape[2] == WIN_B

    tm = ROW_TILE
    n_cmp_p = s_len // CMP_STRIDE
    n_cmp_s = past // CMP_STRIDE
    slope_lanes = jnp.asarray(_slope_lanes(), F32)
    mt = jnp.asarray(_importance_matrix(n_cmp_p, 128).T, BF16)
    mm = jnp.asarray(_importance_matrix(n_cmp_s, 256), BF16)
    eg = jnp.asarray(_gate_expand(), BF16)
    e16 = jnp.asarray((np.arange(SEL_TOPK * PAGE_SIZE)[None, :] // PAGE_SIZE == np.arange(128)[:, None]), BF16)
    pages_t = cache_b_pages.transpose(0, 2, 3, 4, 5, 1)
    caches_a_t = [_rows_last(c) for c in caches_a]
    mem_t = _rows_last(cache_mem_kv)
    win_t = _rows_last(cache_b_win_kv)

    xp = x_prompt
    xs = x_sample.reshape(ns, d)
    mem2 = mem_prompt.reshape(n * N_MEM, d)
    mem_new = []
    a_p = [[] for _ in A_PATTERNS]
    a_s = [[] for _ in A_PATTERNS]
    b_p, b_s, bw_p, bw_s = [], [], [], []
    for i in range(depth):
        li = i // 2
        mkv_p = _rms_proj(mem2, norm_mem[i], w_mem_kv[i].astype(BF16), tm=N_MEM).reshape(n, N_MEM, 2 * X_WIDTH)
        mem_new.append(mkv_p.reshape(n, N_MEM, 2, 4, HEAD_DIM))
        if i % 2 == 0:
            w_in = w_in_a[li].astype(BF16)
            w_out = w_out_a[li].astype(BF16)
            proj_p = _rms_proj(xp.reshape(n * s_len, d), norm_pre[i], w_in, tm=tm).reshape(n, s_len, W_IN_A)
            proj_s = _rms_proj(xs, norm_pre[i], w_in, tm=ns)
            ols = []
            for g, (win, dil) in enumerate(A_PATTERNS):
                ols += _a_prompt_group(proj_p, g, dil)
                kv_p = proj_p[:, s_len - win:, 768:2304].reshape(n, win, 2, 3, 4, HEAD_DIM)[:, :, :, g]
                a_p[g].append(kv_p)
                a_s[g].append(proj_s[:, 768:2304].reshape(ns, 1, 2, 3, 4, HEAD_DIM)[:, :, :, g])
            xp = _finish_a(xp, ols, proj_p, mkv_p, w_out, norm_post[i], tm=tm)
            z = _sample_a(proj_s, caches_a_t, mem_t, li, i)
            xs = _tail(xs, z, w_out, norm_post[i])
        else:
            w_in = _permute_w_in_b(w_in_b[li]).astype(BF16)
            w_out = w_out_b[li].astype(BF16)
            posw, w1bd, w2bd = _compress_weights(cmp_pos[li], cmp_w1[li], cmp_w2[li])
            proj_p, kvs = _rms_proj(xp.reshape(n * s_len, d), norm_pre[i], w_in, tm=tm, side=(1024, 512))
            proj_p = proj_p.reshape(n, s_len, W_IN_B_PAD)
            kvs = kvs.reshape(n, s_len, 512)
            proj_s = _rms_proj(xs, norm_pre[i], w_in, tm=ns)
            cmpd = _compress_prompt(proj_p, posw, w1bd, w2bd)
            pos = jnp.arange(s_len, dtype=jnp.int32)
            cend = CMP_STRIDE * jnp.arange(n_cmp_p, dtype=jnp.int32) + (CMP_LEN - 1)
            mix = _nsa_prompt(proj_p, _keys_with_pos(cmpd[:, :, 0:128], cend), cmpd[:, :, 128:256].transpose(0, 2, 1),
                              _keys_with_pos(kvs[:, :, 256:384], pos), _values_by_group(kvs[:, :, 384:512]),
                              _keys_with_pos(kvs[:, :, 0:128], pos), _values_by_group(kvs[:, :, 128:256]),
                              mt, eg, slope_lanes)
            xp = _finish_b(xp, mix, proj_p, mkv_p, w_out, norm_post[i], tm=tm)
            oc, sel = _sample_b1(page_table, proj_s, posw, w1bd, w2bd, mm, pages_t, li)
            z = _sample_b2(page_table, sel, proj_s, oc, win_t, mem_t, e16, pages_t, li, i)
            xs = _tail(xs, z, w_out, norm_post[i])
            b_p.append(proj_p[:, :, 768:1280].reshape(n, s_len, 4, B_KV, HEAD_DIM))
            bw_p.append(proj_p[:, s_len - WIN_B:, 1280:1536].reshape(n, WIN_B, 2, B_KV, HEAD_DIM))
            b_s.append(proj_s[:, 768:1280].reshape(ns, 1, 4, B_KV, HEAD_DIM))
            bw_s.append(proj_s[:, 1280:1536].reshape(ns, 1, 2, B_KV, HEAD_DIM))
    return (xp, xs.reshape(ns, 1, d), jnp.stack(mem_new, axis=0),
            jnp.stack(a_p[0], axis=0), jnp.stack(a_p[1], axis=0), jnp.stack(a_p[2], axis=0),
            jnp.stack(b_p, axis=2), jnp.stack(bw_p, axis=0),
            jnp.stack(a_s[0], axis=0), jnp.stack(a_s[1], axis=0), jnp.stack(a_s[2], axis=0),
            jnp.stack(b_s, axis=2), jnp.stack(bw_s, axis=0))
```

```python
import functools

import numpy as np
import jax
import jax.numpy as jnp
from jax import lax
from jax.experimental import pallas as pl
from jax.experimental.pallas import tpu as pltpu

F32 = jnp.float32
BF16 = jnp.bfloat16

D_MODEL = 1024
HEAD_DIM = 64
SCALE = HEAD_DIM ** -0.5
LOG2E = 1.4426950408889634
RMS_EPS = 1e-6
N_MEM = 256
X_WIDTH = 256
A_PATTERNS = ((128, 1), (512, 4), (2048, 16))
A_WIDTH = 256
W_IN_A = 3072
B_HEADS = 12
B_KV = 2
B_GROUP = 6
B_WIDTH = 768
W_IN_B = 2852
W_IN_B_PAD = 2944
CMP_LEN = 32
CMP_STRIDE = 16
CMP_HIDDEN = 128
SEL_BLOCK = 64
SEL_TOPK = 16
WIN_B = 512
Q_BLOCK = 128
FORCE_SCORE = 1e4
PAGE_SIZE = 128
NEG = -1e30
GROUPS_PER_CHUNK = 4
V7X_VMEM_BYTES = 64 * 1024 * 1024
VMEM_LIMIT = V7X_VMEM_BYTES * 7 // 8
ROW_TILE = 512

NT = (((1,), (1,)), ((), ()))


def _alibi(n):
    k = np.arange(1, n + 1, dtype=np.float32)
    return [float(v) for v in np.float32(2.0) ** (np.float32(-8.0) * k / np.float32(n))]


SLOPES_A = _alibi(12)
SLOPES_B = _alibi(12)


def _params(n_axes):
    return pltpu.CompilerParams(dimension_semantics=("arbitrary",) * n_axes, vmem_limit_bytes=VMEM_LIMIT)


def _sigmoid(x):
    return 1.0 / (1.0 + jnp.exp(-x))


def _silu(x):
    return x * _sigmoid(x)


def _iota(shape, dim):
    return lax.broadcasted_iota(jnp.int32, shape, dim)


def _split3(x):
    hi = x.astype(BF16)
    r1 = x - hi.astype(F32)
    mid = r1.astype(BF16)
    lo = (r1 - mid.astype(F32)).astype(BF16)
    return hi, mid, lo


def _rms_proj_body(x_ref, g_ref, w_ref, o_ref, *side_refs, sides):
    x = x_ref[...]
    y = x * lax.rsqrt(jnp.mean(x * x, axis=-1, keepdims=True) + RMS_EPS)
    y = (y * g_ref[...]).astype(BF16)
    o = jnp.dot(y, w_ref[...], preferred_element_type=F32)
    o_ref[...] = o
    for ref, (start, width, dtype, _) in zip(side_refs, sides):
        ref[...] = o[:, start:start + width].astype(dtype)


def _rms_proj(x, g, w, tm, sides=(), seq_len=None):
    m, d = x.shape
    n = w.shape[1]
    out_specs = [pl.BlockSpec((tm, n), lambda i: (i, 0))]
    out_shape = [jax.ShapeDtypeStruct((m, n), F32)]
    for _, width, dtype, rows in sides:
        per_seq, keep = seq_len // tm, rows // tm

        def tail_block(i, per_seq=per_seq, keep=keep):
            return ((i // per_seq) * keep + jnp.maximum(i % per_seq - (per_seq - keep), 0), 0)
        out_specs.append(pl.BlockSpec((tm, width), tail_block))
        out_shape.append(jax.ShapeDtypeStruct((m // seq_len * rows, width), dtype))
    side = sides or None
    outs = pl.pallas_call(
        functools.partial(_rms_proj_body, sides=sides),
        grid=(m // tm,),
        in_specs=[pl.BlockSpec((tm, d), lambda i: (i, 0)),
                  pl.BlockSpec((1, d), lambda i: (0, 0)),
                  pl.BlockSpec((d, n), lambda i: (0, 0))],
        out_specs=out_specs,
        out_shape=out_shape,
        compiler_params=_params(1),
        name="rms_proj",
    )(x, g.reshape(1, d), w)
    return outs[0] if side is None else outs


A_SPAN = 2048


def _a_prompt_body(*refs, dil, slopes):
    ins, outs = refs[:10], refs[10:]
    t = pl.program_id(1)
    blk_rows = 128 * dil
    n_ub = A_SPAN // blk_rows
    i = _iota((128, 256), 0)
    j = _iota((128, 256), 1)
    back = 128 + i - j
    in_band = (back >= 0) & (back <= 128)
    dist = (back * dil).astype(F32)
    lane = _iota((128, 128), 1)

    def rows(ref, start):
        return ref[0, pl.ds(start, 128, stride=dil), :] if dil > 1 else ref[0, pl.ds(start, 128), :]

    def block(pair, ub, r, first):
        q_ref, k_ref, kp_ref, v_ref, vp_ref = ins[pair * 5:(pair + 1) * 5]
        o_ref, l_ref = outs[pair * 2:(pair + 1) * 2]
        start = ub * blk_rows + r
        if first:
            k_prev, v_prev = rows(kp_ref, r), rows(vp_ref, r)
            valid = in_band & (j >= jnp.where(t > 0, 0, 128))
        else:
            k_prev, v_prev = rows(k_ref, start - blk_rows), rows(v_ref, start - blk_rows)
            valid = in_band
        qp = rows(q_ref, start)
        kp = jnp.concatenate([k_prev, rows(k_ref, start)], axis=0).astype(BF16)
        vp = jnp.concatenate([v_prev, rows(v_ref, start)], axis=0).astype(BF16)
        o_pair = None
        l_pair = None
        for hh in range(2):
            hm = (lane < 64) if hh == 0 else (lane >= 64)
            qm = jnp.where(hm, qp, 0.0).astype(BF16)
            s = lax.dot_general(qm, kp, NT, preferred_element_type=F32) * SCALE
            s = jnp.where(valid, s - slopes[pair * 2 + hh] * dist, NEG)
            m = jnp.max(s, axis=-1, keepdims=True)
            e = jnp.exp(s - m)
            den = jnp.sum(e, axis=-1, keepdims=True)
            oh = jnp.dot((e * (1.0 / den)).astype(BF16), vp, preferred_element_type=F32)
            lh = jnp.broadcast_to(m + jnp.log(den), (128, 128))
            o_pair = oh if hh == 0 else jnp.where(lane < 64, o_pair, oh)
            l_pair = lh if hh == 0 else jnp.where(lane < 64, l_pair, lh)
        if dil > 1:
            o_ref[0, pl.ds(start, 128, stride=dil), :] = o_pair
            l_ref[0, pl.ds(start, 128, stride=dil), :] = l_pair
        else:
            o_ref[0, pl.ds(start, 128), :] = o_pair
            l_ref[0, pl.ds(start, 128), :] = l_pair

    def run(count, fn):
        if count == 0:
            return
        unroll = next(c for c in (8, 6, 5, 4, 3, 2, 1) if count % c == 0)

        def body(it, c):
            for k in range(unroll):
                fn(it * unroll + k)
            return c
        lax.fori_loop(0, count // unroll, body, 0)

    for pair in range(2):
        run(dil, lambda r, pair=pair: block(pair, 0, r, True))
        run((n_ub - 1) * dil, lambda idx, pair=pair: block(pair, 1 + idx // dil, idx % dil, False))


def _a_prompt_group(proj, g, dil):
    n, s_len, _ = proj.shape
    blk_rows = 128 * dil
    per_span = A_SPAN // blk_rows
    body = functools.partial(_a_prompt_body, dil=dil, slopes=tuple(SLOPES_A[g * 4:(g + 1) * 4]))
    cur = lambda col: pl.BlockSpec((1, A_SPAN, 128), lambda b, t: (b, t, col))
    prev = lambda col: pl.BlockSpec((1, blk_rows, 128), lambda b, t: (b, jnp.maximum(t * per_span - 1, 0), col))
    in_specs = []
    for pair in range(2):
        qc, kc, vc = 2 * g + pair, 6 + 2 * g + pair, 12 + 2 * g + pair
        in_specs += [cur(qc), cur(kc), prev(kc), cur(vc), prev(vc)]
    out_spec = pl.BlockSpec((1, A_SPAN, 128), lambda b, t: (b, t, 0))
    return pl.pallas_call(
        body,
        grid=(n, s_len // A_SPAN),
        in_specs=in_specs,
        out_specs=[out_spec] * 4,
        out_shape=[jax.ShapeDtypeStruct((n, s_len, 128), F32)] * 4,
        compiler_params=_params(2),
        name=f"a_prompt_g{g}",
    )(*([proj] * 10))


def _cross_rows(qx, kx, vx):
    tm = qx.shape[0]
    lane = _iota((tm, 128), 1)
    outs = []
    for pair in range(2):
        sl = slice(pair * 128, (pair + 1) * 128)
        qp, kp, vp = qx[:, sl], kx[:, sl], vx[:, sl]
        o_pair = None
        for hh in range(2):
            hm = (lane < 64) if hh == 0 else (lane >= 64)
            qm = jnp.where(hm, qp, 0.0).astype(BF16)
            s = lax.dot_general(qm, kp, NT, preferred_element_type=F32) * SCALE
            m = jnp.max(s, axis=-1, keepdims=True)
            e = jnp.exp(s - m)
            p = (e / jnp.sum(e, axis=-1, keepdims=True)).astype(BF16)
            oh = jnp.dot(p, vp, preferred_element_type=F32)
            o_pair = oh if hh == 0 else jnp.where(lane < 64, o_pair, oh)
        outs.append(o_pair)
    return jnp.concatenate(outs, axis=1)


def _out_norm_residual(x, z, w, g):
    y = jnp.dot(z.astype(BF16), w, preferred_element_type=F32)
    y = y * lax.rsqrt(jnp.mean(y * y, axis=-1, keepdims=True) + RMS_EPS)
    return x + y * g


def _finish_a_body(x_ref, *refs):
    gm_ref, qx_ref, gx_ref, mkv_ref, w_ref, g_ref, out_ref = refs[12:]
    mixes = []
    for pair in range(2):
        os_ = [refs[4 * g + 2 * pair][0] for g in range(3)]
        ls_ = [refs[4 * g + 2 * pair + 1][0] for g in range(3)]
        m = jnp.maximum(jnp.maximum(ls_[0], ls_[1]), ls_[2])
        es = [jnp.exp(l - m) for l in ls_]
        mixes.append((es[0] * os_[0] + es[1] * os_[1] + es[2] * os_[2]) / (es[0] + es[1] + es[2]))
    mix = jnp.concatenate(mixes, axis=1)
    mkv = mkv_ref[0]
    cx = _cross_rows(qx_ref[0], mkv[:, :X_WIDTH].astype(BF16), mkv[:, X_WIDTH:].astype(BF16))
    z = jnp.concatenate([mix * _silu(gm_ref[0]), cx * _silu(gx_ref[0])], axis=1)
    out_ref[0] = _out_norm_residual(x_ref[0], z, w_ref[...], g_ref[...])


def _finish_b_body(x_ref, mix_ref, gm_ref, qx_ref, gx_ref, mkv_ref, w_ref, g_ref, out_ref):
    mkv = mkv_ref[0]
    cx = _cross_rows(qx_ref[0], mkv[:, :X_WIDTH].astype(BF16), mkv[:, X_WIDTH:].astype(BF16))
    z = jnp.concatenate([mix_ref[0] * _silu(gm_ref[0]), cx * _silu(gx_ref[0])], axis=1)
    out_ref[0] = _out_norm_residual(x_ref[0], z, w_ref[...], g_ref[...])


def _finish_a(x, ols, proj, mkv, w_out, g_post, tm):
    n, s_len, d = x.shape
    row = lambda w, c: pl.BlockSpec((1, tm, w), lambda b, t: (b, t, c))
    in_specs = ([row(d, 0)] + [row(128, 0)] * 12 + [row(256, 9), row(256, 10), row(256, 11)]
                + [pl.BlockSpec((1, N_MEM, 2 * X_WIDTH), lambda b, t: (b, 0, 0)),
                   pl.BlockSpec(w_out.shape, lambda b, t: (0, 0)),
                   pl.BlockSpec((1, d), lambda b, t: (0, 0))])
    return pl.pallas_call(
        _finish_a_body, grid=(n, s_len // tm), in_specs=in_specs, out_specs=row(d, 0),
        out_shape=jax.ShapeDtypeStruct((n, s_len, d), F32), compiler_params=_params(2), name="finish_a",
    )(x, *ols, proj, proj, proj, mkv, w_out, g_post.reshape(1, d))


def _finish_b(x, mix, proj, mkv, w_out, g_post, tm):
    n, s_len, d = x.shape
    row = lambda w, c: pl.BlockSpec((1, tm, w), lambda b, t: (b, t, c))
    in_specs = [row(d, 0), row(B_WIDTH, 0), row(B_WIDTH, 2), row(256, 9), row(256, 10),
                pl.BlockSpec((1, N_MEM, 2 * X_WIDTH), lambda b, t: (b, 0, 0)),
                pl.BlockSpec(w_out.shape, lambda b, t: (0, 0)),
                pl.BlockSpec((1, d), lambda b, t: (0, 0))]
    return pl.pallas_call(
        _finish_b_body, grid=(n, s_len // tm), in_specs=in_specs, out_specs=row(d, 0),
        out_shape=jax.ShapeDtypeStruct((n, s_len, d), F32), compiler_params=_params(2), name="finish_b",
    )(x, mix, proj, proj, proj, mkv, w_out, g_post.reshape(1, d))


def _tail_body(x_ref, z_ref, w_ref, g_ref, out_ref):
    out_ref[...] = _out_norm_residual(x_ref[...], z_ref[...], w_ref[...], g_ref[...])


def _tail(x, z, w_out, g_post):
    m, d = x.shape
    full = lambda a: pl.BlockSpec(a.shape, lambda i: (0,) * a.ndim)
    g2 = g_post.reshape(1, d)
    return pl.pallas_call(
        _tail_body, grid=(1,), in_specs=[full(x), full(z), full(w_out), full(g2)], out_specs=full(x),
        out_shape=jax.ShapeDtypeStruct((m, d), F32), compiler_params=_params(1), name="sample_tail",
    )(x, z, w_out, g2)


def _compress_rows(load_rows, pos_ref, w1_ref, w2_ref, n_cmp, between=None):
    outs = []
    half = 2 * CMP_HIDDEN
    for t in range(2):
        y = jnp.concatenate([load_rows(t, l).astype(BF16) for l in range(CMP_STRIDE)], axis=1)
        ab = jnp.dot(y, w1_ref[t], preferred_element_type=F32) + pos_ref[t]
        h = ab[:, :half] + pltpu.roll(ab[:, half:], n_cmp - 1, axis=0)
        outs.append(jnp.dot(_silu(h).astype(BF16), w2_ref[t], preferred_element_type=F32))
        if between is not None:
            between(t)
    return outs


def _compress_body(k_ref, v_ref, pos_ref, w1_ref, w2_ref, o_ref, *, n_cmp):
    refs = (k_ref, v_ref)
    load = lambda t, l: refs[t][0, pl.ds(l, n_cmp, stride=CMP_STRIDE), :]
    ck, cv = _compress_rows(load, pos_ref, w1_ref, w2_ref, n_cmp)
    o_ref[0, :, 0:128] = ck.astype(BF16)
    o_ref[0, :, 128:256] = cv.astype(BF16)


def _compress_prompt(proj, posw, w1bd, w2bd):
    n, s_len, _ = proj.shape
    n_cmp = s_len // CMP_STRIDE
    full = lambda a: pl.BlockSpec(a.shape, lambda b: (0,) * a.ndim)
    return pl.pallas_call(
        functools.partial(_compress_body, n_cmp=n_cmp), grid=(n,),
        in_specs=[pl.BlockSpec((1, s_len, 128), lambda b: (b, 0, 6)), pl.BlockSpec((1, s_len, 128), lambda b: (b, 0, 7)),
                  full(posw), full(w1bd), full(w2bd)],
        out_specs=pl.BlockSpec((1, n_cmp, 256), lambda b: (b, 0, 0)),
        out_shape=jax.ShapeDtypeStruct((n, n_cmp, 256), BF16), compiler_params=_params(1), name="compress_prompt",
    )(proj, proj, posw, w1bd, w2bd)


def _place_heads(tiles, lane):
    chunks = []
    for c in range(B_HEADS // 2):
        t0, t1 = tiles[2 * c], tiles[2 * c + 1]
        if (2 * c) // B_GROUP == 1:
            t0 = pltpu.roll(t0, 64, axis=1)
        if (2 * c + 1) // B_GROUP == 0:
            t1 = pltpu.roll(t1, 64, axis=1)
        chunks.append(jnp.where(lane < 64, t0, t1))
    return jnp.concatenate(chunks, axis=1)


def _masked_softmax_rows(s, ok):
    s = jnp.where(ok, s, NEG)
    m = jnp.max(s, axis=-1, keepdims=True)
    e = jnp.where(ok, jnp.exp(s - m), 0.0)
    den = jnp.maximum(jnp.sum(e, axis=-1, keepdims=True), 1e-30)
    return e * (1.0 / den)


ONES_ROWS = 16


def _values_and_ones(v_ref, idx, kv):
    v = v_ref[idx + (slice(kv * HEAD_DIM, (kv + 1) * HEAD_DIM), slice(None))]
    return jnp.concatenate([v, jnp.ones((ONES_ROWS, v.shape[1]), BF16)], axis=0)


def _head_tile(num, den, kv):
    x = num if den is None else num * (1.0 / den)
    z = jnp.zeros_like(x)
    return jnp.concatenate([x, z] if kv == 0 else [z, x], axis=0).T


def _nsa_prompt_body(q_ref, gt_ref, kc_ref, vct_ref, kw_ref, vwt_ref, ks_ref, vst_ref, mt_ref, eg_ref,
                     sl_ref, out_ref, q6_sc, m_sc, acc_sc, sel_sc, words_sm, idx_sm, *, s_len):
    qb = pl.program_id(1)
    qstart = qb * Q_BLOCK
    n_cmp = s_len // CMP_STRIDE
    q = q_ref[0] * (SCALE * LOG2E)
    lane = _iota((Q_BLOCK, 128), 1)
    tq_row = qstart + _iota((1, Q_BLOCK), 1)
    oc_t, os_t, ow_t = [None] * B_HEADS, [None] * B_HEADS, [None] * B_HEADS

    sub = _iota((128, 128), 0)
    psums = []

    for kv in range(B_KV):
        for g in range(B_GROUP):
            h = kv * B_GROUP + g
            ch = q[:, (h // 2) * 128:(h // 2 + 1) * 128]
            if h % 2 == 1:
                ch = pltpu.roll(ch, 64, axis=1)
            q6_sc[kv, g * 128:(g + 1) * 128, :] = jnp.where(lane < 64, ch, sl_ref[h:h + 1, :]).astype(BF16)

    cok = (CMP_STRIDE * _iota((n_cmp, Q_BLOCK), 0) + (CMP_LEN - 1)) <= tq_row
    q_ok = tq_row >= (CMP_LEN - 1)
    n_wb = WIN_B // Q_BLOCK + 1

    def cmp_scores(kv):
        return lax.dot_general(kc_ref[0, kv], q6_sc[kv], NT, preferred_element_type=F32)

    def cmp_finish(kv, s_t):
        psum = jnp.zeros((n_cmp, Q_BLOCK), F32)
        ps = []
        for g in range(B_GROUP):
            s = jnp.where(cok, s_t[:, g * 128:(g + 1) * 128], NEG)
            e = jnp.exp2(s - jnp.max(s, axis=0, keepdims=True))
            p = e * jnp.where(q_ok, 1.0 / jnp.sum(e, axis=0, keepdims=True), 0.0)
            psum = psum + p
            ps.append(p.astype(BF16))
        oc = jnp.dot(vct_ref[0, kv * HEAD_DIM:(kv + 1) * HEAD_DIM, :], jnp.concatenate(ps, axis=1),
                     preferred_element_type=F32)
        for g in range(B_GROUP):
            oc_t[kv * B_GROUP + g] = _head_tile(oc[:, g * 128:(g + 1) * 128], None, kv)
        psums.append(psum)

    def win_scores(kv):
        kparts, vparts, pparts = [], [], []
        for wb in range(n_wb):
            b_raw = qb - (n_wb - 1) + wb
            b = jnp.maximum(b_raw, 0)
            r0 = pl.multiple_of(b * Q_BLOCK, Q_BLOCK)
            kparts.append(kw_ref[0, kv, pl.ds(r0, Q_BLOCK), :])
            vparts.append(_values_and_ones(vwt_ref, (0, b), kv))
            pparts.append(jnp.where(b_raw >= 0, r0, s_len) + _iota((128, Q_BLOCK), 0))
        s_t = lax.dot_general(jnp.concatenate(kparts, axis=0), q6_sc[kv], NT, preferred_element_type=F32)
        return s_t, jnp.concatenate(vparts, axis=1), jnp.concatenate(pparts, axis=0)

    def win_finish(kv, s_t, v_t, kpos):
        dw = tq_row - kpos
        wok = (dw >= 0) & (dw <= WIN_B)
        ps = []
        for g in range(B_GROUP):
            s = jnp.where(wok, s_t[:, g * 128:(g + 1) * 128], NEG)
            ps.append(jnp.exp2(s - jnp.max(s, axis=0, keepdims=True)).astype(BF16))
        ow = jnp.dot(v_t, jnp.concatenate(ps, axis=1), preferred_element_type=F32)
        for g in range(B_GROUP):
            cols = slice(g * 128, (g + 1) * 128)
            ow_t[kv * B_GROUP + g] = _head_tile(ow[0:HEAD_DIM, cols], ow[HEAD_DIM:HEAD_DIM + 1, cols], kv)

    sc0 = cmp_scores(0)
    sw0 = win_scores(0)
    cmp_finish(0, sc0)
    sc1 = cmp_scores(1)
    win_finish(0, *sw0)
    sw1 = win_scores(1)
    cmp_finish(1, sc1)
    win_finish(1, *sw1)

    mt = mt_ref[...]
    blk = _iota((128, Q_BLOCK), 0)
    ql = _iota((128, Q_BLOCK), 1)
    cur = jnp.where(ql >= SEL_BLOCK, qb * 2 + 1, qb * 2)
    forced = (blk == 0) | (blk == cur) | (blk == cur - 1)
    blkf = blk.astype(F32)
    imps = [jnp.where((blk > cur) | forced, -jnp.inf,
                      sum(jnp.dot(mt, t, preferred_element_type=F32) for t in _split3(psum))) for psum in psums]
    sels = [jnp.where(forced, 1.0, 0.0)] * B_KV
    for _ in range(SEL_TOPK - 3):
        for kv in range(B_KV):
            mx = jnp.max(imps[kv], axis=0, keepdims=True)
            idx = jnp.min(jnp.where(imps[kv] == mx, blkf, 1e9), axis=0, keepdims=True)
            hit = blkf == idx
            sels[kv] = jnp.where(hit, 1.0, sels[kv])
            imps[kv] = jnp.where(hit, -jnp.inf, imps[kv])

    blk_col = _iota((128, 1), 0)
    weight = lax.shift_left(jnp.ones((128, 1), jnp.int32), blk_col & 15).astype(F32)
    for kv in range(B_KV):
        sel_sc[kv] = sels[kv]
        contrib = jnp.max(sels[kv], axis=1, keepdims=True) * weight
        for w in range(8):
            words_sm[kv * 8 + w] = jnp.sum(contrib[16 * w:16 * (w + 1), :]).astype(jnp.int32)

    m_sc[...] = jnp.full(m_sc.shape, NEG, F32)
    acc_sc[...] = jnp.zeros(acc_sc.shape, F32)
    list_len = idx_sm.shape[0] // B_KV
    cnts = []
    n_grp = s_len // Q_BLOCK
    for kv in range(B_KV):
        cnt = jnp.int32(0)
        for w in range((n_grp + 7) // 8):
            word = words_sm[kv * 8 + w]
            for j in range(min(8, n_grp - 8 * w)):
                idx_sm[kv * list_len + cnt] = 8 * w + j
                cnt = cnt + jnp.where(((word >> (2 * j)) & 3) != 0, 1, 0)
        cnts.append(cnt)
    n_chunks = (jnp.maximum(cnts[0], cnts[1]) + GROUPS_PER_CHUNK - 1) // GROUPS_PER_CHUNK
    for kv in range(B_KV):
        def pad(i, c, kv=kv):
            idx_sm[kv * list_len + i] = -1
            return c
        lax.fori_loop(cnts[kv], n_chunks * GROUPS_PER_CHUNK, pad, 0)

    sub8 = _iota((8, Q_BLOCK), 0)

    def group_hits(kv, gi):
        rows8 = sel_sc[kv, pl.ds(pl.multiple_of((gi >> 2) * 8, 8), 8), :]
        r = (gi & 3) * 2
        lo = jnp.sum(jnp.where(sub8 == r, rows8, 0.0), axis=0, keepdims=True)
        hi = jnp.sum(jnp.where(sub8 == r + 1, rows8, 0.0), axis=0, keepdims=True)
        return jnp.where(sub < SEL_BLOCK, lo, hi)

    def chunk_scores(c, kv):
        kts, vts, hits, kposs = [], [], [], []
        for j in range(GROUPS_PER_CHUNK):
            gi_raw = idx_sm[kv * list_len + c * GROUPS_PER_CHUNK + j]
            gi = jnp.maximum(gi_raw, 0)
            k0 = pl.multiple_of(gi * Q_BLOCK, Q_BLOCK)
            kts.append(ks_ref[0, kv, pl.ds(k0, Q_BLOCK), :])
            vts.append(_values_and_ones(vst_ref, (0, gi), kv))
            hits.append(group_hits(kv, gi))
            kposs.append(jnp.where(gi_raw >= 0, k0, s_len) + _iota((128, 128), 0))
        s_t = lax.dot_general(jnp.concatenate(kts, axis=0), q6_sc[kv], NT, preferred_element_type=F32)
        ok = ((jnp.concatenate(hits, axis=0) > 0.5)
              & (jnp.concatenate(kposs, axis=0) <= qstart + _iota((GROUPS_PER_CHUNK * 128, 128), 1)))
        return s_t, ok, jnp.concatenate(vts, axis=1)

    def chunk_update(kv, s_t, ok, v_t):
        m_old = m_sc[kv]
        m_new, ps = [], []
        for g in range(B_GROUP):
            cols = slice(g * 128, (g + 1) * 128)
            s = jnp.where(ok, s_t[:, cols], NEG)
            mg = jnp.maximum(m_old[:, cols], jnp.max(s, axis=0, keepdims=True))
            ps.append(jnp.exp2(s - mg).astype(BF16))
            m_new.append(mg)
        m_new = jnp.concatenate(m_new, axis=1)
        pv = jnp.dot(v_t, jnp.concatenate(ps, axis=1), preferred_element_type=F32)
        acc_sc[kv] = jnp.exp2(m_old - m_new) * acc_sc[kv] + pv
        m_sc[kv] = m_new

    def chunk(c, carry):
        first = chunk_scores(c, 0)
        second = chunk_scores(c, 1)
        chunk_update(0, *first)
        chunk_update(1, *second)
        return carry

    lax.fori_loop(0, n_chunks, chunk, 0)
    for kv in range(B_KV):
        for g in range(B_GROUP):
            cols = slice(g * 128, (g + 1) * 128)
            os_t[kv * B_GROUP + g] = _head_tile(acc_sc[kv, 0:HEAD_DIM, cols], acc_sc[kv, HEAD_DIM:HEAD_DIM + 1, cols], kv)

    sg = _sigmoid(gt_ref[0])
    eg = eg_ref[...]
    gexp = sum(jnp.dot(t, eg, preferred_element_type=F32) for t in _split3(sg))
    out_ref[0] = (gexp[:, 0:B_WIDTH] * _place_heads(oc_t, lane)
                  + gexp[:, B_WIDTH:2 * B_WIDTH] * _place_heads(os_t, lane)
                  + gexp[:, 2 * B_WIDTH:] * _place_heads(ow_t, lane))


def _nsa_prompt(proj, kc, vc_t, kw, vw_t, ks, vs_t, mt, eg, slope_lanes):
    n, s_len, _ = proj.shape
    n_cmp = s_len // CMP_STRIDE
    n_grp = s_len // Q_BLOCK
    full = lambda a: pl.BlockSpec(a.shape, lambda b, t: (0,) * a.ndim)
    per_n = lambda a: pl.BlockSpec((1,) + a.shape[1:], lambda b, t: (b,) + (0,) * (a.ndim - 1))
    return pl.pallas_call(
        functools.partial(_nsa_prompt_body, s_len=s_len),
        grid=(n, n_grp),
        in_specs=[pl.BlockSpec((1, Q_BLOCK, B_WIDTH), lambda b, t: (b, t, 0)),
                  pl.BlockSpec((1, Q_BLOCK, 128), lambda b, t: (b, t, 22)),
                  per_n(kc), per_n(vc_t), per_n(kw), per_n(vw_t), per_n(ks), per_n(vs_t),
                  full(mt), full(eg), full(slope_lanes)],
        out_specs=pl.BlockSpec((1, Q_BLOCK, B_WIDTH), lambda b, t: (b, t, 0)),
        out_shape=jax.ShapeDtypeStruct((n, s_len, B_WIDTH), F32),
        scratch_shapes=[pltpu.VMEM((B_KV, B_GROUP * Q_BLOCK, 128), BF16), pltpu.VMEM((B_KV, 1, B_GROUP * Q_BLOCK), F32),
                        pltpu.VMEM((B_KV, HEAD_DIM + ONES_ROWS, B_GROUP * Q_BLOCK), F32),
                        pltpu.VMEM((B_KV, 128, Q_BLOCK), F32), pltpu.SMEM((B_KV * 8,), jnp.int32),
                        pltpu.SMEM((B_KV * (n_grp + GROUPS_PER_CHUNK),), jnp.int32)],
        compiler_params=_params(2), name="nsa_prompt",
    )(proj, proj, kc, vc_t, kw, vw_t, ks, vs_t, mt, eg, slope_lanes)


POS_LANE = 64


def _slope_lanes():
    out = np.zeros((B_HEADS, 128), np.float32)
    for h, slope in enumerate(SLOPES_B):
        s = np.float32(np.float64(slope) * LOG2E)
        hi = np.float32(np.asarray(s, np.float32).astype(jnp.bfloat16))
        mid = np.float32(np.asarray(np.float32(s) - hi, np.float32).astype(jnp.bfloat16))
        lo = np.float32(np.asarray(np.float32(s) - hi - mid, np.float32).astype(jnp.bfloat16))
        out[h, POS_LANE:POS_LANE + 6] = [hi, mid, lo, hi, mid, lo]
    return out


def _keys_with_pos(k2, pos):
    n, n_keys, _ = k2.shape
    lo = (pos % Q_BLOCK).astype(BF16)[None, :, None]
    hi = (pos - pos % Q_BLOCK).astype(BF16)[None, :, None]
    tail = jnp.concatenate([jnp.broadcast_to(lo, (n, n_keys, 3)), jnp.broadcast_to(hi, (n, n_keys, 3)),
                            jnp.zeros((n, n_keys, 128 - POS_LANE - 6), BF16)], axis=-1)
    return jnp.stack([jnp.concatenate([k2[..., kv * 64:(kv + 1) * 64], tail], axis=-1) for kv in range(B_KV)], axis=1)


def _values_by_group(v2):
    n, s_len, _ = v2.shape
    return v2.reshape(n, s_len // Q_BLOCK, Q_BLOCK, 128).transpose(0, 1, 3, 2)


def _heads_rows(vec, n_rows, width):
    r = _iota((n_rows, width), 0)
    l = _iota((n_rows, width), 1)
    hm = (l >= r * HEAD_DIM) & (l < r * HEAD_DIM + HEAD_DIM)
    return jnp.where(hm, jnp.broadcast_to(vec, (n_rows, width)), 0.0), hm


def _bf(x):
    return x.astype(BF16).astype(F32)


def _row_consts(n_rows, vals):
    r = _iota((n_rows, 1), 0)
    out = jnp.zeros((n_rows, 1), F32)
    for i, v in enumerate(vals):
        out = jnp.where(r == i, v, out)
    return out


def _rows_last(cache):
    nd = cache.ndim
    return cache.transpose(tuple(range(nd - 4)) + (nd - 3, nd - 2, nd - 1, nd - 4))


def _kv_t(ref, t):
    x = ref[0, 0, t]
    return x.reshape(x.shape[0] * x.shape[1], x.shape[2]).astype(BF16)


def _sample_cross(qx_row, k_t, v_t):
    q8, hm = _heads_rows(qx_row, 8, X_WIDTH)
    s = jnp.dot(q8.astype(BF16), k_t, preferred_element_type=F32) * SCALE
    e = jnp.exp(s - jnp.max(s, axis=-1, keepdims=True))
    p = (e * (1.0 / jnp.sum(e, axis=-1, keepdims=True))).astype(BF16)
    o8 = lax.dot_general(p, v_t, NT, preferred_element_type=F32)
    return jnp.sum(jnp.where(hm, o8, 0.0), axis=0, keepdims=True)


def _sample_a_body(row_ref, c0_ref, c1_ref, c2_ref, mkv_ref, z_ref):
    row = row_ref[0]
    outs, lses = [], []
    hm = None
    for g, (win, dil) in enumerate(A_PATTERNS):
        cref = (c0_ref, c1_ref, c2_ref)[g]
        q8, hm = _heads_rows(row[:, g * 256:(g + 1) * 256], 8, A_WIDTH)
        knew = row[:, 768 + g * 256:768 + (g + 1) * 256]
        vnew = row[:, 1536 + g * 256:1536 + (g + 1) * 256]
        q8b = q8.astype(BF16)
        slope = _row_consts(8, SLOPES_A[g * 4:(g + 1) * 4])
        s = jnp.dot(q8b, _kv_t(cref, 0), preferred_element_type=F32) * SCALE
        r = _iota((8, win), 1)
        s = jnp.where((r & (dil - 1)) == 0, s - slope * (win - r).astype(F32), NEG)
        s_new = jnp.sum(q8b.astype(F32) * _bf(knew), axis=-1, keepdims=True) * SCALE
        m = jnp.maximum(jnp.max(s, axis=-1, keepdims=True), s_new)
        e = jnp.exp(s - m)
        e_new = jnp.exp(s_new - m)
        den = jnp.sum(e, axis=-1, keepdims=True) + e_new
        inv = 1.0 / den
        o8 = (lax.dot_general((e * inv).astype(BF16), _kv_t(cref, 1), NT, preferred_element_type=F32)
              + _bf(e_new * inv) * _bf(vnew))
        outs.append(o8)
        lses.append(m + jnp.log(den))
    mx = jnp.maximum(jnp.maximum(lses[0], lses[1]), lses[2])
    ws = [jnp.exp(l - mx) for l in lses]
    mix8 = (ws[0] * outs[0] + ws[1] * outs[1] + ws[2] * outs[2]) / (ws[0] + ws[1] + ws[2])
    mix = jnp.sum(jnp.where(hm, mix8, 0.0), axis=0, keepdims=True)
    cx = _sample_cross(row[:, 2560:2816], _kv_t(mkv_ref, 0), _kv_t(mkv_ref, 1))
    z_ref[0] = jnp.concatenate([mix * _silu(row[:, 2304:2560]), cx * _silu(row[:, 2816:3072])], axis=1)


def _layer_block(cache_t, layer):
    return pl.BlockSpec((1, 1) + cache_t.shape[2:], lambda b, *_: (layer, b, 0, 0, 0, 0))


def _sample_a(proj_s, caches_t, mem_t, li, i):
    ns = proj_s.shape[0]
    row3 = proj_s.reshape(ns, 1, W_IN_A)
    return pl.pallas_call(
        _sample_a_body, grid=(ns,),
        in_specs=[pl.BlockSpec((1, 1, W_IN_A), lambda b: (b, 0, 0))] + [_layer_block(c, li) for c in caches_t]
                 + [_layer_block(mem_t, i)],
        out_specs=pl.BlockSpec((1, 1, A_WIDTH + X_WIDTH), lambda b: (b, 0, 0)),
        out_shape=jax.ShapeDtypeStruct((ns, 1, A_WIDTH + X_WIDTH), F32), compiler_params=_params(1), name="sample_a",
    )(row3, *caches_t, mem_t).reshape(ns, A_WIDTH + X_WIDTH)


def _q16(row):
    r = _iota((16, 128), 0)
    l = _iota((16, 128), 1)
    acc = jnp.zeros((16, 128), F32)
    for c in range(B_HEADS // 2):
        ch = jnp.broadcast_to(row[:, c * 128:(c + 1) * 128], (16, 128))
        rolled = pltpu.roll(ch, 64, axis=1)
        for hh in range(2):
            h = 2 * c + hh
            kv = h // B_GROUP
            lm = (l < 64) if kv == 0 else (l >= 64)
            acc = jnp.where((r == h) & lm, ch if hh == kv else rolled, acc)
    return acc * SCALE


def _sample_b1_body(pt_ref, row_ref, pos_ref, w1_ref, w2_ref, mm_ref, pages_ref, oc_ref, sel_ref,
                    buf_a, buf_b, rows_a, rows_b, sem, imp_sc, *, li, n_pages, ns):
    n = pl.program_id(0)
    past = n_pages * PAGE_SIZE
    n_cmp = past // CMP_STRIDE
    bufs, rows_bufs = (buf_a, buf_b), (rows_a, rows_b)

    def page_copy(page, p, s):
        return pltpu.make_async_copy(pages_ref.at[page, li, pl.ds(0, 2)], bufs[s].at[p], sem.at[s])

    def fetch(nn, s):
        def body(p, c):
            page_copy(pt_ref[nn * n_pages + p], p, s).start()
            return c
        lax.fori_loop(0, n_pages, body, 0)

    def wait(s):
        def body(p, c):
            page_copy(0, p, s).wait()
            return c
        lax.fori_loop(0, n_pages, body, 0)

    def to_rows(s, p):
        r0 = p * PAGE_SIZE if isinstance(p, int) else pl.multiple_of(p * PAGE_SIZE, PAGE_SIZE)
        for t in range(2):
            rows_bufs[s][t, pl.ds(r0, PAGE_SIZE), :] = bufs[s][p, t].reshape(2 * HEAD_DIM, PAGE_SIZE).T

    @pl.when(n == 0)
    def _():
        fetch(0, 0)
        fetch(1, 1)
        wait(0)

        def body(it, c):
            for k in range(8):
                to_rows(0, it * 8 + k)
            return c
        lax.fori_loop(0, n_pages // 8, body, 0)

    def stage(cur):
        nxt = 1 - cur

        @pl.when(n + 1 < ns)
        def _():
            wait(nxt)

        @pl.when(n + 2 < ns)
        def _():
            fetch(n + 2, cur)

        quarters = [range(q * n_pages // 4, (q + 1) * n_pages // 4) for q in range(4)]

        def next_rows(q):
            for p in quarters[q]:
                to_rows(nxt, p)

        next_rows(0)
        load = lambda t, l: rows_bufs[cur][t, pl.ds(l, n_cmp, stride=CMP_STRIDE), :]
        ck, cv = _compress_rows(load, pos_ref, w1_ref, w2_ref, n_cmp, between=lambda t: next_rows(1 + t))

        q16 = _q16(row_ref[0]).astype(BF16)
        slope = _row_consts(16, SLOPES_B)
        s = lax.dot_general(q16, ck.astype(BF16), NT, preferred_element_type=F32)
        cend = CMP_STRIDE * _iota((1, n_cmp), 1) + (CMP_LEN - 1)
        p = _masked_softmax_rows(s - slope * (past - cend).astype(F32), cend <= past)
        oc_ref[0] = jnp.dot(p.astype(BF16), cv.astype(BF16), preferred_element_type=F32)
        next_rows(3)

        r16 = _iota((16, n_cmp), 0)
        ps0 = jnp.sum(jnp.where(r16 < B_GROUP, p, 0.0), axis=0, keepdims=True)
        ps1 = jnp.sum(jnp.where((r16 >= B_GROUP) & (r16 < B_HEADS), p, 0.0), axis=0, keepdims=True)
        psum = jnp.concatenate([ps0, ps1, jnp.zeros((6, n_cmp), F32)], axis=0)
        mm = mm_ref[...]
        imp = sum(jnp.dot(t, mm, preferred_element_type=F32) for t in _split3(psum))
        blk = _iota((8, 256), 1)
        cur_blk = past // SEL_BLOCK
        forced = (blk == 0) | (blk == cur_blk) | (blk == cur_blk - 1)
        imp_sc[n] = jnp.where(blk > cur_blk, -jnp.inf, jnp.where(forced, FORCE_SCORE, imp))

    @pl.when(n % 2 == 0)
    def _():
        stage(0)

    @pl.when(n % 2 == 1)
    def _():
        stage(1)

    @pl.when(n == ns - 1)
    def _():
        impa = imp_sc[...]
        blkf = _iota(impa.shape, 2).astype(F32)
        lane = _iota((ns, 8, 128), 2)
        out = jnp.zeros((ns, 8, 128), F32)
        for r in range(SEL_TOPK):
            mx = jnp.max(impa, axis=-1, keepdims=True)
            idx = jnp.min(jnp.where(impa == mx, blkf, 1e9), axis=-1, keepdims=True)
            impa = jnp.where(blkf == idx, -jnp.inf, impa)
            out = jnp.where(lane == r, idx, out)
        sel_ref[...] = out.astype(jnp.int32)


def _sample_b1(page_table, proj_s, posw, w1bd, w2bd, mm, pages_t, li):
    ns, n_pages = page_table.shape
    past = n_pages * PAGE_SIZE
    row3 = proj_s.reshape(ns, 1, W_IN_B_PAD)
    full = lambda a: pl.BlockSpec(a.shape, lambda b, pt: (0,) * a.ndim)
    grid_spec = pltpu.PrefetchScalarGridSpec(
        num_scalar_prefetch=1, grid=(ns,),
        in_specs=[pl.BlockSpec((1, 1, W_IN_B_PAD), lambda b, pt: (b, 0, 0)), full(posw), full(w1bd), full(w2bd),
                  full(mm), pl.BlockSpec(memory_space=pl.ANY)],
        out_specs=[pl.BlockSpec((1, 16, 128), lambda b, pt: (b, 0, 0)),
                   pl.BlockSpec((ns, 8, 128), lambda b, pt: (0, 0, 0))],
        scratch_shapes=[pltpu.VMEM((n_pages, 2, B_KV, HEAD_DIM, PAGE_SIZE), F32)] * 2
                       + [pltpu.VMEM((2, past, 128), F32)] * 2
                       + [pltpu.SemaphoreType.DMA((2,)), pltpu.VMEM((ns, 8, 256), F32)])
    return pl.pallas_call(
        functools.partial(_sample_b1_body, li=li, n_pages=n_pages, ns=ns),
        grid_spec=grid_spec,
        out_shape=[jax.ShapeDtypeStruct((ns, 16, 128), F32), jax.ShapeDtypeStruct((ns, 8, 128), jnp.int32)],
        compiler_params=_params(1), name="sample_b1",
    )(page_table.reshape(-1), row3, posw, w1bd, w2bd, mm, pages_t)


def _sample_b2_body(pt_ref, sf_ref, row_ref, oc_ref, sel_ref, win_ref, mkv_ref, e16_ref, pages_ref, z_ref,
                    buf, sem, *, li, n_pages, ns):
    n = pl.program_id(0)
    past = n_pages * PAGE_SIZE
    n_blk = past // SEL_BLOCK
    per_page = PAGE_SIZE // SEL_BLOCK
    n_sel = B_KV * SEL_TOPK

    def blk_copies(page, kv, r, slot):
        return [pltpu.make_async_copy(pages_ref.at[page, li, 2 + t, kv],
                                      buf.at[slot, t, kv, :, pl.ds(r * PAGE_SIZE, PAGE_SIZE)], sem.at[slot])
                for t in range(2)]

    def fetch(nn, slot):
        for kv in range(B_KV):
            for r in range(SEL_TOPK):
                j = jnp.minimum(sf_ref[nn * n_sel + kv * SEL_TOPK + r], n_blk - 1)
                for cp in blk_copies(pt_ref[nn * n_pages + j // per_page], kv, r, slot):
                    cp.start()

    @pl.when(n == 0)
    def _():
        fetch(0, 0)

    @pl.when(n + 1 < ns)
    def _():
        fetch(n + 1, (n + 1) % 2)

    slot = n % 2
    for kv in range(B_KV):
        for r in range(SEL_TOPK):
            for cp in blk_copies(0, kv, r, slot):
                cp.wait()

    row = row_ref[0]
    q16f = _q16(row)
    q16 = q16f.astype(BF16)
    q16r = q16.astype(F32)
    slope = _row_consts(16, SLOPES_B)
    r16 = _iota((16, 128), 0)

    def new_key(col):
        kn = _bf(row[:, col:col + 128])
        return jnp.sum(q16r * kn, axis=-1, keepdims=True)

    def attend(s, s_new, v_t):
        m = jnp.maximum(jnp.max(s, axis=-1, keepdims=True), s_new)
        e = jnp.exp(s - m)
        e_new = jnp.exp(s_new - m)
        inv = 1.0 / (jnp.sum(e, axis=-1, keepdims=True) + e_new)
        return lax.dot_general((e * inv).astype(BF16), v_t, NT, preferred_element_type=F32), _bf(e_new * inv)

    n_keys = SEL_TOPK * PAGE_SIZE
    jv = jnp.dot(sel_ref[0].astype(F32).astype(BF16), e16_ref[...], preferred_element_type=F32).astype(jnp.int32)
    in_page = jnp.bitwise_and(_iota((8, n_keys), 1), PAGE_SIZE - 1)
    blk_shift = SEL_BLOCK.bit_length() - 1
    page_shift = per_page.bit_length() - 1
    ok_sel = ((in_page >> blk_shift) == (jv & (per_page - 1))) & (jv < n_blk)
    dist_sel = (past - ((jv >> page_shift) * PAGE_SIZE + in_page)).astype(F32)
    q64 = jnp.where(r16[:, :HEAD_DIM] < B_GROUP, q16f[:, :HEAD_DIM], q16f[:, HEAD_DIM:]).astype(BF16)
    s_new = new_key(1024)
    outs, p_news = [], []
    for kv in range(B_KV):
        s = jnp.dot(q64, buf[slot, 0, kv].astype(BF16), preferred_element_type=F32) - slope * dist_sel[kv:kv + 1]
        o, p_new = attend(jnp.where(ok_sel[kv:kv + 1], s, NEG), s_new, buf[slot, 1, kv].astype(BF16))
        outs.append(o)
        p_news.append(p_new)
    p_new = jnp.where(_iota((16, 1), 0) < B_GROUP, p_news[0], p_news[1])
    os16 = jnp.concatenate(outs, axis=1) + p_new * _bf(row[:, 1152:1280])

    lb = win_ref.shape[-1]
    dw = (lb - _iota((1, lb), 1)).astype(F32)
    s = jnp.dot(q16, _kv_t(win_ref, 0), preferred_element_type=F32) - slope * dw
    ow16, p_new = attend(s, new_key(1280), _kv_t(win_ref, 1))
    ow16 = ow16 + p_new * _bf(row[:, 1408:1536])

    sg = jnp.broadcast_to(_sigmoid(row[:, 2816:2944]), (16, 128))
    l16 = _iota((16, 128), 1)
    gate = lambda b: jnp.sum(jnp.where(l16 == r16 * 3 + b, sg, 0.0), axis=-1, keepdims=True)
    out16 = gate(0) * oc_ref[0] + gate(1) * os16 + gate(2) * ow16
    lane1 = _iota((1, 128), 1)
    mix = _place_heads([out16[h:h + 1, :] for h in range(B_HEADS)], lane1)

    cx = _sample_cross(row[:, 2304:2560], _kv_t(mkv_ref, 0), _kv_t(mkv_ref, 1))
    z_ref[0] = jnp.concatenate([mix * _silu(row[:, 1536:2304]), cx * _silu(row[:, 2560:2816])], axis=1)


def _sample_b2(page_table, sel, proj_s, oc, win_t, mem_t, e16, pages_t, li, i):
    ns, n_pages = page_table.shape
    row3 = proj_s.reshape(ns, 1, W_IN_B_PAD)
    full = lambda a: pl.BlockSpec(a.shape, lambda b, pt, sf: (0,) * a.ndim)
    per = lambda a: pl.BlockSpec((1,) + a.shape[1:], lambda b, pt, sf: (b,) + (0,) * (a.ndim - 1))
    grid_spec = pltpu.PrefetchScalarGridSpec(
        num_scalar_prefetch=2, grid=(ns,),
        in_specs=[per(row3), per(oc), per(sel), _layer_block(win_t, li), _layer_block(mem_t, i), full(e16),
                  pl.BlockSpec(memory_space=pl.ANY)],
        out_specs=pl.BlockSpec((1, 1, B_WIDTH + X_WIDTH), lambda b, pt, sf: (b, 0, 0)),
        scratch_shapes=[pltpu.VMEM((2, 2, B_KV, HEAD_DIM, SEL_TOPK * PAGE_SIZE), F32), pltpu.SemaphoreType.DMA((2,))])
    return pl.pallas_call(
        functools.partial(_sample_b2_body, li=li, n_pages=n_pages, ns=ns),
        grid_spec=grid_spec,
        out_shape=jax.ShapeDtypeStruct((ns, 1, B_WIDTH + X_WIDTH), F32),
        compiler_params=_params(1), name="sample_b2",
    )(page_table.reshape(-1), sel[:, :B_KV, :SEL_TOPK].reshape(-1), row3, oc, sel, win_t, mem_t, e16, pages_t
      ).reshape(ns, B_WIDTH + X_WIDTH)


def _importance_matrix(n_cmp, n_cols):
    c = np.arange(n_cmp)[:, None]
    j = np.arange(n_cols)[None, :]
    per = SEL_BLOCK // CMP_STRIDE
    m = ((c >= per * j) & (c <= per * j + per - 1)).astype(np.float32)
    m = m + ((c + 1 >= per * j) & (c + 1 <= per * j + per - 1)).astype(np.float32)
    m[n_cmp - 1, :] = 0.0
    return m


def _gate_expand():
    eg = np.zeros((128, 3 * B_WIDTH), np.float32)
    for h in range(B_HEADS):
        for b in range(3):
            eg[h * 3 + b, b * B_WIDTH + h * HEAD_DIM:b * B_WIDTH + (h + 1) * HEAD_DIM] = 1.0
    return eg


def _compress_weights(cmp_pos, cmp_w1, cmp_w2):
    eye = jnp.eye(B_KV, dtype=F32)
    posw = jnp.concatenate([cmp_pos, cmp_pos], axis=-1)
    w1 = cmp_w1.reshape(2, CMP_LEN, HEAD_DIM, CMP_HIDDEN)
    w1bd = jnp.einsum('tlek,jm->tljemk', w1, eye).reshape(2, CMP_LEN, 2 * HEAD_DIM, 2 * CMP_HIDDEN)
    w1cat = jnp.concatenate([w1bd[:, :CMP_STRIDE].reshape(2, CMP_STRIDE * 2 * HEAD_DIM, 2 * CMP_HIDDEN),
                             w1bd[:, CMP_STRIDE:].reshape(2, CMP_STRIDE * 2 * HEAD_DIM, 2 * CMP_HIDDEN)], axis=-1)
    pos_h = jnp.einsum('tlr,tlrh->tlh', posw, w1bd, precision=lax.Precision.HIGHEST)
    bias = jnp.concatenate([pos_h[:, :CMP_STRIDE].sum(axis=1), pos_h[:, CMP_STRIDE:].sum(axis=1)], axis=-1)
    w2bd = jnp.einsum('tke,jm->tjkme', cmp_w2, eye).reshape(2, 2 * CMP_HIDDEN, 2 * HEAD_DIM)
    return bias[:, None, :], w1cat.astype(BF16), w2bd.astype(BF16)


def _permute_w_in_b(w):
    d = w.shape[0]
    return jnp.concatenate([w[:, :1536], w[:, 1572:W_IN_B], w[:, 1536:1572],
                            jnp.zeros((d, W_IN_B_PAD - W_IN_B), w.dtype)], axis=1)


def kernel(x_prompt, x_sample, cache_mem_kv, cache_a_w128_kv, cache_a_w512_kv, cache_a_w2048_kv, cache_b_pages,
           cache_b_win_kv, page_table, mem_prompt, norm_pre, norm_post, norm_mem, w_mem_kv, w_in_a, w_out_a,
           w_in_b, w_out_b, cmp_pos, cmp_w1, cmp_w2):
    n, s_len, d = x_prompt.shape
    ns = x_sample.shape[0]
    depth = norm_pre.shape[0]
    page_size = cache_b_pages.shape[1]
    n_pages = page_table.shape[1]
    past = n_pages * page_size
    assert d == D_MODEL and x_sample.shape[1] == 1 and page_size == PAGE_SIZE
    assert ns >= 2
    assert s_len % A_SPAN == 0 and past % A_SPAN == 0
    caches_a = (cache_a_w128_kv, cache_a_w512_kv, cache_a_w2048_kv)
    for c, (win, _) in zip(caches_a, A_PATTERNS):
        assert c.shape[2] == win
    assert cache_b_win_kv.shape[2] == WIN_B

    tm = ROW_TILE
    n_cmp_p = s_len // CMP_STRIDE
    n_cmp_s = past // CMP_STRIDE
    slope_lanes = jnp.asarray(_slope_lanes(), F32)
    mt = jnp.asarray(_importance_matrix(n_cmp_p, 128).T, BF16)
    mm = jnp.asarray(_importance_matrix(n_cmp_s, 256), BF16)
    eg = jnp.asarray(_gate_expand(), BF16)
    e16 = jnp.asarray((np.arange(SEL_TOPK * PAGE_SIZE)[None, :] // PAGE_SIZE == np.arange(128)[:, None]), BF16)
    pages_t = cache_b_pages.transpose(0, 2, 3, 4, 5, 1)
    caches_a_t = [_rows_last(c) for c in caches_a]
    mem_t = _rows_last(cache_mem_kv)
    win_t = _rows_last(cache_b_win_kv)

    xp = x_prompt
    xs = x_sample.reshape(ns, d)
    mem2 = mem_prompt.reshape(n * N_MEM, d)
    mem_new = []
    a_p = [[] for _ in A_PATTERNS]
    a_s = [[] for _ in A_PATTERNS]
    b_p, b_s, bw_p, bw_s = [], [], [], []
    for i in range(depth):
        li = i // 2
        mkv_p = _rms_proj(mem2, norm_mem[i], w_mem_kv[i].astype(BF16), tm=N_MEM).reshape(n, N_MEM, 2 * X_WIDTH)
        mem_new.append(mkv_p.reshape(n, N_MEM, 2, 4, HEAD_DIM))
        if i % 2 == 0:
            w_in = w_in_a[li].astype(BF16)
            w_out = w_out_a[li].astype(BF16)
            max_win = max(win for win, _ in A_PATTERNS)
            proj_p, kv_tail = _rms_proj(xp.reshape(n * s_len, d), norm_pre[i], w_in, tm=tm, seq_len=s_len,
                                        sides=((768, 1536, F32, max_win),))
            proj_p = proj_p.reshape(n, s_len, W_IN_A)
            kv_tail = kv_tail.reshape(n, max_win, 2, 3, 4, HEAD_DIM)
            proj_s = _rms_proj(xs, norm_pre[i], w_in, tm=ns)
            ols = []
            for g, (win, dil) in enumerate(A_PATTERNS):
                ols += _a_prompt_group(proj_p, g, dil)
                a_p[g].append(kv_tail[:, max_win - win:, :, g])
                a_s[g].append(proj_s[:, 768:2304].reshape(ns, 1, 2, 3, 4, HEAD_DIM)[:, :, :, g])
            xp = _finish_a(xp, ols, proj_p, mkv_p, w_out, norm_post[i], tm=tm)
            z = _sample_a(proj_s, caches_a_t, mem_t, li, i)
            xs = _tail(xs, z, w_out, norm_post[i])
        else:
            w_in = _permute_w_in_b(w_in_b[li]).astype(BF16)
            w_out = w_out_b[li].astype(BF16)
            posw, w1bd, w2bd = _compress_weights(cmp_pos[li], cmp_w1[li], cmp_w2[li])
            proj_p, kvs, kv_rows = _rms_proj(xp.reshape(n * s_len, d), norm_pre[i], w_in, tm=tm, seq_len=s_len,
                                             sides=((1024, 512, BF16, s_len), (768, 768, F32, s_len)))
            proj_p = proj_p.reshape(n, s_len, W_IN_B_PAD)
            kvs = kvs.reshape(n, s_len, 512)
            proj_s = _rms_proj(xs, norm_pre[i], w_in, tm=ns)
            cmpd = _compress_prompt(proj_p, posw, w1bd, w2bd)
            pos = jnp.arange(s_len, dtype=jnp.int32)
            cend = CMP_STRIDE * jnp.arange(n_cmp_p, dtype=jnp.int32) + (CMP_LEN - 1)
            mix = _nsa_prompt(proj_p, _keys_with_pos(cmpd[:, :, 0:128], cend), cmpd[:, :, 128:256].transpose(0, 2, 1),
                              _keys_with_pos(kvs[:, :, 256:384], pos), _values_by_group(kvs[:, :, 384:512]),
                              _keys_with_pos(kvs[:, :, 0:128], pos), _values_by_group(kvs[:, :, 128:256]),
                              mt, eg, slope_lanes)
            xp = _finish_b(xp, mix, proj_p, mkv_p, w_out, norm_post[i], tm=tm)
            oc, sel = _sample_b1(page_table, proj_s, posw, w1bd, w2bd, mm, pages_t, li)
            z = _sample_b2(page_table, sel, proj_s, oc, win_t, mem_t, e16, pages_t, li, i)
            xs = _tail(xs, z, w_out, norm_post[i])
            kv_rows = kv_rows.reshape(n, s_len, 6, B_KV, HEAD_DIM)
            b_p.append(kv_rows[:, :, :4])
            bw_p.append(kv_rows[:, s_len - WIN_B:, 4:])
            b_s.append(proj_s[:, 768:1280].reshape(ns, 1, 4, B_KV, HEAD_DIM))
            bw_s.append(proj_s[:, 1280:1536].reshape(ns, 1, 2, B_KV, HEAD_DIM))
    return (xp, xs.reshape(ns, 1, d), jnp.stack(mem_new, axis=0),
            jnp.stack(a_p[0], axis=0), jnp.stack(a_p[1], axis=0), jnp.stack(a_p[2], axis=0),
            jnp.stack(b_p, axis=2), jnp.stack(bw_p, axis=0),
            jnp.stack(a_s[0], axis=0), jnp.stack(a_s[1], axis=0), jnp.stack(a_s[2], axis=0),
            jnp.stack(b_s, axis=2), jnp.stack(bw_s, axis=0))
```

```python
import functools

import numpy as np
import jax
import jax.numpy as jnp
from jax import lax
from jax.experimental import pallas as pl
from jax.experimental.pallas import tpu as pltpu

F32 = jnp.float32
BF16 = jnp.bfloat16

D_MODEL = 1024
HEAD_DIM = 64
SCALE = HEAD_DIM ** -0.5
LOG2E = 1.4426950408889634
RMS_EPS = 1e-6
N_MEM = 256
X_WIDTH = 256
A_PATTERNS = ((128, 1), (512, 4), (2048, 16))
A_WIDTH = 256
W_IN_A = 3072
B_HEADS = 12
B_KV = 2
B_GROUP = 6
B_WIDTH = 768
W_IN_B = 2852
W_IN_B_PAD = 2944
CMP_LEN = 32
CMP_STRIDE = 16
CMP_HIDDEN = 128
SEL_BLOCK = 64
SEL_TOPK = 16
WIN_B = 512
Q_BLOCK = 128
FORCE_SCORE = 1e4
PAGE_SIZE = 128
NEG = -1e30
GROUPS_PER_CHUNK = 4
V7X_VMEM_BYTES = 64 * 1024 * 1024
VMEM_LIMIT = V7X_VMEM_BYTES * 7 // 8
ROW_TILE = 512

NT = (((1,), (1,)), ((), ()))


def _alibi(n):
    k = np.arange(1, n + 1, dtype=np.float32)
    return [float(v) for v in np.float32(2.0) ** (np.float32(-8.0) * k / np.float32(n))]


SLOPES_A = _alibi(12)
SLOPES_B = _alibi(12)


def _params(n_axes):
    return pltpu.CompilerParams(dimension_semantics=("arbitrary",) * n_axes, vmem_limit_bytes=VMEM_LIMIT)


def _sigmoid(x):
    return 1.0 / (1.0 + jnp.exp(-x))


def _silu(x):
    return x * _sigmoid(x)


def _iota(shape, dim):
    return lax.broadcasted_iota(jnp.int32, shape, dim)


def _split3(x):
    hi = x.astype(BF16)
    r1 = x - hi.astype(F32)
    mid = r1.astype(BF16)
    lo = (r1 - mid.astype(F32)).astype(BF16)
    return hi, mid, lo


def _rms_proj_body(x_ref, g_ref, w_ref, o_ref, *side_ref, side):
    x = x_ref[...]
    y = x * lax.rsqrt(jnp.mean(x * x, axis=-1, keepdims=True) + RMS_EPS)
    y = (y * g_ref[...]).astype(BF16)
    o = jnp.dot(y, w_ref[...], preferred_element_type=F32)
    o_ref[...] = o
    if side is not None:
        side_ref[0][...] = o[:, side[0]:side[0] + side[1]].astype(BF16)


def _rms_proj(x, g, w, tm, side=None):
    m, d = x.shape
    n = w.shape[1]
    out_specs = [pl.BlockSpec((tm, n), lambda i: (i, 0))]
    out_shape = [jax.ShapeDtypeStruct((m, n), F32)]
    if side is not None:
        out_specs.append(pl.BlockSpec((tm, side[1]), lambda i: (i, 0)))
        out_shape.append(jax.ShapeDtypeStruct((m, side[1]), BF16))
    outs = pl.pallas_call(
        functools.partial(_rms_proj_body, side=side),
        grid=(m // tm,),
        in_specs=[pl.BlockSpec((tm, d), lambda i: (i, 0)),
                  pl.BlockSpec((1, d), lambda i: (0, 0)),
                  pl.BlockSpec((d, n), lambda i: (0, 0))],
        out_specs=out_specs,
        out_shape=out_shape,
        compiler_params=_params(1),
        name="rms_proj",
    )(x, g.reshape(1, d), w)
    return outs[0] if side is None else outs


A_SPAN = 2048


def _a_prompt_body(*refs, dil, slopes):
    ins, outs = refs[:10], refs[10:]
    t = pl.program_id(1)
    blk_rows = 128 * dil
    n_ub = A_SPAN // blk_rows
    i = _iota((128, 256), 0)
    j = _iota((128, 256), 1)
    back = 128 + i - j
    in_band = (back >= 0) & (back <= 128)
    dist = (back * dil).astype(F32)
    lane = _iota((128, 128), 1)

    def rows(ref, start):
        return ref[0, pl.ds(start, 128, stride=dil), :] if dil > 1 else ref[0, pl.ds(start, 128), :]

    def block(pair, ub, r, first):
        q_ref, k_ref, kp_ref, v_ref, vp_ref = ins[pair * 5:(pair + 1) * 5]
        o_ref, l_ref = outs[pair * 2:(pair + 1) * 2]
        start = ub * blk_rows + r
        if first:
            k_prev, v_prev = rows(kp_ref, r), rows(vp_ref, r)
            valid = in_band & (j >= jnp.where(t > 0, 0, 128))
        else:
            k_prev, v_prev = rows(k_ref, start - blk_rows), rows(v_ref, start - blk_rows)
            valid = in_band
        qp = rows(q_ref, start)
        kp = jnp.concatenate([k_prev, rows(k_ref, start)], axis=0).astype(BF16)
        vp = jnp.concatenate([v_prev, rows(v_ref, start)], axis=0).astype(BF16)
        o_pair = None
        l_pair = None
        for hh in range(2):
            hm = (lane < 64) if hh == 0 else (lane >= 64)
            qm = jnp.where(hm, qp, 0.0).astype(BF16)
            s = lax.dot_general(qm, kp, NT, preferred_element_type=F32) * SCALE
            s = jnp.where(valid, s - slopes[pair * 2 + hh] * dist, NEG)
            m = jnp.max(s, axis=-1, keepdims=True)
            e = jnp.exp(s - m)
            den = jnp.sum(e, axis=-1, keepdims=True)
            oh = jnp.dot((e * (1.0 / den)).astype(BF16), vp, preferred_element_type=F32)
            lh = jnp.broadcast_to(m + jnp.log(den), (128, 128))
            o_pair = oh if hh == 0 else jnp.where(lane < 64, o_pair, oh)
            l_pair = lh if hh == 0 else jnp.where(lane < 64, l_pair, lh)
        if dil > 1:
            o_ref[0, pl.ds(start, 128, stride=dil), :] = o_pair
            l_ref[0, pl.ds(start, 128, stride=dil), :] = l_pair
        else:
            o_ref[0, pl.ds(start, 128), :] = o_pair
            l_ref[0, pl.ds(start, 128), :] = l_pair

    def run(count, fn):
        if count == 0:
            return
        unroll = next(c for c in (8, 6, 5, 4, 3, 2, 1) if count % c == 0)

        def body(it, c):
            for k in range(unroll):
                fn(it * unroll + k)
            return c
        lax.fori_loop(0, count // unroll, body, 0)

    for pair in range(2):
        run(dil, lambda r, pair=pair: block(pair, 0, r, True))
        run((n_ub - 1) * dil, lambda idx, pair=pair: block(pair, 1 + idx // dil, idx % dil, False))


def _a_prompt_group(proj, g, dil):
    n, s_len, _ = proj.shape
    blk_rows = 128 * dil
    per_span = A_SPAN // blk_rows
    body = functools.partial(_a_prompt_body, dil=dil, slopes=tuple(SLOPES_A[g * 4:(g + 1) * 4]))
    cur = lambda col: pl.BlockSpec((1, A_SPAN, 128), lambda b, t: (b, t, col))
    prev = lambda col: pl.BlockSpec((1, blk_rows, 128), lambda b, t: (b, jnp.maximum(t * per_span - 1, 0), col))
    in_specs = []
    for pair in range(2):
        qc, kc, vc = 2 * g + pair, 6 + 2 * g + pair, 12 + 2 * g + pair
        in_specs += [cur(qc), cur(kc), prev(kc), cur(vc), prev(vc)]
    out_spec = pl.BlockSpec((1, A_SPAN, 128), lambda b, t: (b, t, 0))
    return pl.pallas_call(
        body,
        grid=(n, s_len // A_SPAN),
        in_specs=in_specs,
        out_specs=[out_spec] * 4,
        out_shape=[jax.ShapeDtypeStruct((n, s_len, 128), F32)] * 4,
        compiler_params=_params(2),
        name=f"a_prompt_g{g}",
    )(*([proj] * 10))


def _cross_rows(qx, kx, vx):
    tm = qx.shape[0]
    lane = _iota((tm, 128), 1)
    outs = []
    for pair in range(2):
        sl = slice(pair * 128, (pair + 1) * 128)
        qp, kp, vp = qx[:, sl], kx[:, sl], vx[:, sl]
        o_pair = None
        for hh in range(2):
            hm = (lane < 64) if hh == 0 else (lane >= 64)
            qm = jnp.where(hm, qp, 0.0).astype(BF16)
            s = lax.dot_general(qm, kp, NT, preferred_element_type=F32) * SCALE
            m = jnp.max(s, axis=-1, keepdims=True)
            e = jnp.exp(s - m)
            p = (e / jnp.sum(e, axis=-1, keepdims=True)).astype(BF16)
            oh = jnp.dot(p, vp, preferred_element_type=F32)
            o_pair = oh if hh == 0 else jnp.where(lane < 64, o_pair, oh)
        outs.append(o_pair)
    return jnp.concatenate(outs, axis=1)


def _out_norm_residual(x, z, w, g):
    y = jnp.dot(z.astype(BF16), w, preferred_element_type=F32)
    y = y * lax.rsqrt(jnp.mean(y * y, axis=-1, keepdims=True) + RMS_EPS)
    return x + y * g


def _finish_a_body(x_ref, *refs):
    gm_ref, qx_ref, gx_ref, mkv_ref, w_ref, g_ref, out_ref = refs[12:]
    mixes = []
    for pair in range(2):
        os_ = [refs[4 * g + 2 * pair][0] for g in range(3)]
        ls_ = [refs[4 * g + 2 * pair + 1][0] for g in range(3)]
        m = jnp.maximum(jnp.maximum(ls_[0], ls_[1]), ls_[2])
        es = [jnp.exp(l - m) for l in ls_]
        mixes.append((es[0] * os_[0] + es[1] * os_[1] + es[2] * os_[2]) / (es[0] + es[1] + es[2]))
    mix = jnp.concatenate(mixes, axis=1)
    mkv = mkv_ref[0]
    cx = _cross_rows(qx_ref[0], mkv[:, :X_WIDTH].astype(BF16), mkv[:, X_WIDTH:].astype(BF16))
    z = jnp.concatenate([mix * _silu(gm_ref[0]), cx * _silu(gx_ref[0])], axis=1)
    out_ref[0] = _out_norm_residual(x_ref[0], z, w_ref[...], g_ref[...])


def _finish_b_body(x_ref, mix_ref, gm_ref, qx_ref, gx_ref, mkv_ref, w_ref, g_ref, out_ref):
    mkv = mkv_ref[0]
    cx = _cross_rows(qx_ref[0], mkv[:, :X_WIDTH].astype(BF16), mkv[:, X_WIDTH:].astype(BF16))
    z = jnp.concatenate([mix_ref[0] * _silu(gm_ref[0]), cx * _silu(gx_ref[0])], axis=1)
    out_ref[0] = _out_norm_residual(x_ref[0], z, w_ref[...], g_ref[...])


def _finish_a(x, ols, proj, mkv, w_out, g_post, tm):
    n, s_len, d = x.shape
    row = lambda w, c: pl.BlockSpec((1, tm, w), lambda b, t: (b, t, c))
    in_specs = ([row(d, 0)] + [row(128, 0)] * 12 + [row(256, 9), row(256, 10), row(256, 11)]
                + [pl.BlockSpec((1, N_MEM, 2 * X_WIDTH), lambda b, t: (b, 0, 0)),
                   pl.BlockSpec(w_out.shape, lambda b, t: (0, 0)),
                   pl.BlockSpec((1, d), lambda b, t: (0, 0))])
    return pl.pallas_call(
        _finish_a_body, grid=(n, s_len // tm), in_specs=in_specs, out_specs=row(d, 0),
        out_shape=jax.ShapeDtypeStruct((n, s_len, d), F32), compiler_params=_params(2), name="finish_a",
    )(x, *ols, proj, proj, proj, mkv, w_out, g_post.reshape(1, d))


def _finish_b(x, mix, proj, mkv, w_out, g_post, tm):
    n, s_len, d = x.shape
    row = lambda w, c: pl.BlockSpec((1, tm, w), lambda b, t: (b, t, c))
    in_specs = [row(d, 0), row(B_WIDTH, 0), row(B_WIDTH, 2), row(256, 9), row(256, 10),
                pl.BlockSpec((1, N_MEM, 2 * X_WIDTH), lambda b, t: (b, 0, 0)),
                pl.BlockSpec(w_out.shape, lambda b, t: (0, 0)),
                pl.BlockSpec((1, d), lambda b, t: (0, 0))]
    return pl.pallas_call(
        _finish_b_body, grid=(n, s_len // tm), in_specs=in_specs, out_specs=row(d, 0),
        out_shape=jax.ShapeDtypeStruct((n, s_len, d), F32), compiler_params=_params(2), name="finish_b",
    )(x, mix, proj, proj, proj, mkv, w_out, g_post.reshape(1, d))


def _tail_body(x_ref, z_ref, w_ref, g_ref, out_ref):
    out_ref[...] = _out_norm_residual(x_ref[...], z_ref[...], w_ref[...], g_ref[...])


def _tail(x, z, w_out, g_post):
    m, d = x.shape
    full = lambda a: pl.BlockSpec(a.shape, lambda i: (0,) * a.ndim)
    g2 = g_post.reshape(1, d)
    return pl.pallas_call(
        _tail_body, grid=(1,), in_specs=[full(x), full(z), full(w_out), full(g2)], out_specs=full(x),
        out_shape=jax.ShapeDtypeStruct((m, d), F32), compiler_params=_params(1), name="sample_tail",
    )(x, z, w_out, g2)


def _compress_rows(load_rows, pos_ref, w1_ref, w2_ref, n_cmp, between=None):
    outs = []
    half = 2 * CMP_HIDDEN
    for t in range(2):
        y = jnp.concatenate([load_rows(t, l).astype(BF16) for l in range(CMP_STRIDE)], axis=1)
        ab = jnp.dot(y, w1_ref[t], preferred_element_type=F32) + pos_ref[t]
        h = ab[:, :half] + pltpu.roll(ab[:, half:], n_cmp - 1, axis=0)
        outs.append(jnp.dot(_silu(h).astype(BF16), w2_ref[t], preferred_element_type=F32))
        if between is not None:
            between(t)
    return outs


def _compress_body(k_ref, v_ref, pos_ref, w1_ref, w2_ref, o_ref, *, n_cmp):
    refs = (k_ref, v_ref)
    load = lambda t, l: refs[t][0, pl.ds(l, n_cmp, stride=CMP_STRIDE), :]
    ck, cv = _compress_rows(load, pos_ref, w1_ref, w2_ref, n_cmp)
    o_ref[0, :, 0:128] = ck.astype(BF16)
    o_ref[0, :, 128:256] = cv.astype(BF16)


def _compress_prompt(proj, posw, w1bd, w2bd):
    n, s_len, _ = proj.shape
    n_cmp = s_len // CMP_STRIDE
    full = lambda a: pl.BlockSpec(a.shape, lambda b: (0,) * a.ndim)
    return pl.pallas_call(
        functools.partial(_compress_body, n_cmp=n_cmp), grid=(n,),
        in_specs=[pl.BlockSpec((1, s_len, 128), lambda b: (b, 0, 6)), pl.BlockSpec((1, s_len, 128), lambda b: (b, 0, 7)),
                  full(posw), full(w1bd), full(w2bd)],
        out_specs=pl.BlockSpec((1, n_cmp, 256), lambda b: (b, 0, 0)),
        out_shape=jax.ShapeDtypeStruct((n, n_cmp, 256), BF16), compiler_params=_params(1), name="compress_prompt",
    )(proj, proj, posw, w1bd, w2bd)


def _place_heads(tiles, lane):
    chunks = []
    for c in range(B_HEADS // 2):
        t0, t1 = tiles[2 * c], tiles[2 * c + 1]
        if (2 * c) // B_GROUP == 1:
            t0 = pltpu.roll(t0, 64, axis=1)
        if (2 * c + 1) // B_GROUP == 0:
            t1 = pltpu.roll(t1, 64, axis=1)
        chunks.append(jnp.where(lane < 64, t0, t1))
    return jnp.concatenate(chunks, axis=1)


def _masked_softmax_rows(s, ok):
    s = jnp.where(ok, s, NEG)
    m = jnp.max(s, axis=-1, keepdims=True)
    e = jnp.where(ok, jnp.exp(s - m), 0.0)
    den = jnp.maximum(jnp.sum(e, axis=-1, keepdims=True), 1e-30)
    return e * (1.0 / den)


ONES_ROWS = 16


def _values_and_ones(v_ref, idx, kv):
    v = v_ref[idx + (slice(kv * HEAD_DIM, (kv + 1) * HEAD_DIM), slice(None))]
    return jnp.concatenate([v, jnp.ones((ONES_ROWS, v.shape[1]), BF16)], axis=0)


def _head_tile(num, den, kv):
    x = num if den is None else num * (1.0 / den)
    z = jnp.zeros_like(x)
    return jnp.concatenate([x, z] if kv == 0 else [z, x], axis=0).T


def _nsa_prompt_body(q_ref, gt_ref, kc_ref, vct_ref, kw_ref, vwt_ref, ks_ref, vst_ref, mt_ref, eg_ref,
                     sl_ref, out_ref, q6_sc, m_sc, acc_sc, sel_sc, words_sm, idx_sm, *, s_len):
    qb = pl.program_id(1)
    qstart = qb * Q_BLOCK
    n_cmp = s_len // CMP_STRIDE
    q = q_ref[0] * (SCALE * LOG2E)
    lane = _iota((Q_BLOCK, 128), 1)
    tq_row = qstart + _iota((1, Q_BLOCK), 1)
    oc_t, os_t, ow_t = [None] * B_HEADS, [None] * B_HEADS, [None] * B_HEADS

    sub = _iota((128, 128), 0)
    psums = []

    for kv in range(B_KV):
        for g in range(B_GROUP):
            h = kv * B_GROUP + g
            ch = q[:, (h // 2) * 128:(h // 2 + 1) * 128]
            if h % 2 == 1:
                ch = pltpu.roll(ch, 64, axis=1)
            q6_sc[kv, g * 128:(g + 1) * 128, :] = jnp.where(lane < 64, ch, sl_ref[h:h + 1, :]).astype(BF16)

    cok = (CMP_STRIDE * _iota((n_cmp, Q_BLOCK), 0) + (CMP_LEN - 1)) <= tq_row
    q_ok = tq_row >= (CMP_LEN - 1)
    n_wb = WIN_B // Q_BLOCK + 1

    def cmp_scores(kv):
        return lax.dot_general(kc_ref[0, kv], q6_sc[kv], NT, preferred_element_type=F32)

    def cmp_finish(kv, s_t):
        psum = jnp.zeros((n_cmp, Q_BLOCK), F32)
        ps = []
        for g in range(B_GROUP):
            s = jnp.where(cok, s_t[:, g * 128:(g + 1) * 128], NEG)
            e = jnp.exp2(s - jnp.max(s, axis=0, keepdims=True))
            p = e * jnp.where(q_ok, 1.0 / jnp.sum(e, axis=0, keepdims=True), 0.0)
            psum = psum + p
            ps.append(p.astype(BF16))
        oc = jnp.dot(vct_ref[0, kv * HEAD_DIM:(kv + 1) * HEAD_DIM, :], jnp.concatenate(ps, axis=1),
                     preferred_element_type=F32)
        for g in range(B_GROUP):
            oc_t[kv * B_GROUP + g] = _head_tile(oc[:, g * 128:(g + 1) * 128], None, kv)
        psums.append(psum)

    def win_scores(kv):
        kparts, vparts, pparts = [], [], []
        for wb in range(n_wb):
            b_raw = qb - (n_wb - 1) + wb
            b = jnp.maximum(b_raw, 0)
            r0 = pl.multiple_of(b * Q_BLOCK, Q_BLOCK)
            kparts.append(kw_ref[0, kv, pl.ds(r0, Q_BLOCK), :])
            vparts.append(_values_and_ones(vwt_ref, (0, b), kv))
            pparts.append(jnp.where(b_raw >= 0, r0, s_len) + _iota((128, Q_BLOCK), 0))
        s_t = lax.dot_general(jnp.concatenate(kparts, axis=0), q6_sc[kv], NT, preferred_element_type=F32)
        return s_t, jnp.concatenate(vparts, axis=1), jnp.concatenate(pparts, axis=0)

    def win_finish(kv, s_t, v_t, kpos):
        dw = tq_row - kpos
        wok = (dw >= 0) & (dw <= WIN_B)
        ps = []
        for g in range(B_GROUP):
            s = jnp.where(wok, s_t[:, g * 128:(g + 1) * 128], NEG)
            ps.append(jnp.exp2(s - jnp.max(s, axis=0, keepdims=True)).astype(BF16))
        ow = jnp.dot(v_t, jnp.concatenate(ps, axis=1), preferred_element_type=F32)
        for g in range(B_GROUP):
            cols = slice(g * 128, (g + 1) * 128)
            ow_t[kv * B_GROUP + g] = _head_tile(ow[0:HEAD_DIM, cols], ow[HEAD_DIM:HEAD_DIM + 1, cols], kv)

    sc0 = cmp_scores(0)
    sw0 = win_scores(0)
    cmp_finish(0, sc0)
    sc1 = cmp_scores(1)
    win_finish(0, *sw0)
    sw1 = win_scores(1)
    cmp_finish(1, sc1)
    win_finish(1, *sw1)

    mt = mt_ref[...]
    blk = _iota((128, Q_BLOCK), 0)
    ql = _iota((128, Q_BLOCK), 1)
    cur = jnp.where(ql >= SEL_BLOCK, qb * 2 + 1, qb * 2)
    forced = (blk == 0) | (blk == cur) | (blk == cur - 1)
    blkf = blk.astype(F32)
    imps = [jnp.where((blk > cur) | forced, -jnp.inf,
                      sum(jnp.dot(mt, t, preferred_element_type=F32) for t in _split3(psum))) for psum in psums]
    sels = [jnp.where(forced, 1.0, 0.0)] * B_KV
    for _ in range(SEL_TOPK - 3):
        for kv in range(B_KV):
            mx = jnp.max(imps[kv], axis=0, keepdims=True)
            idx = jnp.min(jnp.where(imps[kv] == mx, blkf, 1e9), axis=0, keepdims=True)
            hit = blkf == idx
            sels[kv] = jnp.where(hit, 1.0, sels[kv])
            imps[kv] = jnp.where(hit, -jnp.inf, imps[kv])

    blk_col = _iota((128, 1), 0)
    weight = lax.shift_left(jnp.ones((128, 1), jnp.int32), blk_col & 15).astype(F32)
    for kv in range(B_KV):
        sel_sc[kv] = sels[kv]
        contrib = jnp.max(sels[kv], axis=1, keepdims=True) * weight
        for w in range(8):
            words_sm[kv * 8 + w] = jnp.sum(contrib[16 * w:16 * (w + 1), :]).astype(jnp.int32)

    m_sc[...] = jnp.full(m_sc.shape, NEG, F32)
    acc_sc[...] = jnp.zeros(acc_sc.shape, F32)
    list_len = idx_sm.shape[0] // B_KV
    cnts = []
    n_grp = s_len // Q_BLOCK
    for kv in range(B_KV):
        cnt = jnp.int32(0)
        for w in range((n_grp + 7) // 8):
            word = words_sm[kv * 8 + w]
            for j in range(min(8, n_grp - 8 * w)):
                idx_sm[kv * list_len + cnt] = 8 * w + j
                cnt = cnt + jnp.where(((word >> (2 * j)) & 3) != 0, 1, 0)
        cnts.append(cnt)
    n_chunks = (jnp.maximum(cnts[0], cnts[1]) + GROUPS_PER_CHUNK - 1) // GROUPS_PER_CHUNK
    for kv in range(B_KV):
        def pad(i, c, kv=kv):
            idx_sm[kv * list_len + i] = -1
            return c
        lax.fori_loop(cnts[kv], n_chunks * GROUPS_PER_CHUNK, pad, 0)

    sub8 = _iota((8, Q_BLOCK), 0)

    def group_hits(kv, gi):
        rows8 = sel_sc[kv, pl.ds(pl.multiple_of((gi >> 2) * 8, 8), 8), :]
        r = (gi & 3) * 2
        lo = jnp.sum(jnp.where(sub8 == r, rows8, 0.0), axis=0, keepdims=True)
        hi = jnp.sum(jnp.where(sub8 == r + 1, rows8, 0.0), axis=0, keepdims=True)
        return jnp.where(sub < SEL_BLOCK, lo, hi)

    def chunk_scores(c, kv):
        kts, vts, hits, kposs = [], [], [], []
        for j in range(GROUPS_PER_CHUNK):
            gi_raw = idx_sm[kv * list_len + c * GROUPS_PER_CHUNK + j]
            gi = jnp.maximum(gi_raw, 0)
            k0 = pl.multiple_of(gi * Q_BLOCK, Q_BLOCK)
            kts.append(ks_ref[0, kv, pl.ds(k0, Q_BLOCK), :])
            vts.append(_values_and_ones(vst_ref, (0, gi), kv))
            hits.append(group_hits(kv, gi))
            kposs.append(jnp.where(gi_raw >= 0, k0, s_len) + _iota((128, 128), 0))
        s_t = lax.dot_general(jnp.concatenate(kts, axis=0), q6_sc[kv], NT, preferred_element_type=F32)
        ok = ((jnp.concatenate(hits, axis=0) > 0.5)
              & (jnp.concatenate(kposs, axis=0) <= qstart + _iota((GROUPS_PER_CHUNK * 128, 128), 1)))
        return s_t, ok, jnp.concatenate(vts, axis=1)

    def chunk_update(kv, s_t, ok, v_t):
        m_old = m_sc[kv]
        m_new, ps = [], []
        for g in range(B_GROUP):
            cols = slice(g * 128, (g + 1) * 128)
            s = jnp.where(ok, s_t[:, cols], NEG)
            mg = jnp.maximum(m_old[:, cols], jnp.max(s, axis=0, keepdims=True))
            ps.append(jnp.exp2(s - mg).astype(BF16))
            m_new.append(mg)
        m_new = jnp.concatenate(m_new, axis=1)
        pv = jnp.dot(v_t, jnp.concatenate(ps, axis=1), preferred_element_type=F32)
        acc_sc[kv] = jnp.exp2(m_old - m_new) * acc_sc[kv] + pv
        m_sc[kv] = m_new

    def chunk(c, carry):
        first = chunk_scores(c, 0)
        second = chunk_scores(c, 1)
        chunk_update(0, *first)
        chunk_update(1, *second)
        return carry

    lax.fori_loop(0, n_chunks, chunk, 0)
    for kv in range(B_KV):
        for g in range(B_GROUP):
            cols = slice(g * 128, (g + 1) * 128)
            os_t[kv * B_GROUP + g] = _head_tile(acc_sc[kv, 0:HEAD_DIM, cols], acc_sc[kv, HEAD_DIM:HEAD_DIM + 1, cols], kv)

    sg = _sigmoid(gt_ref[0])
    eg = eg_ref[...]
    gexp = sum(jnp.dot(t, eg, preferred_element_type=F32) for t in _split3(sg))
    out_ref[0] = (gexp[:, 0:B_WIDTH] * _place_heads(oc_t, lane)
                  + gexp[:, B_WIDTH:2 * B_WIDTH] * _place_heads(os_t, lane)
                  + gexp[:, 2 * B_WIDTH:] * _place_heads(ow_t, lane))


def _nsa_prompt(proj, kc, vc_t, kw, vw_t, ks, vs_t, mt, eg, slope_lanes):
    n, s_len, _ = proj.shape
    n_cmp = s_len // CMP_STRIDE
    n_grp = s_len // Q_BLOCK
    full = lambda a: pl.BlockSpec(a.shape, lambda b, t: (0,) * a.ndim)
    per_n = lambda a: pl.BlockSpec((1,) + a.shape[1:], lambda b, t: (b,) + (0,) * (a.ndim - 1))
    return pl.pallas_call(
        functools.partial(_nsa_prompt_body, s_len=s_len),
        grid=(n, n_grp),
        in_specs=[pl.BlockSpec((1, Q_BLOCK, B_WIDTH), lambda b, t: (b, t, 0)),
                  pl.BlockSpec((1, Q_BLOCK, 128), lambda b, t: (b, t, 22)),
                  per_n(kc), per_n(vc_t), per_n(kw), per_n(vw_t), per_n(ks), per_n(vs_t),
                  full(mt), full(eg), full(slope_lanes)],
        out_specs=pl.BlockSpec((1, Q_BLOCK, B_WIDTH), lambda b, t: (b, t, 0)),
        out_shape=jax.ShapeDtypeStruct((n, s_len, B_WIDTH), F32),
        scratch_shapes=[pltpu.VMEM((B_KV, B_GROUP * Q_BLOCK, 128), BF16), pltpu.VMEM((B_KV, 1, B_GROUP * Q_BLOCK), F32),
                        pltpu.VMEM((B_KV, HEAD_DIM + ONES_ROWS, B_GROUP * Q_BLOCK), F32),
                        pltpu.VMEM((B_KV, 128, Q_BLOCK), F32), pltpu.SMEM((B_KV * 8,), jnp.int32),
                        pltpu.SMEM((B_KV * (n_grp + GROUPS_PER_CHUNK),), jnp.int32)],
        compiler_params=_params(2), name="nsa_prompt",
    )(proj, proj, kc, vc_t, kw, vw_t, ks, vs_t, mt, eg, slope_lanes)


POS_LANE = 64


def _slope_lanes():
    out = np.zeros((B_HEADS, 128), np.float32)
    for h, slope in enumerate(SLOPES_B):
        s = np.float32(np.float64(slope) * LOG2E)
        hi = np.float32(np.asarray(s, np.float32).astype(jnp.bfloat16))
        mid = np.float32(np.asarray(np.float32(s) - hi, np.float32).astype(jnp.bfloat16))
        lo = np.float32(np.asarray(np.float32(s) - hi - mid, np.float32).astype(jnp.bfloat16))
        out[h, POS_LANE:POS_LANE + 6] = [hi, mid, lo, hi, mid, lo]
    return out


def _keys_with_pos(k2, pos):
    n, n_keys, _ = k2.shape
    lo = (pos % Q_BLOCK).astype(BF16)[None, :, None]
    hi = (pos - pos % Q_BLOCK).astype(BF16)[None, :, None]
    tail = jnp.concatenate([jnp.broadcast_to(lo, (n, n_keys, 3)), jnp.broadcast_to(hi, (n, n_keys, 3)),
                            jnp.zeros((n, n_keys, 128 - POS_LANE - 6), BF16)], axis=-1)
    return jnp.stack([jnp.concatenate([k2[..., kv * 64:(kv + 1) * 64], tail], axis=-1) for kv in range(B_KV)], axis=1)


def _values_by_group(v2):
    n, s_len, _ = v2.shape
    return v2.reshape(n, s_len // Q_BLOCK, Q_BLOCK, 128).transpose(0, 1, 3, 2)


def _heads_rows(vec, n_rows, width):
    r = _iota((n_rows, width), 0)
    l = _iota((n_rows, width), 1)
    hm = (l >= r * HEAD_DIM) & (l < r * HEAD_DIM + HEAD_DIM)
    return jnp.where(hm, jnp.broadcast_to(vec, (n_rows, width)), 0.0), hm


def _bf(x):
    return x.astype(BF16).astype(F32)


def _row_consts(n_rows, vals):
    r = _iota((n_rows, 1), 0)
    out = jnp.zeros((n_rows, 1), F32)
    for i, v in enumerate(vals):
        out = jnp.where(r == i, v, out)
    return out


def _rows_last(cache):
    nd = cache.ndim
    return cache.transpose(tuple(range(nd - 4)) + (nd - 3, nd - 2, nd - 1, nd - 4))


def _kv_t(ref, t):
    x = ref[0, 0, t]
    return x.reshape(x.shape[0] * x.shape[1], x.shape[2]).astype(BF16)


def _sample_cross(qx_row, k_t, v_t):
    q8, hm = _heads_rows(qx_row, 8, X_WIDTH)
    s = jnp.dot(q8.astype(BF16), k_t, preferred_element_type=F32) * SCALE
    e = jnp.exp(s - jnp.max(s, axis=-1, keepdims=True))
    p = (e * (1.0 / jnp.sum(e, axis=-1, keepdims=True))).astype(BF16)
    o8 = lax.dot_general(p, v_t, NT, preferred_element_type=F32)
    return jnp.sum(jnp.where(hm, o8, 0.0), axis=0, keepdims=True)


def _sample_a_body(row_ref, c0_ref, c1_ref, c2_ref, mkv_ref, z_ref):
    row = row_ref[0]
    outs, lses = [], []
    hm = None
    for g, (win, dil) in enumerate(A_PATTERNS):
        cref = (c0_ref, c1_ref, c2_ref)[g]
        q8, hm = _heads_rows(row[:, g * 256:(g + 1) * 256], 8, A_WIDTH)
        knew = row[:, 768 + g * 256:768 + (g + 1) * 256]
        vnew = row[:, 1536 + g * 256:1536 + (g + 1) * 256]
        q8b = q8.astype(BF16)
        slope = _row_consts(8, SLOPES_A[g * 4:(g + 1) * 4])
        s = jnp.dot(q8b, _kv_t(cref, 0), preferred_element_type=F32) * SCALE
        r = _iota((8, win), 1)
        s = jnp.where((r & (dil - 1)) == 0, s - slope * (win - r).astype(F32), NEG)
        s_new = jnp.sum(q8b.astype(F32) * _bf(knew), axis=-1, keepdims=True) * SCALE
        m = jnp.maximum(jnp.max(s, axis=-1, keepdims=True), s_new)
        e = jnp.exp(s - m)
        e_new = jnp.exp(s_new - m)
        den = jnp.sum(e, axis=-1, keepdims=True) + e_new
        inv = 1.0 / den
        o8 = (lax.dot_general((e * inv).astype(BF16), _kv_t(cref, 1), NT, preferred_element_type=F32)
              + _bf(e_new * inv) * _bf(vnew))
        outs.append(o8)
        lses.append(m + jnp.log(den))
    mx = jnp.maximum(jnp.maximum(lses[0], lses[1]), lses[2])
    ws = [jnp.exp(l - mx) for l in lses]
    mix8 = (ws[0] * outs[0] + ws[1] * outs[1] + ws[2] * outs[2]) / (ws[0] + ws[1] + ws[2])
    mix = jnp.sum(jnp.where(hm, mix8, 0.0), axis=0, keepdims=True)
    cx = _sample_cross(row[:, 2560:2816], _kv_t(mkv_ref, 0), _kv_t(mkv_ref, 1))
    z_ref[0] = jnp.concatenate([mix * _silu(row[:, 2304:2560]), cx * _silu(row[:, 2816:3072])], axis=1)


def _layer_block(cache_t, layer):
    return pl.BlockSpec((1, 1) + cache_t.shape[2:], lambda b, *_: (layer, b, 0, 0, 0, 0))


def _sample_a(proj_s, caches_t, mem_t, li, i):
    ns = proj_s.shape[0]
    row3 = proj_s.reshape(ns, 1, W_IN_A)
    return pl.pallas_call(
        _sample_a_body, grid=(ns,),
        in_specs=[pl.BlockSpec((1, 1, W_IN_A), lambda b: (b, 0, 0))] + [_layer_block(c, li) for c in caches_t]
                 + [_layer_block(mem_t, i)],
        out_specs=pl.BlockSpec((1, 1, A_WIDTH + X_WIDTH), lambda b: (b, 0, 0)),
        out_shape=jax.ShapeDtypeStruct((ns, 1, A_WIDTH + X_WIDTH), F32), compiler_params=_params(1), name="sample_a",
    )(row3, *caches_t, mem_t).reshape(ns, A_WIDTH + X_WIDTH)


def _q16(row):
    r = _iota((16, 128), 0)
    l = _iota((16, 128), 1)
    acc = jnp.zeros((16, 128), F32)
    for c in range(B_HEADS // 2):
        ch = jnp.broadcast_to(row[:, c * 128:(c + 1) * 128], (16, 128))
        rolled = pltpu.roll(ch, 64, axis=1)
        for hh in range(2):
            h = 2 * c + hh
            kv = h // B_GROUP
            lm = (l < 64) if kv == 0 else (l >= 64)
            acc = jnp.where((r == h) & lm, ch if hh == kv else rolled, acc)
    return acc * SCALE


def _sample_b1_body(pt_ref, row_ref, pos_ref, w1_ref, w2_ref, mm_ref, pages_ref, oc_ref, sel_ref,
                    buf_a, buf_b, rows_a, rows_b, sem, imp_sc, *, li, n_pages, ns):
    n = pl.program_id(0)
    past = n_pages * PAGE_SIZE
    n_cmp = past // CMP_STRIDE
    bufs, rows_bufs = (buf_a, buf_b), (rows_a, rows_b)

    def page_copy(page, p, s):
        return pltpu.make_async_copy(pages_ref.at[page, li, pl.ds(0, 2)], bufs[s].at[p], sem.at[s])

    def fetch(nn, s):
        def body(p, c):
            page_copy(pt_ref[nn * n_pages + p], p, s).start()
            return c
        lax.fori_loop(0, n_pages, body, 0)

    def wait(s):
        def body(p, c):
            page_copy(0, p, s).wait()
            return c
        lax.fori_loop(0, n_pages, body, 0)

    def to_rows(s, p):
        r0 = p * PAGE_SIZE if isinstance(p, int) else pl.multiple_of(p * PAGE_SIZE, PAGE_SIZE)
        for t in range(2):
            rows_bufs[s][t, pl.ds(r0, PAGE_SIZE), :] = bufs[s][p, t].reshape(2 * HEAD_DIM, PAGE_SIZE).T

    @pl.when(n == 0)
    def _():
        fetch(0, 0)
        fetch(1, 1)
        wait(0)

        def body(it, c):
            for k in range(8):
                to_rows(0, it * 8 + k)
            return c
        lax.fori_loop(0, n_pages // 8, body, 0)

    def stage(cur):
        nxt = 1 - cur

        @pl.when(n + 1 < ns)
        def _():
            wait(nxt)

        @pl.when(n + 2 < ns)
        def _():
            fetch(n + 2, cur)

        quarters = [range(q * n_pages // 4, (q + 1) * n_pages // 4) for q in range(4)]

        def next_rows(q):
            for p in quarters[q]:
                to_rows(nxt, p)

        next_rows(0)
        load = lambda t, l: rows_bufs[cur][t, pl.ds(l, n_cmp, stride=CMP_STRIDE), :]
        ck, cv = _compress_rows(load, pos_ref, w1_ref, w2_ref, n_cmp, between=lambda t: next_rows(1 + t))

        q16 = _q16(row_ref[0]).astype(BF16)
        slope = _row_consts(16, SLOPES_B)
        s = lax.dot_general(q16, ck.astype(BF16), NT, preferred_element_type=F32)
        cend = CMP_STRIDE * _iota((1, n_cmp), 1) + (CMP_LEN - 1)
        p = _masked_softmax_rows(s - slope * (past - cend).astype(F32), cend <= past)
        oc_ref[0] = jnp.dot(p.astype(BF16), cv.astype(BF16), preferred_element_type=F32)
        next_rows(3)

        r16 = _iota((16, n_cmp), 0)
        ps0 = jnp.sum(jnp.where(r16 < B_GROUP, p, 0.0), axis=0, keepdims=True)
        ps1 = jnp.sum(jnp.where((r16 >= B_GROUP) & (r16 < B_HEADS), p, 0.0), axis=0, keepdims=True)
        psum = jnp.concatenate([ps0, ps1, jnp.zeros((6, n_cmp), F32)], axis=0)
        mm = mm_ref[...]
        imp = sum(jnp.dot(t, mm, preferred_element_type=F32) for t in _split3(psum))
        blk = _iota((8, 256), 1)
        cur_blk = past // SEL_BLOCK
        forced = (blk == 0) | (blk == cur_blk) | (blk == cur_blk - 1)
        imp_sc[n] = jnp.where(blk > cur_blk, -jnp.inf, jnp.where(forced, FORCE_SCORE, imp))

    @pl.when(n % 2 == 0)
    def _():
        stage(0)

    @pl.when(n % 2 == 1)
    def _():
        stage(1)

    @pl.when(n == ns - 1)
    def _():
        impa = imp_sc[...]
        blkf = _iota(impa.shape, 2).astype(F32)
        lane = _iota((ns, 8, 128), 2)
        out = jnp.zeros((ns, 8, 128), F32)
        for r in range(SEL_TOPK):
            mx = jnp.max(impa, axis=-1, keepdims=True)
            idx = jnp.min(jnp.where(impa == mx, blkf, 1e9), axis=-1, keepdims=True)
            impa = jnp.where(blkf == idx, -jnp.inf, impa)
            out = jnp.where(lane == r, idx, out)
        sel_ref[...] = out.astype(jnp.int32)


def _sample_b1(page_table, proj_s, posw, w1bd, w2bd, mm, pages_t, li):
    ns, n_pages = page_table.shape
    past = n_pages * PAGE_SIZE
    row3 = proj_s.reshape(ns, 1, W_IN_B_PAD)
    full = lambda a: pl.BlockSpec(a.shape, lambda b, pt: (0,) * a.ndim)
    grid_spec = pltpu.PrefetchScalarGridSpec(
        num_scalar_prefetch=1, grid=(ns,),
        in_specs=[pl.BlockSpec((1, 1, W_IN_B_PAD), lambda b, pt: (b, 0, 0)), full(posw), full(w1bd), full(w2bd),
                  full(mm), pl.BlockSpec(memory_space=pl.ANY)],
        out_specs=[pl.BlockSpec((1, 16, 128), lambda b, pt: (b, 0, 0)),
                   pl.BlockSpec((ns, 8, 128), lambda b, pt: (0, 0, 0))],
        scratch_shapes=[pltpu.VMEM((n_pages, 2, B_KV, HEAD_DIM, PAGE_SIZE), F32)] * 2
                       + [pltpu.VMEM((2, past, 128), F32)] * 2
                       + [pltpu.SemaphoreType.DMA((2,)), pltpu.VMEM((ns, 8, 256), F32)])
    return pl.pallas_call(
        functools.partial(_sample_b1_body, li=li, n_pages=n_pages, ns=ns),
        grid_spec=grid_spec,
        out_shape=[jax.ShapeDtypeStruct((ns, 16, 128), F32), jax.ShapeDtypeStruct((ns, 8, 128), jnp.int32)],
        compiler_params=_params(1), name="sample_b1",
    )(page_table.reshape(-1), row3, posw, w1bd, w2bd, mm, pages_t)


def _sample_b2_body(pt_ref, sf_ref, row_ref, oc_ref, sel_ref, win_ref, mkv_ref, e16_ref, pages_ref, z_ref,
                    buf, sem, *, li, n_pages, ns):
    n = pl.program_id(0)
    past = n_pages * PAGE_SIZE
    n_blk = past // SEL_BLOCK
    per_page = PAGE_SIZE // SEL_BLOCK
    n_sel = B_KV * SEL_TOPK

    def blk_copies(page, kv, r, slot):
        return [pltpu.make_async_copy(pages_ref.at[page, li, 2 + t, kv],
                                      buf.at[slot, t, kv, :, pl.ds(r * PAGE_SIZE, PAGE_SIZE)], sem.at[slot])
                for t in range(2)]

    def fetch(nn, slot):
        for kv in range(B_KV):
            for r in range(SEL_TOPK):
                j = jnp.minimum(sf_ref[nn * n_sel + kv * SEL_TOPK + r], n_blk - 1)
                for cp in blk_copies(pt_ref[nn * n_pages + j // per_page], kv, r, slot):
                    cp.start()

    @pl.when(n == 0)
    def _():
        fetch(0, 0)

    @pl.when(n + 1 < ns)
    def _():
        fetch(n + 1, (n + 1) % 2)

    slot = n % 2
    for kv in range(B_KV):
        for r in range(SEL_TOPK):
            for cp in blk_copies(0, kv, r, slot):
                cp.wait()

    row = row_ref[0]
    q16f = _q16(row)
    q16 = q16f.astype(BF16)
    q16r = q16.astype(F32)
    slope = _row_consts(16, SLOPES_B)
    r16 = _iota((16, 128), 0)

    def new_key(col):
        kn = _bf(row[:, col:col + 128])
        return jnp.sum(q16r * kn, axis=-1, keepdims=True)

    def attend(s, s_new, v_t):
        m = jnp.maximum(jnp.max(s, axis=-1, keepdims=True), s_new)
        e = jnp.exp(s - m)
        e_new = jnp.exp(s_new - m)
        inv = 1.0 / (jnp.sum(e, axis=-1, keepdims=True) + e_new)
        return lax.dot_general((e * inv).astype(BF16), v_t, NT, preferred_element_type=F32), _bf(e_new * inv)

    n_keys = SEL_TOPK * PAGE_SIZE
    jv = jnp.dot(sel_ref[0].astype(F32).astype(BF16), e16_ref[...], preferred_element_type=F32).astype(jnp.int32)
    in_page = jnp.bitwise_and(_iota((8, n_keys), 1), PAGE_SIZE - 1)
    blk_shift = SEL_BLOCK.bit_length() - 1
    page_shift = per_page.bit_length() - 1
    ok_sel = ((in_page >> blk_shift) == (jv & (per_page - 1))) & (jv < n_blk)
    dist_sel = (past - ((jv >> page_shift) * PAGE_SIZE + in_page)).astype(F32)
    q64 = jnp.where(r16[:, :HEAD_DIM] < B_GROUP, q16f[:, :HEAD_DIM], q16f[:, HEAD_DIM:]).astype(BF16)
    s_new = new_key(1024)
    outs, p_news = [], []
    for kv in range(B_KV):
        s = jnp.dot(q64, buf[slot, 0, kv].astype(BF16), preferred_element_type=F32) - slope * dist_sel[kv:kv + 1]
        o, p_new = attend(jnp.where(ok_sel[kv:kv + 1], s, NEG), s_new, buf[slot, 1, kv].astype(BF16))
        outs.append(o)
        p_news.append(p_new)
    p_new = jnp.where(_iota((16, 1), 0) < B_GROUP, p_news[0], p_news[1])
    os16 = jnp.concatenate(outs, axis=1) + p_new * _bf(row[:, 1152:1280])

    lb = win_ref.shape[-1]
    dw = (lb - _iota((1, lb), 1)).astype(F32)
    s = jnp.dot(q16, _kv_t(win_ref, 0), preferred_element_type=F32) - slope * dw
    ow16, p_new = attend(s, new_key(1280), _kv_t(win_ref, 1))
    ow16 = ow16 + p_new * _bf(row[:, 1408:1536])

    sg = jnp.broadcast_to(_sigmoid(row[:, 2816:2944]), (16, 128))
    l16 = _iota((16, 128), 1)
    gate = lambda b: jnp.sum(jnp.where(l16 == r16 * 3 + b, sg, 0.0), axis=-1, keepdims=True)
    out16 = gate(0) * oc_ref[0] + gate(1) * os16 + gate(2) * ow16
    lane1 = _iota((1, 128), 1)
    mix = _place_heads([out16[h:h + 1, :] for h in range(B_HEADS)], lane1)

    cx = _sample_cross(row[:, 2304:2560], _kv_t(mkv_ref, 0), _kv_t(mkv_ref, 1))
    z_ref[0] = jnp.concatenate([mix * _silu(row[:, 1536:2304]), cx * _silu(row[:, 2560:2816])], axis=1)


def _sample_b2(page_table, sel, proj_s, oc, win_t, mem_t, e16, pages_t, li, i):
    ns, n_pages = page_table.shape
    row3 = proj_s.reshape(ns, 1, W_IN_B_PAD)
    full = lambda a: pl.BlockSpec(a.shape, lambda b, pt, sf: (0,) * a.ndim)
    per = lambda a: pl.BlockSpec((1,) + a.shape[1:], lambda b, pt, sf: (b,) + (0,) * (a.ndim - 1))
    grid_spec = pltpu.PrefetchScalarGridSpec(
        num_scalar_prefetch=2, grid=(ns,),
        in_specs=[per(row3), per(oc), per(sel), _layer_block(win_t, li), _layer_block(mem_t, i), full(e16),
                  pl.BlockSpec(memory_space=pl.ANY)],
        out_specs=pl.BlockSpec((1, 1, B_WIDTH + X_WIDTH), lambda b, pt, sf: (b, 0, 0)),
        scratch_shapes=[pltpu.VMEM((2, 2, B_KV, HEAD_DIM, SEL_TOPK * PAGE_SIZE), F32), pltpu.SemaphoreType.DMA((2,))])
    return pl.pallas_call(
        functools.partial(_sample_b2_body, li=li, n_pages=n_pages, ns=ns),
        grid_spec=grid_spec,
        out_shape=jax.ShapeDtypeStruct((ns, 1, B_WIDTH + X_WIDTH), F32),
        compiler_params=_params(1), name="sample_b2",
    )(page_table.reshape(-1), sel[:, :B_KV, :SEL_TOPK].reshape(-1), row3, oc, sel, win_t, mem_t, e16, pages_t
      ).reshape(ns, B_WIDTH + X_WIDTH)


def _importance_matrix(n_cmp, n_cols):
    c = np.arange(n_cmp)[:, None]
    j = np.arange(n_cols)[None, :]
    per = SEL_BLOCK // CMP_STRIDE
    m = ((c >= per * j) & (c <= per * j + per - 1)).astype(np.float32)
    m = m + ((c + 1 >= per * j) & (c + 1 <= per * j + per - 1)).astype(np.float32)
    m[n_cmp - 1, :] = 0.0
    return m


def _gate_expand():
    eg = np.zeros((128, 3 * B_WIDTH), np.float32)
    for h in range(B_HEADS):
        for b in range(3):
            eg[h * 3 + b, b * B_WIDTH + h * HEAD_DIM:b * B_WIDTH + (h + 1) * HEAD_DIM] = 1.0
    return eg


def _compress_weights(cmp_pos, cmp_w1, cmp_w2):
    eye = jnp.eye(B_KV, dtype=F32)
    posw = jnp.concatenate([cmp_pos, cmp_pos], axis=-1)
    w1 = cmp_w1.reshape(2, CMP_LEN, HEAD_DIM, CMP_HIDDEN)
    w1bd = jnp.einsum('tlek,jm->tljemk', w1, eye).reshape(2, CMP_LEN, 2 * HEAD_DIM, 2 * CMP_HIDDEN)
    w1cat = jnp.concatenate([w1bd[:, :CMP_STRIDE].reshape(2, CMP_STRIDE * 2 * HEAD_DIM, 2 * CMP_HIDDEN),
                             w1bd[:, CMP_STRIDE:].reshape(2, CMP_STRIDE * 2 * HEAD_DIM, 2 * CMP_HIDDEN)], axis=-1)
    pos_h = jnp.einsum('tlr,tlrh->tlh', posw, w1bd, precision=lax.Precision.HIGHEST)
    bias = jnp.concatenate([pos_h[:, :CMP_STRIDE].sum(axis=1), pos_h[:, CMP_STRIDE:].sum(axis=1)], axis=-1)
    w2bd = jnp.einsum('tke,jm->tjkme', cmp_w2, eye).reshape(2, 2 * CMP_HIDDEN, 2 * HEAD_DIM)
    return bias[:, None, :], w1cat.astype(BF16), w2bd.astype(BF16)


def _permute_w_in_b(w):
    d = w.shape[0]
    return jnp.concatenate([w[:, :1536], w[:, 1572:W_IN_B], w[:, 1536:1572],
                            jnp.zeros((d, W_IN_B_PAD - W_IN_B), w.dtype)], axis=1)


def kernel(x_prompt, x_sample, cache_mem_kv, cache_a_w128_kv, cache_a_w512_kv, cache_a_w2048_kv, cache_b_pages,
           cache_b_win_kv, page_table, mem_prompt, norm_pre, norm_post, norm_mem, w_mem_kv, w_in_a, w_out_a,
           w_in_b, w_out_b, cmp_pos, cmp_w1, cmp_w2):
    n, s_len, d = x_prompt.shape
    ns = x_sample.shape[0]
    depth = norm_pre.shape[0]
    page_size = cache_b_pages.shape[1]
    n_pages = page_table.shape[1]
    past = n_pages * page_size
    assert d == D_MODEL and x_sample.shape[1] == 1 and page_size == PAGE_SIZE
    assert ns >= 2
    assert s_len % A_SPAN == 0 and past % A_SPAN == 0
    caches_a = (cache_a_w128_kv, cache_a_w512_kv, cache_a_w2048_kv)
    for c, (win, _) in zip(caches_a, A_PATTERNS):
        assert c.shape[2] == win
    assert cache_b_win_kv.shape[2] == WIN_B

    tm = ROW_TILE
    n_cmp_p = s_len // CMP_STRIDE
    n_cmp_s = past // CMP_STRIDE
    slope_lanes = jnp.asarray(_slope_lanes(), F32)
    mt = jnp.asarray(_importance_matrix(n_cmp_p, 128).T, BF16)
    mm = jnp.asarray(_importance_matrix(n_cmp_s, 256), BF16)
    eg = jnp.asarray(_gate_expand(), BF16)
    e16 = jnp.asarray((np.arange(SEL_TOPK * PAGE_SIZE)[None, :] // PAGE_SIZE == np.arange(128)[:, None]), BF16)
    pages_t = cache_b_pages.transpose(0, 2, 3, 4, 5, 1)
    caches_a_t = [_rows_last(c) for c in caches_a]
    mem_t = _rows_last(cache_mem_kv)
    win_t = _rows_last(cache_b_win_kv)

    xp = x_prompt
    xs = x_sample.reshape(ns, d)
    mem2 = mem_prompt.reshape(n * N_MEM, d)
    mem_new = []
    a_p = [[] for _ in A_PATTERNS]
    a_s = [[] for _ in A_PATTERNS]
    b_p, b_s, bw_p, bw_s = [], [], [], []
    for i in range(depth):
        li = i // 2
        mkv_p = _rms_proj(mem2, norm_mem[i], w_mem_kv[i].astype(BF16), tm=N_MEM).reshape(n, N_MEM, 2 * X_WIDTH)
        mem_new.append(mkv_p.reshape(n, N_MEM, 2, 4, HEAD_DIM))
        if i % 2 == 0:
            w_in = w_in_a[li].astype(BF16)
            w_out = w_out_a[li].astype(BF16)
            proj_p = _rms_proj(xp.reshape(n * s_len, d), norm_pre[i], w_in, tm=tm).reshape(n, s_len, W_IN_A)
            proj_s = _rms_proj(xs, norm_pre[i], w_in, tm=ns)
            ols = []
            for g, (win, dil) in enumerate(A_PATTERNS):
                ols += _a_prompt_group(proj_p, g, dil)
                kv_p = proj_p[:, s_len - win:, 768:2304].reshape(n, win, 2, 3, 4, HEAD_DIM)[:, :, :, g]
                a_p[g].append(kv_p)
                a_s[g].append(proj_s[:, 768:2304].reshape(ns, 1, 2, 3, 4, HEAD_DIM)[:, :, :, g])
            xp = _finish_a(xp, ols, proj_p, mkv_p, w_out, norm_post[i], tm=2 * tm)
            z = _sample_a(proj_s, caches_a_t, mem_t, li, i)
            xs = _tail(xs, z, w_out, norm_post[i])
        else:
            w_in = _permute_w_in_b(w_in_b[li]).astype(BF16)
            w_out = w_out_b[li].astype(BF16)
            posw, w1bd, w2bd = _compress_weights(cmp_pos[li], cmp_w1[li], cmp_w2[li])
            proj_p, kvs = _rms_proj(xp.reshape(n * s_len, d), norm_pre[i], w_in, tm=tm, side=(1024, 512))
            proj_p = proj_p.reshape(n, s_len, W_IN_B_PAD)
            kvs = kvs.reshape(n, s_len, 512)
            proj_s = _rms_proj(xs, norm_pre[i], w_in, tm=ns)
            cmpd = _compress_prompt(proj_p, posw, w1bd, w2bd)
            pos = jnp.arange(s_len, dtype=jnp.int32)
            cend = CMP_STRIDE * jnp.arange(n_cmp_p, dtype=jnp.int32) + (CMP_LEN - 1)
            mix = _nsa_prompt(proj_p, _keys_with_pos(cmpd[:, :, 0:128], cend), cmpd[:, :, 128:256].transpose(0, 2, 1),
                              _keys_with_pos(kvs[:, :, 256:384], pos), _values_by_group(kvs[:, :, 384:512]),
                              _keys_with_pos(kvs[:, :, 0:128], pos), _values_by_group(kvs[:, :, 128:256]),
                              mt, eg, slope_lanes)
            xp = _finish_b(xp, mix, proj_p, mkv_p, w_out, norm_post[i], tm=2 * tm)
            oc, sel = _sample_b1(page_table, proj_s, posw, w1bd, w2bd, mm, pages_t, li)
            z = _sample_b2(page_table, sel, proj_s, oc, win_t, mem_t, e16, pages_t, li, i)
            xs = _tail(xs, z, w_out, norm_post[i])
            b_p.append(proj_p[:, :, 768:1280].reshape(n, s_len, 4, B_KV, HEAD_DIM))
            bw_p.append(proj_p[:, s_len - WIN_B:, 1280:1536].reshape(n, WIN_B, 2, B_KV, HEAD_DIM))
            b_s.append(proj_s[:, 768:1280].reshape(ns, 1, 4, B_KV, HEAD_DIM))
            bw_s.append(proj_s[:, 1280:1536].reshape(ns, 1, 2, B_KV, HEAD_DIM))
    return (xp, xs.reshape(ns, 1, d), jnp.stack(mem_new, axis=0),
            jnp.stack(a_p[0], axis=0), jnp.stack(a_p[1], axis=0), jnp.stack(a_p[2], axis=0),
            jnp.stack(b_p, axis=2), jnp.stack(bw_p, axis=0),
            jnp.stack(a_s[0], axis=0), jnp.stack(a_s[1], axis=0), jnp.stack(a_s[2], axis=0),
            jnp.stack(b_s, axis=2), jnp.stack(bw_s, axis=0))
```

```python
import functools

import numpy as np
import jax
import jax.numpy as jnp
from jax import lax
from jax.experimental import pallas as pl
from jax.experimental.pallas import tpu as pltpu

F32 = jnp.float32
BF16 = jnp.bfloat16

D_MODEL = 1024
HEAD_DIM = 64
SCALE = HEAD_DIM ** -0.5
LOG2E = 1.4426950408889634
RMS_EPS = 1e-6
N_MEM = 256
X_WIDTH = 256
A_PATTERNS = ((128, 1), (512, 4), (2048, 16))
A_WIDTH = 256
W_IN_A = 3072
B_HEADS = 12
B_KV = 2
B_GROUP = 6
B_WIDTH = 768
W_IN_B = 2852
W_IN_B_PAD = 2944
CMP_LEN = 32
CMP_STRIDE = 16
CMP_HIDDEN = 128
SEL_BLOCK = 64
SEL_TOPK = 16
WIN_B = 512
Q_BLOCK = 128
FORCE_SCORE = 1e4
PAGE_SIZE = 128
NEG = -1e30
GROUPS_PER_CHUNK = 4
V7X_VMEM_BYTES = 64 * 1024 * 1024
VMEM_LIMIT = V7X_VMEM_BYTES * 7 // 8
ROW_TILE = 512

NT = (((1,), (1,)), ((), ()))


def _alibi(n):
    k = np.arange(1, n + 1, dtype=np.float32)
    return [float(v) for v in np.float32(2.0) ** (np.float32(-8.0) * k / np.float32(n))]


SLOPES_A = _alibi(12)
SLOPES_B = _alibi(12)


def _params(n_axes):
    return pltpu.CompilerParams(dimension_semantics=("arbitrary",) * n_axes, vmem_limit_bytes=VMEM_LIMIT)


def _sigmoid(x):
    return 1.0 / (1.0 + jnp.exp(-x))


def _silu(x):
    return x * _sigmoid(x)


def _iota(shape, dim):
    return lax.broadcasted_iota(jnp.int32, shape, dim)


def _split3(x):
    hi = x.astype(BF16)
    r1 = x - hi.astype(F32)
    mid = r1.astype(BF16)
    lo = (r1 - mid.astype(F32)).astype(BF16)
    return hi, mid, lo


def _rms_proj_body(x_ref, g_ref, w_ref, o_ref, *side_ref, side):
    x = x_ref[...]
    y = x * lax.rsqrt(jnp.mean(x * x, axis=-1, keepdims=True) + RMS_EPS)
    y = (y * g_ref[...]).astype(BF16)
    o = jnp.dot(y, w_ref[...], preferred_element_type=F32)
    o_ref[...] = o
    if side is not None:
        side_ref[0][...] = o[:, side[0]:side[0] + side[1]].astype(BF16)


def _rms_proj(x, g, w, tm, side=None):
    m, d = x.shape
    n = w.shape[1]
    out_specs = [pl.BlockSpec((tm, n), lambda i: (i, 0))]
    out_shape = [jax.ShapeDtypeStruct((m, n), F32)]
    if side is not None:
        out_specs.append(pl.BlockSpec((tm, side[1]), lambda i: (i, 0)))
        out_shape.append(jax.ShapeDtypeStruct((m, side[1]), BF16))
    outs = pl.pallas_call(
        functools.partial(_rms_proj_body, side=side),
        grid=(m // tm,),
        in_specs=[pl.BlockSpec((tm, d), lambda i: (i, 0)),
                  pl.BlockSpec((1, d), lambda i: (0, 0)),
                  pl.BlockSpec((d, n), lambda i: (0, 0))],
        out_specs=out_specs,
        out_shape=out_shape,
        compiler_params=_params(1),
        name="rms_proj",
    )(x, g.reshape(1, d), w)
    return outs[0] if side is None else outs


A_SPAN = 2048


def _a_prompt_body(*refs, dil, slopes):
    ins, outs = refs[:10], refs[10:]
    t = pl.program_id(1)
    blk_rows = 128 * dil
    n_ub = A_SPAN // blk_rows
    i = _iota((128, 256), 0)
    j = _iota((128, 256), 1)
    back = 128 + i - j
    in_band = (back >= 0) & (back <= 128)
    dist = (back * dil).astype(F32)
    lane = _iota((128, 128), 1)

    def rows(ref, start):
        return ref[0, pl.ds(start, 128, stride=dil), :] if dil > 1 else ref[0, pl.ds(start, 128), :]

    def block(pair, ub, r, first):
        q_ref, k_ref, kp_ref, v_ref, vp_ref = ins[pair * 5:(pair + 1) * 5]
        o_ref, l_ref = outs[pair * 2:(pair + 1) * 2]
        start = ub * blk_rows + r
        if first:
            k_prev, v_prev = rows(kp_ref, r), rows(vp_ref, r)
            valid = in_band & (j >= jnp.where(t > 0, 0, 128))
        else:
            k_prev, v_prev = rows(k_ref, start - blk_rows), rows(v_ref, start - blk_rows)
            valid = in_band
        qp = rows(q_ref, start)
        kp = jnp.concatenate([k_prev, rows(k_ref, start)], axis=0).astype(BF16)
        vp = jnp.concatenate([v_prev, rows(v_ref, start)], axis=0).astype(BF16)
        o_pair = None
        l_pair = None
        for hh in range(2):
            hm = (lane < 64) if hh == 0 else (lane >= 64)
            qm = jnp.where(hm, qp, 0.0).astype(BF16)
            s = lax.dot_general(qm, kp, NT, preferred_element_type=F32) * SCALE
            s = jnp.where(valid, s - slopes[pair * 2 + hh] * dist, NEG)
            m = jnp.max(s, axis=-1, keepdims=True)
            e = jnp.exp(s - m)
            den = jnp.sum(e, axis=-1, keepdims=True)
            oh = jnp.dot((e * (1.0 / den)).astype(BF16), vp, preferred_element_type=F32)
            lh = jnp.broadcast_to(m + jnp.log(den), (128, 128))
            o_pair = oh if hh == 0 else jnp.where(lane < 64, o_pair, oh)
            l_pair = lh if hh == 0 else jnp.where(lane < 64, l_pair, lh)
        if dil > 1:
            o_ref[0, pl.ds(start, 128, stride=dil), :] = o_pair
            l_ref[0, pl.ds(start, 128, stride=dil), :] = l_pair
        else:
            o_ref[0, pl.ds(start, 128), :] = o_pair
            l_ref[0, pl.ds(start, 128), :] = l_pair

    def run(count, fn):
        if count == 0:
            return
        unroll = next(c for c in (8, 6, 5, 4, 3, 2, 1) if count % c == 0)

        def body(it, c):
            for k in range(unroll):
                fn(it * unroll + k)
            return c
        lax.fori_loop(0, count // unroll, body, 0)

    for pair in range(2):
        run(dil, lambda r, pair=pair: block(pair, 0, r, True))
        run((n_ub - 1) * dil, lambda idx, pair=pair: block(pair, 1 + idx // dil, idx % dil, False))


def _a_prompt_group(proj, g, dil):
    n, s_len, _ = proj.shape
    blk_rows = 128 * dil
    per_span = A_SPAN // blk_rows
    body = functools.partial(_a_prompt_body, dil=dil, slopes=tuple(SLOPES_A[g * 4:(g + 1) * 4]))
    cur = lambda col: pl.BlockSpec((1, A_SPAN, 128), lambda b, t: (b, t, col))
    prev = lambda col: pl.BlockSpec((1, blk_rows, 128), lambda b, t: (b, jnp.maximum(t * per_span - 1, 0), col))
    in_specs = []
    for pair in range(2):
        qc, kc, vc = 2 * g + pair, 6 + 2 * g + pair, 12 + 2 * g + pair
        in_specs += [cur(qc), cur(kc), prev(kc), cur(vc), prev(vc)]
    out_spec = pl.BlockSpec((1, A_SPAN, 128), lambda b, t: (b, t, 0))
    return pl.pallas_call(
        body,
        grid=(n, s_len // A_SPAN),
        in_specs=in_specs,
        out_specs=[out_spec] * 4,
        out_shape=[jax.ShapeDtypeStruct((n, s_len, 128), F32)] * 4,
        compiler_params=_params(2),
        name=f"a_prompt_g{g}",
    )(*([proj] * 10))


def _cross_rows(qx, kx, vx):
    tm = qx.shape[0]
    lane = _iota((tm, 128), 1)
    outs = []
    for pair in range(2):
        sl = slice(pair * 128, (pair + 1) * 128)
        qp, kp, vp = qx[:, sl], kx[:, sl], vx[:, sl]
        o_pair = None
        for hh in range(2):
            hm = (lane < 64) if hh == 0 else (lane >= 64)
            qm = jnp.where(hm, qp, 0.0).astype(BF16)
            s = lax.dot_general(qm, kp, NT, preferred_element_type=F32) * SCALE
            m = jnp.max(s, axis=-1, keepdims=True)
            e = jnp.exp(s - m)
            p = (e / jnp.sum(e, axis=-1, keepdims=True)).astype(BF16)
            oh = jnp.dot(p, vp, preferred_element_type=F32)
            o_pair = oh if hh == 0 else jnp.where(lane < 64, o_pair, oh)
        outs.append(o_pair)
    return jnp.concatenate(outs, axis=1)


def _out_norm_residual(x, z, w, g):
    y = jnp.dot(z.astype(BF16), w, preferred_element_type=F32)
    y = y * lax.rsqrt(jnp.mean(y * y, axis=-1, keepdims=True) + RMS_EPS)
    return x + y * g


def _finish_a_body(x_ref, *refs):
    gm_ref, qx_ref, gx_ref, mkv_ref, w_ref, g_ref, out_ref = refs[12:]
    mixes = []
    for pair in range(2):
        os_ = [refs[4 * g + 2 * pair][0] for g in range(3)]
        ls_ = [refs[4 * g + 2 * pair + 1][0] for g in range(3)]
        m = jnp.maximum(jnp.maximum(ls_[0], ls_[1]), ls_[2])
        es = [jnp.exp(l - m) for l in ls_]
        mixes.append((es[0] * os_[0] + es[1] * os_[1] + es[2] * os_[2]) / (es[0] + es[1] + es[2]))
    mix = jnp.concatenate(mixes, axis=1)
    mkv = mkv_ref[0]
    cx = _cross_rows(qx_ref[0], mkv[:, :X_WIDTH].astype(BF16), mkv[:, X_WIDTH:].astype(BF16))
    z = jnp.concatenate([mix * _silu(gm_ref[0]), cx * _silu(gx_ref[0])], axis=1)
    out_ref[0] = _out_norm_residual(x_ref[0], z, w_ref[...], g_ref[...])


def _finish_b_body(x_ref, mix_ref, gm_ref, qx_ref, gx_ref, mkv_ref, w_ref, g_ref, out_ref):
    mkv = mkv_ref[0]
    cx = _cross_rows(qx_ref[0], mkv[:, :X_WIDTH].astype(BF16), mkv[:, X_WIDTH:].astype(BF16))
    z = jnp.concatenate([mix_ref[0] * _silu(gm_ref[0]), cx * _silu(gx_ref[0])], axis=1)
    out_ref[0] = _out_norm_residual(x_ref[0], z, w_ref[...], g_ref[...])


def _finish_a(x, ols, proj, mkv, w_out, g_post, tm):
    n, s_len, d = x.shape
    row = lambda w, c: pl.BlockSpec((1, tm, w), lambda b, t: (b, t, c))
    in_specs = ([row(d, 0)] + [row(128, 0)] * 12 + [row(256, 9), row(256, 10), row(256, 11)]
                + [pl.BlockSpec((1, N_MEM, 2 * X_WIDTH), lambda b, t: (b, 0, 0)),
                   pl.BlockSpec(w_out.shape, lambda b, t: (0, 0)),
                   pl.BlockSpec((1, d), lambda b, t: (0, 0))])
    return pl.pallas_call(
        _finish_a_body, grid=(n, s_len // tm), in_specs=in_specs, out_specs=row(d, 0),
        out_shape=jax.ShapeDtypeStruct((n, s_len, d), F32), compiler_params=_params(2), name="finish_a",
    )(x, *ols, proj, proj, proj, mkv, w_out, g_post.reshape(1, d))


def _finish_b(x, mix, proj, mkv, w_out, g_post, tm):
    n, s_len, d = x.shape
    row = lambda w, c: pl.BlockSpec((1, tm, w), lambda b, t: (b, t, c))
    in_specs = [row(d, 0), row(B_WIDTH, 0), row(B_WIDTH, 2), row(256, 9), row(256, 10),
                pl.BlockSpec((1, N_MEM, 2 * X_WIDTH), lambda b, t: (b, 0, 0)),
                pl.BlockSpec(w_out.shape, lambda b, t: (0, 0)),
                pl.BlockSpec((1, d), lambda b, t: (0, 0))]
    return pl.pallas_call(
        _finish_b_body, grid=(n, s_len // tm), in_specs=in_specs, out_specs=row(d, 0),
        out_shape=jax.ShapeDtypeStruct((n, s_len, d), F32), compiler_params=_params(2), name="finish_b",
    )(x, mix, proj, proj, proj, mkv, w_out, g_post.reshape(1, d))


def _tail_body(x_ref, z_ref, w_ref, g_ref, out_ref):
    out_ref[...] = _out_norm_residual(x_ref[...], z_ref[...], w_ref[...], g_ref[...])


def _tail(x, z, w_out, g_post):
    m, d = x.shape
    full = lambda a: pl.BlockSpec(a.shape, lambda i: (0,) * a.ndim)
    g2 = g_post.reshape(1, d)
    return pl.pallas_call(
        _tail_body, grid=(1,), in_specs=[full(x), full(z), full(w_out), full(g2)], out_specs=full(x),
        out_shape=jax.ShapeDtypeStruct((m, d), F32), compiler_params=_params(1), name="sample_tail",
    )(x, z, w_out, g2)


def _compress_rows(load_rows, pos_ref, w1_ref, w2_ref, n_cmp, between=None):
    outs = []
    half = 2 * CMP_HIDDEN
    for t in range(2):
        y = jnp.concatenate([load_rows(t, l).astype(BF16) for l in range(CMP_STRIDE)], axis=1)
        ab = jnp.dot(y, w1_ref[t], preferred_element_type=F32) + pos_ref[t]
        h = ab[:, :half] + pltpu.roll(ab[:, half:], n_cmp - 1, axis=0)
        outs.append(jnp.dot(_silu(h).astype(BF16), w2_ref[t], preferred_element_type=F32))
        if between is not None:
            between(t)
    return outs


def _compress_body(k_ref, v_ref, pos_ref, w1_ref, w2_ref, o_ref, *, n_cmp):
    refs = (k_ref, v_ref)
    load = lambda t, l: refs[t][0, pl.ds(l, n_cmp, stride=CMP_STRIDE), :]
    ck, cv = _compress_rows(load, pos_ref, w1_ref, w2_ref, n_cmp)
    o_ref[0, :, 0:128] = ck.astype(BF16)
    o_ref[0, :, 128:256] = cv.astype(BF16)


def _compress_prompt(proj, posw, w1bd, w2bd):
    n, s_len, _ = proj.shape
    n_cmp = s_len // CMP_STRIDE
    full = lambda a: pl.BlockSpec(a.shape, lambda b: (0,) * a.ndim)
    return pl.pallas_call(
        functools.partial(_compress_body, n_cmp=n_cmp), grid=(n,),
        in_specs=[pl.BlockSpec((1, s_len, 128), lambda b: (b, 0, 6)), pl.BlockSpec((1, s_len, 128), lambda b: (b, 0, 7)),
                  full(posw), full(w1bd), full(w2bd)],
        out_specs=pl.BlockSpec((1, n_cmp, 256), lambda b: (b, 0, 0)),
        out_shape=jax.ShapeDtypeStruct((n, n_cmp, 256), BF16), compiler_params=_params(1), name="compress_prompt",
    )(proj, proj, posw, w1bd, w2bd)


def _place_heads(tiles, lane):
    chunks = []
    for c in range(B_HEADS // 2):
        t0, t1 = tiles[2 * c], tiles[2 * c + 1]
        if (2 * c) // B_GROUP == 1:
            t0 = pltpu.roll(t0, 64, axis=1)
        if (2 * c + 1) // B_GROUP == 0:
            t1 = pltpu.roll(t1, 64, axis=1)
        chunks.append(jnp.where(lane < 64, t0, t1))
    return jnp.concatenate(chunks, axis=1)


def _masked_softmax_rows(s, ok):
    s = jnp.where(ok, s, NEG)
    m = jnp.max(s, axis=-1, keepdims=True)
    e = jnp.where(ok, jnp.exp(s - m), 0.0)
    den = jnp.maximum(jnp.sum(e, axis=-1, keepdims=True), 1e-30)
    return e * (1.0 / den)


ONES_ROWS = 16


def _values_and_ones(v_ref, idx, kv):
    v = v_ref[idx + (slice(kv * HEAD_DIM, (kv + 1) * HEAD_DIM), slice(None))]
    return jnp.concatenate([v, jnp.ones((ONES_ROWS, v.shape[1]), BF16)], axis=0)


def _head_tile(num, den, kv):
    x = num if den is None else num * (1.0 / den)
    z = jnp.zeros_like(x)
    return jnp.concatenate([x, z] if kv == 0 else [z, x], axis=0).T


def _nsa_prompt_body(q_ref, gt_ref, kc_ref, vct_ref, kw_ref, vwt_ref, ks_ref, vst_ref, mt_ref, eg_ref,
                     sl_ref, out_ref, q6_sc, m_sc, acc_sc, sel_sc, words_sm, idx_sm, *, s_len):
    qb = pl.program_id(1)
    qstart = qb * Q_BLOCK
    n_cmp = s_len // CMP_STRIDE
    q = q_ref[0] * (SCALE * LOG2E)
    lane = _iota((Q_BLOCK, 128), 1)
    tq_row = qstart + _iota((1, Q_BLOCK), 1)
    oc_t, os_t, ow_t = [None] * B_HEADS, [None] * B_HEADS, [None] * B_HEADS

    sub = _iota((128, 128), 0)
    psums = []

    for kv in range(B_KV):
        for g in range(B_GROUP):
            h = kv * B_GROUP + g
            ch = q[:, (h // 2) * 128:(h // 2 + 1) * 128]
            if h % 2 == 1:
                ch = pltpu.roll(ch, 64, axis=1)
            q6_sc[kv, g * 128:(g + 1) * 128, :] = jnp.where(lane < 64, ch, sl_ref[h:h + 1, :]).astype(BF16)

    cok = (CMP_STRIDE * _iota((n_cmp, Q_BLOCK), 0) + (CMP_LEN - 1)) <= tq_row
    q_ok = tq_row >= (CMP_LEN - 1)
    n_wb = WIN_B // Q_BLOCK + 1

    def cmp_scores(kv):
        return lax.dot_general(kc_ref[0, kv], q6_sc[kv], NT, preferred_element_type=F32)

    def cmp_finish(kv, s_t):
        psum = jnp.zeros((n_cmp, Q_BLOCK), F32)
        ps = []
        for g in range(B_GROUP):
            s = jnp.where(cok, s_t[:, g * 128:(g + 1) * 128], NEG)
            e = jnp.exp2(s - jnp.max(s, axis=0, keepdims=True))
            p = e * jnp.where(q_ok, 1.0 / jnp.sum(e, axis=0, keepdims=True), 0.0)
            psum = psum + p
            ps.append(p.astype(BF16))
        oc = jnp.dot(vct_ref[0, kv * HEAD_DIM:(kv + 1) * HEAD_DIM, :], jnp.concatenate(ps, axis=1),
                     preferred_element_type=F32)
        for g in range(B_GROUP):
            oc_t[kv * B_GROUP + g] = _head_tile(oc[:, g * 128:(g + 1) * 128], None, kv)
        psums.append(psum)

    def win_scores(kv):
        kparts, vparts, pparts = [], [], []
        for wb in range(n_wb):
            b_raw = qb - (n_wb - 1) + wb
            b = jnp.maximum(b_raw, 0)
            r0 = pl.multiple_of(b * Q_BLOCK, Q_BLOCK)
            kparts.append(kw_ref[0, kv, pl.ds(r0, Q_BLOCK), :])
            vparts.append(_values_and_ones(vwt_ref, (0, b), kv))
            pparts.append(jnp.where(b_raw >= 0, r0, s_len) + _iota((128, Q_BLOCK), 0))
        s_t = lax.dot_general(jnp.concatenate(kparts, axis=0), q6_sc[kv], NT, preferred_element_type=F32)
        return s_t, jnp.concatenate(vparts, axis=1), jnp.concatenate(pparts, axis=0)

    def win_finish(kv, s_t, v_t, kpos):
        dw = tq_row - kpos
        wok = (dw >= 0) & (dw <= WIN_B)
        ps = []
        for g in range(B_GROUP):
            s = jnp.where(wok, s_t[:, g * 128:(g + 1) * 128], NEG)
            ps.append(jnp.exp2(s - jnp.max(s, axis=0, keepdims=True)).astype(BF16))
        ow = jnp.dot(v_t, jnp.concatenate(ps, axis=1), preferred_element_type=F32)
        for g in range(B_GROUP):
            cols = slice(g * 128, (g + 1) * 128)
            ow_t[kv * B_GROUP + g] = _head_tile(ow[0:HEAD_DIM, cols], ow[HEAD_DIM:HEAD_DIM + 1, cols], kv)

    sc0 = cmp_scores(0)
    sw0 = win_scores(0)
    cmp_finish(0, sc0)
    sc1 = cmp_scores(1)
    win_finish(0, *sw0)
    sw1 = win_scores(1)
    cmp_finish(1, sc1)
    win_finish(1, *sw1)

    mt = mt_ref[...]
    blk = _iota((128, Q_BLOCK), 0)
    ql = _iota((128, Q_BLOCK), 1)
    cur = jnp.where(ql >= SEL_BLOCK, qb * 2 + 1, qb * 2)
    forced = (blk == 0) | (blk == cur) | (blk == cur - 1)
    blkf = blk.astype(F32)
    imps = [jnp.where((blk > cur) | forced, -jnp.inf,
                      sum(jnp.dot(mt, t, preferred_element_type=F32) for t in _split3(psum))) for psum in psums]
    sels = [jnp.where(forced, 1.0, 0.0)] * B_KV
    for _ in range(SEL_TOPK - 3):
        for kv in range(B_KV):
            mx = jnp.max(imps[kv], axis=0, keepdims=True)
            idx = jnp.min(jnp.where(imps[kv] == mx, blkf, 1e9), axis=0, keepdims=True)
            hit = blkf == idx
            sels[kv] = jnp.where(hit, 1.0, sels[kv])
            imps[kv] = jnp.where(hit, -jnp.inf, imps[kv])

    blk_col = _iota((128, 1), 0)
    weight = lax.shift_left(jnp.ones((128, 1), jnp.int32), blk_col & 15).astype(F32)
    for kv in range(B_KV):
        sel_sc[kv] = sels[kv]
        contrib = jnp.max(sels[kv], axis=1, keepdims=True) * weight
        for w in range(8):
            words_sm[kv * 8 + w] = jnp.sum(contrib[16 * w:16 * (w + 1), :]).astype(jnp.int32)

    m_sc[...] = jnp.full(m_sc.shape, NEG, F32)
    acc_sc[...] = jnp.zeros(acc_sc.shape, F32)
    list_len = idx_sm.shape[0] // B_KV
    cnts = []
    n_grp = s_len // Q_BLOCK
    for kv in range(B_KV):
        cnt = jnp.int32(0)
        for w in range((n_grp + 7) // 8):
            word = words_sm[kv * 8 + w]
            for j in range(min(8, n_grp - 8 * w)):
                idx_sm[kv * list_len + cnt] = 8 * w + j
                cnt = cnt + jnp.where(((word >> (2 * j)) & 3) != 0, 1, 0)
        cnts.append(cnt)
    n_chunks = (jnp.maximum(cnts[0], cnts[1]) + GROUPS_PER_CHUNK - 1) // GROUPS_PER_CHUNK
    for kv in range(B_KV):
        def pad(i, c, kv=kv):
            idx_sm[kv * list_len + i] = -1
            return c
        lax.fori_loop(cnts[kv], n_chunks * GROUPS_PER_CHUNK, pad, 0)

    sub8 = _iota((8, Q_BLOCK), 0)

    def group_hits(kv, gi):
        rows8 = sel_sc[kv, pl.ds(pl.multiple_of((gi >> 2) * 8, 8), 8), :]
        r = (gi & 3) * 2
        lo = jnp.sum(jnp.where(sub8 == r, rows8, 0.0), axis=0, keepdims=True)
        hi = jnp.sum(jnp.where(sub8 == r + 1, rows8, 0.0), axis=0, keepdims=True)
        return jnp.where(sub < SEL_BLOCK, lo, hi)

    def chunk_scores(base, kv, n_g):
        kts, vts, hits, kposs = [], [], [], []
        for j in range(n_g):
            gi_raw = idx_sm[kv * list_len + base + j]
            gi = jnp.maximum(gi_raw, 0)
            k0 = pl.multiple_of(gi * Q_BLOCK, Q_BLOCK)
            kts.append(ks_ref[0, kv, pl.ds(k0, Q_BLOCK), :])
            vts.append(_values_and_ones(vst_ref, (0, gi), kv))
            hits.append(group_hits(kv, gi))
            kposs.append(jnp.where(gi_raw >= 0, k0, s_len) + _iota((128, 128), 0))
        s_t = lax.dot_general(jnp.concatenate(kts, axis=0), q6_sc[kv], NT, preferred_element_type=F32)
        ok = ((jnp.concatenate(hits, axis=0) > 0.5)
              & (jnp.concatenate(kposs, axis=0) <= qstart + _iota((n_g * 128, 128), 1)))
        return s_t, ok, jnp.concatenate(vts, axis=1)

    def chunk_update(kv, s_t, ok, v_t):
        m_old = m_sc[kv]
        m_new, ps = [], []
        for g in range(B_GROUP):
            cols = slice(g * 128, (g + 1) * 128)
            s = jnp.where(ok, s_t[:, cols], NEG)
            mg = jnp.maximum(m_old[:, cols], jnp.max(s, axis=0, keepdims=True))
            ps.append(jnp.exp2(s - mg).astype(BF16))
            m_new.append(mg)
        m_new = jnp.concatenate(m_new, axis=1)
        pv = jnp.dot(v_t, jnp.concatenate(ps, axis=1), preferred_element_type=F32)
        acc_sc[kv] = jnp.exp2(m_old - m_new) * acc_sc[kv] + pv
        m_sc[kv] = m_new

    def step(base, n_g):
        first = chunk_scores(base, 0, n_g)
        second = chunk_scores(base, 1, n_g)
        chunk_update(0, *first)
        chunk_update(1, *second)

    def chunk(c, carry):
        step(c * GROUPS_PER_CHUNK, GROUPS_PER_CHUNK)
        return carry

    longest = jnp.maximum(cnts[0], cnts[1])
    n_full = longest // GROUPS_PER_CHUNK
    rem = longest - n_full * GROUPS_PER_CHUNK
    lax.fori_loop(0, n_full, chunk, 0)

    @pl.when(rem > GROUPS_PER_CHUNK // 2)
    def _():
        step(n_full * GROUPS_PER_CHUNK, GROUPS_PER_CHUNK)

    @pl.when((rem > 0) & (rem <= GROUPS_PER_CHUNK // 2))
    def _():
        step(n_full * GROUPS_PER_CHUNK, GROUPS_PER_CHUNK // 2)
    for kv in range(B_KV):
        for g in range(B_GROUP):
            cols = slice(g * 128, (g + 1) * 128)
            os_t[kv * B_GROUP + g] = _head_tile(acc_sc[kv, 0:HEAD_DIM, cols], acc_sc[kv, HEAD_DIM:HEAD_DIM + 1, cols], kv)

    sg = _sigmoid(gt_ref[0])
    eg = eg_ref[...]
    gexp = sum(jnp.dot(t, eg, preferred_element_type=F32) for t in _split3(sg))
    out_ref[0] = (gexp[:, 0:B_WIDTH] * _place_heads(oc_t, lane)
                  + gexp[:, B_WIDTH:2 * B_WIDTH] * _place_heads(os_t, lane)
                  + gexp[:, 2 * B_WIDTH:] * _place_heads(ow_t, lane))


def _nsa_prompt(proj, kc, vc_t, kw, vw_t, ks, vs_t, mt, eg, slope_lanes):
    n, s_len, _ = proj.shape
    n_cmp = s_len // CMP_STRIDE
    n_grp = s_len // Q_BLOCK
    full = lambda a: pl.BlockSpec(a.shape, lambda b, t: (0,) * a.ndim)
    per_n = lambda a: pl.BlockSpec((1,) + a.shape[1:], lambda b, t: (b,) + (0,) * (a.ndim - 1))
    return pl.pallas_call(
        functools.partial(_nsa_prompt_body, s_len=s_len),
        grid=(n, n_grp),
        in_specs=[pl.BlockSpec((1, Q_BLOCK, B_WIDTH), lambda b, t: (b, t, 0)),
                  pl.BlockSpec((1, Q_BLOCK, 128), lambda b, t: (b, t, 22)),
                  per_n(kc), per_n(vc_t), per_n(kw), per_n(vw_t), per_n(ks), per_n(vs_t),
                  full(mt), full(eg), full(slope_lanes)],
        out_specs=pl.BlockSpec((1, Q_BLOCK, B_WIDTH), lambda b, t: (b, t, 0)),
        out_shape=jax.ShapeDtypeStruct((n, s_len, B_WIDTH), F32),
        scratch_shapes=[pltpu.VMEM((B_KV, B_GROUP * Q_BLOCK, 128), BF16), pltpu.VMEM((B_KV, 1, B_GROUP * Q_BLOCK), F32),
                        pltpu.VMEM((B_KV, HEAD_DIM + ONES_ROWS, B_GROUP * Q_BLOCK), F32),
                        pltpu.VMEM((B_KV, 128, Q_BLOCK), F32), pltpu.SMEM((B_KV * 8,), jnp.int32),
                        pltpu.SMEM((B_KV * (n_grp + GROUPS_PER_CHUNK),), jnp.int32)],
        compiler_params=_params(2), name="nsa_prompt",
    )(proj, proj, kc, vc_t, kw, vw_t, ks, vs_t, mt, eg, slope_lanes)


POS_LANE = 64


def _slope_lanes():
    out = np.zeros((B_HEADS, 128), np.float32)
    for h, slope in enumerate(SLOPES_B):
        s = np.float32(np.float64(slope) * LOG2E)
        hi = np.float32(np.asarray(s, np.float32).astype(jnp.bfloat16))
        mid = np.float32(np.asarray(np.float32(s) - hi, np.float32).astype(jnp.bfloat16))
        lo = np.float32(np.asarray(np.float32(s) - hi - mid, np.float32).astype(jnp.bfloat16))
        out[h, POS_LANE:POS_LANE + 6] = [hi, mid, lo, hi, mid, lo]
    return out


def _keys_with_pos(k2, pos):
    n, n_keys, _ = k2.shape
    lo = (pos % Q_BLOCK).astype(BF16)[None, :, None]
    hi = (pos - pos % Q_BLOCK).astype(BF16)[None, :, None]
    tail = jnp.concatenate([jnp.broadcast_to(lo, (n, n_keys, 3)), jnp.broadcast_to(hi, (n, n_keys, 3)),
                            jnp.zeros((n, n_keys, 128 - POS_LANE - 6), BF16)], axis=-1)
    return jnp.stack([jnp.concatenate([k2[..., kv * 64:(kv + 1) * 64], tail], axis=-1) for kv in range(B_KV)], axis=1)


def _values_by_group(v2):
    n, s_len, _ = v2.shape
    return v2.reshape(n, s_len // Q_BLOCK, Q_BLOCK, 128).transpose(0, 1, 3, 2)


def _heads_rows(vec, n_rows, width):
    r = _iota((n_rows, width), 0)
    l = _iota((n_rows, width), 1)
    hm = (l >= r * HEAD_DIM) & (l < r * HEAD_DIM + HEAD_DIM)
    return jnp.where(hm, jnp.broadcast_to(vec, (n_rows, width)), 0.0), hm


def _bf(x):
    return x.astype(BF16).astype(F32)


def _row_consts(n_rows, vals):
    r = _iota((n_rows, 1), 0)
    out = jnp.zeros((n_rows, 1), F32)
    for i, v in enumerate(vals):
        out = jnp.where(r == i, v, out)
    return out


def _rows_last(cache):
    nd = cache.ndim
    return cache.transpose(tuple(range(nd - 4)) + (nd - 3, nd - 2, nd - 1, nd - 4))


def _kv_t(ref, t):
    x = ref[0, 0, t]
    return x.reshape(x.shape[0] * x.shape[1], x.shape[2]).astype(BF16)


def _sample_cross(qx_row, k_t, v_t):
    q8, hm = _heads_rows(qx_row, 8, X_WIDTH)
    s = jnp.dot(q8.astype(BF16), k_t, preferred_element_type=F32) * SCALE
    e = jnp.exp(s - jnp.max(s, axis=-1, keepdims=True))
    p = (e * (1.0 / jnp.sum(e, axis=-1, keepdims=True))).astype(BF16)
    o8 = lax.dot_general(p, v_t, NT, preferred_element_type=F32)
    return jnp.sum(jnp.where(hm, o8, 0.0), axis=0, keepdims=True)


def _sample_a_body(row_ref, c0_ref, c1_ref, c2_ref, mkv_ref, z_ref):
    row = row_ref[0]
    outs, lses = [], []
    hm = None
    for g, (win, dil) in enumerate(A_PATTERNS):
        cref = (c0_ref, c1_ref, c2_ref)[g]
        q8, hm = _heads_rows(row[:, g * 256:(g + 1) * 256], 8, A_WIDTH)
        knew = row[:, 768 + g * 256:768 + (g + 1) * 256]
        vnew = row[:, 1536 + g * 256:1536 + (g + 1) * 256]
        q8b = q8.astype(BF16)
        slope = _row_consts(8, SLOPES_A[g * 4:(g + 1) * 4])
        s = jnp.dot(q8b, _kv_t(cref, 0), preferred_element_type=F32) * SCALE
        r = _iota((8, win), 1)
        s = jnp.where((r & (dil - 1)) == 0, s - slope * (win - r).astype(F32), NEG)
        s_new = jnp.sum(q8b.astype(F32) * _bf(knew), axis=-1, keepdims=True) * SCALE
        m = jnp.maximum(jnp.max(s, axis=-1, keepdims=True), s_new)
        e = jnp.exp(s - m)
        e_new = jnp.exp(s_new - m)
        den = jnp.sum(e, axis=-1, keepdims=True) + e_new
        inv = 1.0 / den
        o8 = (lax.dot_general((e * inv).astype(BF16), _kv_t(cref, 1), NT, preferred_element_type=F32)
              + _bf(e_new * inv) * _bf(vnew))
        outs.append(o8)
        lses.append(m + jnp.log(den))
    mx = jnp.maximum(jnp.maximum(lses[0], lses[1]), lses[2])
    ws = [jnp.exp(l - mx) for l in lses]
    mix8 = (ws[0] * outs[0] + ws[1] * outs[1] + ws[2] * outs[2]) / (ws[0] + ws[1] + ws[2])
    mix = jnp.sum(jnp.where(hm, mix8, 0.0), axis=0, keepdims=True)
    cx = _sample_cross(row[:, 2560:2816], _kv_t(mkv_ref, 0), _kv_t(mkv_ref, 1))
    z_ref[0] = jnp.concatenate([mix * _silu(row[:, 2304:2560]), cx * _silu(row[:, 2816:3072])], axis=1)


def _layer_block(cache_t, layer):
    return pl.BlockSpec((1, 1) + cache_t.shape[2:], lambda b, *_: (layer, b, 0, 0, 0, 0))


def _sample_a(proj_s, caches_t, mem_t, li, i):
    ns = proj_s.shape[0]
    row3 = proj_s.reshape(ns, 1, W_IN_A)
    return pl.pallas_call(
        _sample_a_body, grid=(ns,),
        in_specs=[pl.BlockSpec((1, 1, W_IN_A), lambda b: (b, 0, 0))] + [_layer_block(c, li) for c in caches_t]
                 + [_layer_block(mem_t, i)],
        out_specs=pl.BlockSpec((1, 1, A_WIDTH + X_WIDTH), lambda b: (b, 0, 0)),
        out_shape=jax.ShapeDtypeStruct((ns, 1, A_WIDTH + X_WIDTH), F32), compiler_params=_params(1), name="sample_a",
    )(row3, *caches_t, mem_t).reshape(ns, A_WIDTH + X_WIDTH)


def _q16(row):
    r = _iota((16, 128), 0)
    l = _iota((16, 128), 1)
    acc = jnp.zeros((16, 128), F32)
    for c in range(B_HEADS // 2):
        ch = jnp.broadcast_to(row[:, c * 128:(c + 1) * 128], (16, 128))
        rolled = pltpu.roll(ch, 64, axis=1)
        for hh in range(2):
            h = 2 * c + hh
            kv = h // B_GROUP
            lm = (l < 64) if kv == 0 else (l >= 64)
            acc = jnp.where((r == h) & lm, ch if hh == kv else rolled, acc)
    return acc * SCALE


def _sample_b1_body(pt_ref, row_ref, pos_ref, w1_ref, w2_ref, mm_ref, pages_ref, oc_ref, sel_ref,
                    buf_a, buf_b, rows_a, rows_b, sem, imp_sc, *, li, n_pages, ns):
    n = pl.program_id(0)
    past = n_pages * PAGE_SIZE
    n_cmp = past // CMP_STRIDE
    bufs, rows_bufs = (buf_a, buf_b), (rows_a, rows_b)

    def page_copy(page, p, s):
        return pltpu.make_async_copy(pages_ref.at[page, li, pl.ds(0, 2)], bufs[s].at[p], sem.at[s])

    def fetch(nn, s):
        def body(p, c):
            page_copy(pt_ref[nn * n_pages + p], p, s).start()
            return c
        lax.fori_loop(0, n_pages, body, 0)

    def wait(s):
        def body(p, c):
            page_copy(0, p, s).wait()
            return c
        lax.fori_loop(0, n_pages, body, 0)

    def to_rows(s, p):
        r0 = p * PAGE_SIZE if isinstance(p, int) else pl.multiple_of(p * PAGE_SIZE, PAGE_SIZE)
        for t in range(2):
            rows_bufs[s][t, pl.ds(r0, PAGE_SIZE), :] = bufs[s][p, t].reshape(2 * HEAD_DIM, PAGE_SIZE).T

    @pl.when(n == 0)
    def _():
        fetch(0, 0)
        fetch(1, 1)
        wait(0)

        def body(it, c):
            for k in range(8):
                to_rows(0, it * 8 + k)
            return c
        lax.fori_loop(0, n_pages // 8, body, 0)

    def stage(cur):
        nxt = 1 - cur

        @pl.when(n + 1 < ns)
        def _():
            wait(nxt)

        @pl.when(n + 2 < ns)
        def _():
            fetch(n + 2, cur)

        quarters = [range(q * n_pages // 4, (q + 1) * n_pages // 4) for q in range(4)]

        def next_rows(q):
            for p in quarters[q]:
                to_rows(nxt, p)

        next_rows(0)
        load = lambda t, l: rows_bufs[cur][t, pl.ds(l, n_cmp, stride=CMP_STRIDE), :]
        ck, cv = _compress_rows(load, pos_ref, w1_ref, w2_ref, n_cmp, between=lambda t: next_rows(1 + t))

        q16 = _q16(row_ref[0]).astype(BF16)
        slope = _row_consts(16, SLOPES_B)
        s = lax.dot_general(q16, ck.astype(BF16), NT, preferred_element_type=F32)
        cend = CMP_STRIDE * _iota((1, n_cmp), 1) + (CMP_LEN - 1)
        p = _masked_softmax_rows(s - slope * (past - cend).astype(F32), cend <= past)
        oc_ref[0] = jnp.dot(p.astype(BF16), cv.astype(BF16), preferred_element_type=F32)
        next_rows(3)

        r16 = _iota((16, n_cmp), 0)
        ps0 = jnp.sum(jnp.where(r16 < B_GROUP, p, 0.0), axis=0, keepdims=True)
        ps1 = jnp.sum(jnp.where((r16 >= B_GROUP) & (r16 < B_HEADS), p, 0.0), axis=0, keepdims=True)
        psum = jnp.concatenate([ps0, ps1, jnp.zeros((6, n_cmp), F32)], axis=0)
        mm = mm_ref[...]
        imp = sum(jnp.dot(t, mm, preferred_element_type=F32) for t in _split3(psum))
        blk = _iota((8, 256), 1)
        cur_blk = past // SEL_BLOCK
        forced = (blk == 0) | (blk == cur_blk) | (blk == cur_blk - 1)
        imp_sc[n] = jnp.where(blk > cur_blk, -jnp.inf, jnp.where(forced, FORCE_SCORE, imp))

    @pl.when(n % 2 == 0)
    def _():
        stage(0)

    @pl.when(n % 2 == 1)
    def _():
        stage(1)

    @pl.when(n == ns - 1)
    def _():
        impa = imp_sc[...]
        blkf = _iota(impa.shape, 2).astype(F32)
        lane = _iota((ns, 8, 128), 2)
        out = jnp.zeros((ns, 8, 128), F32)
        for r in range(SEL_TOPK):
            mx = jnp.max(impa, axis=-1, keepdims=True)
            idx = jnp.min(jnp.where(impa == mx, blkf, 1e9), axis=-1, keepdims=True)
            impa = jnp.where(blkf == idx, -jnp.inf, impa)
            out = jnp.where(lane == r, idx, out)
        sel_ref[...] = out.astype(jnp.int32)


def _sample_b1(page_table, proj_s, posw, w1bd, w2bd, mm, pages_t, li):
    ns, n_pages = page_table.shape
    past = n_pages * PAGE_SIZE
    row3 = proj_s.reshape(ns, 1, W_IN_B_PAD)
    full = lambda a: pl.BlockSpec(a.shape, lambda b, pt: (0,) * a.ndim)
    grid_spec = pltpu.PrefetchScalarGridSpec(
        num_scalar_prefetch=1, grid=(ns,),
        in_specs=[pl.BlockSpec((1, 1, W_IN_B_PAD), lambda b, pt: (b, 0, 0)), full(posw), full(w1bd), full(w2bd),
                  full(mm), pl.BlockSpec(memory_space=pl.ANY)],
        out_specs=[pl.BlockSpec((1, 16, 128), lambda b, pt: (b, 0, 0)),
                   pl.BlockSpec((ns, 8, 128), lambda b, pt: (0, 0, 0))],
        scratch_shapes=[pltpu.VMEM((n_pages, 2, B_KV, HEAD_DIM, PAGE_SIZE), F32)] * 2
                       + [pltpu.VMEM((2, past, 128), F32)] * 2
                       + [pltpu.SemaphoreType.DMA((2,)), pltpu.VMEM((ns, 8, 256), F32)])
    return pl.pallas_call(
        functools.partial(_sample_b1_body, li=li, n_pages=n_pages, ns=ns),
        grid_spec=grid_spec,
        out_shape=[jax.ShapeDtypeStruct((ns, 16, 128), F32), jax.ShapeDtypeStruct((ns, 8, 128), jnp.int32)],
        compiler_params=_params(1), name="sample_b1",
    )(page_table.reshape(-1), row3, posw, w1bd, w2bd, mm, pages_t)


def _sample_b2_body(pt_ref, sf_ref, row_ref, oc_ref, sel_ref, win_ref, mkv_ref, e16_ref, pages_ref, z_ref,
                    buf, sem, *, li, n_pages, ns):
    n = pl.program_id(0)
    past = n_pages * PAGE_SIZE
    n_blk = past // SEL_BLOCK
    per_page = PAGE_SIZE // SEL_BLOCK
    n_sel = B_KV * SEL_TOPK

    def blk_copies(page, kv, r, slot):
        return [pltpu.make_async_copy(pages_ref.at[page, li, 2 + t, kv],
                                      buf.at[slot, t, kv, :, pl.ds(r * PAGE_SIZE, PAGE_SIZE)], sem.at[slot])
                for t in range(2)]

    def fetch(nn, slot):
        for kv in range(B_KV):
            for r in range(SEL_TOPK):
                j = jnp.minimum(sf_ref[nn * n_sel + kv * SEL_TOPK + r], n_blk - 1)
                for cp in blk_copies(pt_ref[nn * n_pages + j // per_page], kv, r, slot):
                    cp.start()

    @pl.when(n == 0)
    def _():
        fetch(0, 0)

    @pl.when(n + 1 < ns)
    def _():
        fetch(n + 1, (n + 1) % 2)

    slot = n % 2
    for kv in range(B_KV):
        for r in range(SEL_TOPK):
            for cp in blk_copies(0, kv, r, slot):
                cp.wait()

    row = row_ref[0]
    q16f = _q16(row)
    q16 = q16f.astype(BF16)
    q16r = q16.astype(F32)
    slope = _row_consts(16, SLOPES_B)
    r16 = _iota((16, 128), 0)

    def new_key(col):
        kn = _bf(row[:, col:col + 128])
        return jnp.sum(q16r * kn, axis=-1, keepdims=True)

    def attend(s, s_new, v_t):
        m = jnp.maximum(jnp.max(s, axis=-1, keepdims=True), s_new)
        e = jnp.exp(s - m)
        e_new = jnp.exp(s_new - m)
        inv = 1.0 / (jnp.sum(e, axis=-1, keepdims=True) + e_new)
        return lax.dot_general((e * inv).astype(BF16), v_t, NT, preferred_element_type=F32), _bf(e_new * inv)

    n_keys = SEL_TOPK * PAGE_SIZE
    jv = jnp.dot(sel_ref[0].astype(F32).astype(BF16), e16_ref[...], preferred_element_type=F32).astype(jnp.int32)
    in_page = jnp.bitwise_and(_iota((8, n_keys), 1), PAGE_SIZE - 1)
    blk_shift = SEL_BLOCK.bit_length() - 1
    page_shift = per_page.bit_length() - 1
    ok_sel = ((in_page >> blk_shift) == (jv & (per_page - 1))) & (jv < n_blk)
    dist_sel = (past - ((jv >> page_shift) * PAGE_SIZE + in_page)).astype(F32)
    q64 = jnp.where(r16[:, :HEAD_DIM] < B_GROUP, q16f[:, :HEAD_DIM], q16f[:, HEAD_DIM:]).astype(BF16)
    s_new = new_key(1024)
    outs, p_news = [], []
    for kv in range(B_KV):
        s = jnp.dot(q64, buf[slot, 0, kv].astype(BF16), preferred_element_type=F32) - slope * dist_sel[kv:kv + 1]
        o, p_new = attend(jnp.where(ok_sel[kv:kv + 1], s, NEG), s_new, buf[slot, 1, kv].astype(BF16))
        outs.append(o)
        p_news.append(p_new)
    p_new = jnp.where(_iota((16, 1), 0) < B_GROUP, p_news[0], p_news[1])
    os16 = jnp.concatenate(outs, axis=1) + p_new * _bf(row[:, 1152:1280])

    lb = win_ref.shape[-1]
    dw = (lb - _iota((1, lb), 1)).astype(F32)
    s = jnp.dot(q16, _kv_t(win_ref, 0), preferred_element_type=F32) - slope * dw
    ow16, p_new = attend(s, new_key(1280), _kv_t(win_ref, 1))
    ow16 = ow16 + p_new * _bf(row[:, 1408:1536])

    sg = jnp.broadcast_to(_sigmoid(row[:, 2816:2944]), (16, 128))
    l16 = _iota((16, 128), 1)
    gate = lambda b: jnp.sum(jnp.where(l16 == r16 * 3 + b, sg, 0.0), axis=-1, keepdims=True)
    out16 = gate(0) * oc_ref[0] + gate(1) * os16 + gate(2) * ow16
    lane1 = _iota((1, 128), 1)
    mix = _place_heads([out16[h:h + 1, :] for h in range(B_HEADS)], lane1)

    cx = _sample_cross(row[:, 2304:2560], _kv_t(mkv_ref, 0), _kv_t(mkv_ref, 1))
    z_ref[0] = jnp.concatenate([mix * _silu(row[:, 1536:2304]), cx * _silu(row[:, 2560:2816])], axis=1)


def _sample_b2(page_table, sel, proj_s, oc, win_t, mem_t, e16, pages_t, li, i):
    ns, n_pages = page_table.shape
    row3 = proj_s.reshape(ns, 1, W_IN_B_PAD)
    full = lambda a: pl.BlockSpec(a.shape, lambda b, pt, sf: (0,) * a.ndim)
    per = lambda a: pl.BlockSpec((1,) + a.shape[1:], lambda b, pt, sf: (b,) + (0,) * (a.ndim - 1))
    grid_spec = pltpu.PrefetchScalarGridSpec(
        num_scalar_prefetch=2, grid=(ns,),
        in_specs=[per(row3), per(oc), per(sel), _layer_block(win_t, li), _layer_block(mem_t, i), full(e16),
                  pl.BlockSpec(memory_space=pl.ANY)],
        out_specs=pl.BlockSpec((1, 1, B_WIDTH + X_WIDTH), lambda b, pt, sf: (b, 0, 0)),
        scratch_shapes=[pltpu.VMEM((2, 2, B_KV, HEAD_DIM, SEL_TOPK * PAGE_SIZE), F32), pltpu.SemaphoreType.DMA((2,))])
    return pl.pallas_call(
        functools.partial(_sample_b2_body, li=li, n_pages=n_pages, ns=ns),
        grid_spec=grid_spec,
        out_shape=jax.ShapeDtypeStruct((ns, 1, B_WIDTH + X_WIDTH), F32),
        compiler_params=_params(1), name="sample_b2",
    )(page_table.reshape(-1), sel[:, :B_KV, :SEL_TOPK].reshape(-1), row3, oc, sel, win_t, mem_t, e16, pages_t
      ).reshape(ns, B_WIDTH + X_WIDTH)


def _importance_matrix(n_cmp, n_cols):
    c = np.arange(n_cmp)[:, None]
    j = np.arange(n_cols)[None, :]
    per = SEL_BLOCK // CMP_STRIDE
    m = ((c >= per * j) & (c <= per * j + per - 1)).astype(np.float32)
    m = m + ((c + 1 >= per * j) & (c + 1 <= per * j + per - 1)).astype(np.float32)
    m[n_cmp - 1, :] = 0.0
    return m


def _gate_expand():
    eg = np.zeros((128, 3 * B_WIDTH), np.float32)
    for h in range(B_HEADS):
        for b in range(3):
            eg[h * 3 + b, b * B_WIDTH + h * HEAD_DIM:b * B_WIDTH + (h + 1) * HEAD_DIM] = 1.0
    return eg


def _compress_weights(cmp_pos, cmp_w1, cmp_w2):
    eye = jnp.eye(B_KV, dtype=F32)
    posw = jnp.concatenate([cmp_pos, cmp_pos], axis=-1)
    w1 = cmp_w1.reshape(2, CMP_LEN, HEAD_DIM, CMP_HIDDEN)
    w1bd = jnp.einsum('tlek,jm->tljemk', w1, eye).reshape(2, CMP_LEN, 2 * HEAD_DIM, 2 * CMP_HIDDEN)
    w1cat = jnp.concatenate([w1bd[:, :CMP_STRIDE].reshape(2, CMP_STRIDE * 2 * HEAD_DIM, 2 * CMP_HIDDEN),
                             w1bd[:, CMP_STRIDE:].reshape(2, CMP_STRIDE * 2 * HEAD_DIM, 2 * CMP_HIDDEN)], axis=-1)
    pos_h = jnp.einsum('tlr,tlrh->tlh', posw, w1bd, precision=lax.Precision.HIGHEST)
    bias = jnp.concatenate([pos_h[:, :CMP_STRIDE].sum(axis=1), pos_h[:, CMP_STRIDE:].sum(axis=1)], axis=-1)
    w2bd = jnp.einsum('tke,jm->tjkme', cmp_w2, eye).reshape(2, 2 * CMP_HIDDEN, 2 * HEAD_DIM)
    return bias[:, None, :], w1cat.astype(BF16), w2bd.astype(BF16)


def _permute_w_in_b(w):
    d = w.shape[0]
    return jnp.concatenate([w[:, :1536], w[:, 1572:W_IN_B], w[:, 1536:1572],
                            jnp.zeros((d, W_IN_B_PAD - W_IN_B), w.dtype)], axis=1)


def kernel(x_prompt, x_sample, cache_mem_kv, cache_a_w128_kv, cache_a_w512_kv, cache_a_w2048_kv, cache_b_pages,
           cache_b_win_kv, page_table, mem_prompt, norm_pre, norm_post, norm_mem, w_mem_kv, w_in_a, w_out_a,
           w_in_b, w_out_b, cmp_pos, cmp_w1, cmp_w2):
    n, s_len, d = x_prompt.shape
    ns = x_sample.shape[0]
    depth = norm_pre.shape[0]
    page_size = cache_b_pages.shape[1]
    n_pages = page_table.shape[1]
    past = n_pages * page_size
    assert d == D_MODEL and x_sample.shape[1] == 1 and page_size == PAGE_SIZE
    assert ns >= 2
    assert s_len % A_SPAN == 0 and past % A_SPAN == 0
    caches_a = (cache_a_w128_kv, cache_a_w512_kv, cache_a_w2048_kv)
    for c, (win, _) in zip(caches_a, A_PATTERNS):
        assert c.shape[2] == win
    assert cache_b_win_kv.shape[2] == WIN_B

    tm = ROW_TILE
    n_cmp_p = s_len // CMP_STRIDE
    n_cmp_s = past // CMP_STRIDE
    slope_lanes = jnp.asarray(_slope_lanes(), F32)
    mt = jnp.asarray(_importance_matrix(n_cmp_p, 128).T, BF16)
    mm = jnp.asarray(_importance_matrix(n_cmp_s, 256), BF16)
    eg = jnp.asarray(_gate_expand(), BF16)
    e16 = jnp.asarray((np.arange(SEL_TOPK * PAGE_SIZE)[None, :] // PAGE_SIZE == np.arange(128)[:, None]), BF16)
    pages_t = cache_b_pages.transpose(0, 2, 3, 4, 5, 1)
    caches_a_t = [_rows_last(c) for c in caches_a]
    mem_t = _rows_last(cache_mem_kv)
    win_t = _rows_last(cache_b_win_kv)

    xp = x_prompt
    xs = x_sample.reshape(ns, d)
    mem2 = mem_prompt.reshape(n * N_MEM, d)
    mem_new = []
    a_p = [[] for _ in A_PATTERNS]
    a_s = [[] for _ in A_PATTERNS]
    b_p, b_s, bw_p, bw_s = [], [], [], []
    for i in range(depth):
        li = i // 2
        mkv_p = _rms_proj(mem2, norm_mem[i], w_mem_kv[i].astype(BF16), tm=N_MEM).reshape(n, N_MEM, 2 * X_WIDTH)
        mem_new.append(mkv_p.reshape(n, N_MEM, 2, 4, HEAD_DIM))
        if i % 2 == 0:
            w_in = w_in_a[li].astype(BF16)
            w_out = w_out_a[li].astype(BF16)
            proj_p = _rms_proj(xp.reshape(n * s_len, d), norm_pre[i], w_in, tm=tm).reshape(n, s_len, W_IN_A)
            proj_s = _rms_proj(xs, norm_pre[i], w_in, tm=ns)
            ols = []
            for g, (win, dil) in enumerate(A_PATTERNS):
                ols += _a_prompt_group(proj_p, g, dil)
                kv_p = proj_p[:, s_len - win:, 768:2304].reshape(n, win, 2, 3, 4, HEAD_DIM)[:, :, :, g]
                a_p[g].append(kv_p)
                a_s[g].append(proj_s[:, 768:2304].reshape(ns, 1, 2, 3, 4, HEAD_DIM)[:, :, :, g])
            xp = _finish_a(xp, ols, proj_p, mkv_p, w_out, norm_post[i], tm=2 * tm)
            z = _sample_a(proj_s, caches_a_t, mem_t, li, i)
            xs = _tail(xs, z, w_out, norm_post[i])
        else:
            w_in = _permute_w_in_b(w_in_b[li]).astype(BF16)
            w_out = w_out_b[li].astype(BF16)
            posw, w1bd, w2bd = _compress_weights(cmp_pos[li], cmp_w1[li], cmp_w2[li])
            proj_p, kvs = _rms_proj(xp.reshape(n * s_len, d), norm_pre[i], w_in, tm=tm, side=(1024, 512))
            proj_p = proj_p.reshape(n, s_len, W_IN_B_PAD)
            kvs = kvs.reshape(n, s_len, 512)
            proj_s = _rms_proj(xs, norm_pre[i], w_in, tm=ns)
            cmpd = _compress_prompt(proj_p, posw, w1bd, w2bd)
            pos = jnp.arange(s_len, dtype=jnp.int32)
            cend = CMP_STRIDE * jnp.arange(n_cmp_p, dtype=jnp.int32) + (CMP_LEN - 1)
            mix = _nsa_prompt(proj_p, _keys_with_pos(cmpd[:, :, 0:128], cend), cmpd[:, :, 128:256].transpose(0, 2, 1),
                              _keys_with_pos(kvs[:, :, 256:384], pos), _values_by_group(kvs[:, :, 384:512]),
                              _keys_with_pos(kvs[:, :, 0:128], pos), _values_by_group(kvs[:, :, 128:256]),
                              mt, eg, slope_lanes)
            xp = _finish_b(xp, mix, proj_p, mkv_p, w_out, norm_post[i], tm=2 * tm)
            oc, sel = _sample_b1(page_table, proj_s, posw, w1bd, w2bd, mm, pages_t, li)
            z = _sample_b2(page_table, sel, proj_s, oc, win_t, mem_t, e16, pages_t, li, i)
            xs = _tail(xs, z, w_out, norm_post[i])
            b_p.append(proj_p[:, :, 768:1280].reshape(n, s_len, 4, B_KV, HEAD_DIM))
            bw_p.append(proj_p[:, s_len - WIN_B:, 1280:1536].reshape(n, WIN_B, 2, B_KV, HEAD_DIM))
            b_s.append(proj_s[:, 768:1280].reshape(ns, 1, 4, B_KV, HEAD_DIM))
            bw_s.append(proj_s[:, 1280:1536].reshape(ns, 1, 2, B_KV, HEAD_DIM))
    return (xp, xs.reshape(ns, 1, d), jnp.stack(mem_new, axis=0),
            jnp.stack(a_p[0], axis=0), jnp.stack(a_p[1], axis=0), jnp.stack(a_p[2], axis=0),
            jnp.stack(b_p, axis=2), jnp.stack(bw_p, axis=0),
            jnp.stack(a_s[0], axis=0), jnp.stack(a_s[1], axis=0), jnp.stack(a_s[2], axis=0),
            jnp.stack(b_s, axis=2), jnp.stack(bw_s, axis=0))
```
